```python
import jax, jax.numpy as jnp
from jax import lax
import numpy as np

D_MODEL = 1024
BATCH = 4
SEQ = 4096
DEPTH = 1

N_META = 16
POOL_WIDTH = D_MODEL // 2
POOL_WINDOWS = (2, 4, 8, 16)
POOL_GROUP = POOL_WIDTH // len(POOL_WINDOWS)
N_HEADS = 8
HEAD_DIM = 64
ATTN_WIDTH = N_HEADS * HEAD_DIM
Q_BLOCK = 128
RMS_EPS = 1e-6
IN_SIZES = (POOL_WIDTH, POOL_WIDTH, ATTN_WIDTH, ATTN_WIDTH, ATTN_WIDTH, ATTN_WIDTH, N_HEADS, D_MODEL, D_MODEL)
N_IN = 2 * POOL_WIDTH + 4 * ATTN_WIDTH + N_HEADS + 2 * D_MODEL

kernel_name = "gated_pool_forgetting_attn_hybrid"


def _split_points():
    pts, acc = [], 0
    for s in IN_SIZES[:-1]:
        acc += s
        pts.append(acc)
    return pts


def rmsnorm(x, g):
    xf = x.astype(jnp.float32)
    xf = xf * lax.rsqrt(jnp.mean(xf * xf, axis=-1, keepdims=True) + RMS_EPS)
    return (xf * g.astype(jnp.float32)).astype(x.dtype)


def causal_multiscale_pool(u, pool_w, pool_scale):
    B, L, _ = u.shape
    groups = jnp.split(u, len(POOL_WINDOWS), axis=-1)
    pos = jnp.arange(L, dtype=jnp.float32)[:, None]
    pooled = []
    for w, ug in zip(POOL_WINDOWS, groups):
        uf = ug.astype(jnp.float32)
        c = jnp.cumsum(uf, axis=1)
        c_prev = jnp.pad(c, ((0, 0), (w, 0), (0, 0)))[:, :L]
        count = jnp.minimum(pos + 1.0, float(w))
        pooled.append((c - c_prev) / count - uf)
    p = jnp.stack(pooled, axis=2).astype(u.dtype)
    y = jnp.einsum('blgc,gcd->blgd', p, pool_w)
    return y.reshape(B, L, POOL_WIDTH) * pool_scale


def forgetting_attention(q, k, v, log_f):
    B, L, H, Dh = q.shape
    pad = (Q_BLOCK - L % Q_BLOCK) % Q_BLOCK
    padw = ((0, 0), (pad, 0), (0, 0), (0, 0))
    qp, kp, vp = jnp.pad(q, padw), jnp.pad(k, padw), jnp.pad(v, padw)
    c = jnp.cumsum(log_f, axis=1)
    c = jnp.transpose(jnp.pad(c, ((0, 0), (pad, 0), (0, 0))), (0, 2, 1))
    Lp = L + pad
    scale = 1.0 / np.sqrt(HEAD_DIM)
    outs = []
    for i in range(Lp // Q_BLOCK):
        q0, q1 = i * Q_BLOCK, (i + 1) * Q_BLOCK
        qb, kb, vb = qp[:, q0:q1], kp[:, :q1], vp[:, :q1]
        s = jnp.einsum('bqhd,bkhd->bhqk', qb, kb).astype(jnp.float32) * scale
        s = s + c[:, :, q0:q1, None] - c[:, :, None, :q1]
        q_idx = jnp.arange(q0, q1)[:, None]
        k_idx = jnp.arange(q1)[None, :]
        valid = (k_idx <= q_idx) & ((k_idx >= pad) | (k_idx == q_idx))
        s = jnp.where(valid, s, -jnp.inf)
        p = jax.nn.softmax(s, axis=-1)
        outs.append(jnp.einsum('bhqk,bkhd->bqhd', p.astype(vb.dtype), vb))
    o = jnp.concatenate(outs, axis=1)[:, pad:]
    return o


def hybrid_layer(x, norm_g, w_in, b_forget, pool_w, pool_scale, w_up_pool, w_up_attn, w_out):
    B, L, _ = x.shape
    h = rmsnorm(x, norm_g)
    proj = jnp.einsum('bld,dn->bln', h, w_in)
    u_pool, z_pool, q, k, v, z_attn, f_logit, g_pool, g_attn = jnp.split(proj, _split_points(), axis=-1)
    y_pool = causal_multiscale_pool(u_pool, pool_w, pool_scale) * jax.nn.silu(z_pool)
    q = q.reshape(B, L, N_HEADS, HEAD_DIM)
    k = k.reshape(B, L, N_HEADS, HEAD_DIM)
    v = v.reshape(B, L, N_HEADS, HEAD_DIM)
    log_f = jax.nn.log_sigmoid((f_logit + b_forget).astype(jnp.float32))
    y_attn = forgetting_attention(q, k, v, log_f).reshape(B, L, ATTN_WIDTH) * jax.nn.silu(z_attn)
    merged = (jax.nn.sigmoid(g_pool) * jnp.einsum('blc,cd->bld', y_pool, w_up_pool)
              + jax.nn.sigmoid(g_attn) * jnp.einsum('blc,cd->bld', y_attn, w_up_attn))
    return x + jnp.einsum('bld,de->ble', merged, w_out)


def setup_inputs(seed: int = 0) -> dict:
    key = jax.random.key(seed)
    ks = jax.random.split(key, 12)
    f0 = sum(IN_SIZES[:6])
    x = jax.random.normal(ks[0], (BATCH, SEQ, D_MODEL), jnp.float32)
    meta_tokens = jax.random.normal(ks[1], (N_META, D_MODEL), jnp.float32)
    norm_g = 1.0 + 0.02 * jax.random.normal(ks[2], (DEPTH, D_MODEL), jnp.float32)
    w_in = jax.random.normal(ks[3], (DEPTH, D_MODEL, N_IN), jnp.float32) * D_MODEL ** -0.5
    w_in = w_in.at[:, :, f0:f0 + N_HEADS].multiply(0.1)
    b_forget = (jnp.linspace(1.0, 6.0, N_HEADS, dtype=jnp.float32)[None, :]
                + 0.1 * jax.random.normal(ks[4], (DEPTH, N_HEADS), jnp.float32))
    pool_w = jax.random.normal(ks[5], (DEPTH, len(POOL_WINDOWS), POOL_GROUP, POOL_GROUP), jnp.float32) * POOL_GROUP ** -0.5
    pool_scale = 1.0 + 0.1 * jax.random.normal(ks[6], (DEPTH, POOL_WIDTH), jnp.float32)
    w_up_pool = jax.random.normal(ks[7], (DEPTH, POOL_WIDTH, D_MODEL), jnp.float32) * POOL_WIDTH ** -0.5
    w_up_attn = jax.random.normal(ks[8], (DEPTH, ATTN_WIDTH, D_MODEL), jnp.float32) * ATTN_WIDTH ** -0.5
    w_out = jax.random.normal(ks[9], (DEPTH, D_MODEL, D_MODEL), jnp.float32) * D_MODEL ** -0.5
    final_norm_g = 1.0 + 0.02 * jax.random.normal(ks[10], (D_MODEL,), jnp.float32)
    return {"x": x, "meta_tokens": meta_tokens, "norm_g": norm_g, "w_in": w_in,
            "b_forget": b_forget, "pool_w": pool_w, "pool_scale": pool_scale,
            "w_up_pool": w_up_pool, "w_up_attn": w_up_attn, "w_out": w_out,
            "final_norm_g": final_norm_g}


def reference(x, meta_tokens, norm_g, w_in, b_forget, pool_w, pool_scale, w_up_pool, w_up_attn, w_out, final_norm_g):
    B = x.shape[0]
    meta = jnp.broadcast_to(meta_tokens.astype(x.dtype)[None], (B, N_META, D_MODEL))
    h = jnp.concatenate([meta, x], axis=1)
    for l in range(DEPTH):
        h = hybrid_layer(h, norm_g[l], w_in[l], b_forget[l], pool_w[l], pool_scale[l],
                         w_up_pool[l], w_up_attn[l], w_out[l])
    h = rmsnorm(h, final_norm_g)
    return h[:, N_META:]
```

```python
import functools

import numpy as np
import jax
import jax.numpy as jnp
from jax import lax
from jax.experimental import pallas as pl
from jax.experimental.pallas import tpu as pltpu

D_MODEL = 1024
N_META = 16
POOL_WIDTH = 512
POOL_WINDOWS = (2, 4, 8, 16)
POOL_GROUP = POOL_WIDTH // len(POOL_WINDOWS)
N_HEADS = 8
HEAD_DIM = 64
ATTN_WIDTH = N_HEADS * HEAD_DIM
RMS_EPS = 1e-6

LANES = 128
BLK = 256
TM = 512
NB = TM // BLK
TM_OUT = 512
V_ROWS = HEAD_DIM + 16
N_SPLIT = 3
MASKED = -1e30
VMEM_LIMIT = 56 * 1024 * 1024

f32 = jnp.float32
bf16 = jnp.bfloat16


def _rmsnorm(x, g):
    return x * lax.rsqrt(jnp.mean(x * x, axis=-1, keepdims=True) + RMS_EPS) * g


def _sigmoid(x):
    return 1.0 / (1.0 + jnp.exp(-x))


def _log_sigmoid(x):
    return jnp.minimum(x, 0.0) - jnp.log1p(jnp.exp(-jnp.abs(x)))


def _dot(a, b):
    return jnp.dot(a, b, preferred_element_type=f32)


def _dot_nt(a, b):
    return lax.dot_general(a, b, (((1,), (1,)), ((), ())), preferred_element_type=f32)


def _lane_iota(shape):
    return lax.broadcasted_iota(jnp.int32, shape, 1)


def _decay_parts(logf3, rows):
    r = lax.broadcasted_iota(jnp.int32, (rows, rows), 0)
    c = lax.broadcasted_iota(jnp.int32, (rows, rows), 1)
    tri = (c <= r).astype(f32)
    beta = jnp.dot(tri, logf3, precision=lax.Precision.HIGHEST, preferred_element_type=f32)
    nb = -beta
    hi = nb.astype(bf16).astype(f32)
    mid = (nb - hi).astype(bf16).astype(f32)
    lo = (nb - hi - mid).astype(bf16).astype(f32)
    lane = _lane_iota(nb.shape)
    parts = jnp.where(lane < N_HEADS, hi, jnp.where(lane < 2 * N_HEADS, mid, lo)).astype(bf16)
    return beta, parts


def _augmented_keys(kproj, parts, e_ref, store):
    aug = _dot(parts, e_ref[...])
    lane = _lane_iota((kproj.shape[0], LANES))
    for h in range(N_HEADS):
        slab = kproj[:, LANES * (h // 2):LANES * (h // 2 + 1)]
        if h % 2:
            slab = pltpu.roll(slab, HEAD_DIM, axis=1)
        store(h, jnp.where(lane < HEAD_DIM, slab, aug[:, LANES * h:LANES * (h + 1)]).astype(bf16))


def _meta_kernel(meta_ref, g_ref, wuk_ref, wvt_ref, wf_ref, bf_ref, e_ref,
                 umeta_ref, kmeta_ref, vmeta_ref, r0_ref):
    hn = _rmsnorm(meta_ref[...], g_ref[...]).astype(bf16)
    uk = _dot(hn, wuk_ref[...])
    umeta_ref[...] = uk[:, :POOL_WIDTH]
    vt = _dot_nt(wvt_ref[...], hn)
    for h in range(N_HEADS):
        vmeta_ref[h, 0:HEAD_DIM, :] = vt[HEAD_DIM * h:HEAD_DIM * (h + 1), :].astype(bf16)
        vmeta_ref[h, HEAD_DIM:V_ROWS, :] = jnp.ones((V_ROWS - HEAD_DIM, N_META), bf16)
    logf3 = _log_sigmoid(_dot(hn, wf_ref[...]) + bf_ref[...])
    beta, parts = _decay_parts(logf3, N_META)

    def store(h, ka):
        kmeta_ref[h] = ka

    _augmented_keys(uk[:, POOL_WIDTH:], parts, e_ref, store)
    r0_ref[...] = jnp.broadcast_to(beta[N_META - 1:N_META, :], r0_ref.shape)


def _proj_kernel(x_ref, g_ref, wmain_ref, wqv_ref, wf_ref, bf_ref, e_ref, pw_ref, ps_ref, wup_ref,
                 umeta_ref, r0_ref,
                 mp_ref, ga_ref, sz_ref, qT_ref, vT_ref, kaug_ref, r_ref,
                 uext_ref, rcarry_ref):
    t = pl.program_id(1)

    @pl.when(t == 0)
    def _():
        uext_ref[0:N_META, :] = umeta_ref[...]
        rcarry_ref[...] = r0_ref[...]

    hn = _rmsnorm(x_ref[0], g_ref[...]).astype(bf16)

    qv = _dot_nt(wqv_ref[...], hn)
    qT_ref[0] = (qv[0:ATTN_WIDTH] * (HEAD_DIM ** -0.5)).astype(bf16)
    for c in range(NB):
        for h in range(N_HEADS):
            rows = slice(ATTN_WIDTH + HEAD_DIM * h, ATTN_WIDTH + HEAD_DIM * (h + 1))
            vT_ref[0, c, h, 0:HEAD_DIM, :] = qv[rows, BLK * c:BLK * (c + 1)].astype(bf16)
            vT_ref[0, c, h, HEAD_DIM:V_ROWS, :] = jnp.ones((V_ROWS - HEAD_DIM, BLK), bf16)

    uz = _dot(hn, wmain_ref[:, 0:2 * POOL_WIDTH])
    u = uz[:, :POOL_WIDTH]
    uext_ref[N_META:N_META + TM, :] = u
    ys = []
    for g, w in enumerate(POOL_WINDOWS):
        cols = slice(POOL_GROUP * g, POOL_GROUP * (g + 1))
        acc = u[:, cols]
        for j in range(1, w):
            acc = acc + uext_ref[N_META - j:N_META - j + TM, cols]
        pooled = acc * (1.0 / w) - u[:, cols]
        pg = _dot(pooled.astype(bf16), pw_ref[g])
        zp = uz[:, POOL_WIDTH + POOL_GROUP * g:POOL_WIDTH + POOL_GROUP * (g + 1)]
        ys.append(pg * ps_ref[:, cols] * (zp * _sigmoid(zp)))
    uext_ref[0:N_META, :] = uext_ref[TM:TM + N_META, :]
    y_pool = jnp.concatenate(ys, axis=1).astype(bf16)
    gp = _dot(hn, wmain_ref[:, 2048:3072])
    mp_ref[0] = (_sigmoid(gp) * _dot(y_pool, wup_ref[...])).astype(bf16)

    gat = _dot(hn, wmain_ref[:, 3072:4096])
    ga_ref[0] = _sigmoid(gat).astype(bf16)

    kz = _dot(hn, wmain_ref[:, 1024:2048])
    za = kz[:, ATTN_WIDTH:]
    sz_ref[0] = (za * _sigmoid(za)).astype(bf16)

    logf3 = _log_sigmoid(_dot(hn, wf_ref[...]) + bf_ref[...])
    r_ref[...] = jnp.zeros(r_ref.shape, f32)
    for c in range(NB):
        rows = slice(BLK * c, BLK * (c + 1))
        beta, parts = _decay_parts(logf3[rows], BLK)
        r_ref[0, 0, c:c + 1, :] = rcarry_ref[0:1, :]
        rcarry_ref[...] = rcarry_ref[...] + beta[BLK - 1:BLK, :]

        def store(h, ka, rows=rows):
            kaug_ref[0, h, rows, :] = ka

        _augmented_keys(kz[rows, :ATTN_WIDTH], parts, e_ref, store)


def _attn_kernel(r_ref, qT_ref, kaug_ref, vT_ref, kmeta_ref, vmeta_ref, o_ref,
                 qaug_ref, m_ref, acc_ref, *, n_blocks):
    b = pl.program_id(0)
    i = pl.program_id(1)

    ones_rows = (lax.broadcasted_iota(jnp.int32, (LANES - HEAD_DIM, BLK), 0) < N_SPLIT).astype(bf16)
    for h in range(N_HEADS):
        qaug_ref[h, 0:HEAD_DIM, :] = qT_ref[0, HEAD_DIM * h:HEAD_DIM * (h + 1), :]
        qaug_ref[h, HEAD_DIM:LANES, :] = ones_rows

    def r_at(j, h):
        return r_ref[(b * n_blocks + j) * N_HEADS + h]

    def block(h, k_blk, v_blk, off, mask, first):
        s = _dot(k_blk, qaug_ref[h])
        if mask is not None:
            s = jnp.where(mask, s, MASKED)
        m_blk = jnp.max(s, axis=0, keepdims=True) + off
        if first:
            m_new = m_blk
        else:
            m_old = m_ref[h]
            m_new = jnp.maximum(m_old, m_blk)
        p = jnp.exp(s - (m_new - off)).astype(bf16)
        pv = _dot(v_blk, p)
        if first:
            acc_ref[h] = pv
        else:
            acc_ref[h] = acc_ref[h] * jnp.exp(m_old - m_new) + pv
        m_ref[h] = m_new

    for h in range(N_HEADS):
        block(h, kmeta_ref[h], vmeta_ref[h], r_at(i, h), None, True)

    def body(j, carry):
        start = pl.multiple_of(j * BLK, BLK)
        for h in range(N_HEADS):
            block(h, kaug_ref[0, h, pl.ds(start, BLK), :], vT_ref[0, j, h], r_at(i, h) - r_at(j, h), None, False)
        return carry

    lax.fori_loop(0, i, body, 0)

    causal = (lax.broadcasted_iota(jnp.int32, (BLK, BLK), 0) <= lax.broadcasted_iota(jnp.int32, (BLK, BLK), 1))
    start = pl.multiple_of(i * BLK, BLK)
    for h in range(N_HEADS):
        block(h, kaug_ref[0, h, pl.ds(start, BLK), :], vT_ref[0, i, h], 0.0, causal, False)

    for pair in range(N_HEADS // 2):
        halves = []
        for h in (2 * pair, 2 * pair + 1):
            a = acc_ref[h]
            halves.append(a[0:HEAD_DIM] * (1.0 / a[HEAD_DIM:HEAD_DIM + 1]))
        o_ref[0, :, LANES * pair:LANES * (pair + 1)] = jnp.concatenate(halves, axis=0).T.astype(bf16)


def _out_kernel(x_ref, o_ref, sz_ref, ga_ref, mp_ref, wua_ref, wout_ref, g_ref, out_ref):
    y_attn = (o_ref[0].astype(f32) * sz_ref[0].astype(f32)).astype(bf16)
    merged = mp_ref[0].astype(f32) + ga_ref[0].astype(f32) * _dot(y_attn, wua_ref[...])
    h_out = x_ref[0] + _dot(merged.astype(bf16), wout_ref[...])
    out_ref[0] = _rmsnorm(h_out, g_ref[...])


def _placement_matrix():
    e = np.zeros((LANES, N_HEADS * LANES), np.float32)
    for part in range(N_SPLIT):
        for h in range(N_HEADS):
            e[part * N_HEADS + h, LANES * h + HEAD_DIM + part] = 1.0
    return jnp.asarray(e, bf16)


def _const(shape):
    return pl.BlockSpec(shape, lambda *_: (0,) * len(shape), pipeline_mode=pl.Buffered(1))


def kernel(x, meta_tokens, norm_g, w_in, b_forget, pool_w, pool_scale, w_up_pool, w_up_attn, w_out, final_norm_g):
    batch, seq, _ = x.shape
    n_tiles = seq // TM
    n_blocks = seq // BLK

    w = w_in[0]
    sizes = (POOL_WIDTH, POOL_WIDTH, ATTN_WIDTH, ATTN_WIDTH, ATTN_WIDTH, ATTN_WIDTH, N_HEADS, D_MODEL, D_MODEL)
    w_u, w_zp, w_q, w_k, w_v, w_za, w_f, w_gp, w_ga = jnp.split(w, np.cumsum(sizes)[:-1].tolist(), axis=1)
    w_main = jnp.concatenate([w_u, w_zp, w_k, w_za, w_gp, w_ga], axis=1).astype(bf16)
    w_qvT = jnp.concatenate([w_q, w_v], axis=1).T.astype(bf16)
    w_f3 = jnp.pad(jnp.tile(w_f, (1, N_SPLIT)), ((0, 0), (0, LANES - N_SPLIT * N_HEADS))).astype(bf16)
    b_f3 = jnp.pad(jnp.tile(b_forget[0], N_SPLIT), (0, LANES - N_SPLIT * N_HEADS)).reshape(1, LANES)
    w_uk = jnp.concatenate([w_u, w_k], axis=1).astype(bf16)
    w_vT = w_v.T.astype(bf16)
    e_mat = _placement_matrix()
    g_in = norm_g[0].reshape(1, D_MODEL)
    g_out = final_norm_g.reshape(1, D_MODEL)
    pw = pool_w[0].astype(bf16)
    ps = pool_scale[0].reshape(1, POOL_WIDTH)
    w_upp = w_up_pool[0].astype(bf16)
    w_upa = w_up_attn[0].astype(bf16)
    w_o = w_out[0].astype(bf16)

    u_meta, k_meta, v_meta, r0 = pl.pallas_call(
        _meta_kernel,
        out_shape=(jax.ShapeDtypeStruct((N_META, POOL_WIDTH), f32),
                   jax.ShapeDtypeStruct((N_HEADS, N_META, LANES), bf16),
                   jax.ShapeDtypeStruct((N_HEADS, V_ROWS, N_META), bf16),
                   jax.ShapeDtypeStruct((8, LANES), f32)),
        compiler_params=pltpu.CompilerParams(vmem_limit_bytes=VMEM_LIMIT),
        name="meta_proj",
    )(meta_tokens, g_in, w_uk, w_vT, w_f3, b_f3, e_mat)

    mp, ga, sz, qT, vT, kaug, r_tiles = pl.pallas_call(
        _proj_kernel,
        grid=(batch, n_tiles),
        in_specs=[
            pl.BlockSpec((1, TM, D_MODEL), lambda b, t: (b, t, 0)),
            _const((1, D_MODEL)),
            _const((D_MODEL, 4096)),
            _const((2 * ATTN_WIDTH, D_MODEL)),
            _const((D_MODEL, LANES)),
            _const((1, LANES)),
            _const((LANES, N_HEADS * LANES)),
            _const((len(POOL_WINDOWS), POOL_GROUP, POOL_GROUP)),
            _const((1, POOL_WIDTH)),
            _const((POOL_WIDTH, D_MODEL)),
            _const((N_META, POOL_WIDTH)),
            _const((8, LANES)),
        ],
        out_specs=[
            pl.BlockSpec((1, TM, D_MODEL), lambda b, t: (b, t, 0)),
            pl.BlockSpec((1, TM, D_MODEL), lambda b, t: (b, t, 0)),
            pl.BlockSpec((1, TM, ATTN_WIDTH), lambda b, t: (b, t, 0)),
            pl.BlockSpec((1, ATTN_WIDTH, TM), lambda b, t: (b, 0, t)),
            pl.BlockSpec((1, NB, N_HEADS, V_ROWS, BLK), lambda b, t: (b, t, 0, 0, 0)),
            pl.BlockSpec((1, N_HEADS, TM, LANES), lambda b, t: (b, 0, t, 0)),
            pl.BlockSpec((1, 1, 8, LANES), lambda b, t: (b, t, 0, 0)),
        ],
        out_shape=(
            jax.ShapeDtypeStruct((batch, seq, D_MODEL), bf16),
            jax.ShapeDtypeStruct((batch, seq, D_MODEL), bf16),
            jax.ShapeDtypeStruct((batch, seq, ATTN_WIDTH), bf16),
            jax.ShapeDtypeStruct((batch, ATTN_WIDTH, seq), bf16),
            jax.ShapeDtypeStruct((batch, n_blocks, N_HEADS, V_ROWS, BLK), bf16),
            jax.ShapeDtypeStruct((batch, N_HEADS, seq, LANES), bf16),
            jax.ShapeDtypeStruct((batch, n_tiles, 8, LANES), f32),
        ),
        scratch_shapes=[pltpu.VMEM((N_META + TM, POOL_WIDTH), f32), pltpu.VMEM((8, LANES), f32)],
        compiler_params=pltpu.CompilerParams(dimension_semantics=("arbitrary", "arbitrary"),
                                             vmem_limit_bytes=VMEM_LIMIT),
        name="in_proj",
    )(x, g_in, w_main, w_qvT, w_f3, b_f3, e_mat, pw, ps, w_upp, u_meta, r0)

    r_blocks = r_tiles[:, :, :NB, :N_HEADS].reshape(batch * n_blocks * N_HEADS)

    o = pl.pallas_call(
        functools.partial(_attn_kernel, n_blocks=n_blocks),
        grid=(batch, n_blocks),
        in_specs=[
            pl.BlockSpec(memory_space=pltpu.SMEM),
            pl.BlockSpec((1, ATTN_WIDTH, BLK), lambda b, i: (b, 0, i)),
            pl.BlockSpec((1, N_HEADS, seq, LANES), lambda b, i: (b, 0, 0, 0)),
            pl.BlockSpec((1, n_blocks, N_HEADS, V_ROWS, BLK), lambda b, i: (b, 0, 0, 0, 0)),
            _const((N_HEADS, N_META, LANES)),
            _const((N_HEADS, V_ROWS, N_META)),
        ],
        out_specs=pl.BlockSpec((1, BLK, ATTN_WIDTH), lambda b, i: (b, i, 0)),
        out_shape=jax.ShapeDtypeStruct((batch, seq, ATTN_WIDTH), bf16),
        scratch_shapes=[pltpu.VMEM((N_HEADS, LANES, BLK), bf16),
                        pltpu.VMEM((N_HEADS, 1, BLK), f32),
                        pltpu.VMEM((N_HEADS, V_ROWS, BLK), f32)],
        compiler_params=pltpu.CompilerParams(dimension_semantics=("arbitrary", "arbitrary"),
                                             vmem_limit_bytes=VMEM_LIMIT),
        name="attention",
    )(r_blocks, qT, kaug, vT, k_meta, v_meta)

    return pl.pallas_call(
        _out_kernel,
        grid=(batch, seq // TM_OUT),
        in_specs=[
            pl.BlockSpec((1, TM_OUT, D_MODEL), lambda b, t: (b, t, 0)),
            pl.BlockSpec((1, TM_OUT, ATTN_WIDTH), lambda b, t: (b, t, 0)),
            pl.BlockSpec((1, TM_OUT, ATTN_WIDTH), lambda b, t: (b, t, 0)),
            pl.BlockSpec((1, TM_OUT, D_MODEL), lambda b, t: (b, t, 0)),
            pl.BlockSpec((1, TM_OUT, D_MODEL), lambda b, t: (b, t, 0)),
            _const((ATTN_WIDTH, D_MODEL)),
            _const((D_MODEL, D_MODEL)),
            _const((1, D_MODEL)),
        ],
        out_specs=pl.BlockSpec((1, TM_OUT, D_MODEL), lambda b, t: (b, t, 0)),
        out_shape=jax.ShapeDtypeStruct((batch, seq, D_MODEL), f32),
        compiler_params=pltpu.CompilerParams(dimension_semantics=("arbitrary", "arbitrary"),
                                             vmem_limit_bytes=VMEM_LIMIT),
        name="out_proj",
    )(x, o, sz, ga, mp, w_upa, w_o, g_out)
```

```python
import functools

import numpy as np
import jax
import jax.numpy as jnp
from jax import lax
from jax.experimental import pallas as pl
from jax.experimental.pallas import tpu as pltpu

D_MODEL = 1024
N_META = 16
POOL_WIDTH = 512
POOL_WINDOWS = (2, 4, 8, 16)
POOL_GROUP = POOL_WIDTH // len(POOL_WINDOWS)
N_HEADS = 8
HEAD_DIM = 64
ATTN_WIDTH = N_HEADS * HEAD_DIM
RMS_EPS = 1e-6

LANES = 128
BLK = 256
TM = 512
NB = TM // BLK
TM_OUT = 512
V_ROWS = HEAD_DIM + 16
N_SPLIT = 3
MASKED = -1e30
LOG2E = 1.4426950408889634
VMEM_LIMIT = 56 * 1024 * 1024

f32 = jnp.float32
bf16 = jnp.bfloat16


def _rmsnorm(x, g):
    return x * lax.rsqrt(jnp.mean(x * x, axis=-1, keepdims=True) + RMS_EPS) * g


def _sigmoid(x):
    return 1.0 / (1.0 + jnp.exp(-x))


def _log_sigmoid(x):
    return jnp.minimum(x, 0.0) - jnp.log1p(jnp.exp(-jnp.abs(x)))


def _dot(a, b):
    return jnp.dot(a, b, preferred_element_type=f32)


def _dot_nt(a, b):
    return lax.dot_general(a, b, (((1,), (1,)), ((), ())), preferred_element_type=f32)


def _lane_iota(shape):
    return lax.broadcasted_iota(jnp.int32, shape, 1)


def _decay_parts(logf3, rows):
    r = lax.broadcasted_iota(jnp.int32, (rows, rows), 0)
    c = lax.broadcasted_iota(jnp.int32, (rows, rows), 1)
    tri = (c <= r).astype(f32)
    beta = jnp.dot(tri, logf3, precision=lax.Precision.HIGHEST, preferred_element_type=f32)
    nb = beta * (-LOG2E)
    hi = nb.astype(bf16).astype(f32)
    mid = (nb - hi).astype(bf16).astype(f32)
    lo = (nb - hi - mid).astype(bf16).astype(f32)
    lane = _lane_iota(nb.shape)
    parts = jnp.where(lane < N_HEADS, hi, jnp.where(lane < 2 * N_HEADS, mid, lo)).astype(bf16)
    return beta, parts


def _augmented_keys(kproj, parts, e_ref, store):
    aug = _dot(parts, e_ref[...])
    lane = _lane_iota((kproj.shape[0], LANES))
    for h in range(N_HEADS):
        slab = kproj[:, LANES * (h // 2):LANES * (h // 2 + 1)]
        if h % 2:
            slab = pltpu.roll(slab, HEAD_DIM, axis=1)
        store(h, jnp.where(lane < HEAD_DIM, slab, aug[:, LANES * h:LANES * (h + 1)]).astype(bf16))


def _meta_kernel(meta_ref, g_ref, wuk_ref, wvt_ref, wf_ref, bf_ref, e_ref,
                 umeta_ref, kmeta_ref, vmeta_ref, r0_ref):
    hn = _rmsnorm(meta_ref[...], g_ref[...]).astype(bf16)
    uk = _dot(hn, wuk_ref[...])
    umeta_ref[...] = uk[:, :POOL_WIDTH]
    vt = _dot_nt(wvt_ref[...], hn)
    for h in range(N_HEADS):
        vmeta_ref[h, 0:HEAD_DIM, :] = vt[HEAD_DIM * h:HEAD_DIM * (h + 1), :].astype(bf16)
        vmeta_ref[h, HEAD_DIM:V_ROWS, :] = jnp.ones((V_ROWS - HEAD_DIM, N_META), bf16)
    logf3 = _log_sigmoid(_dot(hn, wf_ref[...]) + bf_ref[...])
    beta, parts = _decay_parts(logf3, N_META)

    def store(h, ka):
        kmeta_ref[h] = ka

    _augmented_keys(uk[:, POOL_WIDTH:], parts, e_ref, store)
    r0_ref[...] = jnp.broadcast_to(beta[N_META - 1:N_META, :], r0_ref.shape)


def _proj_kernel(x_ref, g_ref, wmain_ref, wqv_ref, wf_ref, bf_ref, e_ref, pw_ref, ps_ref, wup_ref,
                 umeta_ref, r0_ref,
                 mp_ref, ga_ref, sz_ref, qT_ref, vT_ref, kaug_ref, r_ref,
                 uext_ref, rcarry_ref):
    t = pl.program_id(1)

    @pl.when(t == 0)
    def _():
        uext_ref[0:N_META, :] = umeta_ref[...]
        rcarry_ref[...] = r0_ref[...]

    hn = _rmsnorm(x_ref[0], g_ref[...]).astype(bf16)

    qv = _dot_nt(wqv_ref[...], hn)
    qT_ref[0] = (qv[0:ATTN_WIDTH] * (HEAD_DIM ** -0.5 * LOG2E)).astype(bf16)
    for c in range(NB):
        for h in range(N_HEADS):
            rows = slice(ATTN_WIDTH + HEAD_DIM * h, ATTN_WIDTH + HEAD_DIM * (h + 1))
            vT_ref[0, c, h, 0:HEAD_DIM, :] = qv[rows, BLK * c:BLK * (c + 1)].astype(bf16)
            vT_ref[0, c, h, HEAD_DIM:V_ROWS, :] = jnp.ones((V_ROWS - HEAD_DIM, BLK), bf16)

    uz = _dot(hn, wmain_ref[:, 0:2 * POOL_WIDTH])
    u = uz[:, :POOL_WIDTH]
    uext_ref[N_META:N_META + TM, :] = u
    ys = []
    for g, w in enumerate(POOL_WINDOWS):
        cols = slice(POOL_GROUP * g, POOL_GROUP * (g + 1))
        acc = u[:, cols]
        for j in range(1, w):
            acc = acc + uext_ref[N_META - j:N_META - j + TM, cols]
        pooled = acc * (1.0 / w) - u[:, cols]
        pg = _dot(pooled.astype(bf16), pw_ref[g])
        zp = uz[:, POOL_WIDTH + POOL_GROUP * g:POOL_WIDTH + POOL_GROUP * (g + 1)]
        ys.append(pg * ps_ref[:, cols] * (zp * _sigmoid(zp)))
    uext_ref[0:N_META, :] = uext_ref[TM:TM + N_META, :]
    y_pool = jnp.concatenate(ys, axis=1).astype(bf16)
    gp = _dot(hn, wmain_ref[:, 2048:3072])
    mp_ref[0] = (_sigmoid(gp) * _dot(y_pool, wup_ref[...])).astype(bf16)

    gat = _dot(hn, wmain_ref[:, 3072:4096])
    ga_ref[0] = _sigmoid(gat).astype(bf16)

    kz = _dot(hn, wmain_ref[:, 1024:2048])
    za = kz[:, ATTN_WIDTH:]
    sz_ref[0] = (za * _sigmoid(za)).astype(bf16)

    logf3 = _log_sigmoid(_dot(hn, wf_ref[...]) + bf_ref[...])
    r_ref[...] = jnp.zeros(r_ref.shape, f32)
    for c in range(NB):
        rows = slice(BLK * c, BLK * (c + 1))
        beta, parts = _decay_parts(logf3[rows], BLK)
        r_ref[0, 0, c:c + 1, :] = rcarry_ref[0:1, :] * LOG2E
        rcarry_ref[...] = rcarry_ref[...] + beta[BLK - 1:BLK, :]

        def store(h, ka, rows=rows):
            kaug_ref[0, h, rows, :] = ka

        _augmented_keys(kz[rows, :ATTN_WIDTH], parts, e_ref, store)


def _attn_kernel(r_ref, qT_ref, kaug_ref, vT_ref, kmeta_ref, vmeta_ref, o_ref,
                 qaug_ref, s_ref, p_ref, sm_ref, pm_ref, m_ref, alpha_ref, acc_ref, *, n_blocks):
    b = pl.program_id(0)
    i = pl.program_id(1)
    heads = range(N_HEADS)

    ones_rows = (lax.broadcasted_iota(jnp.int32, (LANES - HEAD_DIM, BLK), 0) < N_SPLIT).astype(bf16)
    for h in heads:
        qaug_ref[h, 0:HEAD_DIM, :] = qT_ref[0, HEAD_DIM * h:HEAD_DIM * (h + 1), :]
        qaug_ref[h, HEAD_DIM:LANES, :] = ones_rows

    def r_at(j, h):
        return r_ref[(b * n_blocks + j) * N_HEADS + h]

    def col_max(s):
        return jnp.max(s, axis=0, keepdims=True)

    causal = (lax.broadcasted_iota(jnp.int32, (BLK, BLK), 0) <= lax.broadcasted_iota(jnp.int32, (BLK, BLK), 1))
    start_i = pl.multiple_of(i * BLK, BLK)
    for h in heads:
        s_ref[h] = jnp.where(causal, _dot(kaug_ref[0, h, pl.ds(start_i, BLK), :], qaug_ref[h]), MASKED)
        sm_ref[h] = _dot(kmeta_ref[h], qaug_ref[h])
    for h in heads:
        off = r_at(i, h)
        m = jnp.maximum(col_max(s_ref[h]), col_max(sm_ref[h]) + off)
        m_ref[h] = m
        p_ref[h] = jnp.exp2(s_ref[h] - m).astype(bf16)
        pm_ref[h] = jnp.exp2(sm_ref[h] - (m - off)).astype(bf16)
    for h in heads:
        acc_ref[h] = _dot(vT_ref[0, i, h], p_ref[h]) + _dot(vmeta_ref[h], pm_ref[h])

    def body(j, carry):
        start = pl.multiple_of(j * BLK, BLK)
        for h in heads:
            s_ref[h] = _dot(kaug_ref[0, h, pl.ds(start, BLK), :], qaug_ref[h])
        for h in heads:
            off = r_at(i, h) - r_at(j, h)
            m_old = m_ref[h]
            m_new = jnp.maximum(m_old, col_max(s_ref[h]) + off)
            alpha_ref[h] = jnp.exp2(m_old - m_new)
            m_ref[h] = m_new
            p_ref[h] = jnp.exp2(s_ref[h] - (m_new - off)).astype(bf16)
        for h in heads:
            acc_ref[h] = acc_ref[h] * alpha_ref[h] + _dot(vT_ref[0, j, h], p_ref[h])
        return carry

    lax.fori_loop(0, i, body, 0)

    for pair in range(N_HEADS // 2):
        halves = []
        for h in (2 * pair, 2 * pair + 1):
            a = acc_ref[h]
            halves.append(a[0:HEAD_DIM] * (1.0 / a[HEAD_DIM:HEAD_DIM + 1]))
        o_ref[0, :, LANES * pair:LANES * (pair + 1)] = jnp.concatenate(halves, axis=0).T.astype(bf16)


def _out_kernel(x_ref, o_ref, sz_ref, ga_ref, mp_ref, wua_ref, wout_ref, g_ref, out_ref):
    y_attn = (o_ref[0].astype(f32) * sz_ref[0].astype(f32)).astype(bf16)
    merged = mp_ref[0].astype(f32) + ga_ref[0].astype(f32) * _dot(y_attn, wua_ref[...])
    h_out = x_ref[0] + _dot(merged.astype(bf16), wout_ref[...])
    out_ref[0] = _rmsnorm(h_out, g_ref[...])


def _placement_matrix():
    e = np.zeros((LANES, N_HEADS * LANES), np.float32)
    for part in range(N_SPLIT):
        for h in range(N_HEADS):
            e[part * N_HEADS + h, LANES * h + HEAD_DIM + part] = 1.0
    return jnp.asarray(e, bf16)


def _const(shape):
    return pl.BlockSpec(shape, lambda *_: (0,) * len(shape), pipeline_mode=pl.Buffered(1))


def kernel(x, meta_tokens, norm_g, w_in, b_forget, pool_w, pool_scale, w_up_pool, w_up_attn, w_out, final_norm_g):
    batch, seq, _ = x.shape
    n_tiles = seq // TM
    n_blocks = seq // BLK

    w = w_in[0]
    sizes = (POOL_WIDTH, POOL_WIDTH, ATTN_WIDTH, ATTN_WIDTH, ATTN_WIDTH, ATTN_WIDTH, N_HEADS, D_MODEL, D_MODEL)
    w_u, w_zp, w_q, w_k, w_v, w_za, w_f, w_gp, w_ga = jnp.split(w, np.cumsum(sizes)[:-1].tolist(), axis=1)
    w_main = jnp.concatenate([w_u, w_zp, w_k, w_za, w_gp, w_ga], axis=1).astype(bf16)
    w_qvT = jnp.concatenate([w_q, w_v], axis=1).T.astype(bf16)
    w_f3 = jnp.pad(jnp.tile(w_f, (1, N_SPLIT)), ((0, 0), (0, LANES - N_SPLIT * N_HEADS))).astype(bf16)
    b_f3 = jnp.pad(jnp.tile(b_forget[0], N_SPLIT), (0, LANES - N_SPLIT * N_HEADS)).reshape(1, LANES)
    w_uk = jnp.concatenate([w_u, w_k], axis=1).astype(bf16)
    w_vT = w_v.T.astype(bf16)
    e_mat = _placement_matrix()
    g_in = norm_g[0].reshape(1, D_MODEL)
    g_out = final_norm_g.reshape(1, D_MODEL)
    pw = pool_w[0].astype(bf16)
    ps = pool_scale[0].reshape(1, POOL_WIDTH)
    w_upp = w_up_pool[0].astype(bf16)
    w_upa = w_up_attn[0].astype(bf16)
    w_o = w_out[0].astype(bf16)

    u_meta, k_meta, v_meta, r0 = pl.pallas_call(
        _meta_kernel,
        out_shape=(jax.ShapeDtypeStruct((N_META, POOL_WIDTH), f32),
                   jax.ShapeDtypeStruct((N_HEADS, N_META, LANES), bf16),
                   jax.ShapeDtypeStruct((N_HEADS, V_ROWS, N_META), bf16),
                   jax.ShapeDtypeStruct((8, LANES), f32)),
        compiler_params=pltpu.CompilerParams(vmem_limit_bytes=VMEM_LIMIT),
        name="meta_proj",
    )(meta_tokens, g_in, w_uk, w_vT, w_f3, b_f3, e_mat)

    mp, ga, sz, qT, vT, kaug, r_tiles = pl.pallas_call(
        _proj_kernel,
        grid=(batch, n_tiles),
        in_specs=[
            pl.BlockSpec((1, TM, D_MODEL), lambda b, t: (b, t, 0)),
            _const((1, D_MODEL)),
            _const((D_MODEL, 4096)),
            _const((2 * ATTN_WIDTH, D_MODEL)),
            _const((D_MODEL, LANES)),
            _const((1, LANES)),
            _const((LANES, N_HEADS * LANES)),
            _const((len(POOL_WINDOWS), POOL_GROUP, POOL_GROUP)),
            _const((1, POOL_WIDTH)),
            _const((POOL_WIDTH, D_MODEL)),
            _const((N_META, POOL_WIDTH)),
            _const((8, LANES)),
        ],
        out_specs=[
            pl.BlockSpec((1, TM, D_MODEL), lambda b, t: (b, t, 0)),
            pl.BlockSpec((1, TM, D_MODEL), lambda b, t: (b, t, 0)),
            pl.BlockSpec((1, TM, ATTN_WIDTH), lambda b, t: (b, t, 0)),
            pl.BlockSpec((1, ATTN_WIDTH, TM), lambda b, t: (b, 0, t)),
            pl.BlockSpec((1, NB, N_HEADS, V_ROWS, BLK), lambda b, t: (b, t, 0, 0, 0)),
            pl.BlockSpec((1, N_HEADS, TM, LANES), lambda b, t: (b, 0, t, 0)),
            pl.BlockSpec((1, 1, 8, LANES), lambda b, t: (b, t, 0, 0)),
        ],
        out_shape=(
            jax.ShapeDtypeStruct((batch, seq, D_MODEL), bf16),
            jax.ShapeDtypeStruct((batch, seq, D_MODEL), bf16),
            jax.ShapeDtypeStruct((batch, seq, ATTN_WIDTH), bf16),
            jax.ShapeDtypeStruct((batch, ATTN_WIDTH, seq), bf16),
            jax.ShapeDtypeStruct((batch, n_blocks, N_HEADS, V_ROWS, BLK), bf16),
            jax.ShapeDtypeStruct((batch, N_HEADS, seq, LANES), bf16),
            jax.ShapeDtypeStruct((batch, n_tiles, 8, LANES), f32),
        ),
        scratch_shapes=[pltpu.VMEM((N_META + TM, POOL_WIDTH), f32), pltpu.VMEM((8, LANES), f32)],
        compiler_params=pltpu.CompilerParams(dimension_semantics=("arbitrary", "arbitrary"),
                                             vmem_limit_bytes=VMEM_LIMIT),
        name="in_proj",
    )(x, g_in, w_main, w_qvT, w_f3, b_f3, e_mat, pw, ps, w_upp, u_meta, r0)

    r_blocks = r_tiles[:, :, :NB, :N_HEADS].reshape(batch * n_blocks * N_HEADS)

    o = pl.pallas_call(
        functools.partial(_attn_kernel, n_blocks=n_blocks),
        grid=(batch, n_blocks),
        in_specs=[
            pl.BlockSpec(memory_space=pltpu.SMEM),
            pl.BlockSpec((1, ATTN_WIDTH, BLK), lambda b, i: (b, 0, i)),
            pl.BlockSpec((1, N_HEADS, seq, LANES), lambda b, i: (b, 0, 0, 0)),
            pl.BlockSpec((1, n_blocks, N_HEADS, V_ROWS, BLK), lambda b, i: (b, 0, 0, 0, 0)),
            _const((N_HEADS, N_META, LANES)),
            _const((N_HEADS, V_ROWS, N_META)),
        ],
        out_specs=pl.BlockSpec((1, BLK, ATTN_WIDTH), lambda b, i: (b, i, 0)),
        out_shape=jax.ShapeDtypeStruct((batch, seq, ATTN_WIDTH), bf16),
        scratch_shapes=[pltpu.VMEM((N_HEADS, LANES, BLK), bf16),
                        pltpu.VMEM((N_HEADS, BLK, BLK), f32),
                        pltpu.VMEM((N_HEADS, BLK, BLK), bf16),
                        pltpu.VMEM((N_HEADS, N_META, BLK), f32),
                        pltpu.VMEM((N_HEADS, N_META, BLK), bf16),
                        pltpu.VMEM((N_HEADS, 1, BLK), f32),
                        pltpu.VMEM((N_HEADS, 1, BLK), f32),
                        pltpu.VMEM((N_HEADS, V_ROWS, BLK), f32)],
        compiler_params=pltpu.CompilerParams(dimension_semantics=("arbitrary", "arbitrary"),
                                             vmem_limit_bytes=VMEM_LIMIT),
        name="attention",
    )(r_blocks, qT, kaug, vT, k_meta, v_meta)

    return pl.pallas_call(
        _out_kernel,
        grid=(batch, seq // TM_OUT),
        in_specs=[
            pl.BlockSpec((1, TM_OUT, D_MODEL), lambda b, t: (b, t, 0)),
            pl.BlockSpec((1, TM_OUT, ATTN_WIDTH), lambda b, t: (b, t, 0)),
            pl.BlockSpec((1, TM_OUT, ATTN_WIDTH), lambda b, t: (b, t, 0)),
            pl.BlockSpec((1, TM_OUT, D_MODEL), lambda b, t: (b, t, 0)),
            pl.BlockSpec((1, TM_OUT, D_MODEL), lambda b, t: (b, t, 0)),
            _const((ATTN_WIDTH, D_MODEL)),
            _const((D_MODEL, D_MODEL)),
            _const((1, D_MODEL)),
        ],
        out_specs=pl.BlockSpec((1, TM_OUT, D_MODEL), lambda b, t: (b, t, 0)),
        out_shape=jax.ShapeDtypeStruct((batch, seq, D_MODEL), f32),
        compiler_params=pltpu.CompilerParams(dimension_semantics=("arbitrary", "arbitrary"),
                                             vmem_limit_bytes=VMEM_LIMIT),
        name="out_proj",
    )(x, o, sz, ga, mp, w_upa, w_o, g_out)
```

```python
import functools

import numpy as np
import jax
import jax.numpy as jnp
from jax import lax
from jax.experimental import pallas as pl
from jax.experimental.pallas import tpu as pltpu

D_MODEL = 1024
N_META = 16
POOL_WIDTH = 512
POOL_WINDOWS = (2, 4, 8, 16)
POOL_GROUP = POOL_WIDTH // len(POOL_WINDOWS)
N_HEADS = 8
HEAD_DIM = 64
ATTN_WIDTH = N_HEADS * HEAD_DIM
RMS_EPS = 1e-6

LANES = 128
BLK = 256
BQ = 512
QB = BQ // BLK
assert QB == 2
TM = 512
NB = TM // BLK
TM_OUT = 512
V_ROWS = HEAD_DIM + 16
N_SPLIT = 3
MASKED = -1e30
LOG2E = 1.4426950408889634
VMEM_LIMIT = 56 * 1024 * 1024

f32 = jnp.float32
bf16 = jnp.bfloat16


def _rmsnorm(x, g):
    return x * lax.rsqrt(jnp.mean(x * x, axis=-1, keepdims=True) + RMS_EPS) * g


def _sigmoid(x):
    return 1.0 / (1.0 + jnp.exp(-x))


def _log_sigmoid(x):
    return jnp.minimum(x, 0.0) - jnp.log1p(jnp.exp(-jnp.abs(x)))


def _dot(a, b):
    return jnp.dot(a, b, preferred_element_type=f32)


def _dot_nt(a, b):
    return lax.dot_general(a, b, (((1,), (1,)), ((), ())), preferred_element_type=f32)


def _lane_iota(shape):
    return lax.broadcasted_iota(jnp.int32, shape, 1)


def _decay_parts(logf3, rows):
    r = lax.broadcasted_iota(jnp.int32, (rows, rows), 0)
    c = lax.broadcasted_iota(jnp.int32, (rows, rows), 1)
    tri = (c <= r).astype(f32)
    beta = jnp.dot(tri, logf3, precision=lax.Precision.HIGHEST, preferred_element_type=f32)
    nb = beta * (-LOG2E)
    hi = nb.astype(bf16).astype(f32)
    mid = (nb - hi).astype(bf16).astype(f32)
    lo = (nb - hi - mid).astype(bf16).astype(f32)
    lane = _lane_iota(nb.shape)
    parts = jnp.where(lane < N_HEADS, hi, jnp.where(lane < 2 * N_HEADS, mid, lo)).astype(bf16)
    return beta, parts


def _augmented_keys(kproj, parts, e_ref, store):
    aug = _dot(parts, e_ref[...])
    lane = _lane_iota((kproj.shape[0], LANES))
    for h in range(N_HEADS):
        slab = kproj[:, LANES * (h // 2):LANES * (h // 2 + 1)]
        if h % 2:
            slab = pltpu.roll(slab, HEAD_DIM, axis=1)
        store(h, jnp.where(lane < HEAD_DIM, slab, aug[:, LANES * h:LANES * (h + 1)]).astype(bf16))


def _meta_kernel(meta_ref, g_ref, wuk_ref, wvt_ref, wf_ref, bf_ref, e_ref,
                 umeta_ref, kmeta_ref, vmeta_ref, r0_ref):
    hn = _rmsnorm(meta_ref[...], g_ref[...]).astype(bf16)
    uk = _dot(hn, wuk_ref[...])
    umeta_ref[...] = uk[:, :POOL_WIDTH]
    vt = _dot_nt(wvt_ref[...], hn)
    for h in range(N_HEADS):
        vmeta_ref[h, 0:HEAD_DIM, :] = vt[HEAD_DIM * h:HEAD_DIM * (h + 1), :].astype(bf16)
        vmeta_ref[h, HEAD_DIM:V_ROWS, :] = jnp.ones((V_ROWS - HEAD_DIM, N_META), bf16)
    logf3 = _log_sigmoid(_dot(hn, wf_ref[...]) + bf_ref[...])
    beta, parts = _decay_parts(logf3, N_META)

    def store(h, ka):
        kmeta_ref[h] = ka

    _augmented_keys(uk[:, POOL_WIDTH:], parts, e_ref, store)
    r0_ref[...] = jnp.broadcast_to(beta[N_META - 1:N_META, :], r0_ref.shape)


def _proj_kernel(x_ref, g_ref, wmain_ref, wqv_ref, wf_ref, bf_ref, e_ref, pw_ref, ps_ref, wup_ref,
                 umeta_ref, r0_ref,
                 mp_ref, ga_ref, sz_ref, qT_ref, vT_ref, kaug_ref, r_ref,
                 uext_ref, rcarry_ref):
    t = pl.program_id(1)

    @pl.when(t == 0)
    def _():
        uext_ref[0:N_META, :] = umeta_ref[...]
        rcarry_ref[...] = r0_ref[...]

    hn = _rmsnorm(x_ref[0], g_ref[...]).astype(bf16)

    qv = _dot_nt(wqv_ref[...], hn)
    qT_ref[0] = (qv[0:ATTN_WIDTH] * (HEAD_DIM ** -0.5 * LOG2E)).astype(bf16)
    for c in range(NB):
        for h in range(N_HEADS):
            rows = slice(ATTN_WIDTH + HEAD_DIM * h, ATTN_WIDTH + HEAD_DIM * (h + 1))
            vT_ref[0, c, h, 0:HEAD_DIM, :] = qv[rows, BLK * c:BLK * (c + 1)].astype(bf16)
            vT_ref[0, c, h, HEAD_DIM:V_ROWS, :] = jnp.ones((V_ROWS - HEAD_DIM, BLK), bf16)

    uz = _dot(hn, wmain_ref[:, 0:2 * POOL_WIDTH])
    u = uz[:, :POOL_WIDTH]
    uext_ref[N_META:N_META + TM, :] = u
    ys = []
    for g, w in enumerate(POOL_WINDOWS):
        cols = slice(POOL_GROUP * g, POOL_GROUP * (g + 1))
        acc = u[:, cols]
        for j in range(1, w):
            acc = acc + uext_ref[N_META - j:N_META - j + TM, cols]
        pooled = acc * (1.0 / w) - u[:, cols]
        pg = _dot(pooled.astype(bf16), pw_ref[g])
        zp = uz[:, POOL_WIDTH + POOL_GROUP * g:POOL_WIDTH + POOL_GROUP * (g + 1)]
        ys.append(pg * ps_ref[:, cols] * (zp * _sigmoid(zp)))
    uext_ref[0:N_META, :] = uext_ref[TM:TM + N_META, :]
    y_pool = jnp.concatenate(ys, axis=1).astype(bf16)
    gp = _dot(hn, wmain_ref[:, 2048:3072])
    mp_ref[0] = (_sigmoid(gp) * _dot(y_pool, wup_ref[...])).astype(bf16)

    gat = _dot(hn, wmain_ref[:, 3072:4096])
    ga_ref[0] = _sigmoid(gat).astype(bf16)

    kz = _dot(hn, wmain_ref[:, 1024:2048])
    za = kz[:, ATTN_WIDTH:]
    sz_ref[0] = (za * _sigmoid(za)).astype(bf16)

    logf3 = _log_sigmoid(_dot(hn, wf_ref[...]) + bf_ref[...])
    r_ref[...] = jnp.zeros(r_ref.shape, f32)
    for c in range(NB):
        rows = slice(BLK * c, BLK * (c + 1))
        beta, parts = _decay_parts(logf3[rows], BLK)
        r_ref[0, 0, c:c + 1, :] = rcarry_ref[0:1, :] * LOG2E
        rcarry_ref[...] = rcarry_ref[...] + beta[BLK - 1:BLK, :]

        def store(h, ka, rows=rows):
            kaug_ref[0, h, rows, :] = ka

        _augmented_keys(kz[rows, :ATTN_WIDTH], parts, e_ref, store)


def _attn_kernel(r_ref, qT_ref, kaug_ref, vT_ref, kmeta_ref, vmeta_ref, o_ref,
                 qaug_ref, s_ref, p_ref, sm_ref, pm_ref, m_ref, mblk_ref, mmeta_ref, alpha_ref, acc_ref,
                 *, n_blocks):
    b = pl.program_id(0)
    t = pl.program_id(1)
    first = QB * t
    heads = range(N_HEADS)

    ones_rows = (lax.broadcasted_iota(jnp.int32, (LANES - HEAD_DIM, BQ), 0) < N_SPLIT).astype(bf16)
    for h in heads:
        qaug_ref[h, 0:HEAD_DIM, :] = qT_ref[0, HEAD_DIM * h:HEAD_DIM * (h + 1), :]
        qaug_ref[h, HEAD_DIM:LANES, :] = ones_rows

    def r_at(j, h):
        return r_ref[(b * n_blocks + j) * N_HEADS + h]

    def col_max(s):
        return jnp.max(s, axis=0, keepdims=True)

    def scores(j, h):
        return _dot(kaug_ref[0, h, pl.ds(pl.multiple_of(j * BLK, BLK), BLK), :], qaug_ref[h])

    row = lax.broadcasted_iota(jnp.int32, (BLK, BQ), 0)
    col = lax.broadcasted_iota(jnp.int32, (BLK, BQ), 1)

    def score_pass(j, slot, diag=None):
        for h in heads:
            s = scores(j, h)
            if diag is not None:
                s = jnp.where(row + BLK * diag <= col, s, MASKED)
            s_ref[slot, h] = s
            mblk_ref[slot, h] = col_max(s)

    def meta_pass():
        for h in heads:
            sm = _dot(kmeta_ref[h], qaug_ref[h])
            sm_ref[h] = sm
            mmeta_ref[h] = col_max(sm)

    def finish(j, slot, with_meta=False):
        for h in heads:
            ref = r_at(first, h)
            off = ref - r_at(j, h)
            m_old = m_ref[h]
            m_blk = mblk_ref[slot, h] + off
            if with_meta:
                m_blk = jnp.maximum(m_blk, mmeta_ref[h] + ref)
            m_new = jnp.maximum(m_old, m_blk)
            alpha_ref[h] = jnp.exp2(m_old - m_new)
            m_ref[h] = m_new
            p_ref[h] = jnp.exp2(s_ref[slot, h] - (m_new - off)).astype(bf16)
            if with_meta:
                pm_ref[h] = jnp.exp2(sm_ref[h] - (m_new - ref)).astype(bf16)
        for h in heads:
            pv = _dot(vT_ref[0, j, h], p_ref[h])
            if with_meta:
                pv = pv + _dot(vmeta_ref[h], pm_ref[h])
            acc_ref[h] = acc_ref[h] * alpha_ref[h] + pv

    m_ref[...] = jnp.full(m_ref.shape, MASKED, f32)
    acc_ref[...] = jnp.zeros(acc_ref.shape, f32)

    @pl.when(first == 0)
    def _():
        score_pass(first, 0, diag=0)

    @pl.when(first > 0)
    def _():
        score_pass(0, 0)

        def body(k, carry):
            score_pass(2 * k + 1, 1)
            finish(2 * k, 0)
            score_pass(2 * k + 2, 0)
            finish(2 * k + 1, 1)
            return carry

        lax.fori_loop(0, t - 1, body, 0)
        score_pass(first - 1, 1)
        finish(first - 2, 0)
        score_pass(first, 0, diag=0)
        finish(first - 1, 1)

    score_pass(first + 1, 1, diag=1)
    meta_pass()
    finish(first, 0)
    finish(first + 1, 1, with_meta=True)

    for pair in range(N_HEADS // 2):
        halves = []
        for h in (2 * pair, 2 * pair + 1):
            a = acc_ref[h]
            halves.append(a[0:HEAD_DIM] * (1.0 / a[HEAD_DIM:HEAD_DIM + 1]))
        o_ref[0, :, LANES * pair:LANES * (pair + 1)] = jnp.concatenate(halves, axis=0).T.astype(bf16)


def _out_kernel(x_ref, o_ref, sz_ref, ga_ref, mp_ref, wua_ref, wout_ref, g_ref, out_ref):
    y_attn = (o_ref[0].astype(f32) * sz_ref[0].astype(f32)).astype(bf16)
    merged = mp_ref[0].astype(f32) + ga_ref[0].astype(f32) * _dot(y_attn, wua_ref[...])
    h_out = x_ref[0] + _dot(merged.astype(bf16), wout_ref[...])
    out_ref[0] = _rmsnorm(h_out, g_ref[...])


def _placement_matrix():
    e = np.zeros((LANES, N_HEADS * LANES), np.float32)
    for part in range(N_SPLIT):
        for h in range(N_HEADS):
            e[part * N_HEADS + h, LANES * h + HEAD_DIM + part] = 1.0
    return jnp.asarray(e, bf16)


def _const(shape):
    return pl.BlockSpec(shape, lambda *_: (0,) * len(shape), pipeline_mode=pl.Buffered(1))


def kernel(x, meta_tokens, norm_g, w_in, b_forget, pool_w, pool_scale, w_up_pool, w_up_attn, w_out, final_norm_g):
    batch, seq, _ = x.shape
    n_tiles = seq // TM
    n_blocks = seq // BLK

    w = w_in[0]
    sizes = (POOL_WIDTH, POOL_WIDTH, ATTN_WIDTH, ATTN_WIDTH, ATTN_WIDTH, ATTN_WIDTH, N_HEADS, D_MODEL, D_MODEL)
    w_u, w_zp, w_q, w_k, w_v, w_za, w_f, w_gp, w_ga = jnp.split(w, np.cumsum(sizes)[:-1].tolist(), axis=1)
    w_main = jnp.concatenate([w_u, w_zp, w_k, w_za, w_gp, w_ga], axis=1).astype(bf16)
    w_qvT = jnp.concatenate([w_q, w_v], axis=1).T.astype(bf16)
    w_f3 = jnp.pad(jnp.tile(w_f, (1, N_SPLIT)), ((0, 0), (0, LANES - N_SPLIT * N_HEADS))).astype(bf16)
    b_f3 = jnp.pad(jnp.tile(b_forget[0], N_SPLIT), (0, LANES - N_SPLIT * N_HEADS)).reshape(1, LANES)
    w_uk = jnp.concatenate([w_u, w_k], axis=1).astype(bf16)
    w_vT = w_v.T.astype(bf16)
    e_mat = _placement_matrix()
    g_in = norm_g[0].reshape(1, D_MODEL)
    g_out = final_norm_g.reshape(1, D_MODEL)
    pw = pool_w[0].astype(bf16)
    ps = pool_scale[0].reshape(1, POOL_WIDTH)
    w_upp = w_up_pool[0].astype(bf16)
    w_upa = w_up_attn[0].astype(bf16)
    w_o = w_out[0].astype(bf16)

    u_meta, k_meta, v_meta, r0 = pl.pallas_call(
        _meta_kernel,
        out_shape=(jax.ShapeDtypeStruct((N_META, POOL_WIDTH), f32),
                   jax.ShapeDtypeStruct((N_HEADS, N_META, LANES), bf16),
                   jax.ShapeDtypeStruct((N_HEADS, V_ROWS, N_META), bf16),
                   jax.ShapeDtypeStruct((8, LANES), f32)),
        compiler_params=pltpu.CompilerParams(vmem_limit_bytes=VMEM_LIMIT),
        name="meta_proj",
    )(meta_tokens, g_in, w_uk, w_vT, w_f3, b_f3, e_mat)

    mp, ga, sz, qT, vT, kaug, r_tiles = pl.pallas_call(
        _proj_kernel,
        grid=(batch, n_tiles),
        in_specs=[
            pl.BlockSpec((1, TM, D_MODEL), lambda b, t: (b, t, 0)),
            _const((1, D_MODEL)),
            _const((D_MODEL, 4096)),
            _const((2 * ATTN_WIDTH, D_MODEL)),
            _const((D_MODEL, LANES)),
            _const((1, LANES)),
            _const((LANES, N_HEADS * LANES)),
            _const((len(POOL_WINDOWS), POOL_GROUP, POOL_GROUP)),
            _const((1, POOL_WIDTH)),
            _const((POOL_WIDTH, D_MODEL)),
            _const((N_META, POOL_WIDTH)),
            _const((8, LANES)),
        ],
        out_specs=[
            pl.BlockSpec((1, TM, D_MODEL), lambda b, t: (b, t, 0)),
            pl.BlockSpec((1, TM, D_MODEL), lambda b, t: (b, t, 0)),
            pl.BlockSpec((1, TM, ATTN_WIDTH), lambda b, t: (b, t, 0)),
            pl.BlockSpec((1, ATTN_WIDTH, TM), lambda b, t: (b, 0, t)),
            pl.BlockSpec((1, NB, N_HEADS, V_ROWS, BLK), lambda b, t: (b, t, 0, 0, 0)),
            pl.BlockSpec((1, N_HEADS, TM, LANES), lambda b, t: (b, 0, t, 0)),
            pl.BlockSpec((1, 1, 8, LANES), lambda b, t: (b, t, 0, 0)),
        ],
        out_shape=(
            jax.ShapeDtypeStruct((batch, seq, D_MODEL), bf16),
            jax.ShapeDtypeStruct((batch, seq, D_MODEL), bf16),
            jax.ShapeDtypeStruct((batch, seq, ATTN_WIDTH), bf16),
            jax.ShapeDtypeStruct((batch, ATTN_WIDTH, seq), bf16),
            jax.ShapeDtypeStruct((batch, n_blocks, N_HEADS, V_ROWS, BLK), bf16),
            jax.ShapeDtypeStruct((batch, N_HEADS, seq, LANES), bf16),
            jax.ShapeDtypeStruct((batch, n_tiles, 8, LANES), f32),
        ),
        scratch_shapes=[pltpu.VMEM((N_META + TM, POOL_WIDTH), f32), pltpu.VMEM((8, LANES), f32)],
        compiler_params=pltpu.CompilerParams(dimension_semantics=("arbitrary", "arbitrary"),
                                             vmem_limit_bytes=VMEM_LIMIT),
        name="in_proj",
    )(x, g_in, w_main, w_qvT, w_f3, b_f3, e_mat, pw, ps, w_upp, u_meta, r0)

    r_blocks = r_tiles[:, :, :NB, :N_HEADS].reshape(batch * n_blocks * N_HEADS)

    o = pl.pallas_call(
        functools.partial(_attn_kernel, n_blocks=n_blocks),
        grid=(batch, seq // BQ),
        in_specs=[
            pl.BlockSpec(memory_space=pltpu.SMEM),
            pl.BlockSpec((1, ATTN_WIDTH, BQ), lambda b, t: (b, 0, t)),
            pl.BlockSpec((1, N_HEADS, seq, LANES), lambda b, t: (b, 0, 0, 0)),
            pl.BlockSpec((1, n_blocks, N_HEADS, V_ROWS, BLK), lambda b, t: (b, 0, 0, 0, 0)),
            _const((N_HEADS, N_META, LANES)),
            _const((N_HEADS, V_ROWS, N_META)),
        ],
        out_specs=pl.BlockSpec((1, BQ, ATTN_WIDTH), lambda b, t: (b, t, 0)),
        out_shape=jax.ShapeDtypeStruct((batch, seq, ATTN_WIDTH), bf16),
        scratch_shapes=[pltpu.VMEM((N_HEADS, LANES, BQ), bf16),
                        pltpu.VMEM((2, N_HEADS, BLK, BQ), f32),
                        pltpu.VMEM((N_HEADS, BLK, BQ), bf16),
                        pltpu.VMEM((N_HEADS, N_META, BQ), f32),
                        pltpu.VMEM((N_HEADS, N_META, BQ), bf16),
                        pltpu.VMEM((N_HEADS, 1, BQ), f32),
                        pltpu.VMEM((2, N_HEADS, 1, BQ), f32),
                        pltpu.VMEM((N_HEADS, 1, BQ), f32),
                        pltpu.VMEM((N_HEADS, 1, BQ), f32),
                        pltpu.VMEM((N_HEADS, V_ROWS, BQ), f32)],
        compiler_params=pltpu.CompilerParams(dimension_semantics=("arbitrary", "arbitrary"),
                                             vmem_limit_bytes=VMEM_LIMIT),
        name="attention",
    )(r_blocks, qT, kaug, vT, k_meta, v_meta)

    return pl.pallas_call(
        _out_kernel,
        grid=(batch, seq // TM_OUT),
        in_specs=[
            pl.BlockSpec((1, TM_OUT, D_MODEL), lambda b, t: (b, t, 0)),
            pl.BlockSpec((1, TM_OUT, ATTN_WIDTH), lambda b, t: (b, t, 0)),
            pl.BlockSpec((1, TM_OUT, ATTN_WIDTH), lambda b, t: (b, t, 0)),
            pl.BlockSpec((1, TM_OUT, D_MODEL), lambda b, t: (b, t, 0)),
            pl.BlockSpec((1, TM_OUT, D_MODEL), lambda b, t: (b, t, 0)),
            _const((ATTN_WIDTH, D_MODEL)),
            _const((D_MODEL, D_MODEL)),
            _const((1, D_MODEL)),
        ],
        out_specs=pl.BlockSpec((1, TM_OUT, D_MODEL), lambda b, t: (b, t, 0)),
        out_shape=jax.ShapeDtypeStruct((batch, seq, D_MODEL), f32),
        compiler_params=pltpu.CompilerParams(dimension_semantics=("arbitrary", "arbitrary"),
                                             vmem_limit_bytes=VMEM_LIMIT),
        name="out_proj",
    )(x, o, sz, ga, mp, w_upa, w_o, g_out)
```

```python
import functools

import numpy as np
import jax
import jax.numpy as jnp
from jax import lax
from jax.experimental import pallas as pl
from jax.experimental.pallas import tpu as pltpu

D_MODEL = 1024
N_META = 16
POOL_WIDTH = 512
POOL_WINDOWS = (2, 4, 8, 16)
POOL_GROUP = POOL_WIDTH // len(POOL_WINDOWS)
N_HEADS = 8
HEAD_DIM = 64
ATTN_WIDTH = N_HEADS * HEAD_DIM
RMS_EPS = 1e-6

LANES = 128
BLK = 256
BQ = 512
QB = BQ // BLK
assert QB == 2
TM = 512
NB = TM // BLK
TM_OUT = 512
V_ROWS = HEAD_DIM + 16
N_SPLIT = 3
MASKED = -1e30
SKEW = 2
LOG2E = 1.4426950408889634
VMEM_LIMIT = 56 * 1024 * 1024

f32 = jnp.float32
bf16 = jnp.bfloat16


def _rmsnorm(x, g):
    return x * lax.rsqrt(jnp.mean(x * x, axis=-1, keepdims=True) + RMS_EPS) * g


def _sigmoid(x):
    return 1.0 / (1.0 + jnp.exp(-x))


def _log_sigmoid(x):
    return jnp.minimum(x, 0.0) - jnp.log1p(jnp.exp(-jnp.abs(x)))


def _dot(a, b):
    return jnp.dot(a, b, preferred_element_type=f32)


def _dot_nt(a, b):
    return lax.dot_general(a, b, (((1,), (1,)), ((), ())), preferred_element_type=f32)


def _lane_iota(shape):
    return lax.broadcasted_iota(jnp.int32, shape, 1)


def _decay_parts(logf3, rows):
    r = lax.broadcasted_iota(jnp.int32, (rows, rows), 0)
    c = lax.broadcasted_iota(jnp.int32, (rows, rows), 1)
    tri = (c <= r).astype(f32)
    beta = jnp.dot(tri, logf3, precision=lax.Precision.HIGHEST, preferred_element_type=f32)
    nb = beta * (-LOG2E)
    hi = nb.astype(bf16).astype(f32)
    mid = (nb - hi).astype(bf16).astype(f32)
    lo = (nb - hi - mid).astype(bf16).astype(f32)
    lane = _lane_iota(nb.shape)
    parts = jnp.where(lane < N_HEADS, hi, jnp.where(lane < 2 * N_HEADS, mid, lo)).astype(bf16)
    return beta, parts


def _augmented_keys(kproj, parts, e_ref, store):
    aug = _dot(parts, e_ref[...])
    lane = _lane_iota((kproj.shape[0], LANES))
    for h in range(N_HEADS):
        slab = kproj[:, LANES * (h // 2):LANES * (h // 2 + 1)]
        if h % 2:
            slab = pltpu.roll(slab, HEAD_DIM, axis=1)
        store(h, jnp.where(lane < HEAD_DIM, slab, aug[:, LANES * h:LANES * (h + 1)]).astype(bf16))


def _meta_kernel(meta_ref, g_ref, wuk_ref, wvt_ref, wf_ref, bf_ref, e_ref,
                 umeta_ref, kmeta_ref, vmeta_ref, r0_ref):
    hn = _rmsnorm(meta_ref[...], g_ref[...]).astype(bf16)
    uk = _dot(hn, wuk_ref[...])
    umeta_ref[...] = uk[:, :POOL_WIDTH]
    vt = _dot_nt(wvt_ref[...], hn)
    for h in range(N_HEADS):
        vmeta_ref[h, 0:HEAD_DIM, :] = vt[HEAD_DIM * h:HEAD_DIM * (h + 1), :].astype(bf16)
        vmeta_ref[h, HEAD_DIM:V_ROWS, :] = jnp.ones((V_ROWS - HEAD_DIM, N_META), bf16)
    logf3 = _log_sigmoid(_dot(hn, wf_ref[...]) + bf_ref[...])
    beta, parts = _decay_parts(logf3, N_META)

    def store(h, ka):
        kmeta_ref[h] = ka

    _augmented_keys(uk[:, POOL_WIDTH:], parts, e_ref, store)
    r0_ref[...] = jnp.broadcast_to(beta[N_META - 1:N_META, :], r0_ref.shape)


def _proj_kernel(x_ref, g_ref, wmain_ref, wqv_ref, wf_ref, bf_ref, e_ref, pw_ref, ps_ref, wup_ref,
                 umeta_ref, r0_ref,
                 mp_ref, ga_ref, sz_ref, qT_ref, vT_ref, kaug_ref, r_ref,
                 uext_ref, rcarry_ref):
    t = pl.program_id(1)

    @pl.when(t == 0)
    def _():
        uext_ref[0:N_META, :] = umeta_ref[...]
        rcarry_ref[...] = r0_ref[...]

    hn = _rmsnorm(x_ref[0], g_ref[...]).astype(bf16)

    qv = _dot_nt(wqv_ref[...], hn)
    qT_ref[0] = (qv[0:ATTN_WIDTH] * (HEAD_DIM ** -0.5 * LOG2E)).astype(bf16)
    for c in range(NB):
        for h in range(N_HEADS):
            rows = slice(ATTN_WIDTH + HEAD_DIM * h, ATTN_WIDTH + HEAD_DIM * (h + 1))
            vT_ref[0, c, h, 0:HEAD_DIM, :] = qv[rows, BLK * c:BLK * (c + 1)].astype(bf16)
            vT_ref[0, c, h, HEAD_DIM:V_ROWS, :] = jnp.ones((V_ROWS - HEAD_DIM, BLK), bf16)

    uz = _dot(hn, wmain_ref[:, 0:2 * POOL_WIDTH])
    u = uz[:, :POOL_WIDTH]
    uext_ref[N_META:N_META + TM, :] = u
    ys = []
    for g, w in enumerate(POOL_WINDOWS):
        cols = slice(POOL_GROUP * g, POOL_GROUP * (g + 1))
        acc = u[:, cols]
        for j in range(1, w):
            acc = acc + uext_ref[N_META - j:N_META - j + TM, cols]
        pooled = acc * (1.0 / w) - u[:, cols]
        pg = _dot(pooled.astype(bf16), pw_ref[g])
        zp = uz[:, POOL_WIDTH + POOL_GROUP * g:POOL_WIDTH + POOL_GROUP * (g + 1)]
        ys.append(pg * ps_ref[:, cols] * (zp * _sigmoid(zp)))
    uext_ref[0:N_META, :] = uext_ref[TM:TM + N_META, :]
    y_pool = jnp.concatenate(ys, axis=1).astype(bf16)
    gp = _dot(hn, wmain_ref[:, 2048:3072])
    mp_ref[0] = (_sigmoid(gp) * _dot(y_pool, wup_ref[...])).astype(bf16)

    gat = _dot(hn, wmain_ref[:, 3072:4096])
    ga_ref[0] = _sigmoid(gat).astype(bf16)

    kz = _dot(hn, wmain_ref[:, 1024:2048])
    za = kz[:, ATTN_WIDTH:]
    sz_ref[0] = (za * _sigmoid(za)).astype(bf16)

    logf3 = _log_sigmoid(_dot(hn, wf_ref[...]) + bf_ref[...])
    r_ref[...] = jnp.zeros(r_ref.shape, f32)
    for c in range(NB):
        rows = slice(BLK * c, BLK * (c + 1))
        beta, parts = _decay_parts(logf3[rows], BLK)
        r_ref[0, 0, c:c + 1, :] = rcarry_ref[0:1, :] * LOG2E
        rcarry_ref[...] = rcarry_ref[...] + beta[BLK - 1:BLK, :]

        def store(h, ka, rows=rows):
            kaug_ref[0, h, rows, :] = ka

        _augmented_keys(kz[rows, :ATTN_WIDTH], parts, e_ref, store)


def _attn_kernel(r_ref, qT_ref, kaug_ref, vT_ref, kmeta_ref, vmeta_ref, o_ref,
                 qaug_ref, s_ref, p_ref, sm_ref, pm_ref, m_ref, mblk_ref, mmeta_ref, alpha_ref, acc_ref,
                 *, n_blocks):
    b = pl.program_id(0)
    t = pl.program_id(1)
    first = QB * t
    heads = range(N_HEADS)

    ones_rows = (lax.broadcasted_iota(jnp.int32, (LANES - HEAD_DIM, BQ), 0) < N_SPLIT).astype(bf16)
    for h in heads:
        qaug_ref[h, 0:HEAD_DIM, :] = qT_ref[0, HEAD_DIM * h:HEAD_DIM * (h + 1), :]
        qaug_ref[h, HEAD_DIM:LANES, :] = ones_rows

    def r_at(j, h):
        return r_ref[(b * n_blocks + j) * N_HEADS + h]

    def col_max(s):
        return jnp.max(s, axis=0, keepdims=True)

    def scores(j, h):
        return _dot(kaug_ref[0, h, pl.ds(pl.multiple_of(j * BLK, BLK), BLK), :], qaug_ref[h])

    row = lax.broadcasted_iota(jnp.int32, (BLK, BQ), 0)
    col = lax.broadcasted_iota(jnp.int32, (BLK, BQ), 1)

    def score_item(j, slot, h, diag=None, meta=False):
        s = scores(j, h)
        if diag is not None:
            s = jnp.where(row + BLK * diag <= col, s, MASKED)
        s_ref[slot, h] = s
        mblk_ref[slot, h] = col_max(s)
        if meta:
            sm = _dot(kmeta_ref[h], qaug_ref[h])
            sm_ref[h] = sm
            mmeta_ref[h] = col_max(sm)

    def softmax_item(j, slot, h, meta=False):
        ref = r_at(first, h)
        off = ref - r_at(j, h)
        m_old = m_ref[h]
        m_blk = mblk_ref[slot, h] + off
        if meta:
            m_blk = jnp.maximum(m_blk, mmeta_ref[h] + ref)
        m_new = jnp.maximum(m_old, m_blk)
        alpha_ref[h] = jnp.exp2(m_old - m_new)
        m_ref[h] = m_new
        p_ref[h] = jnp.exp2(s_ref[slot, h] - (m_new - off)).astype(bf16)
        if meta:
            pm_ref[h] = jnp.exp2(sm_ref[h] - (m_new - ref)).astype(bf16)

    def value_item(j, slot, h, meta=False):
        pv = _dot(vT_ref[0, j, h], p_ref[h])
        if meta:
            pv = pv + _dot(vmeta_ref[h], pm_ref[h])
        acc_ref[h] = acc_ref[h] * alpha_ref[h] + pv

    def run(score_blocks, finish_blocks):
        score_items = [(blk + (h,), kw) for blk, kw in score_blocks for h in heads]
        finish_items = [(blk + (h,), kw) for blk, kw in finish_blocks for h in heads]
        for i in range(max(len(score_items), len(finish_items) + SKEW + 1)):
            if i < len(score_items):
                args, kw = score_items[i]
                score_item(*args, **kw)
            if 0 <= i - SKEW - 1 < len(finish_items):
                args, kw = finish_items[i - SKEW - 1]
                value_item(*args, **kw)
            if 0 <= i - SKEW < len(finish_items):
                args, kw = finish_items[i - SKEW]
                softmax_item(*args, **kw)

    m_ref[...] = jnp.full(m_ref.shape, MASKED, f32)
    acc_ref[...] = jnp.zeros(acc_ref.shape, f32)

    @pl.when(first == 0)
    def _():
        run([((first, 0), dict(diag=0))], [])

    @pl.when(first > 0)
    def _():
        run([((0, 0), {})], [])

        def body(k, carry):
            run([((2 * k + 1, 1), {}), ((2 * k + 2, 0), {})],
                [((2 * k, 0), {}), ((2 * k + 1, 1), {})])
            return carry

        lax.fori_loop(0, t - 1, body, 0)
        run([((first - 1, 1), {}), ((first, 0), dict(diag=0))],
            [((first - 2, 0), {}), ((first - 1, 1), {})])

    run([((first + 1, 1), dict(diag=1, meta=True))],
        [((first, 0), {}), ((first + 1, 1), dict(meta=True))])

    for pair in range(N_HEADS // 2):
        halves = []
        for h in (2 * pair, 2 * pair + 1):
            a = acc_ref[h]
            halves.append(a[0:HEAD_DIM] * (1.0 / a[HEAD_DIM:HEAD_DIM + 1]))
        o_ref[0, :, LANES * pair:LANES * (pair + 1)] = jnp.concatenate(halves, axis=0).T.astype(bf16)


def _out_kernel(x_ref, o_ref, sz_ref, ga_ref, mp_ref, wua_ref, wout_ref, g_ref, out_ref):
    y_attn = (o_ref[0].astype(f32) * sz_ref[0].astype(f32)).astype(bf16)
    merged = mp_ref[0].astype(f32) + ga_ref[0].astype(f32) * _dot(y_attn, wua_ref[...])
    h_out = x_ref[0] + _dot(merged.astype(bf16), wout_ref[...])
    out_ref[0] = _rmsnorm(h_out, g_ref[...])


def _placement_matrix():
    e = np.zeros((LANES, N_HEADS * LANES), np.float32)
    for part in range(N_SPLIT):
        for h in range(N_HEADS):
            e[part * N_HEADS + h, LANES * h + HEAD_DIM + part] = 1.0
    return jnp.asarray(e, bf16)


def _const(shape):
    return pl.BlockSpec(shape, lambda *_: (0,) * len(shape), pipeline_mode=pl.Buffered(1))


def kernel(x, meta_tokens, norm_g, w_in, b_forget, pool_w, pool_scale, w_up_pool, w_up_attn, w_out, final_norm_g):
    batch, seq, _ = x.shape
    n_tiles = seq // TM
    n_blocks = seq // BLK

    w = w_in[0]
    sizes = (POOL_WIDTH, POOL_WIDTH, ATTN_WIDTH, ATTN_WIDTH, ATTN_WIDTH, ATTN_WIDTH, N_HEADS, D_MODEL, D_MODEL)
    w_u, w_zp, w_q, w_k, w_v, w_za, w_f, w_gp, w_ga = jnp.split(w, np.cumsum(sizes)[:-1].tolist(), axis=1)
    w_main = jnp.concatenate([w_u, w_zp, w_k, w_za, w_gp, w_ga], axis=1).astype(bf16)
    w_qvT = jnp.concatenate([w_q, w_v], axis=1).T.astype(bf16)
    w_f3 = jnp.pad(jnp.tile(w_f, (1, N_SPLIT)), ((0, 0), (0, LANES - N_SPLIT * N_HEADS))).astype(bf16)
    b_f3 = jnp.pad(jnp.tile(b_forget[0], N_SPLIT), (0, LANES - N_SPLIT * N_HEADS)).reshape(1, LANES)
    w_uk = jnp.concatenate([w_u, w_k], axis=1).astype(bf16)
    w_vT = w_v.T.astype(bf16)
    e_mat = _placement_matrix()
    g_in = norm_g[0].reshape(1, D_MODEL)
    g_out = final_norm_g.reshape(1, D_MODEL)
    pw = pool_w[0].astype(bf16)
    ps = pool_scale[0].reshape(1, POOL_WIDTH)
    w_upp = w_up_pool[0].astype(bf16)
    w_upa = w_up_attn[0].astype(bf16)
    w_o = w_out[0].astype(bf16)

    u_meta, k_meta, v_meta, r0 = pl.pallas_call(
        _meta_kernel,
        out_shape=(jax.ShapeDtypeStruct((N_META, POOL_WIDTH), f32),
                   jax.ShapeDtypeStruct((N_HEADS, N_META, LANES), bf16),
                   jax.ShapeDtypeStruct((N_HEADS, V_ROWS, N_META), bf16),
                   jax.ShapeDtypeStruct((8, LANES), f32)),
        compiler_params=pltpu.CompilerParams(vmem_limit_bytes=VMEM_LIMIT),
        name="meta_proj",
    )(meta_tokens, g_in, w_uk, w_vT, w_f3, b_f3, e_mat)

    mp, ga, sz, qT, vT, kaug, r_tiles = pl.pallas_call(
        _proj_kernel,
        grid=(batch, n_tiles),
        in_specs=[
            pl.BlockSpec((1, TM, D_MODEL), lambda b, t: (b, t, 0)),
            _const((1, D_MODEL)),
            _const((D_MODEL, 4096)),
            _const((2 * ATTN_WIDTH, D_MODEL)),
            _const((D_MODEL, LANES)),
            _const((1, LANES)),
            _const((LANES, N_HEADS * LANES)),
            _const((len(POOL_WINDOWS), POOL_GROUP, POOL_GROUP)),
            _const((1, POOL_WIDTH)),
            _const((POOL_WIDTH, D_MODEL)),
            _const((N_META, POOL_WIDTH)),
            _const((8, LANES)),
        ],
        out_specs=[
            pl.BlockSpec((1, TM, D_MODEL), lambda b, t: (b, t, 0)),
            pl.BlockSpec((1, TM, D_MODEL), lambda b, t: (b, t, 0)),
            pl.BlockSpec((1, TM, ATTN_WIDTH), lambda b, t: (b, t, 0)),
            pl.BlockSpec((1, ATTN_WIDTH, TM), lambda b, t: (b, 0, t)),
            pl.BlockSpec((1, NB, N_HEADS, V_ROWS, BLK), lambda b, t: (b, t, 0, 0, 0)),
            pl.BlockSpec((1, N_HEADS, TM, LANES), lambda b, t: (b, 0, t, 0)),
            pl.BlockSpec((1, 1, 8, LANES), lambda b, t: (b, t, 0, 0)),
        ],
        out_shape=(
            jax.ShapeDtypeStruct((batch, seq, D_MODEL), bf16),
            jax.ShapeDtypeStruct((batch, seq, D_MODEL), bf16),
            jax.ShapeDtypeStruct((batch, seq, ATTN_WIDTH), bf16),
            jax.ShapeDtypeStruct((batch, ATTN_WIDTH, seq), bf16),
            jax.ShapeDtypeStruct((batch, n_blocks, N_HEADS, V_ROWS, BLK), bf16),
            jax.ShapeDtypeStruct((batch, N_HEADS, seq, LANES), bf16),
            jax.ShapeDtypeStruct((batch, n_tiles, 8, LANES), f32),
        ),
        scratch_shapes=[pltpu.VMEM((N_META + TM, POOL_WIDTH), f32), pltpu.VMEM((8, LANES), f32)],
        compiler_params=pltpu.CompilerParams(dimension_semantics=("arbitrary", "arbitrary"),
                                             vmem_limit_bytes=VMEM_LIMIT),
        name="in_proj",
    )(x, g_in, w_main, w_qvT, w_f3, b_f3, e_mat, pw, ps, w_upp, u_meta, r0)

    r_blocks = r_tiles[:, :, :NB, :N_HEADS].reshape(batch * n_blocks * N_HEADS)

    o = pl.pallas_call(
        functools.partial(_attn_kernel, n_blocks=n_blocks),
        grid=(batch, seq // BQ),
        in_specs=[
            pl.BlockSpec(memory_space=pltpu.SMEM),
            pl.BlockSpec((1, ATTN_WIDTH, BQ), lambda b, t: (b, 0, t)),
            pl.BlockSpec((1, N_HEADS, seq, LANES), lambda b, t: (b, 0, 0, 0)),
            pl.BlockSpec((1, n_blocks, N_HEADS, V_ROWS, BLK), lambda b, t: (b, 0, 0, 0, 0)),
            _const((N_HEADS, N_META, LANES)),
            _const((N_HEADS, V_ROWS, N_META)),
        ],
        out_specs=pl.BlockSpec((1, BQ, ATTN_WIDTH), lambda b, t: (b, t, 0)),
        out_shape=jax.ShapeDtypeStruct((batch, seq, ATTN_WIDTH), bf16),
        scratch_shapes=[pltpu.VMEM((N_HEADS, LANES, BQ), bf16),
                        pltpu.VMEM((2, N_HEADS, BLK, BQ), f32),
                        pltpu.VMEM((N_HEADS, BLK, BQ), bf16),
                        pltpu.VMEM((N_HEADS, N_META, BQ), f32),
                        pltpu.VMEM((N_HEADS, N_META, BQ), bf16),
                        pltpu.VMEM((N_HEADS, 1, BQ), f32),
                        pltpu.VMEM((2, N_HEADS, 1, BQ), f32),
                        pltpu.VMEM((N_HEADS, 1, BQ), f32),
                        pltpu.VMEM((N_HEADS, 1, BQ), f32),
                        pltpu.VMEM((N_HEADS, V_ROWS, BQ), f32)],
        compiler_params=pltpu.CompilerParams(dimension_semantics=("arbitrary", "arbitrary"),
                                             vmem_limit_bytes=VMEM_LIMIT),
        name="attention",
    )(r_blocks, qT, kaug, vT, k_meta, v_meta)

    return pl.pallas_call(
        _out_kernel,
        grid=(batch, seq // TM_OUT),
        in_specs=[
            pl.BlockSpec((1, TM_OUT, D_MODEL), lambda b, t: (b, t, 0)),
            pl.BlockSpec((1, TM_OUT, ATTN_WIDTH), lambda b, t: (b, t, 0)),
            pl.BlockSpec((1, TM_OUT, ATTN_WIDTH), lambda b, t: (b, t, 0)),
            pl.BlockSpec((1, TM_OUT, D_MODEL), lambda b, t: (b, t, 0)),
            pl.BlockSpec((1, TM_OUT, D_MODEL), lambda b, t: (b, t, 0)),
            _const((ATTN_WIDTH, D_MODEL)),
            _const((D_MODEL, D_MODEL)),
            _const((1, D_MODEL)),
        ],
        out_specs=pl.BlockSpec((1, TM_OUT, D_MODEL), lambda b, t: (b, t, 0)),
        out_shape=jax.ShapeDtypeStruct((batch, seq, D_MODEL), f32),
        compiler_params=pltpu.CompilerParams(dimension_semantics=("arbitrary", "arbitrary"),
                                             vmem_limit_bytes=VMEM_LIMIT),
        name="out_proj",
    )(x, o, sz, ga, mp, w_upa, w_o, g_out)
```

```python
import functools

import numpy as np
import jax
import jax.numpy as jnp
from jax import lax
from jax.experimental import pallas as pl
from jax.experimental.pallas import tpu as pltpu

D_MODEL = 1024
N_META = 16
POOL_WIDTH = 512
POOL_WINDOWS = (2, 4, 8, 16)
POOL_GROUP = POOL_WIDTH // len(POOL_WINDOWS)
N_HEADS = 8
HEAD_DIM = 64
ATTN_WIDTH = N_HEADS * HEAD_DIM
RMS_EPS = 1e-6

LANES = 128
BLK = 256
BQ = 512
QB = BQ // BLK
assert QB == 2
TM = 512
NB = TM // BLK
TM_OUT = 512
POOL_PAD = 8
assert POOL_WINDOWS == (2, 4, 8, 16) and POOL_PAD >= POOL_WINDOWS[-1] // 2
V_ROWS = HEAD_DIM + 16
N_SPLIT = 3
MASKED = -1e30
SKEW = 2
LOG2E = 1.4426950408889634
VMEM_LIMIT = 56 * 1024 * 1024

f32 = jnp.float32
bf16 = jnp.bfloat16


def _rmsnorm(x, g):
    return x * lax.rsqrt(jnp.mean(x * x, axis=-1, keepdims=True) + RMS_EPS) * g


def _sigmoid(x):
    return 1.0 / (1.0 + jnp.exp(-x))


def _log_sigmoid(x):
    return jnp.minimum(x, 0.0) - jnp.log1p(jnp.exp(-jnp.abs(x)))


def _dot(a, b):
    return jnp.dot(a, b, preferred_element_type=f32)


def _dot_nt(a, b):
    return lax.dot_general(a, b, (((1,), (1,)), ((), ())), preferred_element_type=f32)


def _lane_iota(shape):
    return lax.broadcasted_iota(jnp.int32, shape, 1)


def _decay_parts(logf3, rows):
    n = logf3.shape[0] // rows
    r = lax.broadcasted_iota(jnp.int32, (rows, rows), 0)
    c = lax.broadcasted_iota(jnp.int32, (rows, rows), 1)
    tri = (c <= r).astype(f32)
    side_by_side = jnp.concatenate([logf3[rows * i:rows * (i + 1)] for i in range(n)], axis=1)
    sums = jnp.dot(tri, side_by_side, precision=lax.Precision.HIGHEST, preferred_element_type=f32)
    beta = jnp.concatenate([sums[:, LANES * i:LANES * (i + 1)] for i in range(n)], axis=0)
    nb = beta * (-LOG2E)
    hi = nb.astype(bf16).astype(f32)
    mid = (nb - hi).astype(bf16).astype(f32)
    lo = (nb - hi - mid).astype(bf16).astype(f32)
    lane = _lane_iota(nb.shape)
    return beta, jnp.where(lane < N_HEADS, hi, jnp.where(lane < 2 * N_HEADS, mid, lo))


def _augmented_keys(kproj, parts, store):
    lane = _lane_iota((kproj.shape[0], LANES))
    for h in range(N_HEADS):
        slab = kproj[:, LANES * (h // 2):LANES * (h // 2 + 1)]
        if h % 2:
            slab = pltpu.roll(slab, HEAD_DIM, axis=1)
        store(h, jnp.where(lane < HEAD_DIM, slab, pltpu.roll(parts, HEAD_DIM - h, axis=1)).astype(bf16))


def _meta_kernel(meta_ref, g_ref, wuk_ref, wvt_ref, wf_ref, bf_ref,
                 umeta_ref, kmeta_ref, vmeta_ref, r0_ref):
    hn = _rmsnorm(meta_ref[...], g_ref[...]).astype(bf16)
    uk = _dot(hn, wuk_ref[...])
    umeta_ref[...] = uk[:, :POOL_WIDTH]
    vt = _dot_nt(wvt_ref[...], hn)
    for h in range(N_HEADS):
        vmeta_ref[h, 0:HEAD_DIM, :] = vt[HEAD_DIM * h:HEAD_DIM * (h + 1), :].astype(bf16)
        vmeta_ref[h, HEAD_DIM:V_ROWS, :] = jnp.ones((V_ROWS - HEAD_DIM, N_META), bf16)
    logf3 = _log_sigmoid(_dot(hn, wf_ref[...]) + bf_ref[...])
    beta, parts = _decay_parts(logf3, N_META)

    def store(h, ka):
        kmeta_ref[h] = ka

    _augmented_keys(uk[:, POOL_WIDTH:], parts, store)
    r0_ref[...] = jnp.broadcast_to(beta[N_META - 1:N_META, :], r0_ref.shape)


def _proj_kernel(x_ref, g_ref, wmain_ref, wqv_ref, wf_ref, bf_ref, pw_ref, ps_ref, wup_ref,
                 umeta_ref, r0_ref,
                 mp_ref, ga_ref, sz_ref, qT_ref, vT_ref, kaug_ref, r_ref,
                 uext_ref, lvl_a_ref, lvl_b_ref, rcarry_ref):
    t = pl.program_id(1)
    ext = N_META + TM
    body = slice(POOL_PAD, POOL_PAD + ext)

    @pl.when(t == 0)
    def _():
        uext_ref[0:POOL_PAD, :] = jnp.zeros((POOL_PAD, POOL_WIDTH), f32)
        lvl_a_ref[0:POOL_PAD, :] = jnp.zeros((POOL_PAD, lvl_a_ref.shape[1]), f32)
        lvl_b_ref[0:POOL_PAD, :] = jnp.zeros((POOL_PAD, lvl_b_ref.shape[1]), f32)
        uext_ref[POOL_PAD:POOL_PAD + N_META, :] = umeta_ref[...]
        rcarry_ref[...] = r0_ref[...]

    hn = _rmsnorm(x_ref[0], g_ref[...]).astype(bf16)

    gat = _dot(hn, wmain_ref[:, 3072:4096])
    ga_ref[0] = _sigmoid(gat).astype(bf16)

    uz = _dot(hn, wmain_ref[:, 0:2 * POOL_WIDTH])
    u = uz[:, :POOL_WIDTH]
    uext_ref[POOL_PAD + N_META:POOL_PAD + ext, :] = u

    def shifted_sum(ref, shift, cols):
        return ref[body, cols] + ref[POOL_PAD - shift:POOL_PAD - shift + ext, cols]

    g1, g2, g3 = (slice(POOL_GROUP * g, POOL_GROUP * (g + 1)) for g in range(3))
    sums = [shifted_sum(uext_ref, 1, g1)]
    lvl_a_ref[body, :] = shifted_sum(uext_ref, 1, slice(POOL_GROUP, POOL_WIDTH))
    sums.append(shifted_sum(lvl_a_ref, 2, g1))
    lvl_b_ref[body, :] = shifted_sum(lvl_a_ref, 2, slice(POOL_GROUP, 3 * POOL_GROUP))
    sums.append(shifted_sum(lvl_b_ref, 4, g1))
    lvl_a_ref[body, g1] = shifted_sum(lvl_b_ref, 4, g2)
    sums.append(shifted_sum(lvl_a_ref, 8, g1))
    uext_ref[POOL_PAD:POOL_PAD + N_META, :] = uext_ref[POOL_PAD + TM:POOL_PAD + ext, :]

    ys = []
    for g, w in enumerate(POOL_WINDOWS):
        cols = slice(POOL_GROUP * g, POOL_GROUP * (g + 1))
        pooled = sums[g][N_META:] * (1.0 / w) - u[:, cols]
        pg = _dot(pooled.astype(bf16), pw_ref[g])
        zp = uz[:, POOL_WIDTH + POOL_GROUP * g:POOL_WIDTH + POOL_GROUP * (g + 1)]
        ys.append(pg * ps_ref[:, cols] * (zp * _sigmoid(zp)))
    y_pool = jnp.concatenate(ys, axis=1).astype(bf16)
    gp = _dot(hn, wmain_ref[:, 2048:3072])
    mp_ref[0] = (_sigmoid(gp) * _dot(y_pool, wup_ref[...])).astype(bf16)

    logf3 = _log_sigmoid(_dot(hn, wf_ref[...]) + bf_ref[...])
    r_ref[...] = jnp.zeros(r_ref.shape, f32)
    beta, parts = _decay_parts(logf3, BLK)
    for c in range(NB):
        r_ref[0, 0, c:c + 1, :] = rcarry_ref[0:1, :] * LOG2E
        rcarry_ref[...] = rcarry_ref[...] + beta[BLK * (c + 1) - 1:BLK * (c + 1), :]

    kz = _dot(hn, wmain_ref[:, 1024:2048])
    za = kz[:, ATTN_WIDTH:]
    sz_ref[0] = (za * _sigmoid(za)).astype(bf16)

    def store(h, ka):
        kaug_ref[0, h] = ka

    _augmented_keys(kz[:, :ATTN_WIDTH], parts, store)

    qv = _dot_nt(wqv_ref[...], hn)
    qT_ref[0] = (qv[0:ATTN_WIDTH] * (HEAD_DIM ** -0.5 * LOG2E)).astype(bf16)
    for c in range(NB):
        for h in range(N_HEADS):
            rows = slice(ATTN_WIDTH + HEAD_DIM * h, ATTN_WIDTH + HEAD_DIM * (h + 1))
            vT_ref[0, c, h, 0:HEAD_DIM, :] = qv[rows, BLK * c:BLK * (c + 1)].astype(bf16)
            vT_ref[0, c, h, HEAD_DIM:V_ROWS, :] = jnp.ones((V_ROWS - HEAD_DIM, BLK), bf16)


def _attn_kernel(r_ref, qT_ref, kaug_ref, vT_ref, kmeta_ref, vmeta_ref, o_ref,
                 qaug_ref, s_ref, p_ref, sm_ref, pm_ref, m_ref, mblk_ref, mmeta_ref, alpha_ref, acc_ref,
                 *, n_blocks):
    b = pl.program_id(0)
    t = pl.program_id(1)
    first = QB * t
    heads = range(N_HEADS)

    part_row = lax.broadcasted_iota(jnp.int32, (LANES - HEAD_DIM, BQ), 0)
    ones_rows = ((part_row % N_HEADS == 0) & (part_row < N_SPLIT * N_HEADS)).astype(bf16)
    for h in heads:
        qaug_ref[h, 0:HEAD_DIM, :] = qT_ref[0, HEAD_DIM * h:HEAD_DIM * (h + 1), :]
        qaug_ref[h, HEAD_DIM:LANES, :] = ones_rows

    def r_at(j, h):
        return r_ref[(b * n_blocks + j) * N_HEADS + h]

    def col_max(s):
        return jnp.max(s, axis=0, keepdims=True)

    def scores(j, h):
        return _dot(kaug_ref[0, h, pl.ds(pl.multiple_of(j * BLK, BLK), BLK), :], qaug_ref[h])

    row = lax.broadcasted_iota(jnp.int32, (BLK, BQ), 0)
    col = lax.broadcasted_iota(jnp.int32, (BLK, BQ), 1)

    def score_item(j, slot, h, diag=None, meta=False):
        s = scores(j, h)
        if diag is not None:
            s = jnp.where(row + BLK * diag <= col, s, MASKED)
        s_ref[slot, h] = s
        mblk_ref[slot, h] = col_max(s)
        if meta:
            sm = _dot(kmeta_ref[h], qaug_ref[h])
            sm_ref[h] = sm
            mmeta_ref[h] = col_max(sm)

    def softmax_item(j, slot, h, meta=False):
        ref = r_at(first, h)
        off = ref - r_at(j, h)
        m_old = m_ref[h]
        m_blk = mblk_ref[slot, h] + off
        if meta:
            m_blk = jnp.maximum(m_blk, mmeta_ref[h] + ref)
        m_new = jnp.maximum(m_old, m_blk)
        alpha_ref[h] = jnp.exp2(m_old - m_new)
        m_ref[h] = m_new
        p_ref[h] = jnp.exp2(s_ref[slot, h] - (m_new - off)).astype(bf16)
        if meta:
            pm_ref[h] = jnp.exp2(sm_ref[h] - (m_new - ref)).astype(bf16)

    def value_item(j, slot, h, meta=False):
        pv = _dot(vT_ref[0, j, h], p_ref[h])
        if meta:
            pv = pv + _dot(vmeta_ref[h], pm_ref[h])
        acc_ref[h] = acc_ref[h] * alpha_ref[h] + pv

    def run(score_blocks, finish_blocks):
        score_items = [(blk + (h,), kw) for blk, kw in score_blocks for h in heads]
        finish_items = [(blk + (h,), kw) for blk, kw in finish_blocks for h in heads]
        for i in range(max(len(score_items), len(finish_items) + SKEW + 1)):
            if i < len(score_items):
                args, kw = score_items[i]
                score_item(*args, **kw)
            if 0 <= i - SKEW - 1 < len(finish_items):
                args, kw = finish_items[i - SKEW - 1]
                value_item(*args, **kw)
            if 0 <= i - SKEW < len(finish_items):
                args, kw = finish_items[i - SKEW]
                softmax_item(*args, **kw)

    m_ref[...] = jnp.full(m_ref.shape, MASKED, f32)
    acc_ref[...] = jnp.zeros(acc_ref.shape, f32)

    @pl.when(first == 0)
    def _():
        run([((first, 0), dict(diag=0))], [])

    @pl.when(first > 0)
    def _():
        run([((0, 0), {})], [])

        def body(k, carry):
            run([((2 * k + 1, 1), {}), ((2 * k + 2, 0), {})],
                [((2 * k, 0), {}), ((2 * k + 1, 1), {})])
            return carry

        lax.fori_loop(0, t - 1, body, 0)
        run([((first - 1, 1), {}), ((first, 0), dict(diag=0))],
            [((first - 2, 0), {}), ((first - 1, 1), {})])

    run([((first + 1, 1), dict(diag=1, meta=True))],
        [((first, 0), {}), ((first + 1, 1), dict(meta=True))])

    for pair in range(N_HEADS // 2):
        halves = []
        for h in (2 * pair, 2 * pair + 1):
            a = acc_ref[h]
            halves.append(a[0:HEAD_DIM] * (1.0 / a[HEAD_DIM:HEAD_DIM + 1]))
        o_ref[0, :, LANES * pair:LANES * (pair + 1)] = jnp.concatenate(halves, axis=0).T.astype(bf16)


def _out_kernel(x_ref, o_ref, sz_ref, ga_ref, mp_ref, wua_ref, wout_ref, g_ref, out_ref):
    y_attn = (o_ref[0].astype(f32) * sz_ref[0].astype(f32)).astype(bf16)
    merged = mp_ref[0].astype(f32) + ga_ref[0].astype(f32) * _dot(y_attn, wua_ref[...])
    h_out = x_ref[0] + _dot(merged.astype(bf16), wout_ref[...])
    out_ref[0] = _rmsnorm(h_out, g_ref[...])


def _const(shape):
    return pl.BlockSpec(shape, lambda *_: (0,) * len(shape), pipeline_mode=pl.Buffered(1))


def kernel(x, meta_tokens, norm_g, w_in, b_forget, pool_w, pool_scale, w_up_pool, w_up_attn, w_out, final_norm_g):
    batch, seq, _ = x.shape
    n_tiles = seq // TM
    n_blocks = seq // BLK

    w = w_in[0]
    sizes = (POOL_WIDTH, POOL_WIDTH, ATTN_WIDTH, ATTN_WIDTH, ATTN_WIDTH, ATTN_WIDTH, N_HEADS, D_MODEL, D_MODEL)
    w_u, w_zp, w_q, w_k, w_v, w_za, w_f, w_gp, w_ga = jnp.split(w, np.cumsum(sizes)[:-1].tolist(), axis=1)
    w_main = jnp.concatenate([w_u, w_zp, w_k, w_za, w_gp, w_ga], axis=1).astype(bf16)
    w_qvT = jnp.concatenate([w_q, w_v], axis=1).T.astype(bf16)
    w_f3 = jnp.pad(jnp.tile(w_f, (1, N_SPLIT)), ((0, 0), (0, LANES - N_SPLIT * N_HEADS))).astype(bf16)
    b_f3 = jnp.pad(jnp.tile(b_forget[0], N_SPLIT), (0, LANES - N_SPLIT * N_HEADS)).reshape(1, LANES)
    w_uk = jnp.concatenate([w_u, w_k], axis=1).astype(bf16)
    w_vT = w_v.T.astype(bf16)
    g_in = norm_g[0].reshape(1, D_MODEL)
    g_out = final_norm_g.reshape(1, D_MODEL)
    pw = pool_w[0].astype(bf16)
    ps = pool_scale[0].reshape(1, POOL_WIDTH)
    w_upp = w_up_pool[0].astype(bf16)
    w_upa = w_up_attn[0].astype(bf16)
    w_o = w_out[0].astype(bf16)

    u_meta, k_meta, v_meta, r0 = pl.pallas_call(
        _meta_kernel,
        out_shape=(jax.ShapeDtypeStruct((N_META, POOL_WIDTH), f32),
                   jax.ShapeDtypeStruct((N_HEADS, N_META, LANES), bf16),
                   jax.ShapeDtypeStruct((N_HEADS, V_ROWS, N_META), bf16),
                   jax.ShapeDtypeStruct((8, LANES), f32)),
        compiler_params=pltpu.CompilerParams(vmem_limit_bytes=VMEM_LIMIT),
        name="meta_proj",
    )(meta_tokens, g_in, w_uk, w_vT, w_f3, b_f3)

    mp, ga, sz, qT, vT, kaug, r_tiles = pl.pallas_call(
        _proj_kernel,
        grid=(batch, n_tiles),
        in_specs=[
            pl.BlockSpec((1, TM, D_MODEL), lambda b, t: (b, t, 0)),
            _const((1, D_MODEL)),
            _const((D_MODEL, 4096)),
            _const((2 * ATTN_WIDTH, D_MODEL)),
            _const((D_MODEL, LANES)),
            _const((1, LANES)),
            _const((len(POOL_WINDOWS), POOL_GROUP, POOL_GROUP)),
            _const((1, POOL_WIDTH)),
            _const((POOL_WIDTH, D_MODEL)),
            _const((N_META, POOL_WIDTH)),
            _const((8, LANES)),
        ],
        out_specs=[
            pl.BlockSpec((1, TM, D_MODEL), lambda b, t: (b, t, 0)),
            pl.BlockSpec((1, TM, D_MODEL), lambda b, t: (b, t, 0)),
            pl.BlockSpec((1, TM, ATTN_WIDTH), lambda b, t: (b, t, 0)),
            pl.BlockSpec((1, ATTN_WIDTH, TM), lambda b, t: (b, 0, t)),
            pl.BlockSpec((1, NB, N_HEADS, V_ROWS, BLK), lambda b, t: (b, t, 0, 0, 0)),
            pl.BlockSpec((1, N_HEADS, TM, LANES), lambda b, t: (b, 0, t, 0)),
            pl.BlockSpec((1, 1, 8, LANES), lambda b, t: (b, t, 0, 0)),
        ],
        out_shape=(
            jax.ShapeDtypeStruct((batch, seq, D_MODEL), bf16),
            jax.ShapeDtypeStruct((batch, seq, D_MODEL), bf16),
            jax.ShapeDtypeStruct((batch, seq, ATTN_WIDTH), bf16),
            jax.ShapeDtypeStruct((batch, ATTN_WIDTH, seq), bf16),
            jax.ShapeDtypeStruct((batch, n_blocks, N_HEADS, V_ROWS, BLK), bf16),
            jax.ShapeDtypeStruct((batch, N_HEADS, seq, LANES), bf16),
            jax.ShapeDtypeStruct((batch, n_tiles, 8, LANES), f32),
        ),
        scratch_shapes=[pltpu.VMEM((POOL_PAD + N_META + TM, POOL_WIDTH), f32),
                        pltpu.VMEM((POOL_PAD + N_META + TM, 3 * POOL_GROUP), f32),
                        pltpu.VMEM((POOL_PAD + N_META + TM, 2 * POOL_GROUP), f32),
                        pltpu.VMEM((8, LANES), f32)],
        compiler_params=pltpu.CompilerParams(dimension_semantics=("arbitrary", "arbitrary"),
                                             vmem_limit_bytes=VMEM_LIMIT),
        name="in_proj",
    )(x, g_in, w_main, w_qvT, w_f3, b_f3, pw, ps, w_upp, u_meta, r0)

    r_blocks = r_tiles[:, :, :NB, :N_HEADS].reshape(batch * n_blocks * N_HEADS)

    o = pl.pallas_call(
        functools.partial(_attn_kernel, n_blocks=n_blocks),
        grid=(batch, seq // BQ),
        in_specs=[
            pl.BlockSpec(memory_space=pltpu.SMEM),
            pl.BlockSpec((1, ATTN_WIDTH, BQ), lambda b, t: (b, 0, t)),
            pl.BlockSpec((1, N_HEADS, seq, LANES), lambda b, t: (b, 0, 0, 0)),
            pl.BlockSpec((1, n_blocks, N_HEADS, V_ROWS, BLK), lambda b, t: (b, 0, 0, 0, 0)),
            _const((N_HEADS, N_META, LANES)),
            _const((N_HEADS, V_ROWS, N_META)),
        ],
        out_specs=pl.BlockSpec((1, BQ, ATTN_WIDTH), lambda b, t: (b, t, 0)),
        out_shape=jax.ShapeDtypeStruct((batch, seq, ATTN_WIDTH), bf16),
        scratch_shapes=[pltpu.VMEM((N_HEADS, LANES, BQ), bf16),
                        pltpu.VMEM((2, N_HEADS, BLK, BQ), f32),
                        pltpu.VMEM((N_HEADS, BLK, BQ), bf16),
                        pltpu.VMEM((N_HEADS, N_META, BQ), f32),
                        pltpu.VMEM((N_HEADS, N_META, BQ), bf16),
                        pltpu.VMEM((N_HEADS, 1, BQ), f32),
                        pltpu.VMEM((2, N_HEADS, 1, BQ), f32),
                        pltpu.VMEM((N_HEADS, 1, BQ), f32),
                        pltpu.VMEM((N_HEADS, 1, BQ), f32),
                        pltpu.VMEM((N_HEADS, V_ROWS, BQ), f32)],
        compiler_params=pltpu.CompilerParams(dimension_semantics=("arbitrary", "arbitrary"),
                                             vmem_limit_bytes=VMEM_LIMIT),
        name="attention",
    )(r_blocks, qT, kaug, vT, k_meta, v_meta)

    return pl.pallas_call(
        _out_kernel,
        grid=(batch, seq // TM_OUT),
        in_specs=[
            pl.BlockSpec((1, TM_OUT, D_MODEL), lambda b, t: (b, t, 0)),
            pl.BlockSpec((1, TM_OUT, ATTN_WIDTH), lambda b, t: (b, t, 0)),
            pl.BlockSpec((1, TM_OUT, ATTN_WIDTH), lambda b, t: (b, t, 0)),
            pl.BlockSpec((1, TM_OUT, D_MODEL), lambda b, t: (b, t, 0)),
            pl.BlockSpec((1, TM_OUT, D_MODEL), lambda b, t: (b, t, 0)),
            _const((ATTN_WIDTH, D_MODEL)),
            _const((D_MODEL, D_MODEL)),
            _const((1, D_MODEL)),
        ],
        out_specs=pl.BlockSpec((1, TM_OUT, D_MODEL), lambda b, t: (b, t, 0)),
        out_shape=jax.ShapeDtypeStruct((batch, seq, D_MODEL), f32),
        compiler_params=pltpu.CompilerParams(dimension_semantics=("arbitrary", "arbitrary"),
                                             vmem_limit_bytes=VMEM_LIMIT),
        name="out_proj",
    )(x, o, sz, ga, mp, w_upa, w_o, g_out)
```

```python
import functools

import jax
import jax.numpy as jnp
from jax import lax
from jax.experimental import pallas as pl
from jax.experimental.pallas import tpu as pltpu

D_MODEL = 1024
N_META = 16
POOL_WIDTH = 512
POOL_WINDOWS = (2, 4, 8, 16)
POOL_GROUP = POOL_WIDTH // len(POOL_WINDOWS)
N_HEADS = 8
HEAD_DIM = 64
ATTN_WIDTH = N_HEADS * HEAD_DIM
RMS_EPS = 1e-6

LANES = 128
BLK = 256
BQ = 512
QB = BQ // BLK
assert QB == 2
TM = 512
NB = TM // BLK
TM_OUT = 1024
POOL_PAD = 8
assert POOL_WINDOWS == (2, 4, 8, 16) and POOL_PAD >= POOL_WINDOWS[-1] // 2
V_ROWS = HEAD_DIM + 16
N_SPLIT = 3
MASKED = -1e30
SKEW = 2
LOG2E = 1.4426950408889634
VMEM_LIMIT = 56 * 1024 * 1024
COL_U, COL_ZP, COL_Q, COL_K, COL_V, COL_ZA, COL_F = (512 * i for i in range(7))

f32 = jnp.float32
bf16 = jnp.bfloat16


def _rmsnorm(x, g):
    return x * lax.rsqrt(jnp.mean(x * x, axis=-1, keepdims=True) + RMS_EPS) * g


def _sigmoid(x):
    return 1.0 / (1.0 + jnp.exp(-x))


def _log_sigmoid(x):
    return jnp.minimum(x, 0.0) - jnp.log1p(jnp.exp(-jnp.abs(x)))


def _dot(a, b):
    return jnp.dot(a, b, preferred_element_type=f32)


def _dot_nt(a, b):
    return lax.dot_general(a, b, (((1,), (1,)), ((), ())), preferred_element_type=f32)


def _lane_iota(shape):
    return lax.broadcasted_iota(jnp.int32, shape, 1)


def _decay_parts(logf3, rows):
    n = logf3.shape[0] // rows
    r = lax.broadcasted_iota(jnp.int32, (rows, rows), 0)
    c = lax.broadcasted_iota(jnp.int32, (rows, rows), 1)
    tri = (c <= r).astype(f32)
    side_by_side = jnp.concatenate([logf3[rows * i:rows * (i + 1)] for i in range(n)], axis=1)
    sums = jnp.dot(tri, side_by_side, precision=lax.Precision.HIGHEST, preferred_element_type=f32)
    beta = jnp.concatenate([sums[:, LANES * i:LANES * (i + 1)] for i in range(n)], axis=0)
    nb = beta * (-LOG2E)
    hi = nb.astype(bf16).astype(f32)
    mid = (nb - hi).astype(bf16).astype(f32)
    lo = (nb - hi - mid).astype(bf16).astype(f32)
    lane = _lane_iota(nb.shape)
    return beta, jnp.where(lane < N_HEADS, hi, jnp.where(lane < 2 * N_HEADS, mid, lo))


def _augmented_keys(kproj, parts, store):
    lane = _lane_iota((kproj.shape[0], LANES))
    for h in range(N_HEADS):
        slab = kproj[:, LANES * (h // 2):LANES * (h // 2 + 1)]
        if h % 2:
            slab = pltpu.roll(slab, HEAD_DIM, axis=1)
        store(h, jnp.where(lane < HEAD_DIM, slab, pltpu.roll(parts, HEAD_DIM - h, axis=1)).astype(bf16))


def _meta_kernel(meta_ref, g_ref, wa_ref, wf_ref, bf_ref,
                 umeta_ref, kmeta_ref, vmeta_ref, r0_ref):
    hn = _rmsnorm(meta_ref[...], g_ref[...]).astype(bf16)
    umeta_ref[...] = _dot(hn, wa_ref[:, COL_U:COL_ZP])
    vt = _dot_nt(wa_ref[:, COL_V:COL_ZA].T, hn)
    for h in range(N_HEADS):
        vmeta_ref[h, 0:HEAD_DIM, :] = vt[HEAD_DIM * h:HEAD_DIM * (h + 1), :].astype(bf16)
        vmeta_ref[h, HEAD_DIM:V_ROWS, :] = jnp.ones((V_ROWS - HEAD_DIM, N_META), bf16)
    logf3 = _log_sigmoid(_dot(hn, wf_ref[...]) + bf_ref[...])
    beta, parts = _decay_parts(logf3, N_META)

    def store(h, ka):
        kmeta_ref[h] = ka

    _augmented_keys(_dot(hn, wa_ref[:, COL_K:COL_V]), parts, store)
    r0_ref[...] = jnp.broadcast_to(beta[N_META - 1:N_META, :], r0_ref.shape)


def _proj_kernel(x_ref, g_ref, wa_ref, wg_ref, wf_ref, bf_ref, pw_ref, ps_ref, wup_ref,
                 umeta_ref, r0_ref,
                 mp_ref, ga_ref, sz_ref, qT_ref, vT_ref, kaug_ref, r_ref,
                 uext_ref, lvl_a_ref, lvl_b_ref, rcarry_ref, wqvT_ref):
    t = pl.program_id(1)

    @pl.when((pl.program_id(0) == 0) & (t == 0))
    def _():
        wqvT_ref[0:ATTN_WIDTH, :] = wa_ref[:, COL_Q:COL_K].T
        wqvT_ref[ATTN_WIDTH:2 * ATTN_WIDTH, :] = wa_ref[:, COL_V:COL_ZA].T
    ext = N_META + TM
    body = slice(POOL_PAD, POOL_PAD + ext)

    @pl.when(t == 0)
    def _():
        uext_ref[0:POOL_PAD, :] = jnp.zeros((POOL_PAD, POOL_WIDTH), f32)
        lvl_a_ref[0:POOL_PAD, :] = jnp.zeros((POOL_PAD, lvl_a_ref.shape[1]), f32)
        lvl_b_ref[0:POOL_PAD, :] = jnp.zeros((POOL_PAD, lvl_b_ref.shape[1]), f32)
        uext_ref[POOL_PAD:POOL_PAD + N_META, :] = umeta_ref[...]
        rcarry_ref[...] = r0_ref[...]

    hn = _rmsnorm(x_ref[0], g_ref[...]).astype(bf16)

    gat = _dot(hn, wg_ref[:, D_MODEL:2 * D_MODEL])
    ga_ref[0] = _sigmoid(gat).astype(bf16)

    uz = _dot(hn, wa_ref[:, COL_U:COL_Q])
    u = uz[:, :POOL_WIDTH]
    uext_ref[POOL_PAD + N_META:POOL_PAD + ext, :] = u

    def shifted_sum(ref, shift, cols):
        return ref[body, cols] + ref[POOL_PAD - shift:POOL_PAD - shift + ext, cols]

    g1, g2, g3 = (slice(POOL_GROUP * g, POOL_GROUP * (g + 1)) for g in range(3))
    sums = [shifted_sum(uext_ref, 1, g1)]
    lvl_a_ref[body, :] = shifted_sum(uext_ref, 1, slice(POOL_GROUP, POOL_WIDTH))
    sums.append(shifted_sum(lvl_a_ref, 2, g1))
    lvl_b_ref[body, :] = shifted_sum(lvl_a_ref, 2, slice(POOL_GROUP, 3 * POOL_GROUP))
    sums.append(shifted_sum(lvl_b_ref, 4, g1))
    lvl_a_ref[body, g1] = shifted_sum(lvl_b_ref, 4, g2)
    sums.append(shifted_sum(lvl_a_ref, 8, g1))
    uext_ref[POOL_PAD:POOL_PAD + N_META, :] = uext_ref[POOL_PAD + TM:POOL_PAD + ext, :]

    ys = []
    for g, w in enumerate(POOL_WINDOWS):
        cols = slice(POOL_GROUP * g, POOL_GROUP * (g + 1))
        pooled = sums[g][N_META:] * (1.0 / w) - u[:, cols]
        pg = _dot(pooled.astype(bf16), pw_ref[g])
        zp = uz[:, POOL_WIDTH + POOL_GROUP * g:POOL_WIDTH + POOL_GROUP * (g + 1)]
        ys.append(pg * ps_ref[:, cols] * (zp * _sigmoid(zp)))
    y_pool = jnp.concatenate(ys, axis=1).astype(bf16)
    gp = _dot(hn, wg_ref[:, 0:D_MODEL])
    mp_ref[0] = (_sigmoid(gp) * _dot(y_pool, wup_ref[...])).astype(bf16)

    logf3 = _log_sigmoid(_dot(hn, wf_ref[...]) + bf_ref[...])
    r_ref[...] = jnp.zeros(r_ref.shape, f32)
    beta, parts = _decay_parts(logf3, BLK)
    for c in range(NB):
        r_ref[0, 0, c:c + 1, :] = rcarry_ref[0:1, :] * LOG2E
        rcarry_ref[...] = rcarry_ref[...] + beta[BLK * (c + 1) - 1:BLK * (c + 1), :]

    za = _dot(hn, wa_ref[:, COL_ZA:COL_F])
    sz_ref[0] = (za * _sigmoid(za)).astype(bf16)

    def store(h, ka):
        kaug_ref[0, h] = ka

    _augmented_keys(_dot(hn, wa_ref[:, COL_K:COL_V]), parts, store)

    qv = _dot_nt(wqvT_ref[...], hn)
    qT_ref[0] = (qv[0:ATTN_WIDTH] * (HEAD_DIM ** -0.5 * LOG2E)).astype(bf16)
    for c in range(NB):
        for h in range(N_HEADS):
            rows = slice(ATTN_WIDTH + HEAD_DIM * h, ATTN_WIDTH + HEAD_DIM * (h + 1))
            vT_ref[0, c, h, 0:HEAD_DIM, :] = qv[rows, BLK * c:BLK * (c + 1)].astype(bf16)
            vT_ref[0, c, h, HEAD_DIM:V_ROWS, :] = jnp.ones((V_ROWS - HEAD_DIM, BLK), bf16)


def _attn_kernel(r_ref, qT_ref, kaug_ref, vT_ref, kmeta_ref, vmeta_ref, o_ref,
                 qaug_ref, s_ref, p_ref, sm_ref, pm_ref, m_ref, mblk_ref, mmeta_ref, alpha_ref, acc_ref,
                 *, n_blocks):
    b = pl.program_id(0)
    t = pl.program_id(1)
    first = QB * t
    heads = range(N_HEADS)

    part_row = lax.broadcasted_iota(jnp.int32, (LANES - HEAD_DIM, BQ), 0)
    ones_rows = ((part_row % N_HEADS == 0) & (part_row < N_SPLIT * N_HEADS)).astype(bf16)
    for h in heads:
        qaug_ref[h, 0:HEAD_DIM, :] = qT_ref[0, HEAD_DIM * h:HEAD_DIM * (h + 1), :]
        qaug_ref[h, HEAD_DIM:LANES, :] = ones_rows

    def r_at(j, h):
        return r_ref[(b * n_blocks + j) * N_HEADS + h]

    def col_max(s):
        return jnp.max(s, axis=0, keepdims=True)

    def scores(j, h):
        return _dot(kaug_ref[0, h, pl.ds(pl.multiple_of(j * BLK, BLK), BLK), :], qaug_ref[h])

    row = lax.broadcasted_iota(jnp.int32, (BLK, BQ), 0)
    col = lax.broadcasted_iota(jnp.int32, (BLK, BQ), 1)

    def score_item(j, slot, h, diag=None, meta=False):
        s = scores(j, h)
        if diag is not None:
            s = jnp.where(row + BLK * diag <= col, s, MASKED)
        s_ref[slot, h] = s
        mblk_ref[slot, h] = col_max(s)
        if meta:
            sm = _dot(kmeta_ref[h], qaug_ref[h])
            sm_ref[h] = sm
            mmeta_ref[h] = col_max(sm)

    def softmax_item(j, slot, h, meta=False):
        ref = r_at(first, h)
        off = ref - r_at(j, h)
        m_old = m_ref[h]
        m_blk = mblk_ref[slot, h] + off
        if meta:
            m_blk = jnp.maximum(m_blk, mmeta_ref[h] + ref)
        m_new = jnp.maximum(m_old, m_blk)
        alpha_ref[h] = jnp.exp2(m_old - m_new)
        m_ref[h] = m_new
        p_ref[h] = jnp.exp2(s_ref[slot, h] - (m_new - off)).astype(bf16)
        if meta:
            pm_ref[h] = jnp.exp2(sm_ref[h] - (m_new - ref)).astype(bf16)

    def value_item(j, slot, h, meta=False):
        pv = _dot(vT_ref[0, j, h], p_ref[h])
        if meta:
            pv = pv + _dot(vmeta_ref[h], pm_ref[h])
        acc_ref[h] = acc_ref[h] * alpha_ref[h] + pv

    def run(score_blocks, finish_blocks):
        score_items = [(blk + (h,), kw) for blk, kw in score_blocks for h in heads]
        finish_items = [(blk + (h,), kw) for blk, kw in finish_blocks for h in heads]
        for i in range(max(len(score_items), len(finish_items) + SKEW + 1)):
            if i < len(score_items):
                args, kw = score_items[i]
                score_item(*args, **kw)
            if 0 <= i - SKEW - 1 < len(finish_items):
                args, kw = finish_items[i - SKEW - 1]
                value_item(*args, **kw)
            if 0 <= i - SKEW < len(finish_items):
                args, kw = finish_items[i - SKEW]
                softmax_item(*args, **kw)

    m_ref[...] = jnp.full(m_ref.shape, MASKED, f32)
    acc_ref[...] = jnp.zeros(acc_ref.shape, f32)

    @pl.when(first == 0)
    def _():
        run([((first, 0), dict(diag=0))], [])

    @pl.when(first > 0)
    def _():
        run([((0, 0), {})], [])

        def body(k, carry):
            run([((2 * k + 1, 1), {}), ((2 * k + 2, 0), {})],
                [((2 * k, 0), {}), ((2 * k + 1, 1), {})])
            return carry

        lax.fori_loop(0, t - 1, body, 0)
        run([((first - 1, 1), {}), ((first, 0), dict(diag=0))],
            [((first - 2, 0), {}), ((first - 1, 1), {})])

    run([((first + 1, 1), dict(diag=1, meta=True))],
        [((first, 0), {}), ((first + 1, 1), dict(meta=True))])

    for pair in range(N_HEADS // 2):
        halves = []
        for h in (2 * pair, 2 * pair + 1):
            a = acc_ref[h]
            halves.append(a[0:HEAD_DIM] * (1.0 / a[HEAD_DIM:HEAD_DIM + 1]))
        o_ref[0, :, LANES * pair:LANES * (pair + 1)] = jnp.concatenate(halves, axis=0).T.astype(bf16)


def _out_kernel(x_ref, o_ref, sz_ref, ga_ref, mp_ref, wua_ref, wout_ref, g_ref, out_ref):
    y_attn = (o_ref[0].astype(f32) * sz_ref[0].astype(f32)).astype(bf16)
    merged = mp_ref[0].astype(f32) + ga_ref[0].astype(f32) * _dot(y_attn, wua_ref[...])
    h_out = x_ref[0] + _dot(merged.astype(bf16), wout_ref[...])
    out_ref[0] = _rmsnorm(h_out, g_ref[...])


def _const(shape):
    return pl.BlockSpec(shape, lambda *_: (0,) * len(shape), pipeline_mode=pl.Buffered(1))


def kernel(x, meta_tokens, norm_g, w_in, b_forget, pool_w, pool_scale, w_up_pool, w_up_attn, w_out, final_norm_g):
    batch, seq, _ = x.shape
    n_tiles = seq // TM
    n_blocks = seq // BLK

    w = w_in[0]
    w_a = w[:, :COL_F].astype(bf16)
    w_f = w[:, COL_F:COL_F + N_HEADS]
    w_g = w[:, COL_F + N_HEADS:].astype(bf16)
    w_f3 = jnp.pad(jnp.tile(w_f, (1, N_SPLIT)), ((0, 0), (0, LANES - N_SPLIT * N_HEADS))).astype(bf16)
    b_f3 = jnp.pad(jnp.tile(b_forget[0], N_SPLIT), (0, LANES - N_SPLIT * N_HEADS)).reshape(1, LANES)
    g_in = norm_g[0].reshape(1, D_MODEL)
    g_out = final_norm_g.reshape(1, D_MODEL)
    pw = pool_w[0].astype(bf16)
    ps = pool_scale[0].reshape(1, POOL_WIDTH)
    w_upp = w_up_pool[0].astype(bf16)
    w_upa = w_up_attn[0].astype(bf16)
    w_o = w_out[0].astype(bf16)

    u_meta, k_meta, v_meta, r0 = pl.pallas_call(
        _meta_kernel,
        out_shape=(jax.ShapeDtypeStruct((N_META, POOL_WIDTH), f32),
                   jax.ShapeDtypeStruct((N_HEADS, N_META, LANES), bf16),
                   jax.ShapeDtypeStruct((N_HEADS, V_ROWS, N_META), bf16),
                   jax.ShapeDtypeStruct((8, LANES), f32)),
        compiler_params=pltpu.CompilerParams(vmem_limit_bytes=VMEM_LIMIT),
        name="meta_proj",
    )(meta_tokens, g_in, w_a, w_f3, b_f3)

    mp, ga, sz, qT, vT, kaug, r_tiles = pl.pallas_call(
        _proj_kernel,
        grid=(batch, n_tiles),
        in_specs=[
            pl.BlockSpec((1, TM, D_MODEL), lambda b, t: (b, t, 0)),
            _const((1, D_MODEL)),
            _const((D_MODEL, COL_F)),
            _const((D_MODEL, 2 * D_MODEL)),
            _const((D_MODEL, LANES)),
            _const((1, LANES)),
            _const((len(POOL_WINDOWS), POOL_GROUP, POOL_GROUP)),
            _const((1, POOL_WIDTH)),
            _const((POOL_WIDTH, D_MODEL)),
            _const((N_META, POOL_WIDTH)),
            _const((8, LANES)),
        ],
        out_specs=[
            pl.BlockSpec((1, TM, D_MODEL), lambda b, t: (b, t, 0)),
            pl.BlockSpec((1, TM, D_MODEL), lambda b, t: (b, t, 0)),
            pl.BlockSpec((1, TM, ATTN_WIDTH), lambda b, t: (b, t, 0)),
            pl.BlockSpec((1, ATTN_WIDTH, TM), lambda b, t: (b, 0, t)),
            pl.BlockSpec((1, NB, N_HEADS, V_ROWS, BLK), lambda b, t: (b, t, 0, 0, 0)),
            pl.BlockSpec((1, N_HEADS, TM, LANES), lambda b, t: (b, 0, t, 0)),
            pl.BlockSpec((1, 1, 8, LANES), lambda b, t: (b, t, 0, 0)),
        ],
        out_shape=(
            jax.ShapeDtypeStruct((batch, seq, D_MODEL), bf16),
            jax.ShapeDtypeStruct((batch, seq, D_MODEL), bf16),
            jax.ShapeDtypeStruct((batch, seq, ATTN_WIDTH), bf16),
            jax.ShapeDtypeStruct((batch, ATTN_WIDTH, seq), bf16),
            jax.ShapeDtypeStruct((batch, n_blocks, N_HEADS, V_ROWS, BLK), bf16),
            jax.ShapeDtypeStruct((batch, N_HEADS, seq, LANES), bf16),
            jax.ShapeDtypeStruct((batch, n_tiles, 8, LANES), f32),
        ),
        scratch_shapes=[pltpu.VMEM((POOL_PAD + N_META + TM, POOL_WIDTH), f32),
                        pltpu.VMEM((POOL_PAD + N_META + TM, 3 * POOL_GROUP), f32),
                        pltpu.VMEM((POOL_PAD + N_META + TM, 2 * POOL_GROUP), f32),
                        pltpu.VMEM((8, LANES), f32),
                        pltpu.VMEM((2 * ATTN_WIDTH, D_MODEL), bf16)],
        compiler_params=pltpu.CompilerParams(dimension_semantics=("arbitrary", "arbitrary"),
                                             vmem_limit_bytes=VMEM_LIMIT),
        name="in_proj",
    )(x, g_in, w_a, w_g, w_f3, b_f3, pw, ps, w_upp, u_meta, r0)

    r_blocks = r_tiles[:, :, :NB, :N_HEADS].reshape(batch * n_blocks * N_HEADS)

    o = pl.pallas_call(
        functools.partial(_attn_kernel, n_blocks=n_blocks),
        grid=(batch, seq // BQ),
        in_specs=[
            pl.BlockSpec(memory_space=pltpu.SMEM),
            pl.BlockSpec((1, ATTN_WIDTH, BQ), lambda b, t: (b, 0, t)),
            pl.BlockSpec((1, N_HEADS, seq, LANES), lambda b, t: (b, 0, 0, 0)),
            pl.BlockSpec((1, n_blocks, N_HEADS, V_ROWS, BLK), lambda b, t: (b, 0, 0, 0, 0)),
            _const((N_HEADS, N_META, LANES)),
            _const((N_HEADS, V_ROWS, N_META)),
        ],
        out_specs=pl.BlockSpec((1, BQ, ATTN_WIDTH), lambda b, t: (b, t, 0)),
        out_shape=jax.ShapeDtypeStruct((batch, seq, ATTN_WIDTH), bf16),
        scratch_shapes=[pltpu.VMEM((N_HEADS, LANES, BQ), bf16),
                        pltpu.VMEM((2, N_HEADS, BLK, BQ), f32),
                        pltpu.VMEM((N_HEADS, BLK, BQ), bf16),
                        pltpu.VMEM((N_HEADS, N_META, BQ), f32),
                        pltpu.VMEM((N_HEADS, N_META, BQ), bf16),
                        pltpu.VMEM((N_HEADS, 1, BQ), f32),
                        pltpu.VMEM((2, N_HEADS, 1, BQ), f32),
                        pltpu.VMEM((N_HEADS, 1, BQ), f32),
                        pltpu.VMEM((N_HEADS, 1, BQ), f32),
                        pltpu.VMEM((N_HEADS, V_ROWS, BQ), f32)],
        compiler_params=pltpu.CompilerParams(dimension_semantics=("arbitrary", "arbitrary"),
                                             vmem_limit_bytes=VMEM_LIMIT),
        name="attention",
    )(r_blocks, qT, kaug, vT, k_meta, v_meta)

    return pl.pallas_call(
        _out_kernel,
        grid=(batch, seq // TM_OUT),
        in_specs=[
            pl.BlockSpec((1, TM_OUT, D_MODEL), lambda b, t: (b, t, 0)),
            pl.BlockSpec((1, TM_OUT, ATTN_WIDTH), lambda b, t: (b, t, 0)),
            pl.BlockSpec((1, TM_OUT, ATTN_WIDTH), lambda b, t: (b, t, 0)),
            pl.BlockSpec((1, TM_OUT, D_MODEL), lambda b, t: (b, t, 0)),
            pl.BlockSpec((1, TM_OUT, D_MODEL), lambda b, t: (b, t, 0)),
            _const((ATTN_WIDTH, D_MODEL)),
            _const((D_MODEL, D_MODEL)),
            _const((1, D_MODEL)),
        ],
        out_specs=pl.BlockSpec((1, TM_OUT, D_MODEL), lambda b, t: (b, t, 0)),
        out_shape=jax.ShapeDtypeStruct((batch, seq, D_MODEL), f32),
        compiler_params=pltpu.CompilerParams(dimension_semantics=("arbitrary", "arbitrary"),
                                             vmem_limit_bytes=VMEM_LIMIT),
        name="out_proj",
    )(x, o, sz, ga, mp, w_upa, w_o, g_out)
```

```python
import functools

import jax
import jax.numpy as jnp
from jax import lax
from jax.experimental import pallas as pl
from jax.experimental.pallas import tpu as pltpu

D_MODEL = 1024
N_META = 16
POOL_WIDTH = 512
POOL_WINDOWS = (2, 4, 8, 16)
POOL_GROUP = POOL_WIDTH // len(POOL_WINDOWS)
N_HEADS = 8
HEAD_DIM = 64
ATTN_WIDTH = N_HEADS * HEAD_DIM
RMS_EPS = 1e-6

LANES = 128
BLK = 256
BQ = 512
QB = BQ // BLK
assert QB == 2
TM = 512
NB = TM // BLK
TM_OUT = 1024
POOL_PAD = 8
assert POOL_WINDOWS == (2, 4, 8, 16) and POOL_PAD >= POOL_WINDOWS[-1] // 2
V_ROWS = HEAD_DIM + 16
N_SPLIT = 3
MASKED = -1e30
SKEW = 2
LOG2E = 1.4426950408889634
VMEM_LIMIT = 56 * 1024 * 1024
COL_U, COL_ZP, COL_Q, COL_K, COL_V, COL_ZA, COL_F = (512 * i for i in range(7))
COL_G = COL_F + N_HEADS
N_IN = COL_G + 2 * D_MODEL

f32 = jnp.float32
bf16 = jnp.bfloat16


def _rmsnorm(x, g):
    return x * lax.rsqrt(jnp.mean(x * x, axis=-1, keepdims=True) + RMS_EPS) * g


def _sigmoid(x):
    return 1.0 / (1.0 + jnp.exp(-x))


def _log_sigmoid(x):
    return jnp.minimum(x, 0.0) - jnp.log1p(jnp.exp(-jnp.abs(x)))


def _dot(a, b):
    return jnp.dot(a, b, preferred_element_type=f32)


def _dot_nt(a, b):
    return lax.dot_general(a, b, (((1,), (1,)), ((), ())), preferred_element_type=f32)


def _lane_iota(shape):
    return lax.broadcasted_iota(jnp.int32, shape, 1)


def _decay_parts(logf3, rows):
    n = logf3.shape[0] // rows
    r = lax.broadcasted_iota(jnp.int32, (rows, rows), 0)
    c = lax.broadcasted_iota(jnp.int32, (rows, rows), 1)
    tri = (c <= r).astype(f32)
    side_by_side = jnp.concatenate([logf3[rows * i:rows * (i + 1)] for i in range(n)], axis=1)
    sums = jnp.dot(tri, side_by_side, precision=lax.Precision.HIGHEST, preferred_element_type=f32)
    beta = jnp.concatenate([sums[:, LANES * i:LANES * (i + 1)] for i in range(n)], axis=0)
    nb = beta * (-LOG2E)
    hi = nb.astype(bf16).astype(f32)
    mid = (nb - hi).astype(bf16).astype(f32)
    lo = (nb - hi - mid).astype(bf16).astype(f32)
    lane = _lane_iota(nb.shape)
    return beta, jnp.where(lane < N_HEADS, hi, jnp.where(lane < 2 * N_HEADS, mid, lo))


def _augmented_keys(kproj, parts, store):
    lane = _lane_iota((kproj.shape[0], LANES))
    for h in range(N_HEADS):
        slab = kproj[:, LANES * (h // 2):LANES * (h // 2 + 1)]
        if h % 2:
            slab = pltpu.roll(slab, HEAD_DIM, axis=1)
        store(h, jnp.where(lane < HEAD_DIM, slab, pltpu.roll(parts, HEAD_DIM - h, axis=1)).astype(bf16))


def _meta_kernel(meta_ref, g_ref, wu_ref, wk_ref, wv_ref, wf_ref, bf_ref,
                 umeta_ref, kmeta_ref, vmeta_ref, r0_ref):
    hn = _rmsnorm(meta_ref[...], g_ref[...]).astype(bf16)
    umeta_ref[...] = _dot(hn, wu_ref[...])
    vt = _dot_nt(wv_ref[...].T, hn)
    for h in range(N_HEADS):
        vmeta_ref[h, 0:HEAD_DIM, :] = vt[HEAD_DIM * h:HEAD_DIM * (h + 1), :].astype(bf16)
        vmeta_ref[h, HEAD_DIM:V_ROWS, :] = jnp.ones((V_ROWS - HEAD_DIM, N_META), bf16)
    logf3 = _log_sigmoid(_dot(hn, wf_ref[...]) + bf_ref[...])
    beta, parts = _decay_parts(logf3, N_META)

    def store(h, ka):
        kmeta_ref[h] = ka

    _augmented_keys(_dot(hn, wk_ref[...]), parts, store)
    r0_ref[...] = jnp.broadcast_to(beta[N_META - 1:N_META, :], r0_ref.shape)


def _proj_kernel(x_ref, g_ref, wa_ref, wf_ref, bf_ref, pw_ref, ps_ref, wup_ref,
                 umeta_ref, r0_ref,
                 mp_ref, ga_ref, sz_ref, qT_ref, vT_ref, kaug_ref, r_ref,
                 uext_ref, lvl_a_ref, lvl_b_ref, rcarry_ref, wqvT_ref, wg_ref):
    t = pl.program_id(1)

    @pl.when((pl.program_id(0) == 0) & (t == 0))
    def _():
        wqvT_ref[0:ATTN_WIDTH, :] = wa_ref[:, COL_Q:COL_K].T
        wqvT_ref[ATTN_WIDTH:2 * ATTN_WIDTH, :] = wa_ref[:, COL_V:COL_ZA].T
        wg_ref[...] = wa_ref[:, COL_G:COL_G + 2 * D_MODEL]
    ext = N_META + TM
    body = slice(POOL_PAD, POOL_PAD + ext)

    @pl.when(t == 0)
    def _():
        uext_ref[0:POOL_PAD, :] = jnp.zeros((POOL_PAD, POOL_WIDTH), f32)
        lvl_a_ref[0:POOL_PAD, :] = jnp.zeros((POOL_PAD, lvl_a_ref.shape[1]), f32)
        lvl_b_ref[0:POOL_PAD, :] = jnp.zeros((POOL_PAD, lvl_b_ref.shape[1]), f32)
        uext_ref[POOL_PAD:POOL_PAD + N_META, :] = umeta_ref[...]
        rcarry_ref[...] = r0_ref[...]

    hn = _rmsnorm(x_ref[0], g_ref[...]).astype(bf16)

    gat = _dot(hn, wg_ref[:, D_MODEL:2 * D_MODEL])
    ga_ref[0] = _sigmoid(gat).astype(bf16)

    uz = _dot(hn, wa_ref[:, COL_U:COL_Q])
    u = uz[:, :POOL_WIDTH]
    uext_ref[POOL_PAD + N_META:POOL_PAD + ext, :] = u

    def shifted_sum(ref, shift, cols):
        return ref[body, cols] + ref[POOL_PAD - shift:POOL_PAD - shift + ext, cols]

    g1, g2, g3 = (slice(POOL_GROUP * g, POOL_GROUP * (g + 1)) for g in range(3))
    sums = [shifted_sum(uext_ref, 1, g1)]
    lvl_a_ref[body, :] = shifted_sum(uext_ref, 1, slice(POOL_GROUP, POOL_WIDTH))
    sums.append(shifted_sum(lvl_a_ref, 2, g1))
    lvl_b_ref[body, :] = shifted_sum(lvl_a_ref, 2, slice(POOL_GROUP, 3 * POOL_GROUP))
    sums.append(shifted_sum(lvl_b_ref, 4, g1))
    lvl_a_ref[body, g1] = shifted_sum(lvl_b_ref, 4, g2)
    sums.append(shifted_sum(lvl_a_ref, 8, g1))
    uext_ref[POOL_PAD:POOL_PAD + N_META, :] = uext_ref[POOL_PAD + TM:POOL_PAD + ext, :]

    ys = []
    for g, w in enumerate(POOL_WINDOWS):
        cols = slice(POOL_GROUP * g, POOL_GROUP * (g + 1))
        pooled = sums[g][N_META:] * (1.0 / w) - u[:, cols]
        pg = _dot(pooled.astype(bf16), pw_ref[g])
        zp = uz[:, POOL_WIDTH + POOL_GROUP * g:POOL_WIDTH + POOL_GROUP * (g + 1)]
        ys.append(pg * ps_ref[:, cols] * (zp * _sigmoid(zp)))
    y_pool = jnp.concatenate(ys, axis=1).astype(bf16)
    gp = _dot(hn, wg_ref[:, 0:D_MODEL])
    mp_ref[0] = (_sigmoid(gp) * _dot(y_pool, wup_ref[...])).astype(bf16)

    logf3 = _log_sigmoid(_dot(hn, wf_ref[...]) + bf_ref[...])
    r_ref[...] = jnp.zeros(r_ref.shape, f32)
    beta, parts = _decay_parts(logf3, BLK)
    for c in range(NB):
        r_ref[0, 0, c:c + 1, :] = rcarry_ref[0:1, :] * LOG2E
        rcarry_ref[...] = rcarry_ref[...] + beta[BLK * (c + 1) - 1:BLK * (c + 1), :]

    za = _dot(hn, wa_ref[:, COL_ZA:COL_F])
    sz_ref[0] = (za * _sigmoid(za)).astype(bf16)

    def store(h, ka):
        kaug_ref[0, h] = ka

    _augmented_keys(_dot(hn, wa_ref[:, COL_K:COL_V]), parts, store)

    qv = _dot_nt(wqvT_ref[...], hn)
    qT_ref[0] = (qv[0:ATTN_WIDTH] * (HEAD_DIM ** -0.5 * LOG2E)).astype(bf16)
    for c in range(NB):
        for h in range(N_HEADS):
            rows = slice(ATTN_WIDTH + HEAD_DIM * h, ATTN_WIDTH + HEAD_DIM * (h + 1))
            vT_ref[0, c, h, 0:HEAD_DIM, :] = qv[rows, BLK * c:BLK * (c + 1)].astype(bf16)
            vT_ref[0, c, h, HEAD_DIM:V_ROWS, :] = jnp.ones((V_ROWS - HEAD_DIM, BLK), bf16)


def _attn_kernel(r_ref, qT_ref, kaug_ref, vT_ref, kmeta_ref, vmeta_ref, o_ref,
                 qaug_ref, s_ref, p_ref, sm_ref, pm_ref, m_ref, mblk_ref, mmeta_ref, alpha_ref, acc_ref,
                 *, n_blocks):
    b = pl.program_id(0)
    t = pl.program_id(1)
    first = QB * t
    heads = range(N_HEADS)

    part_row = lax.broadcasted_iota(jnp.int32, (LANES - HEAD_DIM, BQ), 0)
    ones_rows = ((part_row % N_HEADS == 0) & (part_row < N_SPLIT * N_HEADS)).astype(bf16)
    for h in heads:
        qaug_ref[h, 0:HEAD_DIM, :] = qT_ref[0, HEAD_DIM * h:HEAD_DIM * (h + 1), :]
        qaug_ref[h, HEAD_DIM:LANES, :] = ones_rows

    def r_at(j, h):
        return r_ref[(b * n_blocks + j) * N_HEADS + h]

    def col_max(s):
        return jnp.max(s, axis=0, keepdims=True)

    def scores(j, h):
        return _dot(kaug_ref[0, h, pl.ds(pl.multiple_of(j * BLK, BLK), BLK), :], qaug_ref[h])

    row = lax.broadcasted_iota(jnp.int32, (BLK, BQ), 0)
    col = lax.broadcasted_iota(jnp.int32, (BLK, BQ), 1)

    def score_item(j, slot, h, diag=None, meta=False):
        s = scores(j, h)
        if diag is not None:
            s = jnp.where(row + BLK * diag <= col, s, MASKED)
        s_ref[slot, h] = s
        mblk_ref[slot, h] = col_max(s)
        if meta:
            sm = _dot(kmeta_ref[h], qaug_ref[h])
            sm_ref[h] = sm
            mmeta_ref[h] = col_max(sm)

    def softmax_item(j, slot, h, meta=False):
        ref = r_at(first, h)
        off = ref - r_at(j, h)
        m_old = m_ref[h]
        m_blk = mblk_ref[slot, h] + off
        if meta:
            m_blk = jnp.maximum(m_blk, mmeta_ref[h] + ref)
        m_new = jnp.maximum(m_old, m_blk)
        alpha_ref[h] = jnp.exp2(m_old - m_new)
        m_ref[h] = m_new
        p_ref[h] = jnp.exp2(s_ref[slot, h] - (m_new - off)).astype(bf16)
        if meta:
            pm_ref[h] = jnp.exp2(sm_ref[h] - (m_new - ref)).astype(bf16)

    def value_item(j, slot, h, meta=False):
        pv = _dot(vT_ref[0, j, h], p_ref[h])
        if meta:
            pv = pv + _dot(vmeta_ref[h], pm_ref[h])
        acc_ref[h] = acc_ref[h] * alpha_ref[h] + pv

    def run(score_blocks, finish_blocks):
        score_items = [(blk + (h,), kw) for blk, kw in score_blocks for h in heads]
        finish_items = [(blk + (h,), kw) for blk, kw in finish_blocks for h in heads]
        for i in range(max(len(score_items), len(finish_items) + SKEW + 1)):
            if i < len(score_items):
                args, kw = score_items[i]
                score_item(*args, **kw)
            if 0 <= i - SKEW - 1 < len(finish_items):
                args, kw = finish_items[i - SKEW - 1]
                value_item(*args, **kw)
            if 0 <= i - SKEW < len(finish_items):
                args, kw = finish_items[i - SKEW]
                softmax_item(*args, **kw)

    m_ref[...] = jnp.full(m_ref.shape, MASKED, f32)
    acc_ref[...] = jnp.zeros(acc_ref.shape, f32)

    @pl.when(first == 0)
    def _():
        run([((first, 0), dict(diag=0))], [])

    @pl.when(first > 0)
    def _():
        run([((0, 0), {})], [])

        def body(k, carry):
            run([((2 * k + 1, 1), {}), ((2 * k + 2, 0), {})],
                [((2 * k, 0), {}), ((2 * k + 1, 1), {})])
            return carry

        lax.fori_loop(0, t - 1, body, 0)
        run([((first - 1, 1), {}), ((first, 0), dict(diag=0))],
            [((first - 2, 0), {}), ((first - 1, 1), {})])

    run([((first + 1, 1), dict(diag=1, meta=True))],
        [((first, 0), {}), ((first + 1, 1), dict(meta=True))])

    for pair in range(N_HEADS // 2):
        halves = []
        for h in (2 * pair, 2 * pair + 1):
            a = acc_ref[h]
            halves.append(a[0:HEAD_DIM] * (1.0 / a[HEAD_DIM:HEAD_DIM + 1]))
        o_ref[0, :, LANES * pair:LANES * (pair + 1)] = jnp.concatenate(halves, axis=0).T.astype(bf16)


def _out_kernel(x_ref, o_ref, sz_ref, ga_ref, mp_ref, wua_ref, wout_ref, g_ref, out_ref):
    y_attn = (o_ref[0].astype(f32) * sz_ref[0].astype(f32)).astype(bf16)
    merged = mp_ref[0].astype(f32) + ga_ref[0].astype(f32) * _dot(y_attn, wua_ref[...])
    h_out = x_ref[0] + _dot(merged.astype(bf16), wout_ref[...])
    out_ref[0] = _rmsnorm(h_out, g_ref[...])


def _const(shape):
    return pl.BlockSpec(shape, lambda *_: (0,) * len(shape), pipeline_mode=pl.Buffered(1))


def kernel(x, meta_tokens, norm_g, w_in, b_forget, pool_w, pool_scale, w_up_pool, w_up_attn, w_out, final_norm_g):
    batch, seq, _ = x.shape
    n_tiles = seq // TM
    n_blocks = seq // BLK

    w_all = w_in[0].astype(bf16)
    w_f = w_in[0][:, COL_F:COL_G]
    w_f3 = jnp.pad(jnp.tile(w_f, (1, N_SPLIT)), ((0, 0), (0, LANES - N_SPLIT * N_HEADS))).astype(bf16)
    b_f3 = jnp.pad(jnp.tile(b_forget[0], N_SPLIT), (0, LANES - N_SPLIT * N_HEADS)).reshape(1, LANES)
    g_in = norm_g[0].reshape(1, D_MODEL)
    g_out = final_norm_g.reshape(1, D_MODEL)
    pw = pool_w[0].astype(bf16)
    ps = pool_scale[0].reshape(1, POOL_WIDTH)
    w_upp = w_up_pool[0].astype(bf16)
    w_upa = w_up_attn[0].astype(bf16)
    w_o = w_out[0].astype(bf16)

    u_meta, k_meta, v_meta, r0 = pl.pallas_call(
        _meta_kernel,
        out_shape=(jax.ShapeDtypeStruct((N_META, POOL_WIDTH), f32),
                   jax.ShapeDtypeStruct((N_HEADS, N_META, LANES), bf16),
                   jax.ShapeDtypeStruct((N_HEADS, V_ROWS, N_META), bf16),
                   jax.ShapeDtypeStruct((8, LANES), f32)),
        grid=(1,),
        in_specs=[_const((N_META, D_MODEL)), _const((1, D_MODEL))]
        + [pl.BlockSpec((D_MODEL, 512), functools.partial(lambda c, i: (0, c), col // 512),
                        pipeline_mode=pl.Buffered(1)) for col in (COL_U, COL_K, COL_V)]
        + [_const((D_MODEL, LANES)), _const((1, LANES))],
        out_specs=[_const((N_META, POOL_WIDTH)), _const((N_HEADS, N_META, LANES)),
                   _const((N_HEADS, V_ROWS, N_META)), _const((8, LANES))],
        compiler_params=pltpu.CompilerParams(vmem_limit_bytes=VMEM_LIMIT),
        name="meta_proj",
    )(meta_tokens, g_in, w_all, w_all, w_all, w_f3, b_f3)

    mp, ga, sz, qT, vT, kaug, r_tiles = pl.pallas_call(
        _proj_kernel,
        grid=(batch, n_tiles),
        in_specs=[
            pl.BlockSpec((1, TM, D_MODEL), lambda b, t: (b, t, 0)),
            _const((1, D_MODEL)),
            _const((D_MODEL, N_IN)),
            _const((D_MODEL, LANES)),
            _const((1, LANES)),
            _const((len(POOL_WINDOWS), POOL_GROUP, POOL_GROUP)),
            _const((1, POOL_WIDTH)),
            _const((POOL_WIDTH, D_MODEL)),
            _const((N_META, POOL_WIDTH)),
            _const((8, LANES)),
        ],
        out_specs=[
            pl.BlockSpec((1, TM, D_MODEL), lambda b, t: (b, t, 0)),
            pl.BlockSpec((1, TM, D_MODEL), lambda b, t: (b, t, 0)),
            pl.BlockSpec((1, TM, ATTN_WIDTH), lambda b, t: (b, t, 0)),
            pl.BlockSpec((1, ATTN_WIDTH, TM), lambda b, t: (b, 0, t)),
            pl.BlockSpec((1, NB, N_HEADS, V_ROWS, BLK), lambda b, t: (b, t, 0, 0, 0)),
            pl.BlockSpec((1, N_HEADS, TM, LANES), lambda b, t: (b, 0, t, 0)),
            pl.BlockSpec((1, 1, 8, LANES), lambda b, t: (b, t, 0, 0)),
        ],
        out_shape=(
            jax.ShapeDtypeStruct((batch, seq, D_MODEL), bf16),
            jax.ShapeDtypeStruct((batch, seq, D_MODEL), bf16),
            jax.ShapeDtypeStruct((batch, seq, ATTN_WIDTH), bf16),
            jax.ShapeDtypeStruct((batch, ATTN_WIDTH, seq), bf16),
            jax.ShapeDtypeStruct((batch, n_blocks, N_HEADS, V_ROWS, BLK), bf16),
            jax.ShapeDtypeStruct((batch, N_HEADS, seq, LANES), bf16),
            jax.ShapeDtypeStruct((batch, n_tiles, 8, LANES), f32),
        ),
        scratch_shapes=[pltpu.VMEM((POOL_PAD + N_META + TM, POOL_WIDTH), f32),
                        pltpu.VMEM((POOL_PAD + N_META + TM, 3 * POOL_GROUP), f32),
                        pltpu.VMEM((POOL_PAD + N_META + TM, 2 * POOL_GROUP), f32),
                        pltpu.VMEM((8, LANES), f32),
                        pltpu.VMEM((2 * ATTN_WIDTH, D_MODEL), bf16),
                        pltpu.VMEM((D_MODEL, 2 * D_MODEL), bf16)],
        compiler_params=pltpu.CompilerParams(dimension_semantics=("arbitrary", "arbitrary"),
                                             vmem_limit_bytes=VMEM_LIMIT),
        name="in_proj",
    )(x, g_in, w_all, w_f3, b_f3, pw, ps, w_upp, u_meta, r0)

    r_blocks = r_tiles[:, :, :NB, :N_HEADS].reshape(batch * n_blocks * N_HEADS)

    o = pl.pallas_call(
        functools.partial(_attn_kernel, n_blocks=n_blocks),
        grid=(batch, seq // BQ),
        in_specs=[
            pl.BlockSpec(memory_space=pltpu.SMEM),
            pl.BlockSpec((1, ATTN_WIDTH, BQ), lambda b, t: (b, 0, t)),
            pl.BlockSpec((1, N_HEADS, seq, LANES), lambda b, t: (b, 0, 0, 0)),
            pl.BlockSpec((1, n_blocks, N_HEADS, V_ROWS, BLK), lambda b, t: (b, 0, 0, 0, 0)),
            _const((N_HEADS, N_META, LANES)),
            _const((N_HEADS, V_ROWS, N_META)),
        ],
        out_specs=pl.BlockSpec((1, BQ, ATTN_WIDTH), lambda b, t: (b, t, 0)),
        out_shape=jax.ShapeDtypeStruct((batch, seq, ATTN_WIDTH), bf16),
        scratch_shapes=[pltpu.VMEM((N_HEADS, LANES, BQ), bf16),
                        pltpu.VMEM((2, N_HEADS, BLK, BQ), f32),
                        pltpu.VMEM((N_HEADS, BLK, BQ), bf16),
                        pltpu.VMEM((N_HEADS, N_META, BQ), f32),
                        pltpu.VMEM((N_HEADS, N_META, BQ), bf16),
                        pltpu.VMEM((N_HEADS, 1, BQ), f32),
                        pltpu.VMEM((2, N_HEADS, 1, BQ), f32),
                        pltpu.VMEM((N_HEADS, 1, BQ), f32),
                        pltpu.VMEM((N_HEADS, 1, BQ), f32),
                        pltpu.VMEM((N_HEADS, V_ROWS, BQ), f32)],
        compiler_params=pltpu.CompilerParams(dimension_semantics=("arbitrary", "arbitrary"),
                                             vmem_limit_bytes=VMEM_LIMIT),
        name="attention",
    )(r_blocks, qT, kaug, vT, k_meta, v_meta)

    return pl.pallas_call(
        _out_kernel,
        grid=(batch, seq // TM_OUT),
        in_specs=[
            pl.BlockSpec((1, TM_OUT, D_MODEL), lambda b, t: (b, t, 0)),
            pl.BlockSpec((1, TM_OUT, ATTN_WIDTH), lambda b, t: (b, t, 0)),
            pl.BlockSpec((1, TM_OUT, ATTN_WIDTH), lambda b, t: (b, t, 0)),
            pl.BlockSpec((1, TM_OUT, D_MODEL), lambda b, t: (b, t, 0)),
            pl.BlockSpec((1, TM_OUT, D_MODEL), lambda b, t: (b, t, 0)),
            _const((ATTN_WIDTH, D_MODEL)),
            _const((D_MODEL, D_MODEL)),
            _const((1, D_MODEL)),
        ],
        out_specs=pl.BlockSpec((1, TM_OUT, D_MODEL), lambda b, t: (b, t, 0)),
        out_shape=jax.ShapeDtypeStruct((batch, seq, D_MODEL), f32),
        compiler_params=pltpu.CompilerParams(dimension_semantics=("arbitrary", "arbitrary"),
                                             vmem_limit_bytes=VMEM_LIMIT),
        name="out_proj",
    )(x, o, sz, ga, mp, w_upa, w_o, g_out)
```

```python
import functools

import jax
import jax.numpy as jnp
from jax import lax
from jax.experimental import pallas as pl
from jax.experimental.pallas import tpu as pltpu

D_MODEL = 1024
N_META = 16
POOL_WIDTH = 512
POOL_WINDOWS = (2, 4, 8, 16)
POOL_GROUP = POOL_WIDTH // len(POOL_WINDOWS)
N_HEADS = 8
HEAD_DIM = 64
ATTN_WIDTH = N_HEADS * HEAD_DIM
RMS_EPS = 1e-6

LANES = 128
BLK = 256
BQ = 512
QB = BQ // BLK
assert QB == 2
TM = 512
NB = TM // BLK
TM_OUT = 1024
POOL_PAD = 8
assert POOL_WINDOWS == (2, 4, 8, 16) and POOL_PAD >= POOL_WINDOWS[-1] // 2
V_ROWS = HEAD_DIM + 16
N_SPLIT = 3
MASKED = -1e30
SKEW = 2
N_SLOTS = 2
LOG2E = 1.4426950408889634
VMEM_LIMIT = 56 * 1024 * 1024
COL_U, COL_ZP, COL_Q, COL_K, COL_V, COL_ZA, COL_F = (512 * i for i in range(7))
COL_G = COL_F + N_HEADS
N_IN = COL_G + 2 * D_MODEL

f32 = jnp.float32
bf16 = jnp.bfloat16


def _rmsnorm(x, g):
    return x * lax.rsqrt(jnp.mean(x * x, axis=-1, keepdims=True) + RMS_EPS) * g


def _sigmoid(x):
    return 1.0 / (1.0 + jnp.exp(-x))


def _log_sigmoid(x):
    return jnp.minimum(x, 0.0) - jnp.log1p(jnp.exp(-jnp.abs(x)))


def _dot(a, b):
    return jnp.dot(a, b, preferred_element_type=f32)


def _dot_nt(a, b):
    return lax.dot_general(a, b, (((1,), (1,)), ((), ())), preferred_element_type=f32)


def _lane_iota(shape):
    return lax.broadcasted_iota(jnp.int32, shape, 1)


def _decay_parts(logf3, rows):
    n = logf3.shape[0] // rows
    r = lax.broadcasted_iota(jnp.int32, (rows, rows), 0)
    c = lax.broadcasted_iota(jnp.int32, (rows, rows), 1)
    tri = (c <= r).astype(f32)
    side_by_side = jnp.concatenate([logf3[rows * i:rows * (i + 1)] for i in range(n)], axis=1)
    sums = jnp.dot(tri, side_by_side, precision=lax.Precision.HIGHEST, preferred_element_type=f32)
    beta = jnp.concatenate([sums[:, LANES * i:LANES * (i + 1)] for i in range(n)], axis=0)
    nb = beta * (-LOG2E)
    hi = nb.astype(bf16).astype(f32)
    mid = (nb - hi).astype(bf16).astype(f32)
    lo = (nb - hi - mid).astype(bf16).astype(f32)
    lane = _lane_iota(nb.shape)
    return beta, jnp.where(lane < N_HEADS, hi, jnp.where(lane < 2 * N_HEADS, mid, lo))


def _augmented_keys(kproj, parts, store):
    lane = _lane_iota((kproj.shape[0], LANES))
    for h in range(N_HEADS):
        slab = kproj[:, LANES * (h // 2):LANES * (h // 2 + 1)]
        if h % 2:
            slab = pltpu.roll(slab, HEAD_DIM, axis=1)
        store(h, jnp.where(lane < HEAD_DIM, slab, pltpu.roll(parts, HEAD_DIM - h, axis=1)).astype(bf16))


def _meta_kernel(meta_ref, g_ref, wu_ref, wk_ref, wv_ref, wf_ref, bf_ref,
                 umeta_ref, kmeta_ref, vmeta_ref, r0_ref):
    hn = _rmsnorm(meta_ref[...], g_ref[...]).astype(bf16)
    umeta_ref[...] = _dot(hn, wu_ref[...])
    vt = _dot_nt(wv_ref[...].T, hn)
    for h in range(N_HEADS):
        vmeta_ref[h, 0:HEAD_DIM, :] = vt[HEAD_DIM * h:HEAD_DIM * (h + 1), :].astype(bf16)
        vmeta_ref[h, HEAD_DIM:V_ROWS, :] = jnp.ones((V_ROWS - HEAD_DIM, N_META), bf16)
    logf3 = _log_sigmoid(_dot(hn, wf_ref[...]) + bf_ref[...])
    beta, parts = _decay_parts(logf3, N_META)

    def store(h, ka):
        kmeta_ref[h] = ka

    _augmented_keys(_dot(hn, wk_ref[...]), parts, store)
    r0_ref[...] = jnp.broadcast_to(beta[N_META - 1:N_META, :], r0_ref.shape)


def _proj_kernel(x_ref, g_ref, wa_ref, wf_ref, bf_ref, pw_ref, ps_ref, wup_ref,
                 umeta_ref, r0_ref,
                 mp_ref, ga_ref, sz_ref, qT_ref, vT_ref, kaug_ref, r_ref,
                 uext_ref, lvl_a_ref, lvl_b_ref, rcarry_ref, wqvT_ref, wg_ref):
    t = pl.program_id(1)

    @pl.when((pl.program_id(0) == 0) & (t == 0))
    def _():
        wqvT_ref[0:ATTN_WIDTH, :] = wa_ref[:, COL_Q:COL_K].T
        wqvT_ref[ATTN_WIDTH:2 * ATTN_WIDTH, :] = wa_ref[:, COL_V:COL_ZA].T
        wg_ref[...] = wa_ref[:, COL_G:COL_G + 2 * D_MODEL]
    ext = N_META + TM
    body = slice(POOL_PAD, POOL_PAD + ext)

    @pl.when(t == 0)
    def _():
        uext_ref[0:POOL_PAD, :] = jnp.zeros((POOL_PAD, POOL_WIDTH), f32)
        lvl_a_ref[0:POOL_PAD, :] = jnp.zeros((POOL_PAD, lvl_a_ref.shape[1]), f32)
        lvl_b_ref[0:POOL_PAD, :] = jnp.zeros((POOL_PAD, lvl_b_ref.shape[1]), f32)
        uext_ref[POOL_PAD:POOL_PAD + N_META, :] = umeta_ref[...]
        rcarry_ref[...] = r0_ref[...]

    hn = _rmsnorm(x_ref[0], g_ref[...]).astype(bf16)

    gat = _dot(hn, wg_ref[:, D_MODEL:2 * D_MODEL])
    ga_ref[0] = _sigmoid(gat).astype(bf16)

    uz = _dot(hn, wa_ref[:, COL_U:COL_Q])
    u = uz[:, :POOL_WIDTH]
    uext_ref[POOL_PAD + N_META:POOL_PAD + ext, :] = u

    def shifted_sum(ref, shift, cols):
        return ref[body, cols] + ref[POOL_PAD - shift:POOL_PAD - shift + ext, cols]

    g1, g2, g3 = (slice(POOL_GROUP * g, POOL_GROUP * (g + 1)) for g in range(3))
    sums = [shifted_sum(uext_ref, 1, g1)]
    lvl_a_ref[body, :] = shifted_sum(uext_ref, 1, slice(POOL_GROUP, POOL_WIDTH))
    sums.append(shifted_sum(lvl_a_ref, 2, g1))
    lvl_b_ref[body, :] = shifted_sum(lvl_a_ref, 2, slice(POOL_GROUP, 3 * POOL_GROUP))
    sums.append(shifted_sum(lvl_b_ref, 4, g1))
    lvl_a_ref[body, g1] = shifted_sum(lvl_b_ref, 4, g2)
    sums.append(shifted_sum(lvl_a_ref, 8, g1))
    uext_ref[POOL_PAD:POOL_PAD + N_META, :] = uext_ref[POOL_PAD + TM:POOL_PAD + ext, :]

    ys = []
    for g, w in enumerate(POOL_WINDOWS):
        cols = slice(POOL_GROUP * g, POOL_GROUP * (g + 1))
        pooled = sums[g][N_META:] * (1.0 / w) - u[:, cols]
        pg = _dot(pooled.astype(bf16), pw_ref[g])
        zp = uz[:, POOL_WIDTH + POOL_GROUP * g:POOL_WIDTH + POOL_GROUP * (g + 1)]
        ys.append(pg * ps_ref[:, cols] * (zp * _sigmoid(zp)))
    y_pool = jnp.concatenate(ys, axis=1).astype(bf16)
    gp = _dot(hn, wg_ref[:, 0:D_MODEL])
    mp_ref[0] = (_sigmoid(gp) * _dot(y_pool, wup_ref[...])).astype(bf16)

    logf3 = _log_sigmoid(_dot(hn, wf_ref[...]) + bf_ref[...])
    r_ref[...] = jnp.zeros(r_ref.shape, f32)
    beta, parts = _decay_parts(logf3, BLK)
    for c in range(NB):
        r_ref[0, 0, c:c + 1, :] = rcarry_ref[0:1, :] * LOG2E
        rcarry_ref[...] = rcarry_ref[...] + beta[BLK * (c + 1) - 1:BLK * (c + 1), :]

    za = _dot(hn, wa_ref[:, COL_ZA:COL_F])
    sz_ref[0] = (za * _sigmoid(za)).astype(bf16)

    def store(h, ka):
        kaug_ref[0, h] = ka

    _augmented_keys(_dot(hn, wa_ref[:, COL_K:COL_V]), parts, store)

    qv = _dot_nt(wqvT_ref[...], hn)
    qT_ref[0] = (qv[0:ATTN_WIDTH] * (HEAD_DIM ** -0.5 * LOG2E)).astype(bf16)
    for c in range(NB):
        for h in range(N_HEADS):
            rows = slice(ATTN_WIDTH + HEAD_DIM * h, ATTN_WIDTH + HEAD_DIM * (h + 1))
            vT_ref[0, c, h, 0:HEAD_DIM, :] = qv[rows, BLK * c:BLK * (c + 1)].astype(bf16)
            vT_ref[0, c, h, HEAD_DIM:V_ROWS, :] = jnp.ones((V_ROWS - HEAD_DIM, BLK), bf16)


def _attn_kernel(r_ref, qT_ref, kaug_ref, vT_ref, kmeta_ref, vmeta_ref, o_ref,
                 qaug_ref, s_ref, p_ref, sm_ref, pm_ref, m_ref, mblk_ref, mmeta_ref, alpha_ref, acc_ref,
                 *, n_blocks):
    b = pl.program_id(0)
    t = pl.program_id(1)
    first = QB * t
    heads = range(N_HEADS)

    part_row = lax.broadcasted_iota(jnp.int32, (LANES - HEAD_DIM, BQ), 0)
    ones_rows = ((part_row % N_HEADS == 0) & (part_row < N_SPLIT * N_HEADS)).astype(bf16)
    for h in heads:
        qaug_ref[h, 0:HEAD_DIM, :] = qT_ref[0, HEAD_DIM * h:HEAD_DIM * (h + 1), :]
        qaug_ref[h, HEAD_DIM:LANES, :] = ones_rows

    def r_at(j, h):
        return r_ref[(b * n_blocks + j) * N_HEADS + h]

    def col_max(s):
        return jnp.max(s, axis=0, keepdims=True)

    def scores(j, h):
        return _dot(kaug_ref[0, h, pl.ds(pl.multiple_of(j * BLK, BLK), BLK), :], qaug_ref[h])

    row = lax.broadcasted_iota(jnp.int32, (BLK, BQ), 0)
    col = lax.broadcasted_iota(jnp.int32, (BLK, BQ), 1)

    def score_item(j, slot, h, diag=None, meta=False):
        s = scores(j, h)
        if diag is not None:
            s = jnp.where(row + BLK * diag <= col, s, MASKED)
        s_ref[slot, h] = s
        mblk_ref[slot, h] = col_max(s)
        if meta:
            sm = _dot(kmeta_ref[h], qaug_ref[h])
            sm_ref[h] = sm
            mmeta_ref[h] = col_max(sm)

    def softmax_item(j, slot, h, side, meta=False):
        ref = r_at(first, h)
        off = ref - r_at(j, h)
        m_old = m_ref[side, h]
        m_blk = mblk_ref[slot, h] + off
        if meta:
            m_blk = jnp.maximum(m_blk, mmeta_ref[h] + ref)
        m_new = jnp.maximum(m_old, m_blk)
        alpha_ref[side, h] = jnp.exp2(m_old - m_new)
        m_ref[1 - side, h] = m_new
        p_ref[side, h] = jnp.exp2(s_ref[slot, h] - (m_new - off)).astype(bf16)
        if meta:
            pm_ref[h] = jnp.exp2(sm_ref[h] - (m_new - ref)).astype(bf16)

    def value_item(j, slot, h, side, meta=False):
        pv = _dot(vT_ref[0, j, h], p_ref[side, h])
        if meta:
            pv = pv + _dot(vmeta_ref[h], pm_ref[h])
        acc_ref[1 - side, h] = acc_ref[side, h] * alpha_ref[side, h] + pv

    def run(score_blocks, finish_blocks):
        score_items = [(blk + (h,), kw) for blk, kw in score_blocks for h in heads]
        finish_items = [(blk + (h, n % 2), kw) for n, (blk, kw) in enumerate(finish_blocks) for h in heads]
        for i in range(max(len(score_items), len(finish_items) + SKEW + 1)):
            if i < len(score_items):
                args, kw = score_items[i]
                score_item(*args, **kw)
            if 0 <= i - SKEW - 1 < len(finish_items):
                args, kw = finish_items[i - SKEW - 1]
                value_item(*args, **kw)
            if 0 <= i - SKEW < len(finish_items):
                args, kw = finish_items[i - SKEW]
                softmax_item(*args, **kw)

    m_ref[0] = jnp.full(m_ref.shape[1:], MASKED, f32)
    acc_ref[0] = jnp.zeros(acc_ref.shape[1:], f32)

    def blocks(n0, count, **last):
        out = [((n0 + i, i % N_SLOTS), {}) for i in range(count)]
        if last:
            out[-1] = ((n0 + count - 1, last["slot"]), {})
        return out

    @pl.when(t == 0)
    def _():
        run([((first, 0), dict(diag=0))], [])

    @pl.when(t > 0)
    def _():
        run(blocks(0, 1), [])

        def body(g, carry):
            run(blocks(4 * g, 5)[1:], blocks(4 * g, 4))
            return carry

        lax.fori_loop(0, (t - 1) // 2, body, 0)

    @pl.when((t > 0) & (t % 2 == 0))
    def _():
        run(blocks(first - 4, 3, slot=0)[1:], blocks(first - 4, 2))

    @pl.when(t > 0)
    def _():
        run([((first - 1, 1), {}), ((first, 0), dict(diag=0))], blocks(first - 2, 2))

    run([((first + 1, 1), dict(diag=1, meta=True))], [((first, 0), {}), ((first + 1, 1), dict(meta=True))])

    for pair in range(N_HEADS // 2):
        halves = []
        for h in (2 * pair, 2 * pair + 1):
            a = acc_ref[0, h]
            halves.append(a[0:HEAD_DIM] * (1.0 / a[HEAD_DIM:HEAD_DIM + 1]))
        o_ref[0, :, LANES * pair:LANES * (pair + 1)] = jnp.concatenate(halves, axis=0).T.astype(bf16)


def _out_kernel(x_ref, o_ref, sz_ref, ga_ref, mp_ref, wua_ref, wout_ref, g_ref, out_ref):
    y_attn = (o_ref[0].astype(f32) * sz_ref[0].astype(f32)).astype(bf16)
    merged = mp_ref[0].astype(f32) + ga_ref[0].astype(f32) * _dot(y_attn, wua_ref[...])
    h_out = x_ref[0] + _dot(merged.astype(bf16), wout_ref[...])
    out_ref[0] = _rmsnorm(h_out, g_ref[...])


def _const(shape):
    return pl.BlockSpec(shape, lambda *_: (0,) * len(shape), pipeline_mode=pl.Buffered(1))


def kernel(x, meta_tokens, norm_g, w_in, b_forget, pool_w, pool_scale, w_up_pool, w_up_attn, w_out, final_norm_g):
    batch, seq, _ = x.shape
    n_tiles = seq // TM
    n_blocks = seq // BLK

    w_all = w_in[0].astype(bf16)
    w_f = w_in[0][:, COL_F:COL_G]
    w_f3 = jnp.pad(jnp.tile(w_f, (1, N_SPLIT)), ((0, 0), (0, LANES - N_SPLIT * N_HEADS))).astype(bf16)
    b_f3 = jnp.pad(jnp.tile(b_forget[0], N_SPLIT), (0, LANES - N_SPLIT * N_HEADS)).reshape(1, LANES)
    g_in = norm_g[0].reshape(1, D_MODEL)
    g_out = final_norm_g.reshape(1, D_MODEL)
    pw = pool_w[0].astype(bf16)
    ps = pool_scale[0].reshape(1, POOL_WIDTH)
    w_upp = w_up_pool[0].astype(bf16)
    w_upa = w_up_attn[0].astype(bf16)
    w_o = w_out[0].astype(bf16)

    u_meta, k_meta, v_meta, r0 = pl.pallas_call(
        _meta_kernel,
        out_shape=(jax.ShapeDtypeStruct((N_META, POOL_WIDTH), f32),
                   jax.ShapeDtypeStruct((N_HEADS, N_META, LANES), bf16),
                   jax.ShapeDtypeStruct((N_HEADS, V_ROWS, N_META), bf16),
                   jax.ShapeDtypeStruct((8, LANES), f32)),
        grid=(1,),
        in_specs=[_const((N_META, D_MODEL)), _const((1, D_MODEL))]
        + [pl.BlockSpec((D_MODEL, 512), functools.partial(lambda c, i: (0, c), col // 512),
                        pipeline_mode=pl.Buffered(1)) for col in (COL_U, COL_K, COL_V)]
        + [_const((D_MODEL, LANES)), _const((1, LANES))],
        out_specs=[_const((N_META, POOL_WIDTH)), _const((N_HEADS, N_META, LANES)),
                   _const((N_HEADS, V_ROWS, N_META)), _const((8, LANES))],
        compiler_params=pltpu.CompilerParams(vmem_limit_bytes=VMEM_LIMIT),
        name="meta_proj",
    )(meta_tokens, g_in, w_all, w_all, w_all, w_f3, b_f3)

    mp, ga, sz, qT, vT, kaug, r_tiles = pl.pallas_call(
        _proj_kernel,
        grid=(batch, n_tiles),
        in_specs=[
            pl.BlockSpec((1, TM, D_MODEL), lambda b, t: (b, t, 0)),
            _const((1, D_MODEL)),
            _const((D_MODEL, N_IN)),
            _const((D_MODEL, LANES)),
            _const((1, LANES)),
            _const((len(POOL_WINDOWS), POOL_GROUP, POOL_GROUP)),
            _const((1, POOL_WIDTH)),
            _const((POOL_WIDTH, D_MODEL)),
            _const((N_META, POOL_WIDTH)),
            _const((8, LANES)),
        ],
        out_specs=[
            pl.BlockSpec((1, TM, D_MODEL), lambda b, t: (b, t, 0)),
            pl.BlockSpec((1, TM, D_MODEL), lambda b, t: (b, t, 0)),
            pl.BlockSpec((1, TM, ATTN_WIDTH), lambda b, t: (b, t, 0)),
            pl.BlockSpec((1, ATTN_WIDTH, TM), lambda b, t: (b, 0, t)),
            pl.BlockSpec((1, NB, N_HEADS, V_ROWS, BLK), lambda b, t: (b, t, 0, 0, 0)),
            pl.BlockSpec((1, N_HEADS, TM, LANES), lambda b, t: (b, 0, t, 0)),
            pl.BlockSpec((1, 1, 8, LANES), lambda b, t: (b, t, 0, 0)),
        ],
        out_shape=(
            jax.ShapeDtypeStruct((batch, seq, D_MODEL), bf16),
            jax.ShapeDtypeStruct((batch, seq, D_MODEL), bf16),
            jax.ShapeDtypeStruct((batch, seq, ATTN_WIDTH), bf16),
            jax.ShapeDtypeStruct((batch, ATTN_WIDTH, seq), bf16),
            jax.ShapeDtypeStruct((batch, n_blocks, N_HEADS, V_ROWS, BLK), bf16),
            jax.ShapeDtypeStruct((batch, N_HEADS, seq, LANES), bf16),
            jax.ShapeDtypeStruct((batch, n_tiles, 8, LANES), f32),
        ),
        scratch_shapes=[pltpu.VMEM((POOL_PAD + N_META + TM, POOL_WIDTH), f32),
                        pltpu.VMEM((POOL_PAD + N_META + TM, 3 * POOL_GROUP), f32),
                        pltpu.VMEM((POOL_PAD + N_META + TM, 2 * POOL_GROUP), f32),
                        pltpu.VMEM((8, LANES), f32),
                        pltpu.VMEM((2 * ATTN_WIDTH, D_MODEL), bf16),
                        pltpu.VMEM((D_MODEL, 2 * D_MODEL), bf16)],
        compiler_params=pltpu.CompilerParams(dimension_semantics=("arbitrary", "arbitrary"),
                                             vmem_limit_bytes=VMEM_LIMIT),
        name="in_proj",
    )(x, g_in, w_all, w_f3, b_f3, pw, ps, w_upp, u_meta, r0)

    r_blocks = r_tiles[:, :, :NB, :N_HEADS].reshape(batch * n_blocks * N_HEADS)

    o = pl.pallas_call(
        functools.partial(_attn_kernel, n_blocks=n_blocks),
        grid=(batch, seq // BQ),
        in_specs=[
            pl.BlockSpec(memory_space=pltpu.SMEM),
            pl.BlockSpec((1, ATTN_WIDTH, BQ), lambda b, t: (b, 0, t)),
            pl.BlockSpec((1, N_HEADS, seq, LANES), lambda b, t: (b, 0, 0, 0)),
            pl.BlockSpec((1, n_blocks, N_HEADS, V_ROWS, BLK), lambda b, t: (b, 0, 0, 0, 0)),
            _const((N_HEADS, N_META, LANES)),
            _const((N_HEADS, V_ROWS, N_META)),
        ],
        out_specs=pl.BlockSpec((1, BQ, ATTN_WIDTH), lambda b, t: (b, t, 0)),
        out_shape=jax.ShapeDtypeStruct((batch, seq, ATTN_WIDTH), bf16),
        scratch_shapes=[pltpu.VMEM((N_HEADS, LANES, BQ), bf16),
                        pltpu.VMEM((N_SLOTS, N_HEADS, BLK, BQ), f32),
                        pltpu.VMEM((2, N_HEADS, BLK, BQ), bf16),
                        pltpu.VMEM((N_HEADS, N_META, BQ), f32),
                        pltpu.VMEM((N_HEADS, N_META, BQ), bf16),
                        pltpu.VMEM((2, N_HEADS, 1, BQ), f32),
                        pltpu.VMEM((N_SLOTS, N_HEADS, 1, BQ), f32),
                        pltpu.VMEM((N_HEADS, 1, BQ), f32),
                        pltpu.VMEM((2, N_HEADS, 1, BQ), f32),
                        pltpu.VMEM((2, N_HEADS, V_ROWS, BQ), f32)],
        compiler_params=pltpu.CompilerParams(dimension_semantics=("arbitrary", "arbitrary"),
                                             vmem_limit_bytes=VMEM_LIMIT),
        name="attention",
    )(r_blocks, qT, kaug, vT, k_meta, v_meta)

    return pl.pallas_call(
        _out_kernel,
        grid=(batch, seq // TM_OUT),
        in_specs=[
            pl.BlockSpec((1, TM_OUT, D_MODEL), lambda b, t: (b, t, 0)),
            pl.BlockSpec((1, TM_OUT, ATTN_WIDTH), lambda b, t: (b, t, 0)),
            pl.BlockSpec((1, TM_OUT, ATTN_WIDTH), lambda b, t: (b, t, 0)),
            pl.BlockSpec((1, TM_OUT, D_MODEL), lambda b, t: (b, t, 0)),
            pl.BlockSpec((1, TM_OUT, D_MODEL), lambda b, t: (b, t, 0)),
            _const((ATTN_WIDTH, D_MODEL)),
            _const((D_MODEL, D_MODEL)),
            _const((1, D_MODEL)),
        ],
        out_specs=pl.BlockSpec((1, TM_OUT, D_MODEL), lambda b, t: (b, t, 0)),
        out_shape=jax.ShapeDtypeStruct((batch, seq, D_MODEL), f32),
        compiler_params=pltpu.CompilerParams(dimension_semantics=("arbitrary", "arbitrary"),
                                             vmem_limit_bytes=VMEM_LIMIT),
        name="out_proj",
    )(x, o, sz, ga, mp, w_upa, w_o, g_out)
```

```python
import functools

import jax
import jax.numpy as jnp
from jax import lax
from jax.experimental import pallas as pl
from jax.experimental.pallas import tpu as pltpu

D_MODEL = 1024
N_META = 16
POOL_WIDTH = 512
POOL_WINDOWS = (2, 4, 8, 16)
POOL_GROUP = POOL_WIDTH // len(POOL_WINDOWS)
N_HEADS = 8
HEAD_DIM = 64
ATTN_WIDTH = N_HEADS * HEAD_DIM
RMS_EPS = 1e-6

LANES = 128
BLK = 256
BQ = 512
QB = BQ // BLK
assert QB == 2
TM = 512
NB = TM // BLK
TM_OUT = 1024
CAST_ROWS = 128
POOL_PAD = 8
assert POOL_WINDOWS == (2, 4, 8, 16) and POOL_PAD >= POOL_WINDOWS[-1] // 2
V_ROWS = HEAD_DIM + 16
N_SPLIT = 3
MASKED = -1e30
SKEW = 2
LOG2E = 1.4426950408889634
VMEM_LIMIT = 56 * 1024 * 1024
COL_U, COL_ZP, COL_Q, COL_K, COL_V, COL_ZA, COL_F = (512 * i for i in range(7))
COL_G = COL_F + N_HEADS
N_IN = COL_G + 2 * D_MODEL

f32 = jnp.float32
bf16 = jnp.bfloat16


def _rmsnorm(x, g):
    return x * lax.rsqrt(jnp.mean(x * x, axis=-1, keepdims=True) + RMS_EPS) * g


def _sigmoid(x):
    return 1.0 / (1.0 + jnp.exp(-x))


def _log_sigmoid(x):
    return jnp.minimum(x, 0.0) - jnp.log1p(jnp.exp(-jnp.abs(x)))


def _dot(a, b):
    return jnp.dot(a, b, preferred_element_type=f32)


def _dot_nt(a, b):
    return lax.dot_general(a, b, (((1,), (1,)), ((), ())), preferred_element_type=f32)


def _lane_iota(shape):
    return lax.broadcasted_iota(jnp.int32, shape, 1)


def _decay_parts(logf3, rows):
    n = logf3.shape[0] // rows
    r = lax.broadcasted_iota(jnp.int32, (rows, rows), 0)
    c = lax.broadcasted_iota(jnp.int32, (rows, rows), 1)
    tri = (c <= r).astype(f32)
    side_by_side = jnp.concatenate([logf3[rows * i:rows * (i + 1)] for i in range(n)], axis=1)
    sums = jnp.dot(tri, side_by_side, precision=lax.Precision.HIGHEST, preferred_element_type=f32)
    beta = jnp.concatenate([sums[:, LANES * i:LANES * (i + 1)] for i in range(n)], axis=0)
    nb = beta * (-LOG2E)
    hi = nb.astype(bf16).astype(f32)
    mid = (nb - hi).astype(bf16).astype(f32)
    lo = (nb - hi - mid).astype(bf16).astype(f32)
    lane = _lane_iota(nb.shape)
    return beta, jnp.where(lane < N_HEADS, hi, jnp.where(lane < 2 * N_HEADS, mid, lo))


def _augmented_keys(kproj, parts, store):
    lane = _lane_iota((kproj.shape[0], LANES))
    for h in range(N_HEADS):
        slab = kproj[:, LANES * (h // 2):LANES * (h // 2 + 1)]
        if h % 2:
            slab = pltpu.roll(slab, HEAD_DIM, axis=1)
        store(h, jnp.where(lane < HEAD_DIM, slab, pltpu.roll(parts, HEAD_DIM - h, axis=1)).astype(bf16))


def _cast_kernel(w_ref, o_ref):
    o_ref[...] = w_ref[0].astype(bf16)


def _pool_mapped(wu, pw_ref):
    return jnp.concatenate([_dot(wu[:, POOL_GROUP * g:POOL_GROUP * (g + 1)], pw_ref[g])
                            for g in range(len(POOL_WINDOWS))], axis=1).astype(bf16)


def _meta_kernel(meta_ref, g_ref, wu_ref, wk_ref, wv_ref, wf_ref, bf_ref, pw_ref,
                 umeta_ref, kmeta_ref, vmeta_ref, r0_ref):
    hn = _rmsnorm(meta_ref[...], g_ref[...]).astype(bf16)
    umeta_ref[...] = _dot(hn, _pool_mapped(wu_ref[...], pw_ref))
    vt = _dot_nt(wv_ref[...].T, hn)
    for h in range(N_HEADS):
        vmeta_ref[h, 0:HEAD_DIM, :] = vt[HEAD_DIM * h:HEAD_DIM * (h + 1), :].astype(bf16)
        vmeta_ref[h, HEAD_DIM:V_ROWS, :] = jnp.ones((V_ROWS - HEAD_DIM, N_META), bf16)
    logf3 = _log_sigmoid(_dot(hn, wf_ref[...]) + bf_ref[...])
    beta, parts = _decay_parts(logf3, N_META)

    def store(h, ka):
        kmeta_ref[h] = ka

    _augmented_keys(_dot(hn, wk_ref[...]), parts, store)
    r0_ref[...] = jnp.broadcast_to(beta[N_META - 1:N_META, :], r0_ref.shape)


def _proj_kernel(x_ref, g_ref, wa_ref, wf_ref, bf_ref, pw_ref, ps_ref, wup_ref,
                 umeta_ref, r0_ref,
                 mp_ref, ga_ref, sz_ref, qT_ref, vT_ref, kaug_ref, r_ref,
                 uext_ref, lvl_a_ref, lvl_b_ref, rcarry_ref, wqvT_ref, wg_ref, wu_ref):
    t = pl.program_id(1)

    @pl.when((pl.program_id(0) == 0) & (t == 0))
    def _():
        wqvT_ref[0:ATTN_WIDTH, :] = wa_ref[:, COL_Q:COL_K].T
        wqvT_ref[ATTN_WIDTH:2 * ATTN_WIDTH, :] = wa_ref[:, COL_V:COL_ZA].T
        wg_ref[...] = wa_ref[:, COL_G:COL_G + 2 * D_MODEL]
        wu_ref[...] = _pool_mapped(wa_ref[:, COL_U:COL_ZP], pw_ref)
    ext = N_META + TM
    body = slice(POOL_PAD, POOL_PAD + ext)

    @pl.when(t == 0)
    def _():
        uext_ref[0:POOL_PAD, :] = jnp.zeros((POOL_PAD, POOL_WIDTH), f32)
        lvl_a_ref[0:POOL_PAD, :] = jnp.zeros((POOL_PAD, lvl_a_ref.shape[1]), f32)
        lvl_b_ref[0:POOL_PAD, :] = jnp.zeros((POOL_PAD, lvl_b_ref.shape[1]), f32)
        uext_ref[POOL_PAD:POOL_PAD + N_META, :] = umeta_ref[...]
        rcarry_ref[...] = r0_ref[...]

    hn = _rmsnorm(x_ref[0], g_ref[...]).astype(bf16)

    gat = _dot(hn, wg_ref[:, D_MODEL:2 * D_MODEL])
    ga_ref[0] = _sigmoid(gat).astype(bf16)

    u = _dot(hn, wu_ref[...])
    zp = _dot(hn, wa_ref[:, COL_ZP:COL_Q])
    uext_ref[POOL_PAD + N_META:POOL_PAD + ext, :] = u

    def shifted_sum(ref, shift, cols):
        return ref[body, cols] + ref[POOL_PAD - shift:POOL_PAD - shift + ext, cols]

    g1, g2, g3 = (slice(POOL_GROUP * g, POOL_GROUP * (g + 1)) for g in range(3))
    sums = [shifted_sum(uext_ref, 1, g1)]
    lvl_a_ref[body, :] = shifted_sum(uext_ref, 1, slice(POOL_GROUP, POOL_WIDTH))
    sums.append(shifted_sum(lvl_a_ref, 2, g1))
    lvl_b_ref[body, :] = shifted_sum(lvl_a_ref, 2, slice(POOL_GROUP, 3 * POOL_GROUP))
    sums.append(shifted_sum(lvl_b_ref, 4, g1))
    lvl_a_ref[body, g1] = shifted_sum(lvl_b_ref, 4, g2)
    sums.append(shifted_sum(lvl_a_ref, 8, g1))
    uext_ref[POOL_PAD:POOL_PAD + N_META, :] = uext_ref[POOL_PAD + TM:POOL_PAD + ext, :]

    pooled = jnp.concatenate([sums[g][N_META:] * (1.0 / w) for g, w in enumerate(POOL_WINDOWS)], axis=1) - u
    y_pool = (pooled * ps_ref[...] * (zp * _sigmoid(zp))).astype(bf16)
    gp = _dot(hn, wg_ref[:, 0:D_MODEL])
    mp_ref[0] = (_sigmoid(gp) * _dot(y_pool, wup_ref[...])).astype(bf16)

    logf3 = _log_sigmoid(_dot(hn, wf_ref[...]) + bf_ref[...])
    r_ref[...] = jnp.zeros(r_ref.shape, f32)
    beta, parts = _decay_parts(logf3, BLK)
    for c in range(NB):
        r_ref[0, 0, c:c + 1, :] = rcarry_ref[0:1, :] * LOG2E
        rcarry_ref[...] = rcarry_ref[...] + beta[BLK * (c + 1) - 1:BLK * (c + 1), :]

    za = _dot(hn, wa_ref[:, COL_ZA:COL_F])
    sz_ref[0] = (za * _sigmoid(za)).astype(bf16)

    def store(h, ka):
        kaug_ref[0, h] = ka

    _augmented_keys(_dot(hn, wa_ref[:, COL_K:COL_V]), parts, store)

    qv = _dot_nt(wqvT_ref[...], hn)
    qT_ref[0] = (qv[0:ATTN_WIDTH] * (HEAD_DIM ** -0.5 * LOG2E)).astype(bf16)
    for c in range(NB):
        for h in range(N_HEADS):
            rows = slice(ATTN_WIDTH + HEAD_DIM * h, ATTN_WIDTH + HEAD_DIM * (h + 1))
            vT_ref[0, c, h, 0:HEAD_DIM, :] = qv[rows, BLK * c:BLK * (c + 1)].astype(bf16)
            vT_ref[0, c, h, HEAD_DIM:V_ROWS, :] = jnp.ones((V_ROWS - HEAD_DIM, BLK), bf16)


def _attn_kernel(r_ref, qT_ref, kaug_ref, vT_ref, kmeta_ref, vmeta_ref, o_ref,
                 qaug_ref, s_ref, p_ref, sm_ref, pm_ref, m_ref, mblk_ref, mmeta_ref, alpha_ref, acc_ref,
                 *, n_blocks):
    b = pl.program_id(0)
    t = pl.program_id(1)
    first = QB * t
    heads = range(N_HEADS)

    part_row = lax.broadcasted_iota(jnp.int32, (LANES - HEAD_DIM, BQ), 0)
    ones_rows = ((part_row % N_HEADS == 0) & (part_row < N_SPLIT * N_HEADS)).astype(bf16)
    for h in heads:
        qaug_ref[h, 0:HEAD_DIM, :] = qT_ref[0, HEAD_DIM * h:HEAD_DIM * (h + 1), :]
        qaug_ref[h, HEAD_DIM:LANES, :] = ones_rows

    def r_at(j, h):
        return r_ref[(b * n_blocks + j) * N_HEADS + h]

    def col_max(s):
        return jnp.max(s, axis=0, keepdims=True)

    def scores(j, h):
        return _dot(kaug_ref[0, h, pl.ds(pl.multiple_of(j * BLK, BLK), BLK), :], qaug_ref[h])

    row = lax.broadcasted_iota(jnp.int32, (BLK, BQ), 0)
    col = lax.broadcasted_iota(jnp.int32, (BLK, BQ), 1)

    def score_item(j, slot, h, diag=None, meta=False):
        s = scores(j, h)
        if diag is not None:
            s = jnp.where(row + BLK * diag <= col, s, MASKED)
        s_ref[slot, h] = s
        mblk_ref[slot, h] = col_max(s)
        if meta:
            sm = _dot(kmeta_ref[h], qaug_ref[h])
            sm_ref[h] = sm
            mmeta_ref[h] = col_max(sm)

    def softmax_item(j, slot, h, meta=False):
        ref = r_at(first, h)
        off = ref - r_at(j, h)
        m_old = m_ref[h]
        m_blk = mblk_ref[slot, h] + off
        if meta:
            m_blk = jnp.maximum(m_blk, mmeta_ref[h] + ref)
        m_new = jnp.maximum(m_old, m_blk)
        alpha_ref[h] = jnp.exp2(m_old - m_new)
        m_ref[h] = m_new
        p_ref[h] = jnp.exp2(s_ref[slot, h] - (m_new - off)).astype(bf16)
        if meta:
            pm_ref[h] = jnp.exp2(sm_ref[h] - (m_new - ref)).astype(bf16)

    def value_item(j, slot, h, meta=False):
        pv = _dot(vT_ref[0, j, h], p_ref[h])
        if meta:
            pv = pv + _dot(vmeta_ref[h], pm_ref[h])
        acc_ref[h] = acc_ref[h] * alpha_ref[h] + pv

    def run(score_blocks, finish_blocks):
        score_items = [(blk + (h,), kw) for blk, kw in score_blocks for h in heads]
        finish_items = [(blk + (h,), kw) for blk, kw in finish_blocks for h in heads]
        for i in range(max(len(score_items), len(finish_items) + SKEW + 1)):
            if i < len(score_items):
                args, kw = score_items[i]
                score_item(*args, **kw)
            if 0 <= i - SKEW - 1 < len(finish_items):
                args, kw = finish_items[i - SKEW - 1]
                value_item(*args, **kw)
            if 0 <= i - SKEW < len(finish_items):
                args, kw = finish_items[i - SKEW]
                softmax_item(*args, **kw)

    m_ref[...] = jnp.full(m_ref.shape, MASKED, f32)
    acc_ref[...] = jnp.zeros(acc_ref.shape, f32)

    @pl.when(first == 0)
    def _():
        run([((first, 0), dict(diag=0))], [])

    @pl.when(first > 0)
    def _():
        run([((0, 0), {})], [])

        def body(k, carry):
            run([((2 * k + 1, 1), {}), ((2 * k + 2, 0), {})],
                [((2 * k, 0), {}), ((2 * k + 1, 1), {})])
            return carry

        lax.fori_loop(0, t - 1, body, 0)
        run([((first - 1, 1), {}), ((first, 0), dict(diag=0))],
            [((first - 2, 0), {}), ((first - 1, 1), {})])

    run([((first + 1, 1), dict(diag=1, meta=True))],
        [((first, 0), {}), ((first + 1, 1), dict(meta=True))])

    for pair in range(N_HEADS // 2):
        halves = []
        for h in (2 * pair, 2 * pair + 1):
            a = acc_ref[h]
            halves.append(a[0:HEAD_DIM] * (1.0 / a[HEAD_DIM:HEAD_DIM + 1]))
        o_ref[0, :, LANES * pair:LANES * (pair + 1)] = jnp.concatenate(halves, axis=0).T.astype(bf16)


def _out_kernel(x_ref, o_ref, sz_ref, ga_ref, mp_ref, wua_ref, wout_ref, g_ref, out_ref):
    y_attn = (o_ref[0].astype(f32) * sz_ref[0].astype(f32)).astype(bf16)
    merged = mp_ref[0].astype(f32) + ga_ref[0].astype(f32) * _dot(y_attn, wua_ref[...])
    h_out = x_ref[0] + _dot(merged.astype(bf16), wout_ref[...])
    out_ref[0] = _rmsnorm(h_out, g_ref[...])


def _const(shape):
    return pl.BlockSpec(shape, lambda *_: (0,) * len(shape), pipeline_mode=pl.Buffered(1))


def kernel(x, meta_tokens, norm_g, w_in, b_forget, pool_w, pool_scale, w_up_pool, w_up_attn, w_out, final_norm_g):
    batch, seq, _ = x.shape
    n_tiles = seq // TM
    n_blocks = seq // BLK

    w_all = pl.pallas_call(
        _cast_kernel,
        grid=(D_MODEL // CAST_ROWS,),
        in_specs=[pl.BlockSpec((1, CAST_ROWS, N_IN), lambda i: (0, i, 0))],
        out_specs=pl.BlockSpec((CAST_ROWS, N_IN), lambda i: (i, 0)),
        out_shape=jax.ShapeDtypeStruct((D_MODEL, N_IN), bf16),
        compiler_params=pltpu.CompilerParams(dimension_semantics=("arbitrary",), vmem_limit_bytes=VMEM_LIMIT),
        name="cast_w_in",
    )(w_in)
    w_f = w_in[0][:, COL_F:COL_G]
    w_f3 = jnp.pad(jnp.tile(w_f, (1, N_SPLIT)), ((0, 0), (0, LANES - N_SPLIT * N_HEADS))).astype(bf16)
    b_f3 = jnp.pad(jnp.tile(b_forget[0], N_SPLIT), (0, LANES - N_SPLIT * N_HEADS)).reshape(1, LANES)
    g_in = norm_g[0].reshape(1, D_MODEL)
    g_out = final_norm_g.reshape(1, D_MODEL)
    pw = pool_w[0].astype(bf16)
    ps = pool_scale[0].reshape(1, POOL_WIDTH)
    w_upp = w_up_pool[0].astype(bf16)
    w_upa = w_up_attn[0].astype(bf16)
    w_o = w_out[0].astype(bf16)

    u_meta, k_meta, v_meta, r0 = pl.pallas_call(
        _meta_kernel,
        out_shape=(jax.ShapeDtypeStruct((N_META, POOL_WIDTH), f32),
                   jax.ShapeDtypeStruct((N_HEADS, N_META, LANES), bf16),
                   jax.ShapeDtypeStruct((N_HEADS, V_ROWS, N_META), bf16),
                   jax.ShapeDtypeStruct((8, LANES), f32)),
        grid=(1,),
        in_specs=[_const((N_META, D_MODEL)), _const((1, D_MODEL))]
        + [pl.BlockSpec((D_MODEL, 512), functools.partial(lambda c, i: (0, c), col // 512),
                        pipeline_mode=pl.Buffered(1)) for col in (COL_U, COL_K, COL_V)]
        + [_const((D_MODEL, LANES)), _const((1, LANES)), _const((len(POOL_WINDOWS), POOL_GROUP, POOL_GROUP))],
        out_specs=[_const((N_META, POOL_WIDTH)), _const((N_HEADS, N_META, LANES)),
                   _const((N_HEADS, V_ROWS, N_META)), _const((8, LANES))],
        compiler_params=pltpu.CompilerParams(vmem_limit_bytes=VMEM_LIMIT),
        name="meta_proj",
    )(meta_tokens, g_in, w_all, w_all, w_all, w_f3, b_f3, pw)

    mp, ga, sz, qT, vT, kaug, r_tiles = pl.pallas_call(
        _proj_kernel,
        grid=(batch, n_tiles),
        in_specs=[
            pl.BlockSpec((1, TM, D_MODEL), lambda b, t: (b, t, 0)),
            _const((1, D_MODEL)),
            _const((D_MODEL, N_IN)),
            _const((D_MODEL, LANES)),
            _const((1, LANES)),
            _const((len(POOL_WINDOWS), POOL_GROUP, POOL_GROUP)),
            _const((1, POOL_WIDTH)),
            _const((POOL_WIDTH, D_MODEL)),
            _const((N_META, POOL_WIDTH)),
            _const((8, LANES)),
        ],
        out_specs=[
            pl.BlockSpec((1, TM, D_MODEL), lambda b, t: (b, t, 0)),
            pl.BlockSpec((1, TM, D_MODEL), lambda b, t: (b, t, 0)),
            pl.BlockSpec((1, TM, ATTN_WIDTH), lambda b, t: (b, t, 0)),
            pl.BlockSpec((1, ATTN_WIDTH, TM), lambda b, t: (b, 0, t)),
            pl.BlockSpec((1, NB, N_HEADS, V_ROWS, BLK), lambda b, t: (b, t, 0, 0, 0)),
            pl.BlockSpec((1, N_HEADS, TM, LANES), lambda b, t: (b, 0, t, 0)),
            pl.BlockSpec((1, 1, 8, LANES), lambda b, t: (b, t, 0, 0)),
        ],
        out_shape=(
            jax.ShapeDtypeStruct((batch, seq, D_MODEL), bf16),
            jax.ShapeDtypeStruct((batch, seq, D_MODEL), bf16),
            jax.ShapeDtypeStruct((batch, seq, ATTN_WIDTH), bf16),
            jax.ShapeDtypeStruct((batch, ATTN_WIDTH, seq), bf16),
            jax.ShapeDtypeStruct((batch, n_blocks, N_HEADS, V_ROWS, BLK), bf16),
            jax.ShapeDtypeStruct((batch, N_HEADS, seq, LANES), bf16),
            jax.ShapeDtypeStruct((batch, n_tiles, 8, LANES), f32),
        ),
        scratch_shapes=[pltpu.VMEM((POOL_PAD + N_META + TM, POOL_WIDTH), f32),
                        pltpu.VMEM((POOL_PAD + N_META + TM, 3 * POOL_GROUP), f32),
                        pltpu.VMEM((POOL_PAD + N_META + TM, 2 * POOL_GROUP), f32),
                        pltpu.VMEM((8, LANES), f32),
                        pltpu.VMEM((2 * ATTN_WIDTH, D_MODEL), bf16),
                        pltpu.VMEM((D_MODEL, 2 * D_MODEL), bf16),
                        pltpu.VMEM((D_MODEL, POOL_WIDTH), bf16)],
        compiler_params=pltpu.CompilerParams(dimension_semantics=("arbitrary", "arbitrary"),
                                             vmem_limit_bytes=VMEM_LIMIT),
        name="in_proj",
    )(x, g_in, w_all, w_f3, b_f3, pw, ps, w_upp, u_meta, r0)

    r_blocks = r_tiles[:, :, :NB, :N_HEADS].reshape(batch * n_blocks * N_HEADS)

    o = pl.pallas_call(
        functools.partial(_attn_kernel, n_blocks=n_blocks),
        grid=(batch, seq // BQ),
        in_specs=[
            pl.BlockSpec(memory_space=pltpu.SMEM),
            pl.BlockSpec((1, ATTN_WIDTH, BQ), lambda b, t: (b, 0, t)),
            pl.BlockSpec((1, N_HEADS, seq, LANES), lambda b, t: (b, 0, 0, 0)),
            pl.BlockSpec((1, n_blocks, N_HEADS, V_ROWS, BLK), lambda b, t: (b, 0, 0, 0, 0)),
            _const((N_HEADS, N_META, LANES)),
            _const((N_HEADS, V_ROWS, N_META)),
        ],
        out_specs=pl.BlockSpec((1, BQ, ATTN_WIDTH), lambda b, t: (b, t, 0)),
        out_shape=jax.ShapeDtypeStruct((batch, seq, ATTN_WIDTH), bf16),
        scratch_shapes=[pltpu.VMEM((N_HEADS, LANES, BQ), bf16),
                        pltpu.VMEM((2, N_HEADS, BLK, BQ), f32),
                        pltpu.VMEM((N_HEADS, BLK, BQ), bf16),
                        pltpu.VMEM((N_HEADS, N_META, BQ), f32),
                        pltpu.VMEM((N_HEADS, N_META, BQ), bf16),
                        pltpu.VMEM((N_HEADS, 1, BQ), f32),
                        pltpu.VMEM((2, N_HEADS, 1, BQ), f32),
                        pltpu.VMEM((N_HEADS, 1, BQ), f32),
                        pltpu.VMEM((N_HEADS, 1, BQ), f32),
                        pltpu.VMEM((N_HEADS, V_ROWS, BQ), f32)],
        compiler_params=pltpu.CompilerParams(dimension_semantics=("arbitrary", "arbitrary"),
                                             vmem_limit_bytes=VMEM_LIMIT),
        name="attention",
    )(r_blocks, qT, kaug, vT, k_meta, v_meta)

    return pl.pallas_call(
        _out_kernel,
        grid=(batch, seq // TM_OUT),
        in_specs=[
            pl.BlockSpec((1, TM_OUT, D_MODEL), lambda b, t: (b, t, 0)),
            pl.BlockSpec((1, TM_OUT, ATTN_WIDTH), lambda b, t: (b, t, 0)),
            pl.BlockSpec((1, TM_OUT, ATTN_WIDTH), lambda b, t: (b, t, 0)),
            pl.BlockSpec((1, TM_OUT, D_MODEL), lambda b, t: (b, t, 0)),
            pl.BlockSpec((1, TM_OUT, D_MODEL), lambda b, t: (b, t, 0)),
            _const((ATTN_WIDTH, D_MODEL)),
            _const((D_MODEL, D_MODEL)),
            _const((1, D_MODEL)),
        ],
        out_specs=pl.BlockSpec((1, TM_OUT, D_MODEL), lambda b, t: (b, t, 0)),
        out_shape=jax.ShapeDtypeStruct((batch, seq, D_MODEL), f32),
        compiler_params=pltpu.CompilerParams(dimension_semantics=("arbitrary", "arbitrary"),
                                             vmem_limit_bytes=VMEM_LIMIT),
        name="out_proj",
    )(x, o, sz, ga, mp, w_upa, w_o, g_out)
```

```python
import functools

import jax
import jax.numpy as jnp
from jax import lax
from jax.experimental import pallas as pl
from jax.experimental.pallas import tpu as pltpu

D_MODEL = 1024
N_META = 16
POOL_WIDTH = 512
POOL_WINDOWS = (2, 4, 8, 16)
POOL_GROUP = POOL_WIDTH // len(POOL_WINDOWS)
N_HEADS = 8
HEAD_DIM = 64
ATTN_WIDTH = N_HEADS * HEAD_DIM
RMS_EPS = 1e-6

LANES = 128
BLK = 256
BQ = 512
QB = BQ // BLK
assert QB == 2
TM = 512
NB = TM // BLK
TM_OUT = 1024
POOL_PAD = 8
assert POOL_WINDOWS == (2, 4, 8, 16) and POOL_PAD >= POOL_WINDOWS[-1] // 2
V_ROWS = HEAD_DIM + 16
N_SPLIT = 3
MASKED = -1e30
SKEW = 2
LOG2E = 1.4426950408889634
VMEM_LIMIT = 56 * 1024 * 1024
COL_U, COL_ZP, COL_Q, COL_K, COL_V, COL_ZA, COL_F = (512 * i for i in range(7))
COL_G = COL_F + N_HEADS
N_IN = COL_G + 2 * D_MODEL

f32 = jnp.float32
bf16 = jnp.bfloat16


def _rmsnorm(x, g):
    return x * lax.rsqrt(jnp.mean(x * x, axis=-1, keepdims=True) + RMS_EPS) * g


def _sigmoid(x):
    return 1.0 / (1.0 + jnp.exp(-x))


def _log_sigmoid(x):
    return jnp.minimum(x, 0.0) - jnp.log1p(jnp.exp(-jnp.abs(x)))


def _dot(a, b):
    return jnp.dot(a, b, preferred_element_type=f32)


def _dot_nt(a, b):
    return lax.dot_general(a, b, (((1,), (1,)), ((), ())), preferred_element_type=f32)


def _lane_iota(shape):
    return lax.broadcasted_iota(jnp.int32, shape, 1)


def _decay_parts(logf3, rows):
    n = logf3.shape[0] // rows
    r = lax.broadcasted_iota(jnp.int32, (rows, rows), 0)
    c = lax.broadcasted_iota(jnp.int32, (rows, rows), 1)
    tri = (c <= r).astype(bf16)
    side_by_side = jnp.concatenate([logf3[rows * i:rows * (i + 1)] for i in range(n)], axis=1)
    sums, rest = None, side_by_side
    for _ in range(N_SPLIT):
        piece = rest.astype(bf16)
        rest = rest - piece.astype(f32)
        sums = _dot(tri, piece) if sums is None else sums + _dot(tri, piece)
    beta = jnp.concatenate([sums[:, LANES * i:LANES * (i + 1)] for i in range(n)], axis=0)
    nb = beta * (-LOG2E)
    hi = nb.astype(bf16).astype(f32)
    mid = (nb - hi).astype(bf16).astype(f32)
    lo = (nb - hi - mid).astype(bf16).astype(f32)
    lane = _lane_iota(nb.shape)
    return beta, jnp.where(lane < N_HEADS, hi, jnp.where(lane < 2 * N_HEADS, mid, lo))


def _augmented_keys(kproj, parts, store):
    lane = _lane_iota((kproj.shape[0], LANES))
    for h in range(N_HEADS):
        slab = kproj[:, LANES * (h // 2):LANES * (h // 2 + 1)]
        if h % 2:
            slab = pltpu.roll(slab, HEAD_DIM, axis=1)
        store(h, jnp.where(lane < HEAD_DIM, slab, pltpu.roll(parts, HEAD_DIM - h, axis=1)).astype(bf16))


def _pool_mapped(wu, pw_ref):
    return jnp.concatenate([_dot(wu[:, POOL_GROUP * g:POOL_GROUP * (g + 1)], pw_ref[g])
                            for g in range(len(POOL_WINDOWS))], axis=1).astype(bf16)


def _meta_kernel(meta_ref, g_ref, wu_ref, wk_ref, wv_ref, wf_ref, bf_ref, pw_ref,
                 umeta_ref, kmeta_ref, vmeta_ref, r0_ref):
    hn = _rmsnorm(meta_ref[...], g_ref[...]).astype(bf16)
    umeta_ref[...] = _dot(hn, _pool_mapped(wu_ref[...], pw_ref))
    vt = _dot_nt(wv_ref[...].T, hn)
    for h in range(N_HEADS):
        vmeta_ref[h, 0:HEAD_DIM, :] = vt[HEAD_DIM * h:HEAD_DIM * (h + 1), :].astype(bf16)
        vmeta_ref[h, HEAD_DIM:V_ROWS, :] = jnp.ones((V_ROWS - HEAD_DIM, N_META), bf16)
    logf3 = _log_sigmoid(_dot(hn, wf_ref[...]) + bf_ref[...])
    beta, parts = _decay_parts(logf3, N_META)

    def store(h, ka):
        kmeta_ref[h] = ka

    _augmented_keys(_dot(hn, wk_ref[...]), parts, store)
    r0_ref[...] = jnp.broadcast_to(beta[N_META - 1:N_META, :], r0_ref.shape)


def _proj_kernel(x_ref, g_ref, wa_ref, wf_ref, bf_ref, pw_ref, ps_ref, wup_ref,
                 umeta_ref, r0_ref,
                 mp_ref, ga_ref, sz_ref, qT_ref, vT_ref, kaug_ref, r_ref,
                 uext_ref, lvl_a_ref, lvl_b_ref, rcarry_ref, wqvT_ref, wg_ref, wu_ref):
    t = pl.program_id(1)

    @pl.when((pl.program_id(0) == 0) & (t == 0))
    def _():
        wqvT_ref[0:ATTN_WIDTH, :] = wa_ref[:, COL_Q:COL_K].T
        wqvT_ref[ATTN_WIDTH:2 * ATTN_WIDTH, :] = wa_ref[:, COL_V:COL_ZA].T
        wg_ref[...] = wa_ref[:, COL_G:COL_G + 2 * D_MODEL]
        wu_ref[...] = _pool_mapped(wa_ref[:, COL_U:COL_ZP], pw_ref)
    ext = N_META + TM
    body = slice(POOL_PAD, POOL_PAD + ext)

    @pl.when(t == 0)
    def _():
        uext_ref[0:POOL_PAD, :] = jnp.zeros((POOL_PAD, POOL_WIDTH), f32)
        lvl_a_ref[0:POOL_PAD, :] = jnp.zeros((POOL_PAD, lvl_a_ref.shape[1]), f32)
        lvl_b_ref[0:POOL_PAD, :] = jnp.zeros((POOL_PAD, lvl_b_ref.shape[1]), f32)
        uext_ref[POOL_PAD:POOL_PAD + N_META, :] = umeta_ref[...]
        rcarry_ref[...] = r0_ref[...]

    hn = _rmsnorm(x_ref[0], g_ref[...]).astype(bf16)

    gat = _dot(hn, wg_ref[:, D_MODEL:2 * D_MODEL])
    ga_ref[0] = _sigmoid(gat).astype(bf16)

    u = _dot(hn, wu_ref[...])
    zp = _dot(hn, wa_ref[:, COL_ZP:COL_Q])
    uext_ref[POOL_PAD + N_META:POOL_PAD + ext, :] = u

    def shifted_sum(ref, shift, cols):
        return ref[body, cols] + ref[POOL_PAD - shift:POOL_PAD - shift + ext, cols]

    g1, g2, g3 = (slice(POOL_GROUP * g, POOL_GROUP * (g + 1)) for g in range(3))
    sums = [shifted_sum(uext_ref, 1, g1)]
    lvl_a_ref[body, :] = shifted_sum(uext_ref, 1, slice(POOL_GROUP, POOL_WIDTH))
    sums.append(shifted_sum(lvl_a_ref, 2, g1))
    lvl_b_ref[body, :] = shifted_sum(lvl_a_ref, 2, slice(POOL_GROUP, 3 * POOL_GROUP))
    sums.append(shifted_sum(lvl_b_ref, 4, g1))
    lvl_a_ref[body, g1] = shifted_sum(lvl_b_ref, 4, g2)
    sums.append(shifted_sum(lvl_a_ref, 8, g1))
    uext_ref[POOL_PAD:POOL_PAD + N_META, :] = uext_ref[POOL_PAD + TM:POOL_PAD + ext, :]

    pooled = jnp.concatenate([sums[g][N_META:] * (1.0 / w) for g, w in enumerate(POOL_WINDOWS)], axis=1) - u
    y_pool = (pooled * ps_ref[...] * (zp * _sigmoid(zp))).astype(bf16)
    gp = _dot(hn, wg_ref[:, 0:D_MODEL])
    mp_ref[0] = (_sigmoid(gp) * _dot(y_pool, wup_ref[...])).astype(bf16)

    logf3 = _log_sigmoid(_dot(hn, wf_ref[...]) + bf_ref[...])
    r_ref[...] = jnp.zeros(r_ref.shape, f32)
    beta, parts = _decay_parts(logf3, BLK)
    for c in range(NB):
        r_ref[0, 0, c:c + 1, :] = rcarry_ref[0:1, :] * LOG2E
        rcarry_ref[...] = rcarry_ref[...] + beta[BLK * (c + 1) - 1:BLK * (c + 1), :]

    za = _dot(hn, wa_ref[:, COL_ZA:COL_F])
    sz_ref[0] = (za * _sigmoid(za)).astype(bf16)

    def store(h, ka):
        kaug_ref[0, h] = ka

    _augmented_keys(_dot(hn, wa_ref[:, COL_K:COL_V]), parts, store)

    qv = _dot_nt(wqvT_ref[...], hn)
    qT_ref[0] = (qv[0:ATTN_WIDTH] * (HEAD_DIM ** -0.5 * LOG2E)).astype(bf16)
    for c in range(NB):
        for h in range(N_HEADS):
            rows = slice(ATTN_WIDTH + HEAD_DIM * h, ATTN_WIDTH + HEAD_DIM * (h + 1))
            vT_ref[0, c, h, 0:HEAD_DIM, :] = qv[rows, BLK * c:BLK * (c + 1)].astype(bf16)
            vT_ref[0, c, h, HEAD_DIM:V_ROWS, :] = jnp.ones((V_ROWS - HEAD_DIM, BLK), bf16)


def _attn_kernel(r_ref, qT_ref, kaug_ref, vT_ref, kmeta_ref, vmeta_ref, o_ref,
                 qaug_ref, s_ref, p_ref, sm_ref, pm_ref, m_ref, mblk_ref, mmeta_ref, alpha_ref, acc_ref,
                 *, n_blocks):
    b = pl.program_id(0)
    t = pl.program_id(1)
    first = QB * t
    heads = range(N_HEADS)

    part_row = lax.broadcasted_iota(jnp.int32, (LANES - HEAD_DIM, BQ), 0)
    ones_rows = ((part_row % N_HEADS == 0) & (part_row < N_SPLIT * N_HEADS)).astype(bf16)
    for h in heads:
        qaug_ref[h, 0:HEAD_DIM, :] = qT_ref[0, HEAD_DIM * h:HEAD_DIM * (h + 1), :]
        qaug_ref[h, HEAD_DIM:LANES, :] = ones_rows

    def r_at(j, h):
        return r_ref[(b * n_blocks + j) * N_HEADS + h]

    def col_max(s):
        return jnp.max(s, axis=0, keepdims=True)

    def scores(j, h):
        return _dot(kaug_ref[0, h, pl.ds(pl.multiple_of(j * BLK, BLK), BLK), :], qaug_ref[h])

    row = lax.broadcasted_iota(jnp.int32, (BLK, BQ), 0)
    col = lax.broadcasted_iota(jnp.int32, (BLK, BQ), 1)

    def score_item(j, slot, h, diag=None, meta=False):
        s = scores(j, h)
        if diag is not None:
            s = jnp.where(row + BLK * diag <= col, s, MASKED)
        s_ref[slot, h] = s
        mblk_ref[slot, h] = col_max(s)
        if meta:
            sm = _dot(kmeta_ref[h], qaug_ref[h])
            sm_ref[h] = sm
            mmeta_ref[h] = col_max(sm)

    def softmax_item(j, slot, h, meta=False):
        ref = r_at(first, h)
        off = ref - r_at(j, h)
        m_old = m_ref[h]
        m_blk = mblk_ref[slot, h] + off
        if meta:
            m_blk = jnp.maximum(m_blk, mmeta_ref[h] + ref)
        m_new = jnp.maximum(m_old, m_blk)
        alpha_ref[h] = jnp.exp2(m_old - m_new)
        m_ref[h] = m_new
        p_ref[h] = jnp.exp2(s_ref[slot, h] - (m_new - off)).astype(bf16)
        if meta:
            pm_ref[h] = jnp.exp2(sm_ref[h] - (m_new - ref)).astype(bf16)

    def value_item(j, slot, h, meta=False):
        pv = _dot(vT_ref[0, j, h], p_ref[h])
        if meta:
            pv = pv + _dot(vmeta_ref[h], pm_ref[h])
        acc_ref[h] = acc_ref[h] * alpha_ref[h] + pv

    def run(score_blocks, finish_blocks):
        score_items = [(blk + (h,), kw) for blk, kw in score_blocks for h in heads]
        finish_items = [(blk + (h,), kw) for blk, kw in finish_blocks for h in heads]
        for i in range(max(len(score_items), len(finish_items) + SKEW + 1)):
            if i < len(score_items):
                args, kw = score_items[i]
                score_item(*args, **kw)
            if 0 <= i - SKEW - 1 < len(finish_items):
                args, kw = finish_items[i - SKEW - 1]
                value_item(*args, **kw)
            if 0 <= i - SKEW < len(finish_items):
                args, kw = finish_items[i - SKEW]
                softmax_item(*args, **kw)

    m_ref[...] = jnp.full(m_ref.shape, MASKED, f32)
    acc_ref[...] = jnp.zeros(acc_ref.shape, f32)

    @pl.when(first == 0)
    def _():
        run([((first, 0), dict(diag=0))], [])

    @pl.when(first > 0)
    def _():
        run([((0, 0), {})], [])

        def body(k, carry):
            run([((2 * k + 1, 1), {}), ((2 * k + 2, 0), {})],
                [((2 * k, 0), {}), ((2 * k + 1, 1), {})])
            return carry

        lax.fori_loop(0, t - 1, body, 0)
        run([((first - 1, 1), {}), ((first, 0), dict(diag=0))],
            [((first - 2, 0), {}), ((first - 1, 1), {})])

    run([((first + 1, 1), dict(diag=1, meta=True))],
        [((first, 0), {}), ((first + 1, 1), dict(meta=True))])

    for pair in range(N_HEADS // 2):
        halves = []
        for h in (2 * pair, 2 * pair + 1):
            a = acc_ref[h]
            halves.append(a[0:HEAD_DIM] * (1.0 / a[HEAD_DIM:HEAD_DIM + 1]))
        o_ref[0, :, LANES * pair:LANES * (pair + 1)] = jnp.concatenate(halves, axis=0).T.astype(bf16)


def _out_kernel(x_ref, o_ref, sz_ref, ga_ref, mp_ref, wua_ref, wout_ref, g_ref, out_ref):
    y_attn = (o_ref[0].astype(f32) * sz_ref[0].astype(f32)).astype(bf16)
    merged = mp_ref[0].astype(f32) + ga_ref[0].astype(f32) * _dot(y_attn, wua_ref[...])
    h_out = x_ref[0] + _dot(merged.astype(bf16), wout_ref[...])
    out_ref[0] = _rmsnorm(h_out, g_ref[...])


def _const(shape):
    return pl.BlockSpec(shape, lambda *_: (0,) * len(shape), pipeline_mode=pl.Buffered(1))


def kernel(x, meta_tokens, norm_g, w_in, b_forget, pool_w, pool_scale, w_up_pool, w_up_attn, w_out, final_norm_g):
    batch, seq, _ = x.shape
    n_tiles = seq // TM
    n_blocks = seq // BLK

    w_all = w_in[0].astype(bf16)
    w_f = w_in[0][:, COL_F:COL_G]
    w_f3 = jnp.pad(jnp.tile(w_f, (1, N_SPLIT)), ((0, 0), (0, LANES - N_SPLIT * N_HEADS))).astype(bf16)
    b_f3 = jnp.pad(jnp.tile(b_forget[0], N_SPLIT), (0, LANES - N_SPLIT * N_HEADS)).reshape(1, LANES)
    g_in = norm_g[0].reshape(1, D_MODEL)
    g_out = final_norm_g.reshape(1, D_MODEL)
    pw = pool_w[0].astype(bf16)
    ps = pool_scale[0].reshape(1, POOL_WIDTH)
    w_upp = w_up_pool[0].astype(bf16)
    w_upa = w_up_attn[0].astype(bf16)
    w_o = w_out[0].astype(bf16)

    u_meta, k_meta, v_meta, r0 = pl.pallas_call(
        _meta_kernel,
        out_shape=(jax.ShapeDtypeStruct((N_META, POOL_WIDTH), f32),
                   jax.ShapeDtypeStruct((N_HEADS, N_META, LANES), bf16),
                   jax.ShapeDtypeStruct((N_HEADS, V_ROWS, N_META), bf16),
                   jax.ShapeDtypeStruct((8, LANES), f32)),
        grid=(1,),
        in_specs=[_const((N_META, D_MODEL)), _const((1, D_MODEL))]
        + [pl.BlockSpec((D_MODEL, 512), functools.partial(lambda c, i: (0, c), col // 512),
                        pipeline_mode=pl.Buffered(1)) for col in (COL_U, COL_K, COL_V)]
        + [_const((D_MODEL, LANES)), _const((1, LANES)), _const((len(POOL_WINDOWS), POOL_GROUP, POOL_GROUP))],
        out_specs=[_const((N_META, POOL_WIDTH)), _const((N_HEADS, N_META, LANES)),
                   _const((N_HEADS, V_ROWS, N_META)), _const((8, LANES))],
        compiler_params=pltpu.CompilerParams(vmem_limit_bytes=VMEM_LIMIT),
        name="meta_proj",
    )(meta_tokens, g_in, w_all, w_all, w_all, w_f3, b_f3, pw)

    mp, ga, sz, qT, vT, kaug, r_tiles = pl.pallas_call(
        _proj_kernel,
        grid=(batch, n_tiles),
        in_specs=[
            pl.BlockSpec((1, TM, D_MODEL), lambda b, t: (b, t, 0)),
            _const((1, D_MODEL)),
            _const((D_MODEL, N_IN)),
            _const((D_MODEL, LANES)),
            _const((1, LANES)),
            _const((len(POOL_WINDOWS), POOL_GROUP, POOL_GROUP)),
            _const((1, POOL_WIDTH)),
            _const((POOL_WIDTH, D_MODEL)),
            _const((N_META, POOL_WIDTH)),
            _const((8, LANES)),
        ],
        out_specs=[
            pl.BlockSpec((1, TM, D_MODEL), lambda b, t: (b, t, 0)),
            pl.BlockSpec((1, TM, D_MODEL), lambda b, t: (b, t, 0)),
            pl.BlockSpec((1, TM, ATTN_WIDTH), lambda b, t: (b, t, 0)),
            pl.BlockSpec((1, ATTN_WIDTH, TM), lambda b, t: (b, 0, t)),
            pl.BlockSpec((1, NB, N_HEADS, V_ROWS, BLK), lambda b, t: (b, t, 0, 0, 0)),
            pl.BlockSpec((1, N_HEADS, TM, LANES), lambda b, t: (b, 0, t, 0)),
            pl.BlockSpec((1, 1, 8, LANES), lambda b, t: (b, t, 0, 0)),
        ],
        out_shape=(
            jax.ShapeDtypeStruct((batch, seq, D_MODEL), bf16),
            jax.ShapeDtypeStruct((batch, seq, D_MODEL), bf16),
            jax.ShapeDtypeStruct((batch, seq, ATTN_WIDTH), bf16),
            jax.ShapeDtypeStruct((batch, ATTN_WIDTH, seq), bf16),
            jax.ShapeDtypeStruct((batch, n_blocks, N_HEADS, V_ROWS, BLK), bf16),
            jax.ShapeDtypeStruct((batch, N_HEADS, seq, LANES), bf16),
            jax.ShapeDtypeStruct((batch, n_tiles, 8, LANES), f32),
        ),
        scratch_shapes=[pltpu.VMEM((POOL_PAD + N_META + TM, POOL_WIDTH), f32),
                        pltpu.VMEM((POOL_PAD + N_META + TM, 3 * POOL_GROUP), f32),
                        pltpu.VMEM((POOL_PAD + N_META + TM, 2 * POOL_GROUP), f32),
                        pltpu.VMEM((8, LANES), f32),
                        pltpu.VMEM((2 * ATTN_WIDTH, D_MODEL), bf16),
                        pltpu.VMEM((D_MODEL, 2 * D_MODEL), bf16),
                        pltpu.VMEM((D_MODEL, POOL_WIDTH), bf16)],
        compiler_params=pltpu.CompilerParams(dimension_semantics=("arbitrary", "arbitrary"),
                                             vmem_limit_bytes=VMEM_LIMIT),
        name="in_proj",
    )(x, g_in, w_all, w_f3, b_f3, pw, ps, w_upp, u_meta, r0)

    r_blocks = r_tiles[:, :, :NB, :N_HEADS].reshape(batch * n_blocks * N_HEADS)

    o = pl.pallas_call(
        functools.partial(_attn_kernel, n_blocks=n_blocks),
        grid=(batch, seq // BQ),
        in_specs=[
            pl.BlockSpec(memory_space=pltpu.SMEM),
            pl.BlockSpec((1, ATTN_WIDTH, BQ), lambda b, t: (b, 0, t)),
            pl.BlockSpec((1, N_HEADS, seq, LANES), lambda b, t: (b, 0, 0, 0)),
            pl.BlockSpec((1, n_blocks, N_HEADS, V_ROWS, BLK), lambda b, t: (b, 0, 0, 0, 0)),
            _const((N_HEADS, N_META, LANES)),
            _const((N_HEADS, V_ROWS, N_META)),
        ],
        out_specs=pl.BlockSpec((1, BQ, ATTN_WIDTH), lambda b, t: (b, t, 0)),
        out_shape=jax.ShapeDtypeStruct((batch, seq, ATTN_WIDTH), bf16),
        scratch_shapes=[pltpu.VMEM((N_HEADS, LANES, BQ), bf16),
                        pltpu.VMEM((2, N_HEADS, BLK, BQ), f32),
                        pltpu.VMEM((N_HEADS, BLK, BQ), bf16),
                        pltpu.VMEM((N_HEADS, N_META, BQ), f32),
                        pltpu.VMEM((N_HEADS, N_META, BQ), bf16),
                        pltpu.VMEM((N_HEADS, 1, BQ), f32),
                        pltpu.VMEM((2, N_HEADS, 1, BQ), f32),
                        pltpu.VMEM((N_HEADS, 1, BQ), f32),
                        pltpu.VMEM((N_HEADS, 1, BQ), f32),
                        pltpu.VMEM((N_HEADS, V_ROWS, BQ), f32)],
        compiler_params=pltpu.CompilerParams(dimension_semantics=("arbitrary", "arbitrary"),
                                             vmem_limit_bytes=VMEM_LIMIT),
        name="attention",
    )(r_blocks, qT, kaug, vT, k_meta, v_meta)

    return pl.pallas_call(
        _out_kernel,
        grid=(batch, seq // TM_OUT),
        in_specs=[
            pl.BlockSpec((1, TM_OUT, D_MODEL), lambda b, t: (b, t, 0)),
            pl.BlockSpec((1, TM_OUT, ATTN_WIDTH), lambda b, t: (b, t, 0)),
            pl.BlockSpec((1, TM_OUT, ATTN_WIDTH), lambda b, t: (b, t, 0)),
            pl.BlockSpec((1, TM_OUT, D_MODEL), lambda b, t: (b, t, 0)),
            pl.BlockSpec((1, TM_OUT, D_MODEL), lambda b, t: (b, t, 0)),
            _const((ATTN_WIDTH, D_MODEL)),
            _const((D_MODEL, D_MODEL)),
            _const((1, D_MODEL)),
        ],
        out_specs=pl.BlockSpec((1, TM_OUT, D_MODEL), lambda b, t: (b, t, 0)),
        out_shape=jax.ShapeDtypeStruct((batch, seq, D_MODEL), f32),
        compiler_params=pltpu.CompilerParams(dimension_semantics=("arbitrary", "arbitrary"),
                                             vmem_limit_bytes=VMEM_LIMIT),
        name="out_proj",
    )(x, o, sz, ga, mp, w_upa, w_o, g_out)
```

```python
import functools

import jax
import jax.numpy as jnp
from jax import lax
from jax.experimental import pallas as pl
from jax.experimental.pallas import tpu as pltpu

D_MODEL = 1024
N_META = 16
POOL_WIDTH = 512
POOL_WINDOWS = (2, 4, 8, 16)
POOL_GROUP = POOL_WIDTH // len(POOL_WINDOWS)
N_HEADS = 8
HEAD_DIM = 64
ATTN_WIDTH = N_HEADS * HEAD_DIM
RMS_EPS = 1e-6

LANES = 128
BLK = 256
BQ = 512
QB = BQ // BLK
assert QB == 2
TM = 512
NB = TM // BLK
TM_OUT = 1024
POOL_PAD = 8
assert POOL_WINDOWS == (2, 4, 8, 16) and POOL_PAD >= POOL_WINDOWS[-1] // 2
V_ROWS = HEAD_DIM + 16
N_SPLIT = 3
MASKED = -1e30
SKEW = 2
LOG2E = 1.4426950408889634
VMEM_LIMIT = 56 * 1024 * 1024
COL_U, COL_ZP, COL_Q, COL_K, COL_V, COL_ZA, COL_F = (512 * i for i in range(7))
COL_G = COL_F + N_HEADS
N_IN = COL_G + 2 * D_MODEL

f32 = jnp.float32
bf16 = jnp.bfloat16


def _rmsnorm(x, g):
    return x * lax.rsqrt(jnp.mean(x * x, axis=-1, keepdims=True) + RMS_EPS) * g


def _sigmoid(x):
    return 1.0 / (1.0 + jnp.exp(-x))


def _log_sigmoid(x):
    return jnp.minimum(x, 0.0) - jnp.log1p(jnp.exp(-jnp.abs(x)))


def _dot(a, b):
    return jnp.dot(a, b, preferred_element_type=f32)


def _dot_nt(a, b):
    return lax.dot_general(a, b, (((1,), (1,)), ((), ())), preferred_element_type=f32)


def _lane_iota(shape):
    return lax.broadcasted_iota(jnp.int32, shape, 1)


def _decay_parts(logf3, rows):
    n = logf3.shape[0] // rows
    r = lax.broadcasted_iota(jnp.int32, (rows, rows), 0)
    c = lax.broadcasted_iota(jnp.int32, (rows, rows), 1)
    tri = (c <= r).astype(f32)
    side_by_side = jnp.concatenate([logf3[rows * i:rows * (i + 1)] for i in range(n)], axis=1)
    sums = jnp.dot(tri, side_by_side, precision=lax.Precision.HIGHEST, preferred_element_type=f32)
    beta = jnp.concatenate([sums[:, LANES * i:LANES * (i + 1)] for i in range(n)], axis=0)
    nb = beta * (-LOG2E)
    hi = nb.astype(bf16).astype(f32)
    mid = (nb - hi).astype(bf16).astype(f32)
    lo = (nb - hi - mid).astype(bf16).astype(f32)
    lane = _lane_iota(nb.shape)
    return beta, jnp.where(lane < N_HEADS, hi, jnp.where(lane < 2 * N_HEADS, mid, lo))


def _augmented_keys(kproj, parts, store):
    lane = _lane_iota((kproj.shape[0], LANES))
    for h in range(N_HEADS):
        slab = kproj[:, LANES * (h // 2):LANES * (h // 2 + 1)]
        if h % 2:
            slab = pltpu.roll(slab, HEAD_DIM, axis=1)
        store(h, jnp.where(lane < HEAD_DIM, slab, pltpu.roll(parts, HEAD_DIM - h, axis=1)).astype(bf16))


def _pool_mapped(wu, pw_ref):
    return jnp.concatenate([_dot(wu[:, POOL_GROUP * g:POOL_GROUP * (g + 1)], pw_ref[g])
                            for g in range(len(POOL_WINDOWS))], axis=1).astype(bf16)


def _meta_kernel(meta_ref, g_ref, wu_ref, wk_ref, wv_ref, wf_ref, bf_ref, pw_ref,
                 umeta_ref, kmeta_ref, vmeta_ref, r0_ref):
    hn = _rmsnorm(meta_ref[...], g_ref[...]).astype(bf16)
    umeta_ref[...] = _dot(hn, _pool_mapped(wu_ref[...], pw_ref))
    vt = _dot_nt(wv_ref[...].T, hn)
    for h in range(N_HEADS):
        vmeta_ref[h, 0:HEAD_DIM, :] = vt[HEAD_DIM * h:HEAD_DIM * (h + 1), :].astype(bf16)
        vmeta_ref[h, HEAD_DIM:V_ROWS, :] = jnp.ones((V_ROWS - HEAD_DIM, N_META), bf16)
    logf3 = _log_sigmoid(_dot(hn, wf_ref[...]) + bf_ref[...])
    beta, parts = _decay_parts(logf3, N_META)

    def store(h, ka):
        kmeta_ref[h] = ka

    _augmented_keys(_dot(hn, wk_ref[...]), parts, store)
    r0_ref[...] = jnp.broadcast_to(beta[N_META - 1:N_META, :], r0_ref.shape)


def _proj_kernel(x_ref, g_ref, wa_ref, wf_ref, bf_ref, pw_ref, ps_ref, wup_ref,
                 umeta_ref, r0_ref,
                 mp_ref, ga_ref, sz_ref, qT_ref, vT_ref, kaug_ref, r_ref,
                 uext_ref, lvl_a_ref, lvl_b_ref, rcarry_ref, wqvT_ref, wg_ref, wu_ref):
    t = pl.program_id(1)

    @pl.when((pl.program_id(0) == 0) & (t == 0))
    def _():
        wqvT_ref[0:ATTN_WIDTH, :] = wa_ref[:, COL_Q:COL_K].T
        wqvT_ref[ATTN_WIDTH:2 * ATTN_WIDTH, :] = wa_ref[:, COL_V:COL_ZA].T
        wg_ref[...] = wa_ref[:, COL_G:COL_G + 2 * D_MODEL]
        wu_ref[...] = _pool_mapped(wa_ref[:, COL_U:COL_ZP], pw_ref)
    ext = N_META + TM
    body = slice(POOL_PAD, POOL_PAD + ext)

    @pl.when(t == 0)
    def _():
        uext_ref[0:POOL_PAD, :] = jnp.zeros((POOL_PAD, POOL_WIDTH), f32)
        lvl_a_ref[0:POOL_PAD, :] = jnp.zeros((POOL_PAD, lvl_a_ref.shape[1]), f32)
        lvl_b_ref[0:POOL_PAD, :] = jnp.zeros((POOL_PAD, lvl_b_ref.shape[1]), f32)
        uext_ref[POOL_PAD:POOL_PAD + N_META, :] = umeta_ref[...]
        rcarry_ref[...] = r0_ref[...]

    hn = _rmsnorm(x_ref[0], g_ref[...]).astype(bf16)

    gat = _dot(hn, wg_ref[:, D_MODEL:2 * D_MODEL])
    ga_ref[0] = _sigmoid(gat).astype(bf16)

    u = _dot(hn, wu_ref[...])
    zp = _dot(hn, wa_ref[:, COL_ZP:COL_Q])
    uext_ref[POOL_PAD + N_META:POOL_PAD + ext, :] = u

    def shifted_sum(ref, shift, cols):
        return ref[body, cols] + ref[POOL_PAD - shift:POOL_PAD - shift + ext, cols]

    g1, g2, g3 = (slice(POOL_GROUP * g, POOL_GROUP * (g + 1)) for g in range(3))
    sums = [shifted_sum(uext_ref, 1, g1)]
    lvl_a_ref[body, :] = shifted_sum(uext_ref, 1, slice(POOL_GROUP, POOL_WIDTH))
    sums.append(shifted_sum(lvl_a_ref, 2, g1))
    lvl_b_ref[body, :] = shifted_sum(lvl_a_ref, 2, slice(POOL_GROUP, 3 * POOL_GROUP))
    sums.append(shifted_sum(lvl_b_ref, 4, g1))
    lvl_a_ref[body, g1] = shifted_sum(lvl_b_ref, 4, g2)
    sums.append(shifted_sum(lvl_a_ref, 8, g1))
    uext_ref[POOL_PAD:POOL_PAD + N_META, :] = uext_ref[POOL_PAD + TM:POOL_PAD + ext, :]

    pooled = jnp.concatenate([sums[g][N_META:] * (1.0 / w) for g, w in enumerate(POOL_WINDOWS)], axis=1) - u
    y_pool = (pooled * ps_ref[...] * (zp * _sigmoid(zp))).astype(bf16)
    gp = _dot(hn, wg_ref[:, 0:D_MODEL])
    mp_ref[0] = (_sigmoid(gp) * _dot(y_pool, wup_ref[...])).astype(bf16)

    logf3 = _log_sigmoid(_dot(hn, wf_ref[...]) + bf_ref[...])
    r_ref[...] = jnp.zeros(r_ref.shape, f32)
    beta, parts = _decay_parts(logf3, BLK)
    for c in range(NB):
        r_ref[0, 0, c:c + 1, :] = rcarry_ref[0:1, :] * LOG2E
        rcarry_ref[...] = rcarry_ref[...] + beta[BLK * (c + 1) - 1:BLK * (c + 1), :]

    za = _dot(hn, wa_ref[:, COL_ZA:COL_F])
    sz_ref[0] = (za * _sigmoid(za)).astype(bf16)

    def store(h, ka):
        kaug_ref[0, h] = ka

    _augmented_keys(_dot(hn, wa_ref[:, COL_K:COL_V]), parts, store)

    qv = _dot_nt(wqvT_ref[...], hn)
    qT_ref[0] = (qv[0:ATTN_WIDTH] * (HEAD_DIM ** -0.5 * LOG2E)).astype(bf16)
    for c in range(NB):
        for h in range(N_HEADS):
            rows = slice(ATTN_WIDTH + HEAD_DIM * h, ATTN_WIDTH + HEAD_DIM * (h + 1))
            vT_ref[0, c, h, 0:HEAD_DIM, :] = qv[rows, BLK * c:BLK * (c + 1)].astype(bf16)
            vT_ref[0, c, h, HEAD_DIM:V_ROWS, :] = jnp.ones((V_ROWS - HEAD_DIM, BLK), bf16)


def _attn_kernel(r_ref, qT_ref, qnext_ref, kaug_ref, vT_ref, kmeta_ref, vmeta_ref, o_ref,
                 qaug_ref, qaug_next_ref, s_ref, p_ref, sm_ref, pm_ref, m_ref, mblk_ref, mmeta_ref, alpha_ref, acc_ref,
                 *, n_blocks):
    b = pl.program_id(0)
    t = pl.program_id(1)
    first = QB * t
    heads = range(N_HEADS)

    part_row = lax.broadcasted_iota(jnp.int32, (LANES - HEAD_DIM, BQ), 0)
    ones_rows = ((part_row % N_HEADS == 0) & (part_row < N_SPLIT * N_HEADS)).astype(bf16)
    for h in heads:
        for q_ref, aug_ref in ((qT_ref, qaug_ref), (qnext_ref, qaug_next_ref)):
            aug_ref[h, 0:HEAD_DIM, :] = q_ref[0, HEAD_DIM * h:HEAD_DIM * (h + 1), :]
            aug_ref[h, HEAD_DIM:LANES, :] = ones_rows

    def r_at(j, h):
        return r_ref[(b * n_blocks + j) * N_HEADS + h]

    def col_max(s):
        return jnp.max(s, axis=0, keepdims=True)

    def scores(j, h, q_ref):
        return _dot(kaug_ref[0, h, pl.ds(pl.multiple_of(j * BLK, BLK), BLK), :], q_ref[h])

    row = lax.broadcasted_iota(jnp.int32, (BLK, BQ), 0)
    col = lax.broadcasted_iota(jnp.int32, (BLK, BQ), 1)

    def score_item(j, slot, h, diag=None, meta=False, next_tile=False):
        s = scores(j, h, qaug_next_ref if next_tile else qaug_ref)
        if diag is not None:
            s = jnp.where(row + BLK * diag <= col, s, MASKED)
        s_ref[slot, h] = s
        mblk_ref[slot, h] = col_max(s)
        if meta:
            sm = _dot(kmeta_ref[h], qaug_ref[h])
            sm_ref[h] = sm
            mmeta_ref[h] = col_max(sm)

    def softmax_item(j, slot, h, meta=False):
        ref = r_at(first, h)
        off = ref - r_at(j, h)
        m_old = m_ref[h]
        m_blk = mblk_ref[slot, h] + off
        if meta:
            m_blk = jnp.maximum(m_blk, mmeta_ref[h] + ref)
        m_new = jnp.maximum(m_old, m_blk)
        alpha_ref[h] = jnp.exp2(m_old - m_new)
        m_ref[h] = m_new
        p_ref[h] = jnp.exp2(s_ref[slot, h] - (m_new - off)).astype(bf16)
        if meta:
            pm_ref[h] = jnp.exp2(sm_ref[h] - (m_new - ref)).astype(bf16)

    def value_item(j, slot, h, meta=False):
        pv = _dot(vT_ref[0, j, h], p_ref[h])
        if meta:
            pv = pv + _dot(vmeta_ref[h], pm_ref[h])
        acc_ref[h] = acc_ref[h] * alpha_ref[h] + pv

    def run(score_blocks, finish_blocks):
        score_items = [(blk + (h,), kw) for blk, kw in score_blocks for h in heads]
        finish_items = [(blk + (h,), kw) for blk, kw in finish_blocks for h in heads]
        for i in range(max(len(score_items), len(finish_items) + SKEW + 1)):
            if i < len(score_items):
                args, kw = score_items[i]
                score_item(*args, **kw)
            if 0 <= i - SKEW - 1 < len(finish_items):
                args, kw = finish_items[i - SKEW - 1]
                value_item(*args, **kw)
            if 0 <= i - SKEW < len(finish_items):
                args, kw = finish_items[i - SKEW]
                softmax_item(*args, **kw)

    m_ref[...] = jnp.full(m_ref.shape, MASKED, f32)
    acc_ref[...] = jnp.zeros(acc_ref.shape, f32)

    @pl.when(first == 0)
    def _():
        run([((first, 0), dict(diag=0))], [])

    @pl.when(first > 0)
    def _():
        def body(k, carry):
            run([((2 * k + 1, 1), {}), ((2 * k + 2, 0), {})],
                [((2 * k, 0), {}), ((2 * k + 1, 1), {})])
            return carry

        lax.fori_loop(0, t - 1, body, 0)
        run([((first - 1, 1), {}), ((first, 0), dict(diag=0))],
            [((first - 2, 0), {}), ((first - 1, 1), {})])

    run([((first + 1, 1), dict(diag=1, meta=True)), ((0, 0), dict(next_tile=True))],
        [((first, 0), {}), ((first + 1, 1), dict(meta=True))])

    for pair in range(N_HEADS // 2):
        halves = []
        for h in (2 * pair, 2 * pair + 1):
            a = acc_ref[h]
            halves.append(a[0:HEAD_DIM] * (1.0 / a[HEAD_DIM:HEAD_DIM + 1]))
        o_ref[0, :, LANES * pair:LANES * (pair + 1)] = jnp.concatenate(halves, axis=0).T.astype(bf16)


def _out_kernel(x_ref, o_ref, sz_ref, ga_ref, mp_ref, wua_ref, wout_ref, g_ref, out_ref):
    y_attn = (o_ref[0].astype(f32) * sz_ref[0].astype(f32)).astype(bf16)
    merged = mp_ref[0].astype(f32) + ga_ref[0].astype(f32) * _dot(y_attn, wua_ref[...])
    h_out = x_ref[0] + _dot(merged.astype(bf16), wout_ref[...])
    out_ref[0] = _rmsnorm(h_out, g_ref[...])


def _const(shape):
    return pl.BlockSpec(shape, lambda *_: (0,) * len(shape), pipeline_mode=pl.Buffered(1))


def kernel(x, meta_tokens, norm_g, w_in, b_forget, pool_w, pool_scale, w_up_pool, w_up_attn, w_out, final_norm_g):
    batch, seq, _ = x.shape
    n_tiles = seq // TM
    n_blocks = seq // BLK

    w_all = w_in[0].astype(bf16)
    w_f = w_in[0][:, COL_F:COL_G]
    w_f3 = jnp.pad(jnp.tile(w_f, (1, N_SPLIT)), ((0, 0), (0, LANES - N_SPLIT * N_HEADS))).astype(bf16)
    b_f3 = jnp.pad(jnp.tile(b_forget[0], N_SPLIT), (0, LANES - N_SPLIT * N_HEADS)).reshape(1, LANES)
    g_in = norm_g[0].reshape(1, D_MODEL)
    g_out = final_norm_g.reshape(1, D_MODEL)
    pw = pool_w[0].astype(bf16)
    ps = pool_scale[0].reshape(1, POOL_WIDTH)
    w_upp = w_up_pool[0].astype(bf16)
    w_upa = w_up_attn[0].astype(bf16)
    w_o = w_out[0].astype(bf16)

    u_meta, k_meta, v_meta, r0 = pl.pallas_call(
        _meta_kernel,
        out_shape=(jax.ShapeDtypeStruct((N_META, POOL_WIDTH), f32),
                   jax.ShapeDtypeStruct((N_HEADS, N_META, LANES), bf16),
                   jax.ShapeDtypeStruct((N_HEADS, V_ROWS, N_META), bf16),
                   jax.ShapeDtypeStruct((8, LANES), f32)),
        grid=(1,),
        in_specs=[_const((N_META, D_MODEL)), _const((1, D_MODEL))]
        + [pl.BlockSpec((D_MODEL, 512), functools.partial(lambda c, i: (0, c), col // 512),
                        pipeline_mode=pl.Buffered(1)) for col in (COL_U, COL_K, COL_V)]
        + [_const((D_MODEL, LANES)), _const((1, LANES)), _const((len(POOL_WINDOWS), POOL_GROUP, POOL_GROUP))],
        out_specs=[_const((N_META, POOL_WIDTH)), _const((N_HEADS, N_META, LANES)),
                   _const((N_HEADS, V_ROWS, N_META)), _const((8, LANES))],
        compiler_params=pltpu.CompilerParams(vmem_limit_bytes=VMEM_LIMIT),
        name="meta_proj",
    )(meta_tokens, g_in, w_all, w_all, w_all, w_f3, b_f3, pw)

    mp, ga, sz, qT, vT, kaug, r_tiles = pl.pallas_call(
        _proj_kernel,
        grid=(batch, n_tiles),
        in_specs=[
            pl.BlockSpec((1, TM, D_MODEL), lambda b, t: (b, t, 0)),
            _const((1, D_MODEL)),
            _const((D_MODEL, N_IN)),
            _const((D_MODEL, LANES)),
            _const((1, LANES)),
            _const((len(POOL_WINDOWS), POOL_GROUP, POOL_GROUP)),
            _const((1, POOL_WIDTH)),
            _const((POOL_WIDTH, D_MODEL)),
            _const((N_META, POOL_WIDTH)),
            _const((8, LANES)),
        ],
        out_specs=[
            pl.BlockSpec((1, TM, D_MODEL), lambda b, t: (b, t, 0)),
            pl.BlockSpec((1, TM, D_MODEL), lambda b, t: (b, t, 0)),
            pl.BlockSpec((1, TM, ATTN_WIDTH), lambda b, t: (b, t, 0)),
            pl.BlockSpec((1, ATTN_WIDTH, TM), lambda b, t: (b, 0, t)),
            pl.BlockSpec((1, NB, N_HEADS, V_ROWS, BLK), lambda b, t: (b, t, 0, 0, 0)),
            pl.BlockSpec((1, N_HEADS, TM, LANES), lambda b, t: (b, 0, t, 0)),
            pl.BlockSpec((1, 1, 8, LANES), lambda b, t: (b, t, 0, 0)),
        ],
        out_shape=(
            jax.ShapeDtypeStruct((batch, seq, D_MODEL), bf16),
            jax.ShapeDtypeStruct((batch, seq, D_MODEL), bf16),
            jax.ShapeDtypeStruct((batch, seq, ATTN_WIDTH), bf16),
            jax.ShapeDtypeStruct((batch, ATTN_WIDTH, seq), bf16),
            jax.ShapeDtypeStruct((batch, n_blocks, N_HEADS, V_ROWS, BLK), bf16),
            jax.ShapeDtypeStruct((batch, N_HEADS, seq, LANES), bf16),
            jax.ShapeDtypeStruct((batch, n_tiles, 8, LANES), f32),
        ),
        scratch_shapes=[pltpu.VMEM((POOL_PAD + N_META + TM, POOL_WIDTH), f32),
                        pltpu.VMEM((POOL_PAD + N_META + TM, 3 * POOL_GROUP), f32),
                        pltpu.VMEM((POOL_PAD + N_META + TM, 2 * POOL_GROUP), f32),
                        pltpu.VMEM((8, LANES), f32),
                        pltpu.VMEM((2 * ATTN_WIDTH, D_MODEL), bf16),
                        pltpu.VMEM((D_MODEL, 2 * D_MODEL), bf16),
                        pltpu.VMEM((D_MODEL, POOL_WIDTH), bf16)],
        compiler_params=pltpu.CompilerParams(dimension_semantics=("arbitrary", "arbitrary"),
                                             vmem_limit_bytes=VMEM_LIMIT),
        name="in_proj",
    )(x, g_in, w_all, w_f3, b_f3, pw, ps, w_upp, u_meta, r0)

    r_blocks = r_tiles[:, :, :NB, :N_HEADS].reshape(batch * n_blocks * N_HEADS)

    o = pl.pallas_call(
        functools.partial(_attn_kernel, n_blocks=n_blocks),
        grid=(batch, seq // BQ),
        in_specs=[
            pl.BlockSpec(memory_space=pltpu.SMEM),
            pl.BlockSpec((1, ATTN_WIDTH, BQ), lambda b, t: (b, 0, t)),
            pl.BlockSpec((1, ATTN_WIDTH, BQ), lambda b, t: (b, 0, jnp.minimum(t + 1, seq // BQ - 1))),
            pl.BlockSpec((1, N_HEADS, seq, LANES), lambda b, t: (b, 0, 0, 0)),
            pl.BlockSpec((1, n_blocks, N_HEADS, V_ROWS, BLK), lambda b, t: (b, 0, 0, 0, 0)),
            _const((N_HEADS, N_META, LANES)),
            _const((N_HEADS, V_ROWS, N_META)),
        ],
        out_specs=pl.BlockSpec((1, BQ, ATTN_WIDTH), lambda b, t: (b, t, 0)),
        out_shape=jax.ShapeDtypeStruct((batch, seq, ATTN_WIDTH), bf16),
        scratch_shapes=[pltpu.VMEM((N_HEADS, LANES, BQ), bf16),
                        pltpu.VMEM((N_HEADS, LANES, BQ), bf16),
                        pltpu.VMEM((2, N_HEADS, BLK, BQ), f32),
                        pltpu.VMEM((N_HEADS, BLK, BQ), bf16),
                        pltpu.VMEM((N_HEADS, N_META, BQ), f32),
                        pltpu.VMEM((N_HEADS, N_META, BQ), bf16),
                        pltpu.VMEM((N_HEADS, 1, BQ), f32),
                        pltpu.VMEM((2, N_HEADS, 1, BQ), f32),
                        pltpu.VMEM((N_HEADS, 1, BQ), f32),
                        pltpu.VMEM((N_HEADS, 1, BQ), f32),
                        pltpu.VMEM((N_HEADS, V_ROWS, BQ), f32)],
        compiler_params=pltpu.CompilerParams(dimension_semantics=("arbitrary", "arbitrary"),
                                             vmem_limit_bytes=VMEM_LIMIT),
        name="attention",
    )(r_blocks, qT, qT, kaug, vT, k_meta, v_meta)

    return pl.pallas_call(
        _out_kernel,
        grid=(batch, seq // TM_OUT),
        in_specs=[
            pl.BlockSpec((1, TM_OUT, D_MODEL), lambda b, t: (b, t, 0)),
            pl.BlockSpec((1, TM_OUT, ATTN_WIDTH), lambda b, t: (b, t, 0)),
            pl.BlockSpec((1, TM_OUT, ATTN_WIDTH), lambda b, t: (b, t, 0)),
            pl.BlockSpec((1, TM_OUT, D_MODEL), lambda b, t: (b, t, 0)),
            pl.BlockSpec((1, TM_OUT, D_MODEL), lambda b, t: (b, t, 0)),
            _const((ATTN_WIDTH, D_MODEL)),
            _const((D_MODEL, D_MODEL)),
            _const((1, D_MODEL)),
        ],
        out_specs=pl.BlockSpec((1, TM_OUT, D_MODEL), lambda b, t: (b, t, 0)),
        out_shape=jax.ShapeDtypeStruct((batch, seq, D_MODEL), f32),
        compiler_params=pltpu.CompilerParams(dimension_semantics=("arbitrary", "arbitrary"),
                                             vmem_limit_bytes=VMEM_LIMIT),
        name="out_proj",
    )(x, o, sz, ga, mp, w_upa, w_o, g_out)
```

```python
import functools

import jax
import jax.numpy as jnp
from jax import lax
from jax.experimental import pallas as pl
from jax.experimental.pallas import tpu as pltpu

D_MODEL = 1024
N_META = 16
POOL_WIDTH = 512
POOL_WINDOWS = (2, 4, 8, 16)
POOL_GROUP = POOL_WIDTH // len(POOL_WINDOWS)
N_HEADS = 8
HEAD_DIM = 64
ATTN_WIDTH = N_HEADS * HEAD_DIM
RMS_EPS = 1e-6

LANES = 128
BLK = 256
BQ = 512
QB = BQ // BLK
assert QB == 2
TM = 512
NB = TM // BLK
TM_OUT = 1024
POOL_PAD = 8
assert POOL_WINDOWS == (2, 4, 8, 16) and POOL_PAD >= POOL_WINDOWS[-1] // 2
V_ROWS = HEAD_DIM + 16
N_SPLIT = 3
MASKED = -1e30
SKEW = 2
LOG2E = 1.4426950408889634
VMEM_LIMIT = 56 * 1024 * 1024
COL_U, COL_ZP, COL_Q, COL_K, COL_V, COL_ZA, COL_F = (512 * i for i in range(7))
COL_G = COL_F + N_HEADS
N_IN = COL_G + 2 * D_MODEL

f32 = jnp.float32
bf16 = jnp.bfloat16


def _rmsnorm(x, g):
    return x * lax.rsqrt(jnp.mean(x * x, axis=-1, keepdims=True) + RMS_EPS) * g


def _sigmoid(x):
    return 1.0 / (1.0 + jnp.exp(-x))


def _log_sigmoid(x):
    return jnp.minimum(x, 0.0) - jnp.log1p(jnp.exp(-jnp.abs(x)))


def _dot(a, b):
    return jnp.dot(a, b, preferred_element_type=f32)


def _dot_nt(a, b):
    return lax.dot_general(a, b, (((1,), (1,)), ((), ())), preferred_element_type=f32)


def _lane_iota(shape):
    return lax.broadcasted_iota(jnp.int32, shape, 1)


def _decay_parts(logf3, rows):
    n = logf3.shape[0] // rows
    r = lax.broadcasted_iota(jnp.int32, (rows, rows), 0)
    c = lax.broadcasted_iota(jnp.int32, (rows, rows), 1)
    tri = (c <= r).astype(f32)
    side_by_side = jnp.concatenate([logf3[rows * i:rows * (i + 1)] for i in range(n)], axis=1)
    sums = jnp.dot(tri, side_by_side, precision=lax.Precision.HIGHEST, preferred_element_type=f32)
    beta = jnp.concatenate([sums[:, LANES * i:LANES * (i + 1)] for i in range(n)], axis=0)
    nb = beta * (-LOG2E)
    hi = nb.astype(bf16).astype(f32)
    mid = (nb - hi).astype(bf16).astype(f32)
    lo = (nb - hi - mid).astype(bf16).astype(f32)
    lane = _lane_iota(nb.shape)
    return beta, jnp.where(lane < N_HEADS, hi, jnp.where(lane < 2 * N_HEADS, mid, lo))


def _augmented_keys(kproj, parts, store):
    lane = _lane_iota((kproj.shape[0], LANES))
    for h in range(N_HEADS):
        slab = kproj[:, LANES * (h // 2):LANES * (h // 2 + 1)]
        if h % 2:
            slab = pltpu.roll(slab, HEAD_DIM, axis=1)
        store(h, jnp.where(lane < HEAD_DIM, slab, pltpu.roll(parts, HEAD_DIM - h, axis=1)).astype(bf16))


def _pool_mapped(wu, pw_ref):
    return jnp.concatenate([_dot(wu[:, POOL_GROUP * g:POOL_GROUP * (g + 1)], pw_ref[g])
                            for g in range(len(POOL_WINDOWS))], axis=1).astype(bf16)


def _meta_kernel(meta_ref, g_ref, wu_ref, wk_ref, wv_ref, wf_ref, bf_ref, pw_ref,
                 umeta_ref, kmeta_ref, vmeta_ref, r0_ref):
    hn = _rmsnorm(meta_ref[...], g_ref[...]).astype(bf16)
    umeta_ref[...] = _dot(hn, _pool_mapped(wu_ref[...], pw_ref))
    vt = _dot_nt(wv_ref[...].T, hn)
    for h in range(N_HEADS):
        vmeta_ref[h, 0:HEAD_DIM, :] = vt[HEAD_DIM * h:HEAD_DIM * (h + 1), :].astype(bf16)
        vmeta_ref[h, HEAD_DIM:V_ROWS, :] = jnp.ones((V_ROWS - HEAD_DIM, N_META), bf16)
    logf3 = _log_sigmoid(_dot(hn, wf_ref[...]) + bf_ref[...])
    beta, parts = _decay_parts(logf3, N_META)

    def store(h, ka):
        kmeta_ref[h] = ka

    _augmented_keys(_dot(hn, wk_ref[...]), parts, store)
    r0_ref[...] = jnp.broadcast_to(beta[N_META - 1:N_META, :], r0_ref.shape)


def _proj_kernel(x_ref, g_ref, wa_ref, wf_ref, bf_ref, pw_ref, ps_ref, wup_ref,
                 umeta_ref, r0_ref,
                 mp_ref, ga_ref, sz_ref, qT_ref, vT_ref, kaug_ref, r_ref,
                 uext_ref, lvl_a_ref, lvl_b_ref, rcarry_ref, wqvT_ref, wg_ref, wu_ref):
    t = pl.program_id(1)

    @pl.when((pl.program_id(0) == 0) & (t == 0))
    def _():
        wqvT_ref[0:ATTN_WIDTH, :] = wa_ref[:, COL_Q:COL_K].T
        wqvT_ref[ATTN_WIDTH:2 * ATTN_WIDTH, :] = wa_ref[:, COL_V:COL_ZA].T
        wg_ref[...] = wa_ref[:, COL_G:COL_G + 2 * D_MODEL]
        wu_ref[...] = _pool_mapped(wa_ref[:, COL_U:COL_ZP], pw_ref)
    ext = N_META + TM
    body = slice(POOL_PAD, POOL_PAD + ext)

    @pl.when(t == 0)
    def _():
        uext_ref[0:POOL_PAD, :] = jnp.zeros((POOL_PAD, POOL_WIDTH), f32)
        lvl_a_ref[0:POOL_PAD, :] = jnp.zeros((POOL_PAD, lvl_a_ref.shape[1]), f32)
        lvl_b_ref[0:POOL_PAD, :] = jnp.zeros((POOL_PAD, lvl_b_ref.shape[1]), f32)
        uext_ref[POOL_PAD:POOL_PAD + N_META, :] = umeta_ref[...]
        rcarry_ref[...] = r0_ref[...]

    hn = _rmsnorm(x_ref[0], g_ref[...]).astype(bf16)

    gat = _dot(hn, wg_ref[:, D_MODEL:2 * D_MODEL])
    ga_ref[0] = _sigmoid(gat).astype(bf16)

    u = _dot(hn, wu_ref[...])
    zp = _dot(hn, wa_ref[:, COL_ZP:COL_Q])
    uext_ref[POOL_PAD + N_META:POOL_PAD + ext, :] = u

    def shifted_sum(ref, shift, cols):
        return ref[body, cols] + ref[POOL_PAD - shift:POOL_PAD - shift + ext, cols]

    g1, g2, g3 = (slice(POOL_GROUP * g, POOL_GROUP * (g + 1)) for g in range(3))
    sums = [shifted_sum(uext_ref, 1, g1)]
    lvl_a_ref[body, :] = shifted_sum(uext_ref, 1, slice(POOL_GROUP, POOL_WIDTH))
    sums.append(shifted_sum(lvl_a_ref, 2, g1))
    lvl_b_ref[body, :] = shifted_sum(lvl_a_ref, 2, slice(POOL_GROUP, 3 * POOL_GROUP))
    sums.append(shifted_sum(lvl_b_ref, 4, g1))
    lvl_a_ref[body, g1] = shifted_sum(lvl_b_ref, 4, g2)
    sums.append(shifted_sum(lvl_a_ref, 8, g1))
    uext_ref[POOL_PAD:POOL_PAD + N_META, :] = uext_ref[POOL_PAD + TM:POOL_PAD + ext, :]

    pooled = jnp.concatenate([sums[g][N_META:] * (1.0 / w) for g, w in enumerate(POOL_WINDOWS)], axis=1) - u
    y_pool = (pooled * ps_ref[...] * (zp * _sigmoid(zp))).astype(bf16)
    gp = _dot(hn, wg_ref[:, 0:D_MODEL])
    mp_ref[0] = (_sigmoid(gp) * _dot(y_pool, wup_ref[...])).astype(bf16)

    logf3 = _log_sigmoid(_dot(hn, wf_ref[...]) + bf_ref[...])
    r_ref[...] = jnp.zeros(r_ref.shape, f32)
    beta, parts = _decay_parts(logf3, BLK)
    for c in range(NB):
        r_ref[0, 0, c:c + 1, :] = rcarry_ref[0:1, :] * LOG2E
        rcarry_ref[...] = rcarry_ref[...] + beta[BLK * (c + 1) - 1:BLK * (c + 1), :]

    za = _dot(hn, wa_ref[:, COL_ZA:COL_F])
    sz_ref[0] = (za * _sigmoid(za)).astype(bf16)

    def store(h, ka):
        kaug_ref[0, h] = ka

    _augmented_keys(_dot(hn, wa_ref[:, COL_K:COL_V]), parts, store)

    qv = _dot_nt(wqvT_ref[...], hn)
    qT_ref[0] = (qv[0:ATTN_WIDTH] * (HEAD_DIM ** -0.5 * LOG2E)).astype(bf16)
    for c in range(NB):
        for h in range(N_HEADS):
            rows = slice(ATTN_WIDTH + HEAD_DIM * h, ATTN_WIDTH + HEAD_DIM * (h + 1))
            vT_ref[0, c, h, 0:HEAD_DIM, :] = qv[rows, BLK * c:BLK * (c + 1)].astype(bf16)
            vT_ref[0, c, h, HEAD_DIM:V_ROWS, :] = jnp.ones((V_ROWS - HEAD_DIM, BLK), bf16)


def _attn_kernel(r_ref, qT_ref, qnext_ref, kaug_ref, vT_ref, kmeta_ref, vmeta_ref, sz_ref, o_ref,
                 qaug_ref, qaug_next_ref, s_ref, p_ref, sm_ref, pm_ref, m_ref, mblk_ref, mmeta_ref, alpha_ref, acc_ref,
                 *, n_blocks):
    b = pl.program_id(0)
    t = pl.program_id(1)
    first = QB * t
    heads = range(N_HEADS)

    part_row = lax.broadcasted_iota(jnp.int32, (LANES - HEAD_DIM, BQ), 0)
    ones_rows = ((part_row % N_HEADS == 0) & (part_row < N_SPLIT * N_HEADS)).astype(bf16)
    for h in heads:
        for q_ref, aug_ref in ((qT_ref, qaug_ref), (qnext_ref, qaug_next_ref)):
            aug_ref[h, 0:HEAD_DIM, :] = q_ref[0, HEAD_DIM * h:HEAD_DIM * (h + 1), :]
            aug_ref[h, HEAD_DIM:LANES, :] = ones_rows

    def r_at(j, h):
        return r_ref[(b * n_blocks + j) * N_HEADS + h]

    def col_max(s):
        return jnp.max(s, axis=0, keepdims=True)

    def scores(j, h, q_ref):
        return _dot(kaug_ref[0, h, pl.ds(pl.multiple_of(j * BLK, BLK), BLK), :], q_ref[h])

    row = lax.broadcasted_iota(jnp.int32, (BLK, BQ), 0)
    col = lax.broadcasted_iota(jnp.int32, (BLK, BQ), 1)

    def score_item(j, slot, h, diag=None, meta=False, next_tile=False):
        s = scores(j, h, qaug_next_ref if next_tile else qaug_ref)
        if diag is not None:
            s = jnp.where(row + BLK * diag <= col, s, MASKED)
        s_ref[slot, h] = s
        mblk_ref[slot, h] = col_max(s)
        if meta:
            sm = _dot(kmeta_ref[h], qaug_ref[h])
            sm_ref[h] = sm
            mmeta_ref[h] = col_max(sm)

    def softmax_item(j, slot, h, meta=False):
        ref = r_at(first, h)
        off = ref - r_at(j, h)
        m_old = m_ref[h]
        m_blk = mblk_ref[slot, h] + off
        if meta:
            m_blk = jnp.maximum(m_blk, mmeta_ref[h] + ref)
        m_new = jnp.maximum(m_old, m_blk)
        alpha_ref[h] = jnp.exp2(m_old - m_new)
        m_ref[h] = m_new
        p_ref[h] = jnp.exp2(s_ref[slot, h] - (m_new - off)).astype(bf16)
        if meta:
            pm_ref[h] = jnp.exp2(sm_ref[h] - (m_new - ref)).astype(bf16)

    def value_item(j, slot, h, meta=False):
        pv = _dot(vT_ref[0, j, h], p_ref[h])
        if meta:
            pv = pv + _dot(vmeta_ref[h], pm_ref[h])
        acc_ref[h] = acc_ref[h] * alpha_ref[h] + pv

    def run(score_blocks, finish_blocks):
        score_items = [(blk + (h,), kw) for blk, kw in score_blocks for h in heads]
        finish_items = [(blk + (h,), kw) for blk, kw in finish_blocks for h in heads]
        for i in range(max(len(score_items), len(finish_items) + SKEW + 1)):
            if i < len(score_items):
                args, kw = score_items[i]
                score_item(*args, **kw)
            if 0 <= i - SKEW - 1 < len(finish_items):
                args, kw = finish_items[i - SKEW - 1]
                value_item(*args, **kw)
            if 0 <= i - SKEW < len(finish_items):
                args, kw = finish_items[i - SKEW]
                softmax_item(*args, **kw)

    m_ref[...] = jnp.full(m_ref.shape, MASKED, f32)
    acc_ref[...] = jnp.zeros(acc_ref.shape, f32)

    @pl.when(first == 0)
    def _():
        run([((first, 0), dict(diag=0))], [])

    @pl.when(first > 0)
    def _():
        def body(k, carry):
            run([((2 * k + 1, 1), {}), ((2 * k + 2, 0), {})],
                [((2 * k, 0), {}), ((2 * k + 1, 1), {})])
            return carry

        lax.fori_loop(0, t - 1, body, 0)
        run([((first - 1, 1), {}), ((first, 0), dict(diag=0))],
            [((first - 2, 0), {}), ((first - 1, 1), {})])

    run([((first + 1, 1), dict(diag=1, meta=True)), ((0, 0), dict(next_tile=True))],
        [((first, 0), {}), ((first + 1, 1), dict(meta=True))])

    for pair in range(N_HEADS // 2):
        halves = []
        for h in (2 * pair, 2 * pair + 1):
            a = acc_ref[h]
            halves.append(a[0:HEAD_DIM] * (1.0 / a[HEAD_DIM:HEAD_DIM + 1]))
        cols = slice(LANES * pair, LANES * (pair + 1))
        o_ref[0, :, cols] = (jnp.concatenate(halves, axis=0).T * sz_ref[0, :, cols].astype(f32)).astype(bf16)


def _out_kernel(x_ref, y_ref, ga_ref, mp_ref, wua_ref, wout_ref, g_ref, out_ref):
    merged = mp_ref[0].astype(f32) + ga_ref[0].astype(f32) * _dot(y_ref[0], wua_ref[...])
    h_out = x_ref[0] + _dot(merged.astype(bf16), wout_ref[...])
    out_ref[0] = _rmsnorm(h_out, g_ref[...])


def _const(shape):
    return pl.BlockSpec(shape, lambda *_: (0,) * len(shape), pipeline_mode=pl.Buffered(1))


def kernel(x, meta_tokens, norm_g, w_in, b_forget, pool_w, pool_scale, w_up_pool, w_up_attn, w_out, final_norm_g):
    batch, seq, _ = x.shape
    n_tiles = seq // TM
    n_blocks = seq // BLK

    w_all = w_in[0].astype(bf16)
    w_f = w_in[0][:, COL_F:COL_G]
    w_f3 = jnp.pad(jnp.tile(w_f, (1, N_SPLIT)), ((0, 0), (0, LANES - N_SPLIT * N_HEADS))).astype(bf16)
    b_f3 = jnp.pad(jnp.tile(b_forget[0], N_SPLIT), (0, LANES - N_SPLIT * N_HEADS)).reshape(1, LANES)
    g_in = norm_g[0].reshape(1, D_MODEL)
    g_out = final_norm_g.reshape(1, D_MODEL)
    pw = pool_w[0].astype(bf16)
    ps = pool_scale[0].reshape(1, POOL_WIDTH)
    w_upp = w_up_pool[0].astype(bf16)
    w_upa = w_up_attn[0].astype(bf16)
    w_o = w_out[0].astype(bf16)

    u_meta, k_meta, v_meta, r0 = pl.pallas_call(
        _meta_kernel,
        out_shape=(jax.ShapeDtypeStruct((N_META, POOL_WIDTH), f32),
                   jax.ShapeDtypeStruct((N_HEADS, N_META, LANES), bf16),
                   jax.ShapeDtypeStruct((N_HEADS, V_ROWS, N_META), bf16),
                   jax.ShapeDtypeStruct((8, LANES), f32)),
        grid=(1,),
        in_specs=[_const((N_META, D_MODEL)), _const((1, D_MODEL))]
        + [pl.BlockSpec((D_MODEL, 512), functools.partial(lambda c, i: (0, c), col // 512),
                        pipeline_mode=pl.Buffered(1)) for col in (COL_U, COL_K, COL_V)]
        + [_const((D_MODEL, LANES)), _const((1, LANES)), _const((len(POOL_WINDOWS), POOL_GROUP, POOL_GROUP))],
        out_specs=[_const((N_META, POOL_WIDTH)), _const((N_HEADS, N_META, LANES)),
                   _const((N_HEADS, V_ROWS, N_META)), _const((8, LANES))],
        compiler_params=pltpu.CompilerParams(vmem_limit_bytes=VMEM_LIMIT),
        name="meta_proj",
    )(meta_tokens, g_in, w_all, w_all, w_all, w_f3, b_f3, pw)

    mp, ga, sz, qT, vT, kaug, r_tiles = pl.pallas_call(
        _proj_kernel,
        grid=(batch, n_tiles),
        in_specs=[
            pl.BlockSpec((1, TM, D_MODEL), lambda b, t: (b, t, 0)),
            _const((1, D_MODEL)),
            _const((D_MODEL, N_IN)),
            _const((D_MODEL, LANES)),
            _const((1, LANES)),
            _const((len(POOL_WINDOWS), POOL_GROUP, POOL_GROUP)),
            _const((1, POOL_WIDTH)),
            _const((POOL_WIDTH, D_MODEL)),
            _const((N_META, POOL_WIDTH)),
            _const((8, LANES)),
        ],
        out_specs=[
            pl.BlockSpec((1, TM, D_MODEL), lambda b, t: (b, t, 0)),
            pl.BlockSpec((1, TM, D_MODEL), lambda b, t: (b, t, 0)),
            pl.BlockSpec((1, TM, ATTN_WIDTH), lambda b, t: (b, t, 0)),
            pl.BlockSpec((1, ATTN_WIDTH, TM), lambda b, t: (b, 0, t)),
            pl.BlockSpec((1, NB, N_HEADS, V_ROWS, BLK), lambda b, t: (b, t, 0, 0, 0)),
            pl.BlockSpec((1, N_HEADS, TM, LANES), lambda b, t: (b, 0, t, 0)),
            pl.BlockSpec((1, 1, 8, LANES), lambda b, t: (b, t, 0, 0)),
        ],
        out_shape=(
            jax.ShapeDtypeStruct((batch, seq, D_MODEL), bf16),
            jax.ShapeDtypeStruct((batch, seq, D_MODEL), bf16),
            jax.ShapeDtypeStruct((batch, seq, ATTN_WIDTH), bf16),
            jax.ShapeDtypeStruct((batch, ATTN_WIDTH, seq), bf16),
            jax.ShapeDtypeStruct((batch, n_blocks, N_HEADS, V_ROWS, BLK), bf16),
            jax.ShapeDtypeStruct((batch, N_HEADS, seq, LANES), bf16),
            jax.ShapeDtypeStruct((batch, n_tiles, 8, LANES), f32),
        ),
        scratch_shapes=[pltpu.VMEM((POOL_PAD + N_META + TM, POOL_WIDTH), f32),
                        pltpu.VMEM((POOL_PAD + N_META + TM, 3 * POOL_GROUP), f32),
                        pltpu.VMEM((POOL_PAD + N_META + TM, 2 * POOL_GROUP), f32),
                        pltpu.VMEM((8, LANES), f32),
                        pltpu.VMEM((2 * ATTN_WIDTH, D_MODEL), bf16),
                        pltpu.VMEM((D_MODEL, 2 * D_MODEL), bf16),
                        pltpu.VMEM((D_MODEL, POOL_WIDTH), bf16)],
        compiler_params=pltpu.CompilerParams(dimension_semantics=("arbitrary", "arbitrary"),
                                             vmem_limit_bytes=VMEM_LIMIT),
        name="in_proj",
    )(x, g_in, w_all, w_f3, b_f3, pw, ps, w_upp, u_meta, r0)

    r_blocks = r_tiles[:, :, :NB, :N_HEADS].reshape(batch * n_blocks * N_HEADS)

    y_attn = pl.pallas_call(
        functools.partial(_attn_kernel, n_blocks=n_blocks),
        grid=(batch, seq // BQ),
        in_specs=[
            pl.BlockSpec(memory_space=pltpu.SMEM),
            pl.BlockSpec((1, ATTN_WIDTH, BQ), lambda b, t: (b, 0, t)),
            pl.BlockSpec((1, ATTN_WIDTH, BQ), lambda b, t: (b, 0, jnp.minimum(t + 1, seq // BQ - 1))),
            pl.BlockSpec((1, N_HEADS, seq, LANES), lambda b, t: (b, 0, 0, 0)),
            pl.BlockSpec((1, n_blocks, N_HEADS, V_ROWS, BLK), lambda b, t: (b, 0, 0, 0, 0)),
            _const((N_HEADS, N_META, LANES)),
            _const((N_HEADS, V_ROWS, N_META)),
            pl.BlockSpec((1, BQ, ATTN_WIDTH), lambda b, t: (b, t, 0)),
        ],
        out_specs=pl.BlockSpec((1, BQ, ATTN_WIDTH), lambda b, t: (b, t, 0)),
        out_shape=jax.ShapeDtypeStruct((batch, seq, ATTN_WIDTH), bf16),
        scratch_shapes=[pltpu.VMEM((N_HEADS, LANES, BQ), bf16),
                        pltpu.VMEM((N_HEADS, LANES, BQ), bf16),
                        pltpu.VMEM((2, N_HEADS, BLK, BQ), f32),
                        pltpu.VMEM((N_HEADS, BLK, BQ), bf16),
                        pltpu.VMEM((N_HEADS, N_META, BQ), f32),
                        pltpu.VMEM((N_HEADS, N_META, BQ), bf16),
                        pltpu.VMEM((N_HEADS, 1, BQ), f32),
                        pltpu.VMEM((2, N_HEADS, 1, BQ), f32),
                        pltpu.VMEM((N_HEADS, 1, BQ), f32),
                        pltpu.VMEM((N_HEADS, 1, BQ), f32),
                        pltpu.VMEM((N_HEADS, V_ROWS, BQ), f32)],
        compiler_params=pltpu.CompilerParams(dimension_semantics=("arbitrary", "arbitrary"),
                                             vmem_limit_bytes=VMEM_LIMIT),
        name="attention",
    )(r_blocks, qT, qT, kaug, vT, k_meta, v_meta, sz)

    return pl.pallas_call(
        _out_kernel,
        grid=(batch, seq // TM_OUT),
        in_specs=[
            pl.BlockSpec((1, TM_OUT, D_MODEL), lambda b, t: (b, t, 0)),
            pl.BlockSpec((1, TM_OUT, ATTN_WIDTH), lambda b, t: (b, t, 0)),
            pl.BlockSpec((1, TM_OUT, D_MODEL), lambda b, t: (b, t, 0)),
            pl.BlockSpec((1, TM_OUT, D_MODEL), lambda b, t: (b, t, 0)),
            _const((ATTN_WIDTH, D_MODEL)),
            _const((D_MODEL, D_MODEL)),
            _const((1, D_MODEL)),
        ],
        out_specs=pl.BlockSpec((1, TM_OUT, D_MODEL), lambda b, t: (b, t, 0)),
        out_shape=jax.ShapeDtypeStruct((batch, seq, D_MODEL), f32),
        compiler_params=pltpu.CompilerParams(dimension_semantics=("arbitrary", "arbitrary"),
                                             vmem_limit_bytes=VMEM_LIMIT),
        name="out_proj",
    )(x, y_attn, ga, mp, w_upa, w_o, g_out)
```

```python
import functools

import jax
import jax.numpy as jnp
from jax import lax
from jax.experimental import pallas as pl
from jax.experimental.pallas import tpu as pltpu

D_MODEL = 1024
N_META = 16
POOL_WIDTH = 512
POOL_WINDOWS = (2, 4, 8, 16)
POOL_GROUP = POOL_WIDTH // len(POOL_WINDOWS)
N_HEADS = 8
HEAD_DIM = 64
ATTN_WIDTH = N_HEADS * HEAD_DIM
RMS_EPS = 1e-6

LANES = 128
BLK = 256
BQ = 512
QB = BQ // BLK
assert QB == 2
TM = 512
NB = TM // BLK
TM_OUT = 1024
POOL_PAD = 8
assert POOL_WINDOWS == (2, 4, 8, 16) and POOL_PAD >= POOL_WINDOWS[-1] // 2
V_ROWS = HEAD_DIM + 16
N_SPLIT = 3
MASKED = -1e30
SKEW = 2
LOG2E = 1.4426950408889634
VMEM_LIMIT = 56 * 1024 * 1024
COL_U, COL_ZP, COL_Q, COL_K, COL_V, COL_ZA, COL_F = (512 * i for i in range(7))
COL_G = COL_F + N_HEADS
N_IN = COL_G + 2 * D_MODEL

f32 = jnp.float32
bf16 = jnp.bfloat16


def _rmsnorm(x, g):
    return x * lax.rsqrt(jnp.mean(x * x, axis=-1, keepdims=True) + RMS_EPS) * g


def _sigmoid(x):
    return 1.0 / (1.0 + jnp.exp(-x))


def _log_sigmoid(x):
    return jnp.minimum(x, 0.0) - jnp.log1p(jnp.exp(-jnp.abs(x)))


def _dot(a, b):
    return jnp.dot(a, b, preferred_element_type=f32)


def _dot_nt(a, b):
    return lax.dot_general(a, b, (((1,), (1,)), ((), ())), preferred_element_type=f32)


def _lane_iota(shape):
    return lax.broadcasted_iota(jnp.int32, shape, 1)


def _decay_parts(logf3, rows):
    n = logf3.shape[0] // rows
    r = lax.broadcasted_iota(jnp.int32, (rows, rows), 0)
    c = lax.broadcasted_iota(jnp.int32, (rows, rows), 1)
    tri = (c <= r).astype(f32)
    side_by_side = jnp.concatenate([logf3[rows * i:rows * (i + 1)] for i in range(n)], axis=1)
    sums = jnp.dot(tri, side_by_side, precision=lax.Precision.HIGHEST, preferred_element_type=f32)
    beta = jnp.concatenate([sums[:, LANES * i:LANES * (i + 1)] for i in range(n)], axis=0)
    nb = beta * (-LOG2E)
    hi = nb.astype(bf16).astype(f32)
    mid = (nb - hi).astype(bf16).astype(f32)
    lo = (nb - hi - mid).astype(bf16).astype(f32)
    lane = _lane_iota(nb.shape)
    return beta, jnp.where(lane < N_HEADS, hi, jnp.where(lane < 2 * N_HEADS, mid, lo))


def _augmented_keys(kproj, parts, store):
    lane = _lane_iota((kproj.shape[0], LANES))
    for h in range(N_HEADS):
        slab = kproj[:, LANES * (h // 2):LANES * (h // 2 + 1)]
        if h % 2:
            slab = pltpu.roll(slab, HEAD_DIM, axis=1)
        store(h, jnp.where(lane < HEAD_DIM, slab, pltpu.roll(parts, HEAD_DIM - h, axis=1)).astype(bf16))


def _pool_mapped(wu, pw_ref):
    return jnp.concatenate([_dot(wu[:, POOL_GROUP * g:POOL_GROUP * (g + 1)], pw_ref[g])
                            for g in range(len(POOL_WINDOWS))], axis=1).astype(bf16)


def _meta_kernel(meta_ref, g_ref, wu_ref, wk_ref, wv_ref, wf_ref, bf_ref, pw_ref,
                 umeta_ref, kmeta_ref, vmeta_ref, r0_ref):
    hn = _rmsnorm(meta_ref[...], g_ref[...]).astype(bf16)
    umeta_ref[...] = _dot(hn, _pool_mapped(wu_ref[...], pw_ref))
    vt = _dot_nt(wv_ref[...].T, hn)
    for h in range(N_HEADS):
        vmeta_ref[h, 0:HEAD_DIM, :] = vt[HEAD_DIM * h:HEAD_DIM * (h + 1), :].astype(bf16)
        vmeta_ref[h, HEAD_DIM:V_ROWS, :] = jnp.ones((V_ROWS - HEAD_DIM, N_META), bf16)
    logf3 = _log_sigmoid(_dot(hn, wf_ref[...]) + bf_ref[...])
    beta, parts = _decay_parts(logf3, N_META)

    def store(h, ka):
        kmeta_ref[h] = ka

    _augmented_keys(_dot(hn, wk_ref[...]), parts, store)
    r0_ref[...] = jnp.broadcast_to(beta[N_META - 1:N_META, :], r0_ref.shape)


def _proj_kernel(x_ref, g_ref, wa_ref, wf_ref, bf_ref, pw_ref, ps_ref, wup_ref,
                 umeta_ref, r0_ref,
                 mp_ref, ga_ref, sz_ref, qT_ref, vT_ref, kaug_ref, r_ref,
                 uext_ref, lvl_a_ref, lvl_b_ref, rcarry_ref, wqvT_ref, wg_ref, wu_ref):
    t = pl.program_id(1)

    @pl.when((pl.program_id(0) == 0) & (t == 0))
    def _():
        wqvT_ref[0:ATTN_WIDTH, :] = wa_ref[:, COL_Q:COL_K].T
        wqvT_ref[ATTN_WIDTH:2 * ATTN_WIDTH, :] = wa_ref[:, COL_V:COL_ZA].T
        wg_ref[...] = wa_ref[:, COL_G:COL_G + 2 * D_MODEL]
        wu_ref[...] = _pool_mapped(wa_ref[:, COL_U:COL_ZP], pw_ref)
    ext = N_META + TM
    body = slice(POOL_PAD, POOL_PAD + ext)

    @pl.when(t == 0)
    def _():
        uext_ref[0:POOL_PAD, :] = jnp.zeros((POOL_PAD, POOL_WIDTH), f32)
        lvl_a_ref[0:POOL_PAD, :] = jnp.zeros((POOL_PAD, lvl_a_ref.shape[1]), f32)
        lvl_b_ref[0:POOL_PAD, :] = jnp.zeros((POOL_PAD, lvl_b_ref.shape[1]), f32)
        uext_ref[POOL_PAD:POOL_PAD + N_META, :] = umeta_ref[...]
        rcarry_ref[...] = r0_ref[...]

    hn = _rmsnorm(x_ref[0], g_ref[...]).astype(bf16)

    gat = _dot(hn, wg_ref[:, D_MODEL:2 * D_MODEL])
    ga_ref[0] = _sigmoid(gat).astype(bf16)

    u = _dot(hn, wu_ref[...])
    zp = _dot(hn, wa_ref[:, COL_ZP:COL_Q])
    uext_ref[POOL_PAD + N_META:POOL_PAD + ext, :] = u

    def shifted_sum(ref, shift, cols):
        return ref[body, cols] + ref[POOL_PAD - shift:POOL_PAD - shift + ext, cols]

    g1, g2, g3 = (slice(POOL_GROUP * g, POOL_GROUP * (g + 1)) for g in range(3))
    sums = [shifted_sum(uext_ref, 1, g1)]
    lvl_a_ref[body, :] = shifted_sum(uext_ref, 1, slice(POOL_GROUP, POOL_WIDTH))
    sums.append(shifted_sum(lvl_a_ref, 2, g1))
    lvl_b_ref[body, :] = shifted_sum(lvl_a_ref, 2, slice(POOL_GROUP, 3 * POOL_GROUP))
    sums.append(shifted_sum(lvl_b_ref, 4, g1))
    lvl_a_ref[body, g1] = shifted_sum(lvl_b_ref, 4, g2)
    sums.append(shifted_sum(lvl_a_ref, 8, g1))
    uext_ref[POOL_PAD:POOL_PAD + N_META, :] = uext_ref[POOL_PAD + TM:POOL_PAD + ext, :]

    pooled = jnp.concatenate([sums[g][N_META:] * (1.0 / w) for g, w in enumerate(POOL_WINDOWS)], axis=1) - u
    y_pool = (pooled * ps_ref[...] * (zp * _sigmoid(zp))).astype(bf16)
    gp = _dot(hn, wg_ref[:, 0:D_MODEL])
    mp_ref[0] = (_sigmoid(gp) * _dot(y_pool, wup_ref[...])).astype(bf16)

    logf3 = _log_sigmoid(_dot(hn, wf_ref[...]) + bf_ref[...])
    r_ref[...] = jnp.zeros(r_ref.shape, f32)
    beta, parts = _decay_parts(logf3, BLK)
    for c in range(NB):
        r_ref[0, 0, c:c + 1, :] = rcarry_ref[0:1, :] * LOG2E
        rcarry_ref[...] = rcarry_ref[...] + beta[BLK * (c + 1) - 1:BLK * (c + 1), :]

    za = _dot(hn, wa_ref[:, COL_ZA:COL_F])
    sz_ref[0] = (za * _sigmoid(za)).astype(bf16)

    def store(h, ka):
        kaug_ref[0, h] = ka

    _augmented_keys(_dot(hn, wa_ref[:, COL_K:COL_V]), parts, store)

    qv = _dot_nt(wqvT_ref[...], hn)
    qT_ref[0] = (qv[0:ATTN_WIDTH] * (HEAD_DIM ** -0.5 * LOG2E)).astype(bf16)
    for c in range(NB):
        for h in range(N_HEADS):
            rows = slice(ATTN_WIDTH + HEAD_DIM * h, ATTN_WIDTH + HEAD_DIM * (h + 1))
            vT_ref[0, c, h, 0:HEAD_DIM, :] = qv[rows, BLK * c:BLK * (c + 1)].astype(bf16)
            vT_ref[0, c, h, HEAD_DIM:V_ROWS, :] = jnp.ones((V_ROWS - HEAD_DIM, BLK), bf16)


def _attn_kernel(r_ref, qT_ref, qnext_ref, kaug_ref, vT_ref, kmeta_ref, vmeta_ref, sz_ref, o_ref,
                 qaug_ref, qaug_next_ref, s_ref, p_ref, sm_ref, pm_ref, m_ref, mblk_ref, mmeta_ref, alpha_ref, acc_ref,
                 *, n_blocks):
    b = pl.program_id(0)
    t = pl.program_id(1)
    first = QB * t
    heads = range(N_HEADS)

    part_row = lax.broadcasted_iota(jnp.int32, (LANES - HEAD_DIM, BQ), 0)
    ones_rows = ((part_row % N_HEADS == 0) & (part_row < N_SPLIT * N_HEADS)).astype(bf16)
    for h in heads:
        for q_ref, aug_ref in ((qT_ref, qaug_ref), (qnext_ref, qaug_next_ref)):
            aug_ref[h, 0:HEAD_DIM, :] = q_ref[0, HEAD_DIM * h:HEAD_DIM * (h + 1), :]
            aug_ref[h, HEAD_DIM:LANES, :] = ones_rows

    def r_at(j, h):
        return r_ref[(b * n_blocks + j) * N_HEADS + h]

    def col_max(s):
        return jnp.max(s, axis=0, keepdims=True)

    key_index = functools.partial(lax.broadcasted_iota, jnp.int32, dimension=0)
    query_index = functools.partial(lax.broadcasted_iota, jnp.int32, dimension=1)

    def seen_by(diag):
        return slice(BLK * diag, BQ) if diag else slice(None)

    def score_item(j, slot, h, diag=None, meta=False, next_tile=False):
        q_ref = qaug_next_ref if next_tile else qaug_ref
        cols = seen_by(diag)
        s = _dot(kaug_ref[0, h, pl.ds(pl.multiple_of(j * BLK, BLK), BLK), :], q_ref[h, :, cols])
        if diag is not None:
            s = jnp.where(key_index(s.shape) <= query_index(s.shape), s, MASKED)
        s_ref[slot, h, :, cols] = s
        unseen = [jnp.full((1, BQ - s.shape[1]), MASKED, f32)] if s.shape[1] < BQ else []
        mblk_ref[slot, h] = jnp.concatenate(unseen + [col_max(s)], axis=1)
        if meta:
            sm = _dot(kmeta_ref[h], qaug_ref[h])
            sm_ref[h] = sm
            mmeta_ref[h] = col_max(sm)

    def softmax_item(j, slot, h, meta=False, diag=None):
        cols = seen_by(diag)
        assert not (meta and diag)
        ref = r_at(first, h)
        off = ref - r_at(j, h)
        m_old = m_ref[h]
        m_blk = mblk_ref[slot, h] + off
        if meta:
            m_blk = jnp.maximum(m_blk, mmeta_ref[h] + ref)
        m_new = jnp.maximum(m_old, m_blk)
        alpha_ref[h] = jnp.exp2(m_old - m_new)
        m_seen = m_new
        if diag:
            m_seen = jnp.maximum(m_ref[h, :, cols], mblk_ref[slot, h, :, cols] + off)
        m_ref[h] = m_new
        p_ref[h, :, cols] = jnp.exp2(s_ref[slot, h, :, cols] - (m_seen - off)).astype(bf16)
        if meta:
            pm_ref[h] = jnp.exp2(sm_ref[h] - (m_new - ref)).astype(bf16)

    def value_item(j, slot, h, meta=False, diag=None):
        cols = seen_by(diag)
        pv = _dot(vT_ref[0, j, h], p_ref[h, :, cols])
        if meta:
            pv = pv + _dot(vmeta_ref[h], pm_ref[h])
        acc_ref[h, :, cols] = acc_ref[h, :, cols] * alpha_ref[h, :, cols] + pv

    def run(score_blocks, finish_blocks):
        score_items = [(blk + (h,), kw) for blk, kw in score_blocks for h in heads]
        finish_items = [(blk + (h,), kw) for blk, kw in finish_blocks for h in heads]
        for i in range(max(len(score_items), len(finish_items) + SKEW + 1)):
            if i < len(score_items):
                args, kw = score_items[i]
                score_item(*args, **kw)
            if 0 <= i - SKEW - 1 < len(finish_items):
                args, kw = finish_items[i - SKEW - 1]
                value_item(*args, **kw)
            if 0 <= i - SKEW < len(finish_items):
                args, kw = finish_items[i - SKEW]
                softmax_item(*args, **kw)

    m_ref[...] = jnp.full(m_ref.shape, MASKED, f32)
    acc_ref[...] = jnp.zeros(acc_ref.shape, f32)

    @pl.when(first == 0)
    def _():
        run([((first, 0), dict(diag=0))], [])

    @pl.when(first > 0)
    def _():
        def body(k, carry):
            run([((2 * k + 1, 1), {}), ((2 * k + 2, 0), {})],
                [((2 * k, 0), {}), ((2 * k + 1, 1), {})])
            return carry

        lax.fori_loop(0, t - 1, body, 0)
        run([((first - 1, 1), {}), ((first, 0), dict(diag=0))],
            [((first - 2, 0), {}), ((first - 1, 1), {})])

    run([((first + 1, 1), dict(diag=1, meta=True)), ((0, 0), dict(next_tile=True))],
        [((first, 0), dict(meta=True)), ((first + 1, 1), dict(diag=1))])

    for pair in range(N_HEADS // 2):
        halves = []
        for h in (2 * pair, 2 * pair + 1):
            a = acc_ref[h]
            halves.append(a[0:HEAD_DIM] * (1.0 / a[HEAD_DIM:HEAD_DIM + 1]))
        cols = slice(LANES * pair, LANES * (pair + 1))
        o_ref[0, :, cols] = (jnp.concatenate(halves, axis=0).T * sz_ref[0, :, cols].astype(f32)).astype(bf16)


def _out_kernel(x_ref, y_ref, ga_ref, mp_ref, wua_ref, wout_ref, g_ref, out_ref):
    merged = mp_ref[0].astype(f32) + ga_ref[0].astype(f32) * _dot(y_ref[0], wua_ref[...])
    h_out = x_ref[0] + _dot(merged.astype(bf16), wout_ref[...])
    out_ref[0] = _rmsnorm(h_out, g_ref[...])


def _const(shape):
    return pl.BlockSpec(shape, lambda *_: (0,) * len(shape), pipeline_mode=pl.Buffered(1))


def kernel(x, meta_tokens, norm_g, w_in, b_forget, pool_w, pool_scale, w_up_pool, w_up_attn, w_out, final_norm_g):
    batch, seq, _ = x.shape
    n_tiles = seq // TM
    n_blocks = seq // BLK

    w_all = w_in[0].astype(bf16)
    w_f = w_in[0][:, COL_F:COL_G]
    w_f3 = jnp.pad(jnp.tile(w_f, (1, N_SPLIT)), ((0, 0), (0, LANES - N_SPLIT * N_HEADS))).astype(bf16)
    b_f3 = jnp.pad(jnp.tile(b_forget[0], N_SPLIT), (0, LANES - N_SPLIT * N_HEADS)).reshape(1, LANES)
    g_in = norm_g[0].reshape(1, D_MODEL)
    g_out = final_norm_g.reshape(1, D_MODEL)
    pw = pool_w[0].astype(bf16)
    ps = pool_scale[0].reshape(1, POOL_WIDTH)
    w_upp = w_up_pool[0].astype(bf16)
    w_upa = w_up_attn[0].astype(bf16)
    w_o = w_out[0].astype(bf16)

    u_meta, k_meta, v_meta, r0 = pl.pallas_call(
        _meta_kernel,
        out_shape=(jax.ShapeDtypeStruct((N_META, POOL_WIDTH), f32),
                   jax.ShapeDtypeStruct((N_HEADS, N_META, LANES), bf16),
                   jax.ShapeDtypeStruct((N_HEADS, V_ROWS, N_META), bf16),
                   jax.ShapeDtypeStruct((8, LANES), f32)),
        grid=(1,),
        in_specs=[_const((N_META, D_MODEL)), _const((1, D_MODEL))]
        + [pl.BlockSpec((D_MODEL, 512), functools.partial(lambda c, i: (0, c), col // 512),
                        pipeline_mode=pl.Buffered(1)) for col in (COL_U, COL_K, COL_V)]
        + [_const((D_MODEL, LANES)), _const((1, LANES)), _const((len(POOL_WINDOWS), POOL_GROUP, POOL_GROUP))],
        out_specs=[_const((N_META, POOL_WIDTH)), _const((N_HEADS, N_META, LANES)),
                   _const((N_HEADS, V_ROWS, N_META)), _const((8, LANES))],
        compiler_params=pltpu.CompilerParams(vmem_limit_bytes=VMEM_LIMIT),
        name="meta_proj",
    )(meta_tokens, g_in, w_all, w_all, w_all, w_f3, b_f3, pw)

    mp, ga, sz, qT, vT, kaug, r_tiles = pl.pallas_call(
        _proj_kernel,
        grid=(batch, n_tiles),
        in_specs=[
            pl.BlockSpec((1, TM, D_MODEL), lambda b, t: (b, t, 0)),
            _const((1, D_MODEL)),
            _const((D_MODEL, N_IN)),
            _const((D_MODEL, LANES)),
            _const((1, LANES)),
            _const((len(POOL_WINDOWS), POOL_GROUP, POOL_GROUP)),
            _const((1, POOL_WIDTH)),
            _const((POOL_WIDTH, D_MODEL)),
            _const((N_META, POOL_WIDTH)),
            _const((8, LANES)),
        ],
        out_specs=[
            pl.BlockSpec((1, TM, D_MODEL), lambda b, t: (b, t, 0)),
            pl.BlockSpec((1, TM, D_MODEL), lambda b, t: (b, t, 0)),
            pl.BlockSpec((1, TM, ATTN_WIDTH), lambda b, t: (b, t, 0)),
            pl.BlockSpec((1, ATTN_WIDTH, TM), lambda b, t: (b, 0, t)),
            pl.BlockSpec((1, NB, N_HEADS, V_ROWS, BLK), lambda b, t: (b, t, 0, 0, 0)),
            pl.BlockSpec((1, N_HEADS, TM, LANES), lambda b, t: (b, 0, t, 0)),
            pl.BlockSpec((1, 1, 8, LANES), lambda b, t: (b, t, 0, 0)),
        ],
        out_shape=(
            jax.ShapeDtypeStruct((batch, seq, D_MODEL), bf16),
            jax.ShapeDtypeStruct((batch, seq, D_MODEL), bf16),
            jax.ShapeDtypeStruct((batch, seq, ATTN_WIDTH), bf16),
            jax.ShapeDtypeStruct((batch, ATTN_WIDTH, seq), bf16),
            jax.ShapeDtypeStruct((batch, n_blocks, N_HEADS, V_ROWS, BLK), bf16),
            jax.ShapeDtypeStruct((batch, N_HEADS, seq, LANES), bf16),
            jax.ShapeDtypeStruct((batch, n_tiles, 8, LANES), f32),
        ),
        scratch_shapes=[pltpu.VMEM((POOL_PAD + N_META + TM, POOL_WIDTH), f32),
                        pltpu.VMEM((POOL_PAD + N_META + TM, 3 * POOL_GROUP), f32),
                        pltpu.VMEM((POOL_PAD + N_META + TM, 2 * POOL_GROUP), f32),
                        pltpu.VMEM((8, LANES), f32),
                        pltpu.VMEM((2 * ATTN_WIDTH, D_MODEL), bf16),
                        pltpu.VMEM((D_MODEL, 2 * D_MODEL), bf16),
                        pltpu.VMEM((D_MODEL, POOL_WIDTH), bf16)],
        compiler_params=pltpu.CompilerParams(dimension_semantics=("arbitrary", "arbitrary"),
                                             vmem_limit_bytes=VMEM_LIMIT),
        name="in_proj",
    )(x, g_in, w_all, w_f3, b_f3, pw, ps, w_upp, u_meta, r0)

    r_blocks = r_tiles[:, :, :NB, :N_HEADS].reshape(batch * n_blocks * N_HEADS)

    y_attn = pl.pallas_call(
        functools.partial(_attn_kernel, n_blocks=n_blocks),
        grid=(batch, seq // BQ),
        in_specs=[
            pl.BlockSpec(memory_space=pltpu.SMEM),
            pl.BlockSpec((1, ATTN_WIDTH, BQ), lambda b, t: (b, 0, t)),
            pl.BlockSpec((1, ATTN_WIDTH, BQ), lambda b, t: (b, 0, jnp.minimum(t + 1, seq // BQ - 1))),
            pl.BlockSpec((1, N_HEADS, seq, LANES), lambda b, t: (b, 0, 0, 0)),
            pl.BlockSpec((1, n_blocks, N_HEADS, V_ROWS, BLK), lambda b, t: (b, 0, 0, 0, 0)),
            _const((N_HEADS, N_META, LANES)),
            _const((N_HEADS, V_ROWS, N_META)),
            pl.BlockSpec((1, BQ, ATTN_WIDTH), lambda b, t: (b, t, 0)),
        ],
        out_specs=pl.BlockSpec((1, BQ, ATTN_WIDTH), lambda b, t: (b, t, 0)),
        out_shape=jax.ShapeDtypeStruct((batch, seq, ATTN_WIDTH), bf16),
        scratch_shapes=[pltpu.VMEM((N_HEADS, LANES, BQ), bf16),
                        pltpu.VMEM((N_HEADS, LANES, BQ), bf16),
                        pltpu.VMEM((2, N_HEADS, BLK, BQ), f32),
                        pltpu.VMEM((N_HEADS, BLK, BQ), bf16),
                        pltpu.VMEM((N_HEADS, N_META, BQ), f32),
                        pltpu.VMEM((N_HEADS, N_META, BQ), bf16),
                        pltpu.VMEM((N_HEADS, 1, BQ), f32),
                        pltpu.VMEM((2, N_HEADS, 1, BQ), f32),
                        pltpu.VMEM((N_HEADS, 1, BQ), f32),
                        pltpu.VMEM((N_HEADS, 1, BQ), f32),
                        pltpu.VMEM((N_HEADS, V_ROWS, BQ), f32)],
        compiler_params=pltpu.CompilerParams(dimension_semantics=("arbitrary", "arbitrary"),
                                             vmem_limit_bytes=VMEM_LIMIT),
        name="attention",
    )(r_blocks, qT, qT, kaug, vT, k_meta, v_meta, sz)

    return pl.pallas_call(
        _out_kernel,
        grid=(batch, seq // TM_OUT),
        in_specs=[
            pl.BlockSpec((1, TM_OUT, D_MODEL), lambda b, t: (b, t, 0)),
            pl.BlockSpec((1, TM_OUT, ATTN_WIDTH), lambda b, t: (b, t, 0)),
            pl.BlockSpec((1, TM_OUT, D_MODEL), lambda b, t: (b, t, 0)),
            pl.BlockSpec((1, TM_OUT, D_MODEL), lambda b, t: (b, t, 0)),
            _const((ATTN_WIDTH, D_MODEL)),
            _const((D_MODEL, D_MODEL)),
            _const((1, D_MODEL)),
        ],
        out_specs=pl.BlockSpec((1, TM_OUT, D_MODEL), lambda b, t: (b, t, 0)),
        out_shape=jax.ShapeDtypeStruct((batch, seq, D_MODEL), f32),
        compiler_params=pltpu.CompilerParams(dimension_semantics=("arbitrary", "arbitrary"),
                                             vmem_limit_bytes=VMEM_LIMIT),
        name="out_proj",
    )(x, y_attn, ga, mp, w_upa, w_o, g_out)
```

```python
import functools

import jax
import jax.numpy as jnp
from jax import lax
from jax.experimental import pallas as pl
from jax.experimental.pallas import tpu as pltpu

D_MODEL = 1024
N_META = 16
POOL_WIDTH = 512
POOL_WINDOWS = (2, 4, 8, 16)
POOL_GROUP = POOL_WIDTH // len(POOL_WINDOWS)
N_HEADS = 8
HEAD_DIM = 64
ATTN_WIDTH = N_HEADS * HEAD_DIM
RMS_EPS = 1e-6

LANES = 128
BLK = 256
BQ = 512
QB = BQ // BLK
assert QB == 2
TM = 512
NB = TM // BLK
TM_OUT = 1024
POOL_PAD = 8
assert POOL_WINDOWS == (2, 4, 8, 16) and POOL_PAD >= POOL_WINDOWS[-1] // 2
V_ROWS = HEAD_DIM + 16
N_SPLIT = 3
MASKED = -1e30
SKEW = 2
LOG2E = 1.4426950408889634
VMEM_LIMIT = 56 * 1024 * 1024
assert POOL_WIDTH == ATTN_WIDTH
IN_GROUP = POOL_WIDTH
COL_U, COL_ZP, COL_Q, COL_K, COL_V, COL_ZA, COL_F = (IN_GROUP * i for i in range(7))
COL_G = COL_F + N_HEADS
N_IN = COL_G + 2 * D_MODEL

f32 = jnp.float32
bf16 = jnp.bfloat16


def _rmsnorm(x, g):
    return x * lax.rsqrt(jnp.mean(x * x, axis=-1, keepdims=True) + RMS_EPS) * g


def _sigmoid(x):
    return 1.0 / (1.0 + jnp.exp(-x))


def _log_sigmoid(x):
    return jnp.minimum(x, 0.0) - jnp.log1p(jnp.exp(-jnp.abs(x)))


def _dot(a, b):
    return jnp.dot(a, b, preferred_element_type=f32)


def _dot_nt(a, b):
    return lax.dot_general(a, b, (((1,), (1,)), ((), ())), preferred_element_type=f32)


def _lane_iota(shape):
    return lax.broadcasted_iota(jnp.int32, shape, 1)


def _decay_parts(logf3, rows):
    n = logf3.shape[0] // rows
    r = lax.broadcasted_iota(jnp.int32, (rows, rows), 0)
    c = lax.broadcasted_iota(jnp.int32, (rows, rows), 1)
    tri = (c <= r).astype(f32)
    side_by_side = jnp.concatenate([logf3[rows * i:rows * (i + 1)] for i in range(n)], axis=1)
    sums = jnp.dot(tri, side_by_side, precision=lax.Precision.HIGHEST, preferred_element_type=f32)
    beta = jnp.concatenate([sums[:, LANES * i:LANES * (i + 1)] for i in range(n)], axis=0)
    nb = beta * (-LOG2E)
    hi = nb.astype(bf16).astype(f32)
    mid = (nb - hi).astype(bf16).astype(f32)
    lo = (nb - hi - mid).astype(bf16).astype(f32)
    lane = _lane_iota(nb.shape)
    return beta, jnp.where(lane < N_HEADS, hi, jnp.where(lane < 2 * N_HEADS, mid, lo))


def _augmented_keys(kproj, parts, store):
    lane = _lane_iota((kproj.shape[0], LANES))
    for h in range(N_HEADS):
        slab = kproj[:, LANES * (h // 2):LANES * (h // 2 + 1)]
        if h % 2:
            slab = pltpu.roll(slab, HEAD_DIM, axis=1)
        store(h, jnp.where(lane < HEAD_DIM, slab, pltpu.roll(parts, HEAD_DIM - h, axis=1)).astype(bf16))


def _pool_mapped(wu_t, pw_ref):
    return jnp.concatenate([_dot(pw_ref[g].T, wu_t[POOL_GROUP * g:POOL_GROUP * (g + 1)])
                            for g in range(len(POOL_WINDOWS))], axis=0).astype(bf16)


def _forget_rows(wf_t):
    unused = jnp.zeros((LANES - N_SPLIT * N_HEADS, wf_t.shape[1]), f32)
    return jnp.concatenate([wf_t] * N_SPLIT + [unused], axis=0).astype(bf16)


def _meta_kernel(meta_ref, g_ref, wu_ref, wk_ref, wv_ref, wf_ref, bf_ref, pw_ref,
                 umeta_ref, kmeta_ref, vmeta_ref, r0_ref):
    hn = _rmsnorm(meta_ref[...], g_ref[...]).astype(bf16)
    umeta_ref[...] = _dot_nt(hn, _pool_mapped(wu_ref[...].astype(bf16), pw_ref))
    vt = _dot_nt(wv_ref[...].astype(bf16), hn)
    for h in range(N_HEADS):
        vmeta_ref[h, 0:HEAD_DIM, :] = vt[HEAD_DIM * h:HEAD_DIM * (h + 1), :].astype(bf16)
        vmeta_ref[h, HEAD_DIM:V_ROWS, :] = jnp.ones((V_ROWS - HEAD_DIM, N_META), bf16)
    logf3 = _log_sigmoid(_dot_nt(hn, _forget_rows(wf_ref[...])) + bf_ref[...])
    beta, parts = _decay_parts(logf3, N_META)

    def store(h, ka):
        kmeta_ref[h] = ka

    _augmented_keys(_dot_nt(hn, wk_ref[...].astype(bf16)), parts, store)
    r0_ref[...] = jnp.broadcast_to(beta[N_META - 1:N_META, :], r0_ref.shape)


def _proj_kernel(x_ref, g_ref, wt_ref, bf_ref, pw_ref, ps_ref, wup_ref,
                 umeta_ref, r0_ref,
                 mp_ref, ga_ref, sz_ref, qT_ref, vT_ref, kaug_ref, r_ref,
                 uext_ref, lvl_a_ref, lvl_b_ref, rcarry_ref, wb_ref, wg_ref, wu_ref, wf_ref):
    t = pl.program_id(1)
    zp_rows, k_rows, za_rows, q_rows, v_rows = (slice(IN_GROUP * i, IN_GROUP * (i + 1)) for i in range(5))

    @pl.when((pl.program_id(0) == 0) & (t == 0))
    def _():
        for rows, col in ((zp_rows, COL_ZP), (k_rows, COL_K), (za_rows, COL_ZA), (q_rows, COL_Q), (v_rows, COL_V)):
            wb_ref[rows, :] = wt_ref[col:col + IN_GROUP, :].astype(bf16)
        wg_ref[...] = wt_ref[COL_G:N_IN, :].astype(bf16)
        wu_ref[...] = _pool_mapped(wt_ref[COL_U:COL_ZP, :].astype(bf16), pw_ref)
        wf_ref[...] = _forget_rows(wt_ref[COL_F:COL_G, :])
    ext = N_META + TM
    body = slice(POOL_PAD, POOL_PAD + ext)

    @pl.when(t == 0)
    def _():
        uext_ref[0:POOL_PAD, :] = jnp.zeros((POOL_PAD, POOL_WIDTH), f32)
        lvl_a_ref[0:POOL_PAD, :] = jnp.zeros((POOL_PAD, lvl_a_ref.shape[1]), f32)
        lvl_b_ref[0:POOL_PAD, :] = jnp.zeros((POOL_PAD, lvl_b_ref.shape[1]), f32)
        uext_ref[POOL_PAD:POOL_PAD + N_META, :] = umeta_ref[...]
        rcarry_ref[...] = r0_ref[...]

    hn = _rmsnorm(x_ref[0], g_ref[...]).astype(bf16)

    gat = _dot_nt(hn, wg_ref[D_MODEL:2 * D_MODEL, :])
    ga_ref[0] = _sigmoid(gat).astype(bf16)

    u = _dot_nt(hn, wu_ref[...])
    zp = _dot_nt(hn, wb_ref[zp_rows, :])
    uext_ref[POOL_PAD + N_META:POOL_PAD + ext, :] = u

    def shifted_sum(ref, shift, cols):
        return ref[body, cols] + ref[POOL_PAD - shift:POOL_PAD - shift + ext, cols]

    g1, g2, g3 = (slice(POOL_GROUP * g, POOL_GROUP * (g + 1)) for g in range(3))
    sums = [shifted_sum(uext_ref, 1, g1)]
    lvl_a_ref[body, :] = shifted_sum(uext_ref, 1, slice(POOL_GROUP, POOL_WIDTH))
    sums.append(shifted_sum(lvl_a_ref, 2, g1))
    lvl_b_ref[body, :] = shifted_sum(lvl_a_ref, 2, slice(POOL_GROUP, 3 * POOL_GROUP))
    sums.append(shifted_sum(lvl_b_ref, 4, g1))
    lvl_a_ref[body, g1] = shifted_sum(lvl_b_ref, 4, g2)
    sums.append(shifted_sum(lvl_a_ref, 8, g1))
    uext_ref[POOL_PAD:POOL_PAD + N_META, :] = uext_ref[POOL_PAD + TM:POOL_PAD + ext, :]

    pooled = jnp.concatenate([sums[g][N_META:] * (1.0 / w) for g, w in enumerate(POOL_WINDOWS)], axis=1) - u
    y_pool = (pooled * ps_ref[...] * (zp * _sigmoid(zp))).astype(bf16)
    gp = _dot_nt(hn, wg_ref[0:D_MODEL, :])
    mp_ref[0] = (_sigmoid(gp) * _dot(y_pool, wup_ref[...])).astype(bf16)

    logf3 = _log_sigmoid(_dot_nt(hn, wf_ref[...]) + bf_ref[...])
    r_ref[...] = jnp.zeros(r_ref.shape, f32)
    beta, parts = _decay_parts(logf3, BLK)
    for c in range(NB):
        r_ref[0, 0, c:c + 1, :] = rcarry_ref[0:1, :] * LOG2E
        rcarry_ref[...] = rcarry_ref[...] + beta[BLK * (c + 1) - 1:BLK * (c + 1), :]

    za = _dot_nt(hn, wb_ref[za_rows, :])
    sz_ref[0] = (za * _sigmoid(za)).astype(bf16)

    def store(h, ka):
        kaug_ref[0, h] = ka

    _augmented_keys(_dot_nt(hn, wb_ref[k_rows, :]), parts, store)

    qv = _dot_nt(wb_ref[q_rows.start:v_rows.stop, :], hn)
    qT_ref[0] = (qv[0:ATTN_WIDTH] * (HEAD_DIM ** -0.5 * LOG2E)).astype(bf16)
    for c in range(NB):
        for h in range(N_HEADS):
            rows = slice(ATTN_WIDTH + HEAD_DIM * h, ATTN_WIDTH + HEAD_DIM * (h + 1))
            vT_ref[0, c, h, 0:HEAD_DIM, :] = qv[rows, BLK * c:BLK * (c + 1)].astype(bf16)
            vT_ref[0, c, h, HEAD_DIM:V_ROWS, :] = jnp.ones((V_ROWS - HEAD_DIM, BLK), bf16)


def _attn_kernel(r_ref, qT_ref, qnext_ref, kaug_ref, vT_ref, kmeta_ref, vmeta_ref, sz_ref, o_ref,
                 qaug_ref, qaug_next_ref, s_ref, p_ref, sm_ref, pm_ref, m_ref, mblk_ref, mmeta_ref, alpha_ref, acc_ref,
                 *, n_blocks):
    b = pl.program_id(0)
    t = pl.program_id(1)
    first = QB * t
    heads = range(N_HEADS)

    part_row = lax.broadcasted_iota(jnp.int32, (LANES - HEAD_DIM, BQ), 0)
    ones_rows = ((part_row % N_HEADS == 0) & (part_row < N_SPLIT * N_HEADS)).astype(bf16)
    for h in heads:
        for q_ref, aug_ref in ((qT_ref, qaug_ref), (qnext_ref, qaug_next_ref)):
            aug_ref[h, 0:HEAD_DIM, :] = q_ref[0, HEAD_DIM * h:HEAD_DIM * (h + 1), :]
            aug_ref[h, HEAD_DIM:LANES, :] = ones_rows

    def r_at(j, h):
        return r_ref[(b * n_blocks + j) * N_HEADS + h]

    def col_max(s):
        return jnp.max(s, axis=0, keepdims=True)

    key_index = functools.partial(lax.broadcasted_iota, jnp.int32, dimension=0)
    query_index = functools.partial(lax.broadcasted_iota, jnp.int32, dimension=1)

    def seen_by(diag):
        return slice(BLK * diag, BQ) if diag else slice(None)

    def score_item(j, slot, h, diag=None, meta=False, next_tile=False):
        q_ref = qaug_next_ref if next_tile else qaug_ref
        cols = seen_by(diag)
        s = _dot(kaug_ref[0, h, pl.ds(pl.multiple_of(j * BLK, BLK), BLK), :], q_ref[h, :, cols])
        if diag is not None:
            s = jnp.where(key_index(s.shape) <= query_index(s.shape), s, MASKED)
        s_ref[slot, h, :, cols] = s
        unseen = [jnp.full((1, BQ - s.shape[1]), MASKED, f32)] if s.shape[1] < BQ else []
        mblk_ref[slot, h] = jnp.concatenate(unseen + [col_max(s)], axis=1)
        if meta:
            sm = _dot(kmeta_ref[h], qaug_ref[h])
            sm_ref[h] = sm
            mmeta_ref[h] = col_max(sm)

    def softmax_item(j, slot, h, meta=False, diag=None):
        cols = seen_by(diag)
        assert not (meta and diag)
        ref = r_at(first, h)
        off = ref - r_at(j, h)
        m_old = m_ref[h]
        m_blk = mblk_ref[slot, h] + off
        if meta:
            m_blk = jnp.maximum(m_blk, mmeta_ref[h] + ref)
        m_new = jnp.maximum(m_old, m_blk)
        alpha_ref[h] = jnp.exp2(m_old - m_new)
        m_seen = m_new
        if diag:
            m_seen = jnp.maximum(m_ref[h, :, cols], mblk_ref[slot, h, :, cols] + off)
        m_ref[h] = m_new
        p_ref[h, :, cols] = jnp.exp2(s_ref[slot, h, :, cols] - (m_seen - off)).astype(bf16)
        if meta:
            pm_ref[h] = jnp.exp2(sm_ref[h] - (m_new - ref)).astype(bf16)

    def value_item(j, slot, h, meta=False, diag=None):
        cols = seen_by(diag)
        pv = _dot(vT_ref[0, j, h], p_ref[h, :, cols])
        if meta:
            pv = pv + _dot(vmeta_ref[h], pm_ref[h])
        acc_ref[h, :, cols] = acc_ref[h, :, cols] * alpha_ref[h, :, cols] + pv

    def run(score_blocks, finish_blocks):
        score_items = [(blk + (h,), kw) for blk, kw in score_blocks for h in heads]
        finish_items = [(blk + (h,), kw) for blk, kw in finish_blocks for h in heads]
        for i in range(max(len(score_items), len(finish_items) + SKEW + 1)):
            if i < len(score_items):
                args, kw = score_items[i]
                score_item(*args, **kw)
            if 0 <= i - SKEW - 1 < len(finish_items):
                args, kw = finish_items[i - SKEW - 1]
                value_item(*args, **kw)
            if 0 <= i - SKEW < len(finish_items):
                args, kw = finish_items[i - SKEW]
                softmax_item(*args, **kw)

    m_ref[...] = jnp.full(m_ref.shape, MASKED, f32)
    acc_ref[...] = jnp.zeros(acc_ref.shape, f32)

    @pl.when(first == 0)
    def _():
        run([((first, 0), dict(diag=0))], [])

    @pl.when(first > 0)
    def _():
        def body(k, carry):
            run([((2 * k + 1, 1), {}), ((2 * k + 2, 0), {})],
                [((2 * k, 0), {}), ((2 * k + 1, 1), {})])
            return carry

        lax.fori_loop(0, t - 1, body, 0)
        run([((first - 1, 1), {}), ((first, 0), dict(diag=0))],
            [((first - 2, 0), {}), ((first - 1, 1), {})])

    run([((first + 1, 1), dict(diag=1, meta=True)), ((0, 0), dict(next_tile=True))],
        [((first, 0), dict(meta=True)), ((first + 1, 1), dict(diag=1))])

    for pair in range(N_HEADS // 2):
        halves = []
        for h in (2 * pair, 2 * pair + 1):
            a = acc_ref[h]
            halves.append(a[0:HEAD_DIM] * (1.0 / a[HEAD_DIM:HEAD_DIM + 1]))
        cols = slice(LANES * pair, LANES * (pair + 1))
        o_ref[0, :, cols] = (jnp.concatenate(halves, axis=0).T * sz_ref[0, :, cols].astype(f32)).astype(bf16)


def _out_kernel(x_ref, y_ref, ga_ref, mp_ref, wua_ref, wout_ref, g_ref, out_ref):
    merged = mp_ref[0].astype(f32) + ga_ref[0].astype(f32) * _dot(y_ref[0], wua_ref[...])
    h_out = x_ref[0] + _dot(merged.astype(bf16), wout_ref[...])
    out_ref[0] = _rmsnorm(h_out, g_ref[...])


def _const(shape):
    return pl.BlockSpec(shape, lambda *_: (0,) * len(shape), pipeline_mode=pl.Buffered(1))


def kernel(x, meta_tokens, norm_g, w_in, b_forget, pool_w, pool_scale, w_up_pool, w_up_attn, w_out, final_norm_g):
    batch, seq, _ = x.shape
    n_tiles = seq // TM
    n_blocks = seq // BLK

    w_t = jnp.transpose(w_in[0])
    b_f3 = jnp.pad(jnp.tile(b_forget[0], N_SPLIT), (0, LANES - N_SPLIT * N_HEADS)).reshape(1, LANES)
    g_in = norm_g[0].reshape(1, D_MODEL)
    g_out = final_norm_g.reshape(1, D_MODEL)
    pw = pool_w[0].astype(bf16)
    ps = pool_scale[0].reshape(1, POOL_WIDTH)
    w_upp = w_up_pool[0].astype(bf16)
    w_upa = w_up_attn[0].astype(bf16)
    w_o = w_out[0].astype(bf16)

    u_meta, k_meta, v_meta, r0 = pl.pallas_call(
        _meta_kernel,
        out_shape=(jax.ShapeDtypeStruct((N_META, POOL_WIDTH), f32),
                   jax.ShapeDtypeStruct((N_HEADS, N_META, LANES), bf16),
                   jax.ShapeDtypeStruct((N_HEADS, V_ROWS, N_META), bf16),
                   jax.ShapeDtypeStruct((8, LANES), f32)),
        grid=(1,),
        in_specs=[_const((N_META, D_MODEL)), _const((1, D_MODEL))]
        + [pl.BlockSpec((rows, D_MODEL), functools.partial(lambda r, i: (r, 0), col // rows),
                        pipeline_mode=pl.Buffered(1))
           for col, rows in ((COL_U, IN_GROUP), (COL_K, IN_GROUP), (COL_V, IN_GROUP), (COL_F, N_HEADS))]
        + [_const((1, LANES)), _const((len(POOL_WINDOWS), POOL_GROUP, POOL_GROUP))],
        out_specs=[_const((N_META, POOL_WIDTH)), _const((N_HEADS, N_META, LANES)),
                   _const((N_HEADS, V_ROWS, N_META)), _const((8, LANES))],
        compiler_params=pltpu.CompilerParams(vmem_limit_bytes=VMEM_LIMIT),
        name="meta_proj",
    )(meta_tokens, g_in, w_t, w_t, w_t, w_t, b_f3, pw)

    mp, ga, sz, qT, vT, kaug, r_tiles = pl.pallas_call(
        _proj_kernel,
        grid=(batch, n_tiles),
        in_specs=[
            pl.BlockSpec((1, TM, D_MODEL), lambda b, t: (b, t, 0)),
            _const((1, D_MODEL)),
            _const((N_IN, D_MODEL)),
            _const((1, LANES)),
            _const((len(POOL_WINDOWS), POOL_GROUP, POOL_GROUP)),
            _const((1, POOL_WIDTH)),
            _const((POOL_WIDTH, D_MODEL)),
            _const((N_META, POOL_WIDTH)),
            _const((8, LANES)),
        ],
        out_specs=[
            pl.BlockSpec((1, TM, D_MODEL), lambda b, t: (b, t, 0)),
            pl.BlockSpec((1, TM, D_MODEL), lambda b, t: (b, t, 0)),
            pl.BlockSpec((1, TM, ATTN_WIDTH), lambda b, t: (b, t, 0)),
            pl.BlockSpec((1, ATTN_WIDTH, TM), lambda b, t: (b, 0, t)),
            pl.BlockSpec((1, NB, N_HEADS, V_ROWS, BLK), lambda b, t: (b, t, 0, 0, 0)),
            pl.BlockSpec((1, N_HEADS, TM, LANES), lambda b, t: (b, 0, t, 0)),
            pl.BlockSpec((1, 1, 8, LANES), lambda b, t: (b, t, 0, 0)),
        ],
        out_shape=(
            jax.ShapeDtypeStruct((batch, seq, D_MODEL), bf16),
            jax.ShapeDtypeStruct((batch, seq, D_MODEL), bf16),
            jax.ShapeDtypeStruct((batch, seq, ATTN_WIDTH), bf16),
            jax.ShapeDtypeStruct((batch, ATTN_WIDTH, seq), bf16),
            jax.ShapeDtypeStruct((batch, n_blocks, N_HEADS, V_ROWS, BLK), bf16),
            jax.ShapeDtypeStruct((batch, N_HEADS, seq, LANES), bf16),
            jax.ShapeDtypeStruct((batch, n_tiles, 8, LANES), f32),
        ),
        scratch_shapes=[pltpu.VMEM((POOL_PAD + N_META + TM, POOL_WIDTH), f32),
                        pltpu.VMEM((POOL_PAD + N_META + TM, 3 * POOL_GROUP), f32),
                        pltpu.VMEM((POOL_PAD + N_META + TM, 2 * POOL_GROUP), f32),
                        pltpu.VMEM((8, LANES), f32),
                        pltpu.VMEM((5 * IN_GROUP, D_MODEL), bf16),
                        pltpu.VMEM((2 * D_MODEL, D_MODEL), bf16),
                        pltpu.VMEM((POOL_WIDTH, D_MODEL), bf16),
                        pltpu.VMEM((LANES, D_MODEL), bf16)],
        compiler_params=pltpu.CompilerParams(dimension_semantics=("arbitrary", "arbitrary"),
                                             vmem_limit_bytes=VMEM_LIMIT),
        name="in_proj",
    )(x, g_in, w_t, b_f3, pw, ps, w_upp, u_meta, r0)

    r_blocks = r_tiles[:, :, :NB, :N_HEADS].reshape(batch * n_blocks * N_HEADS)

    y_attn = pl.pallas_call(
        functools.partial(_attn_kernel, n_blocks=n_blocks),
        grid=(batch, seq // BQ),
        in_specs=[
            pl.BlockSpec(memory_space=pltpu.SMEM),
            pl.BlockSpec((1, ATTN_WIDTH, BQ), lambda b, t: (b, 0, t)),
            pl.BlockSpec((1, ATTN_WIDTH, BQ), lambda b, t: (b, 0, jnp.minimum(t + 1, seq // BQ - 1))),
            pl.BlockSpec((1, N_HEADS, seq, LANES), lambda b, t: (b, 0, 0, 0)),
            pl.BlockSpec((1, n_blocks, N_HEADS, V_ROWS, BLK), lambda b, t: (b, 0, 0, 0, 0)),
            _const((N_HEADS, N_META, LANES)),
            _const((N_HEADS, V_ROWS, N_META)),
            pl.BlockSpec((1, BQ, ATTN_WIDTH), lambda b, t: (b, t, 0)),
        ],
        out_specs=pl.BlockSpec((1, BQ, ATTN_WIDTH), lambda b, t: (b, t, 0)),
        out_shape=jax.ShapeDtypeStruct((batch, seq, ATTN_WIDTH), bf16),
        scratch_shapes=[pltpu.VMEM((N_HEADS, LANES, BQ), bf16),
                        pltpu.VMEM((N_HEADS, LANES, BQ), bf16),
                        pltpu.VMEM((2, N_HEADS, BLK, BQ), f32),
                        pltpu.VMEM((N_HEADS, BLK, BQ), bf16),
                        pltpu.VMEM((N_HEADS, N_META, BQ), f32),
                        pltpu.VMEM((N_HEADS, N_META, BQ), bf16),
                        pltpu.VMEM((N_HEADS, 1, BQ), f32),
                        pltpu.VMEM((2, N_HEADS, 1, BQ), f32),
                        pltpu.VMEM((N_HEADS, 1, BQ), f32),
                        pltpu.VMEM((N_HEADS, 1, BQ), f32),
                        pltpu.VMEM((N_HEADS, V_ROWS, BQ), f32)],
        compiler_params=pltpu.CompilerParams(dimension_semantics=("arbitrary", "arbitrary"),
                                             vmem_limit_bytes=VMEM_LIMIT),
        name="attention",
    )(r_blocks, qT, qT, kaug, vT, k_meta, v_meta, sz)

    return pl.pallas_call(
        _out_kernel,
        grid=(batch, seq // TM_OUT),
        in_specs=[
            pl.BlockSpec((1, TM_OUT, D_MODEL), lambda b, t: (b, t, 0)),
            pl.BlockSpec((1, TM_OUT, ATTN_WIDTH), lambda b, t: (b, t, 0)),
            pl.BlockSpec((1, TM_OUT, D_MODEL), lambda b, t: (b, t, 0)),
            pl.BlockSpec((1, TM_OUT, D_MODEL), lambda b, t: (b, t, 0)),
            _const((ATTN_WIDTH, D_MODEL)),
            _const((D_MODEL, D_MODEL)),
            _const((1, D_MODEL)),
        ],
        out_specs=pl.BlockSpec((1, TM_OUT, D_MODEL), lambda b, t: (b, t, 0)),
        out_shape=jax.ShapeDtypeStruct((batch, seq, D_MODEL), f32),
        compiler_params=pltpu.CompilerParams(dimension_semantics=("arbitrary", "arbitrary"),
                                             vmem_limit_bytes=VMEM_LIMIT),
        name="out_proj",
    )(x, y_attn, ga, mp, w_upa, w_o, g_out)
```

```python
import functools

import jax
import jax.numpy as jnp
from jax import lax
from jax.experimental import pallas as pl
from jax.experimental.pallas import tpu as pltpu

D_MODEL = 1024
N_META = 16
POOL_WIDTH = 512
POOL_WINDOWS = (2, 4, 8, 16)
POOL_GROUP = POOL_WIDTH // len(POOL_WINDOWS)
N_HEADS = 8
HEAD_DIM = 64
ATTN_WIDTH = N_HEADS * HEAD_DIM
RMS_EPS = 1e-6

LANES = 128
BLK = 256
BQ = 512
QB = BQ // BLK
assert QB == 2
TM = 512
NB = TM // BLK
TM_OUT = 1024
POOL_PAD = 8
assert POOL_WINDOWS == (2, 4, 8, 16) and POOL_PAD >= POOL_WINDOWS[-1] // 2
V_ROWS = HEAD_DIM + 16
N_SPLIT = 3
MASKED = -1e30
SKEW = 2
LOG2E = 1.4426950408889634
VMEM_LIMIT = 56 * 1024 * 1024
assert POOL_WIDTH == ATTN_WIDTH
IN_GROUP = POOL_WIDTH
COL_U, COL_ZP, COL_Q, COL_K, COL_V, COL_ZA, COL_F = (IN_GROUP * i for i in range(7))
COL_G = COL_F + N_HEADS
N_IN = COL_G + 2 * D_MODEL

f32 = jnp.float32
bf16 = jnp.bfloat16


def _rmsnorm(x, g):
    return x * lax.rsqrt(jnp.mean(x * x, axis=-1, keepdims=True) + RMS_EPS) * g


def _sigmoid(x):
    return 1.0 / (1.0 + jnp.exp(-x))


def _log_sigmoid(x):
    return jnp.minimum(x, 0.0) - jnp.log1p(jnp.exp(-jnp.abs(x)))


def _dot(a, b):
    return jnp.dot(a, b, preferred_element_type=f32)


def _dot_nt(a, b):
    return lax.dot_general(a, b, (((1,), (1,)), ((), ())), preferred_element_type=f32)


def _lane_iota(shape):
    return lax.broadcasted_iota(jnp.int32, shape, 1)


def _decay_parts(logf3, rows):
    n = logf3.shape[0] // rows
    r = lax.broadcasted_iota(jnp.int32, (rows, rows), 0)
    c = lax.broadcasted_iota(jnp.int32, (rows, rows), 1)
    tri = (c <= r).astype(f32)
    side_by_side = jnp.concatenate([logf3[rows * i:rows * (i + 1)] for i in range(n)], axis=1)
    sums = jnp.dot(tri, side_by_side, precision=lax.Precision.HIGHEST, preferred_element_type=f32)
    beta = jnp.concatenate([sums[:, LANES * i:LANES * (i + 1)] for i in range(n)], axis=0)
    nb = beta * (-LOG2E)
    hi = nb.astype(bf16).astype(f32)
    mid = (nb - hi).astype(bf16).astype(f32)
    lo = (nb - hi - mid).astype(bf16).astype(f32)
    lane = _lane_iota(nb.shape)
    return beta, jnp.where(lane < N_HEADS, hi, jnp.where(lane < 2 * N_HEADS, mid, lo))


def _augmented_keys(kproj, parts, store):
    lane = _lane_iota((kproj.shape[0], LANES))
    for h in range(N_HEADS):
        slab = kproj[:, LANES * (h // 2):LANES * (h // 2 + 1)]
        if h % 2:
            slab = pltpu.roll(slab, HEAD_DIM, axis=1)
        store(h, jnp.where(lane < HEAD_DIM, slab, pltpu.roll(parts, HEAD_DIM - h, axis=1)).astype(bf16))


def _pool_mapped(wu_t, pw_ref):
    return jnp.concatenate([_dot(pw_ref[g].T.astype(bf16), wu_t[POOL_GROUP * g:POOL_GROUP * (g + 1)])
                            for g in range(len(POOL_WINDOWS))], axis=0).astype(bf16)


def _forget_rows(wf_t):
    unused = jnp.zeros((LANES - N_SPLIT * N_HEADS, wf_t.shape[1]), f32)
    return jnp.concatenate([wf_t] * N_SPLIT + [unused], axis=0).astype(bf16)


def _forget_bias(b_ref):
    lane = _lane_iota((1, LANES))
    bias = jnp.zeros((1, LANES), f32)
    for h in range(N_HEADS):
        bias = jnp.where((lane % N_HEADS == h) & (lane < N_SPLIT * N_HEADS), b_ref[h], bias)
    return bias


def _meta_kernel(meta_ref, g_ref, wu_ref, wk_ref, wv_ref, wf_ref, bf_ref, pw_ref,
                 umeta_ref, kmeta_ref, vmeta_ref, r0_ref):
    hn = _rmsnorm(meta_ref[...], g_ref[...]).astype(bf16)
    umeta_ref[...] = _dot_nt(hn, _pool_mapped(wu_ref[...].astype(bf16), pw_ref))
    vt = _dot_nt(wv_ref[...].astype(bf16), hn)
    for h in range(N_HEADS):
        vmeta_ref[h, 0:HEAD_DIM, :] = vt[HEAD_DIM * h:HEAD_DIM * (h + 1), :].astype(bf16)
        vmeta_ref[h, HEAD_DIM:V_ROWS, :] = jnp.ones((V_ROWS - HEAD_DIM, N_META), bf16)
    logf3 = _log_sigmoid(_dot_nt(hn, _forget_rows(wf_ref[...])) + _forget_bias(bf_ref))
    beta, parts = _decay_parts(logf3, N_META)

    def store(h, ka):
        kmeta_ref[h] = ka

    _augmented_keys(_dot_nt(hn, wk_ref[...].astype(bf16)), parts, store)
    r0_ref[...] = jnp.broadcast_to(beta[N_META - 1:N_META, :], r0_ref.shape)


def _proj_kernel(x_ref, g_ref, wt_ref, bf_ref, pw_ref, ps_ref, wup32_ref,
                 umeta_ref, r0_ref,
                 mp_ref, ga_ref, sz_ref, qT_ref, vT_ref, kaug_ref, r_ref,
                 uext_ref, lvl_a_ref, lvl_b_ref, rcarry_ref, wb_ref, wg_ref, wu_ref, wf_ref, wup_ref):
    t = pl.program_id(1)
    zp_rows, k_rows, za_rows, q_rows, v_rows = (slice(IN_GROUP * i, IN_GROUP * (i + 1)) for i in range(5))

    @pl.when((pl.program_id(0) == 0) & (t == 0))
    def _():
        for rows, col in ((zp_rows, COL_ZP), (k_rows, COL_K), (za_rows, COL_ZA), (q_rows, COL_Q), (v_rows, COL_V)):
            wb_ref[rows, :] = wt_ref[col:col + IN_GROUP, :].astype(bf16)
        wg_ref[...] = wt_ref[COL_G:N_IN, :].astype(bf16)
        wu_ref[...] = _pool_mapped(wt_ref[COL_U:COL_ZP, :].astype(bf16), pw_ref)
        wf_ref[...] = _forget_rows(wt_ref[COL_F:COL_G, :])
        wup_ref[...] = wup32_ref[...].astype(bf16)
    ext = N_META + TM
    body = slice(POOL_PAD, POOL_PAD + ext)

    @pl.when(t == 0)
    def _():
        uext_ref[0:POOL_PAD, :] = jnp.zeros((POOL_PAD, POOL_WIDTH), f32)
        lvl_a_ref[0:POOL_PAD, :] = jnp.zeros((POOL_PAD, lvl_a_ref.shape[1]), f32)
        lvl_b_ref[0:POOL_PAD, :] = jnp.zeros((POOL_PAD, lvl_b_ref.shape[1]), f32)
        uext_ref[POOL_PAD:POOL_PAD + N_META, :] = umeta_ref[...]
        rcarry_ref[...] = r0_ref[...]

    hn = _rmsnorm(x_ref[0], g_ref[...]).astype(bf16)

    gat = _dot_nt(hn, wg_ref[D_MODEL:2 * D_MODEL, :])
    ga_ref[0] = _sigmoid(gat).astype(bf16)

    u = _dot_nt(hn, wu_ref[...])
    zp = _dot_nt(hn, wb_ref[zp_rows, :])
    uext_ref[POOL_PAD + N_META:POOL_PAD + ext, :] = u

    def shifted_sum(ref, shift, cols):
        return ref[body, cols] + ref[POOL_PAD - shift:POOL_PAD - shift + ext, cols]

    g1, g2, g3 = (slice(POOL_GROUP * g, POOL_GROUP * (g + 1)) for g in range(3))
    sums = [shifted_sum(uext_ref, 1, g1)]
    lvl_a_ref[body, :] = shifted_sum(uext_ref, 1, slice(POOL_GROUP, POOL_WIDTH))
    sums.append(shifted_sum(lvl_a_ref, 2, g1))
    lvl_b_ref[body, :] = shifted_sum(lvl_a_ref, 2, slice(POOL_GROUP, 3 * POOL_GROUP))
    sums.append(shifted_sum(lvl_b_ref, 4, g1))
    lvl_a_ref[body, g1] = shifted_sum(lvl_b_ref, 4, g2)
    sums.append(shifted_sum(lvl_a_ref, 8, g1))
    uext_ref[POOL_PAD:POOL_PAD + N_META, :] = uext_ref[POOL_PAD + TM:POOL_PAD + ext, :]

    pooled = jnp.concatenate([sums[g][N_META:] * (1.0 / w) for g, w in enumerate(POOL_WINDOWS)], axis=1) - u
    y_pool = (pooled * ps_ref[...] * (zp * _sigmoid(zp))).astype(bf16)
    gp = _dot_nt(hn, wg_ref[0:D_MODEL, :])
    mp_ref[0] = (_sigmoid(gp) * _dot(y_pool, wup_ref[...])).astype(bf16)

    logf3 = _log_sigmoid(_dot_nt(hn, wf_ref[...]) + _forget_bias(bf_ref))
    r_ref[...] = jnp.zeros(r_ref.shape, f32)
    beta, parts = _decay_parts(logf3, BLK)
    for c in range(NB):
        r_ref[0, 0, c:c + 1, :] = rcarry_ref[0:1, :] * LOG2E
        rcarry_ref[...] = rcarry_ref[...] + beta[BLK * (c + 1) - 1:BLK * (c + 1), :]

    za = _dot_nt(hn, wb_ref[za_rows, :])
    sz_ref[0] = (za * _sigmoid(za)).astype(bf16)

    def store(h, ka):
        kaug_ref[0, h] = ka

    _augmented_keys(_dot_nt(hn, wb_ref[k_rows, :]), parts, store)

    qv = _dot_nt(wb_ref[q_rows.start:v_rows.stop, :], hn)
    qT_ref[0] = (qv[0:ATTN_WIDTH] * (HEAD_DIM ** -0.5 * LOG2E)).astype(bf16)
    for c in range(NB):
        for h in range(N_HEADS):
            rows = slice(ATTN_WIDTH + HEAD_DIM * h, ATTN_WIDTH + HEAD_DIM * (h + 1))
            vT_ref[0, c, h, 0:HEAD_DIM, :] = qv[rows, BLK * c:BLK * (c + 1)].astype(bf16)
            vT_ref[0, c, h, HEAD_DIM:V_ROWS, :] = jnp.ones((V_ROWS - HEAD_DIM, BLK), bf16)


def _attn_kernel(r_ref, qT_ref, qnext_ref, kaug_ref, vT_ref, kmeta_ref, vmeta_ref, sz_ref, o_ref,
                 qaug_ref, qaug_next_ref, s_ref, p_ref, sm_ref, pm_ref, m_ref, mblk_ref, mmeta_ref, alpha_ref, acc_ref,
                 *, n_blocks):
    b = pl.program_id(0)
    t = pl.program_id(1)
    first = QB * t
    heads = range(N_HEADS)

    part_row = lax.broadcasted_iota(jnp.int32, (LANES - HEAD_DIM, BQ), 0)
    ones_rows = ((part_row % N_HEADS == 0) & (part_row < N_SPLIT * N_HEADS)).astype(bf16)
    for h in heads:
        for q_ref, aug_ref in ((qT_ref, qaug_ref), (qnext_ref, qaug_next_ref)):
            aug_ref[h, 0:HEAD_DIM, :] = q_ref[0, HEAD_DIM * h:HEAD_DIM * (h + 1), :]
            aug_ref[h, HEAD_DIM:LANES, :] = ones_rows

    def r_at(j, h):
        return r_ref[(b * n_blocks + j) * N_HEADS + h]

    def col_max(s):
        return jnp.max(s, axis=0, keepdims=True)

    key_index = functools.partial(lax.broadcasted_iota, jnp.int32, dimension=0)
    query_index = functools.partial(lax.broadcasted_iota, jnp.int32, dimension=1)

    def seen_by(diag):
        return slice(BLK * diag, BQ) if diag else slice(None)

    def score_item(j, slot, h, diag=None, meta=False, next_tile=False):
        q_ref = qaug_next_ref if next_tile else qaug_ref
        cols = seen_by(diag)
        s = _dot(kaug_ref[0, h, pl.ds(pl.multiple_of(j * BLK, BLK), BLK), :], q_ref[h, :, cols])
        if diag is not None:
            s = jnp.where(key_index(s.shape) <= query_index(s.shape), s, MASKED)
        s_ref[slot, h, :, cols] = s
        unseen = [jnp.full((1, BQ - s.shape[1]), MASKED, f32)] if s.shape[1] < BQ else []
        mblk_ref[slot, h] = jnp.concatenate(unseen + [col_max(s)], axis=1)
        if meta:
            sm = _dot(kmeta_ref[h], qaug_ref[h])
            sm_ref[h] = sm
            mmeta_ref[h] = col_max(sm)

    def softmax_item(j, slot, h, meta=False, diag=None):
        cols = seen_by(diag)
        assert not (meta and diag)
        ref = r_at(first, h)
        off = ref - r_at(j, h)
        m_old = m_ref[h]
        m_blk = mblk_ref[slot, h] + off
        if meta:
            m_blk = jnp.maximum(m_blk, mmeta_ref[h] + ref)
        m_new = jnp.maximum(m_old, m_blk)
        alpha_ref[h] = jnp.exp2(m_old - m_new)
        m_seen = m_new
        if diag:
            m_seen = jnp.maximum(m_ref[h, :, cols], mblk_ref[slot, h, :, cols] + off)
        m_ref[h] = m_new
        p_ref[h, :, cols] = jnp.exp2(s_ref[slot, h, :, cols] - (m_seen - off)).astype(bf16)
        if meta:
            pm_ref[h] = jnp.exp2(sm_ref[h] - (m_new - ref)).astype(bf16)

    def value_item(j, slot, h, meta=False, diag=None):
        cols = seen_by(diag)
        pv = _dot(vT_ref[0, j, h], p_ref[h, :, cols])
        if meta:
            pv = pv + _dot(vmeta_ref[h], pm_ref[h])
        acc_ref[h, :, cols] = acc_ref[h, :, cols] * alpha_ref[h, :, cols] + pv

    def run(score_blocks, finish_blocks):
        score_items = [(blk + (h,), kw) for blk, kw in score_blocks for h in heads]
        finish_items = [(blk + (h,), kw) for blk, kw in finish_blocks for h in heads]
        for i in range(max(len(score_items), len(finish_items) + SKEW + 1)):
            if i < len(score_items):
                args, kw = score_items[i]
                score_item(*args, **kw)
            if 0 <= i - SKEW - 1 < len(finish_items):
                args, kw = finish_items[i - SKEW - 1]
                value_item(*args, **kw)
            if 0 <= i - SKEW < len(finish_items):
                args, kw = finish_items[i - SKEW]
                softmax_item(*args, **kw)

    m_ref[...] = jnp.full(m_ref.shape, MASKED, f32)
    acc_ref[...] = jnp.zeros(acc_ref.shape, f32)

    @pl.when(first == 0)
    def _():
        run([((first, 0), dict(diag=0))], [])

    @pl.when(first > 0)
    def _():
        def body(k, carry):
            run([((2 * k + 1, 1), {}), ((2 * k + 2, 0), {})],
                [((2 * k, 0), {}), ((2 * k + 1, 1), {})])
            return carry

        lax.fori_loop(0, t - 1, body, 0)
        run([((first - 1, 1), {}), ((first, 0), dict(diag=0))],
            [((first - 2, 0), {}), ((first - 1, 1), {})])

    run([((first + 1, 1), dict(diag=1, meta=True)), ((0, 0), dict(next_tile=True))],
        [((first, 0), dict(meta=True)), ((first + 1, 1), dict(diag=1))])

    for pair in range(N_HEADS // 2):
        halves = []
        for h in (2 * pair, 2 * pair + 1):
            a = acc_ref[h]
            halves.append(a[0:HEAD_DIM] * (1.0 / a[HEAD_DIM:HEAD_DIM + 1]))
        cols = slice(LANES * pair, LANES * (pair + 1))
        o_ref[0, :, cols] = (jnp.concatenate(halves, axis=0).T * sz_ref[0, :, cols].astype(f32)).astype(bf16)


def _out_kernel(x_ref, y_ref, ga_ref, mp_ref, wua32_ref, wout32_ref, g_ref, out_ref, wua_ref, wout_ref):
    @pl.when((pl.program_id(0) == 0) & (pl.program_id(1) == 0))
    def _():
        wua_ref[...] = wua32_ref[...].astype(bf16)
        wout_ref[...] = wout32_ref[...].astype(bf16)

    merged = mp_ref[0].astype(f32) + ga_ref[0].astype(f32) * _dot(y_ref[0], wua_ref[...])
    h_out = x_ref[0] + _dot(merged.astype(bf16), wout_ref[...])
    out_ref[0] = _rmsnorm(h_out, g_ref[...])


def _const(shape):
    return pl.BlockSpec(shape, lambda *_: (0,) * len(shape), pipeline_mode=pl.Buffered(1))


def kernel(x, meta_tokens, norm_g, w_in, b_forget, pool_w, pool_scale, w_up_pool, w_up_attn, w_out, final_norm_g):
    batch, seq, _ = x.shape
    n_tiles = seq // TM
    n_blocks = seq // BLK

    w_t = jnp.transpose(w_in[0])
    b_f = b_forget.reshape(N_HEADS)
    g_in = norm_g[0].reshape(1, D_MODEL)
    g_out = final_norm_g.reshape(1, D_MODEL)
    pw = pool_w[0]
    ps = pool_scale[0].reshape(1, POOL_WIDTH)
    w_upp, w_upa, w_o = w_up_pool[0], w_up_attn[0], w_out[0]

    u_meta, k_meta, v_meta, r0 = pl.pallas_call(
        _meta_kernel,
        out_shape=(jax.ShapeDtypeStruct((N_META, POOL_WIDTH), f32),
                   jax.ShapeDtypeStruct((N_HEADS, N_META, LANES), bf16),
                   jax.ShapeDtypeStruct((N_HEADS, V_ROWS, N_META), bf16),
                   jax.ShapeDtypeStruct((8, LANES), f32)),
        grid=(1,),
        in_specs=[_const((N_META, D_MODEL)), _const((1, D_MODEL))]
        + [pl.BlockSpec((rows, D_MODEL), functools.partial(lambda r, i: (r, 0), col // rows),
                        pipeline_mode=pl.Buffered(1))
           for col, rows in ((COL_U, IN_GROUP), (COL_K, IN_GROUP), (COL_V, IN_GROUP), (COL_F, N_HEADS))]
        + [pl.BlockSpec(memory_space=pltpu.SMEM), _const((len(POOL_WINDOWS), POOL_GROUP, POOL_GROUP))],
        out_specs=[_const((N_META, POOL_WIDTH)), _const((N_HEADS, N_META, LANES)),
                   _const((N_HEADS, V_ROWS, N_META)), _const((8, LANES))],
        compiler_params=pltpu.CompilerParams(vmem_limit_bytes=VMEM_LIMIT),
        name="meta_proj",
    )(meta_tokens, g_in, w_t, w_t, w_t, w_t, b_f, pw)

    mp, ga, sz, qT, vT, kaug, r_tiles = pl.pallas_call(
        _proj_kernel,
        grid=(batch, n_tiles),
        in_specs=[
            pl.BlockSpec((1, TM, D_MODEL), lambda b, t: (b, t, 0)),
            _const((1, D_MODEL)),
            _const((N_IN, D_MODEL)),
            pl.BlockSpec(memory_space=pltpu.SMEM),
            _const((len(POOL_WINDOWS), POOL_GROUP, POOL_GROUP)),
            _const((1, POOL_WIDTH)),
            _const((POOL_WIDTH, D_MODEL)),
            _const((N_META, POOL_WIDTH)),
            _const((8, LANES)),
        ],
        out_specs=[
            pl.BlockSpec((1, TM, D_MODEL), lambda b, t: (b, t, 0)),
            pl.BlockSpec((1, TM, D_MODEL), lambda b, t: (b, t, 0)),
            pl.BlockSpec((1, TM, ATTN_WIDTH), lambda b, t: (b, t, 0)),
            pl.BlockSpec((1, ATTN_WIDTH, TM), lambda b, t: (b, 0, t)),
            pl.BlockSpec((1, NB, N_HEADS, V_ROWS, BLK), lambda b, t: (b, t, 0, 0, 0)),
            pl.BlockSpec((1, N_HEADS, TM, LANES), lambda b, t: (b, 0, t, 0)),
            pl.BlockSpec((1, 1, 8, LANES), lambda b, t: (b, t, 0, 0)),
        ],
        out_shape=(
            jax.ShapeDtypeStruct((batch, seq, D_MODEL), bf16),
            jax.ShapeDtypeStruct((batch, seq, D_MODEL), bf16),
            jax.ShapeDtypeStruct((batch, seq, ATTN_WIDTH), bf16),
            jax.ShapeDtypeStruct((batch, ATTN_WIDTH, seq), bf16),
            jax.ShapeDtypeStruct((batch, n_blocks, N_HEADS, V_ROWS, BLK), bf16),
            jax.ShapeDtypeStruct((batch, N_HEADS, seq, LANES), bf16),
            jax.ShapeDtypeStruct((batch, n_tiles, 8, LANES), f32),
        ),
        scratch_shapes=[pltpu.VMEM((POOL_PAD + N_META + TM, POOL_WIDTH), f32),
                        pltpu.VMEM((POOL_PAD + N_META + TM, 3 * POOL_GROUP), f32),
                        pltpu.VMEM((POOL_PAD + N_META + TM, 2 * POOL_GROUP), f32),
                        pltpu.VMEM((8, LANES), f32),
                        pltpu.VMEM((5 * IN_GROUP, D_MODEL), bf16),
                        pltpu.VMEM((2 * D_MODEL, D_MODEL), bf16),
                        pltpu.VMEM((POOL_WIDTH, D_MODEL), bf16),
                        pltpu.VMEM((LANES, D_MODEL), bf16),
                        pltpu.VMEM((POOL_WIDTH, D_MODEL), bf16)],
        compiler_params=pltpu.CompilerParams(dimension_semantics=("arbitrary", "arbitrary"),
                                             vmem_limit_bytes=VMEM_LIMIT),
        name="in_proj",
    )(x, g_in, w_t, b_f, pw, ps, w_upp, u_meta, r0)

    r_blocks = r_tiles[:, :, :NB, :N_HEADS].reshape(batch * n_blocks * N_HEADS)

    y_attn = pl.pallas_call(
        functools.partial(_attn_kernel, n_blocks=n_blocks),
        grid=(batch, seq // BQ),
        in_specs=[
            pl.BlockSpec(memory_space=pltpu.SMEM),
            pl.BlockSpec((1, ATTN_WIDTH, BQ), lambda b, t: (b, 0, t)),
            pl.BlockSpec((1, ATTN_WIDTH, BQ), lambda b, t: (b, 0, jnp.minimum(t + 1, seq // BQ - 1))),
            pl.BlockSpec((1, N_HEADS, seq, LANES), lambda b, t: (b, 0, 0, 0)),
            pl.BlockSpec((1, n_blocks, N_HEADS, V_ROWS, BLK), lambda b, t: (b, 0, 0, 0, 0)),
            _const((N_HEADS, N_META, LANES)),
            _const((N_HEADS, V_ROWS, N_META)),
            pl.BlockSpec((1, BQ, ATTN_WIDTH), lambda b, t: (b, t, 0)),
        ],
        out_specs=pl.BlockSpec((1, BQ, ATTN_WIDTH), lambda b, t: (b, t, 0)),
        out_shape=jax.ShapeDtypeStruct((batch, seq, ATTN_WIDTH), bf16),
        scratch_shapes=[pltpu.VMEM((N_HEADS, LANES, BQ), bf16),
                        pltpu.VMEM((N_HEADS, LANES, BQ), bf16),
                        pltpu.VMEM((2, N_HEADS, BLK, BQ), f32),
                        pltpu.VMEM((N_HEADS, BLK, BQ), bf16),
                        pltpu.VMEM((N_HEADS, N_META, BQ), f32),
                        pltpu.VMEM((N_HEADS, N_META, BQ), bf16),
                        pltpu.VMEM((N_HEADS, 1, BQ), f32),
                        pltpu.VMEM((2, N_HEADS, 1, BQ), f32),
                        pltpu.VMEM((N_HEADS, 1, BQ), f32),
                        pltpu.VMEM((N_HEADS, 1, BQ), f32),
                        pltpu.VMEM((N_HEADS, V_ROWS, BQ), f32)],
        compiler_params=pltpu.CompilerParams(dimension_semantics=("arbitrary", "arbitrary"),
                                             vmem_limit_bytes=VMEM_LIMIT),
        name="attention",
    )(r_blocks, qT, qT, kaug, vT, k_meta, v_meta, sz)

    return pl.pallas_call(
        _out_kernel,
        grid=(batch, seq // TM_OUT),
        in_specs=[
            pl.BlockSpec((1, TM_OUT, D_MODEL), lambda b, t: (b, t, 0)),
            pl.BlockSpec((1, TM_OUT, ATTN_WIDTH), lambda b, t: (b, t, 0)),
            pl.BlockSpec((1, TM_OUT, D_MODEL), lambda b, t: (b, t, 0)),
            pl.BlockSpec((1, TM_OUT, D_MODEL), lambda b, t: (b, t, 0)),
            _const((ATTN_WIDTH, D_MODEL)),
            _const((D_MODEL, D_MODEL)),
            _const((1, D_MODEL)),
        ],
        out_specs=pl.BlockSpec((1, TM_OUT, D_MODEL), lambda b, t: (b, t, 0)),
        out_shape=jax.ShapeDtypeStruct((batch, seq, D_MODEL), f32),
        scratch_shapes=[pltpu.VMEM((ATTN_WIDTH, D_MODEL), bf16),
                        pltpu.VMEM((D_MODEL, D_MODEL), bf16)],
        compiler_params=pltpu.CompilerParams(dimension_semantics=("arbitrary", "arbitrary"),
                                             vmem_limit_bytes=VMEM_LIMIT),
        name="out_proj",
    )(x, y_attn, ga, mp, w_upa, w_o, g_out)
```

```python
import functools

import jax
import jax.numpy as jnp
from jax import lax
from jax.experimental import pallas as pl
from jax.experimental.pallas import tpu as pltpu

D_MODEL = 1024
N_META = 16
POOL_WIDTH = 512
POOL_WINDOWS = (2, 4, 8, 16)
POOL_GROUP = POOL_WIDTH // len(POOL_WINDOWS)
N_HEADS = 8
HEAD_DIM = 64
ATTN_WIDTH = N_HEADS * HEAD_DIM
RMS_EPS = 1e-6

LANES = 128
BLK = 256
BQ = 512
QB = BQ // BLK
assert QB == 2
TM = 512
NB = TM // BLK
TM_OUT = 1024
POOL_PAD = 8
assert POOL_WINDOWS == (2, 4, 8, 16) and POOL_PAD >= POOL_WINDOWS[-1] // 2
V_ROWS = HEAD_DIM + 16
N_SPLIT = 3
MASKED = -1e30
SKEW = 2
LOG2E = 1.4426950408889634
VMEM_LIMIT = 56 * 1024 * 1024
assert POOL_WIDTH == ATTN_WIDTH
IN_GROUP = POOL_WIDTH
COL_U, COL_ZP, COL_Q, COL_K, COL_V, COL_ZA, COL_F = (IN_GROUP * i for i in range(7))
COL_G = COL_F + N_HEADS
F_ROWS = 16
N_IN = COL_G + 2 * D_MODEL

f32 = jnp.float32
bf16 = jnp.bfloat16


def _rmsnorm(x, g):
    return x * lax.rsqrt(jnp.mean(x * x, axis=-1, keepdims=True) + RMS_EPS) * g


def _sigmoid(x):
    return 1.0 / (1.0 + jnp.exp(-x))


def _log_sigmoid(x):
    return jnp.minimum(x, 0.0) - jnp.log1p(jnp.exp(-jnp.abs(x)))


def _dot(a, b):
    return jnp.dot(a, b, preferred_element_type=f32)


def _dot_nt(a, b):
    return lax.dot_general(a, b, (((1,), (1,)), ((), ())), preferred_element_type=f32)


def _lane_iota(shape):
    return lax.broadcasted_iota(jnp.int32, shape, 1)


def _decay_parts(logf3, rows):
    n = logf3.shape[0] // rows
    r = lax.broadcasted_iota(jnp.int32, (rows, rows), 0)
    c = lax.broadcasted_iota(jnp.int32, (rows, rows), 1)
    tri = (c <= r).astype(f32)
    side_by_side = jnp.concatenate([logf3[rows * i:rows * (i + 1)] for i in range(n)], axis=1)
    sums = jnp.dot(tri, side_by_side, precision=lax.Precision.HIGHEST, preferred_element_type=f32)
    beta = jnp.concatenate([sums[:, LANES * i:LANES * (i + 1)] for i in range(n)], axis=0)
    return beta, _split_decay(beta)


def _decay_parts_from_rows(logf_t, rows):
    n = logf_t.shape[1] // rows
    k = lax.broadcasted_iota(jnp.int32, (rows, rows), 0)
    j = lax.broadcasted_iota(jnp.int32, (rows, rows), 1)
    stacked = jnp.concatenate([logf_t[:, rows * i:rows * (i + 1)] for i in range(n)], axis=0)
    sums_t = jnp.dot(stacked, (k <= j).astype(f32), precision=lax.Precision.HIGHEST, preferred_element_type=f32)
    unused = jnp.zeros((LANES - N_SPLIT * N_HEADS, rows), f32)
    beta = jnp.concatenate(
        [jnp.concatenate([sums_t[N_HEADS * i:N_HEADS * (i + 1)]] * N_SPLIT + [unused], axis=0).T for i in range(n)],
        axis=0)
    return beta, _split_decay(beta)


def _split_decay(beta):
    nb = beta * (-LOG2E)
    hi = nb.astype(bf16).astype(f32)
    mid = (nb - hi).astype(bf16).astype(f32)
    lo = (nb - hi - mid).astype(bf16).astype(f32)
    lane = _lane_iota(nb.shape)
    return jnp.where(lane < N_HEADS, hi, jnp.where(lane < 2 * N_HEADS, mid, lo))


def _augmented_keys(kproj, parts, store):
    lane = _lane_iota((kproj.shape[0], LANES))
    for h in range(N_HEADS):
        slab = kproj[:, LANES * (h // 2):LANES * (h // 2 + 1)]
        if h % 2:
            slab = pltpu.roll(slab, HEAD_DIM, axis=1)
        store(h, jnp.where(lane < HEAD_DIM, slab, pltpu.roll(parts, HEAD_DIM - h, axis=1)).astype(bf16))


def _pool_mapped(wu_t, pw_ref):
    return jnp.concatenate([_dot(pw_ref[g].T.astype(bf16), wu_t[POOL_GROUP * g:POOL_GROUP * (g + 1)])
                            for g in range(len(POOL_WINDOWS))], axis=0).astype(bf16)


def _forget_rows(wf_t):
    unused = jnp.zeros((LANES - N_SPLIT * N_HEADS, wf_t.shape[1]), f32)
    return jnp.concatenate([wf_t] * N_SPLIT + [unused], axis=0).astype(bf16)


def _forget_bias(b_ref, shape, axis):
    index = lax.broadcasted_iota(jnp.int32, shape, axis)
    bias = jnp.zeros(shape, f32)
    for h in range(N_HEADS):
        bias = jnp.where((index % N_HEADS == h) & (index < N_SPLIT * N_HEADS), b_ref[h], bias)
    return bias


def _meta_kernel(meta_ref, g_ref, wu_ref, wk_ref, wv_ref, wf_ref, bf_ref, pw_ref,
                 umeta_ref, kmeta_ref, vmeta_ref, r0_ref):
    hn = _rmsnorm(meta_ref[...], g_ref[...]).astype(bf16)
    umeta_ref[...] = _dot_nt(hn, _pool_mapped(wu_ref[...].astype(bf16), pw_ref))
    vt = _dot_nt(wv_ref[...].astype(bf16), hn)
    for h in range(N_HEADS):
        vmeta_ref[h, 0:HEAD_DIM, :] = vt[HEAD_DIM * h:HEAD_DIM * (h + 1), :].astype(bf16)
        vmeta_ref[h, HEAD_DIM:V_ROWS, :] = jnp.ones((V_ROWS - HEAD_DIM, N_META), bf16)
    logf3 = _log_sigmoid(_dot_nt(hn, _forget_rows(wf_ref[...])) + _forget_bias(bf_ref, (1, LANES), 1))
    beta, parts = _decay_parts(logf3, N_META)

    def store(h, ka):
        kmeta_ref[h] = ka

    _augmented_keys(_dot_nt(hn, wk_ref[...].astype(bf16)), parts, store)
    r0_ref[...] = jnp.broadcast_to(beta[N_META - 1:N_META, :], r0_ref.shape)


def _proj_kernel(x_ref, g_ref, wt_ref, bf_ref, pw_ref, ps_ref, wup32_ref,
                 umeta_ref, r0_ref,
                 mp_ref, ga_ref, sz_ref, qT_ref, vT_ref, kaug_ref, r_ref,
                 uext_ref, lvl_a_ref, lvl_b_ref, rcarry_ref, wb_ref, wg_ref, wu_ref, wup_ref):
    t = pl.program_id(1)
    zp_rows, k_rows, za_rows, q_rows, v_rows = (slice(IN_GROUP * i, IN_GROUP * (i + 1)) for i in range(5))
    f_rows = slice(v_rows.stop, v_rows.stop + F_ROWS)

    @pl.when((pl.program_id(0) == 0) & (t == 0))
    def _():
        for rows, col in ((zp_rows, COL_ZP), (k_rows, COL_K), (za_rows, COL_ZA), (q_rows, COL_Q), (v_rows, COL_V)):
            wb_ref[rows, :] = wt_ref[col:col + IN_GROUP, :].astype(bf16)
        wb_ref[f_rows, :] = wt_ref[COL_F:COL_F + F_ROWS, :].astype(bf16)
        wg_ref[...] = wt_ref[COL_G:N_IN, :].astype(bf16)
        wu_ref[...] = _pool_mapped(wt_ref[COL_U:COL_ZP, :].astype(bf16), pw_ref)
        wup_ref[...] = wup32_ref[...].astype(bf16)
    ext = N_META + TM
    body = slice(POOL_PAD, POOL_PAD + ext)

    @pl.when(t == 0)
    def _():
        uext_ref[0:POOL_PAD, :] = jnp.zeros((POOL_PAD, POOL_WIDTH), f32)
        lvl_a_ref[0:POOL_PAD, :] = jnp.zeros((POOL_PAD, lvl_a_ref.shape[1]), f32)
        lvl_b_ref[0:POOL_PAD, :] = jnp.zeros((POOL_PAD, lvl_b_ref.shape[1]), f32)
        uext_ref[POOL_PAD:POOL_PAD + N_META, :] = umeta_ref[...]
        rcarry_ref[...] = r0_ref[...]

    hn = _rmsnorm(x_ref[0], g_ref[...]).astype(bf16)

    qv = _dot_nt(wb_ref[q_rows.start:, :], hn)
    qT_ref[0] = (qv[0:ATTN_WIDTH] * (HEAD_DIM ** -0.5 * LOG2E)).astype(bf16)
    for c in range(NB):
        for h in range(N_HEADS):
            rows = slice(ATTN_WIDTH + HEAD_DIM * h, ATTN_WIDTH + HEAD_DIM * (h + 1))
            vT_ref[0, c, h, 0:HEAD_DIM, :] = qv[rows, BLK * c:BLK * (c + 1)].astype(bf16)
            vT_ref[0, c, h, HEAD_DIM:V_ROWS, :] = jnp.ones((V_ROWS - HEAD_DIM, BLK), bf16)

    logf_t = _log_sigmoid(qv[2 * IN_GROUP:2 * IN_GROUP + N_HEADS] + _forget_bias(bf_ref, (N_HEADS, TM), 0))
    r_ref[...] = jnp.zeros(r_ref.shape, f32)
    beta, parts = _decay_parts_from_rows(logf_t, BLK)
    for c in range(NB):
        r_ref[0, 0, c:c + 1, :] = rcarry_ref[0:1, :] * LOG2E
        rcarry_ref[...] = rcarry_ref[...] + beta[BLK * (c + 1) - 1:BLK * (c + 1), :]

    gat = _dot_nt(hn, wg_ref[D_MODEL:2 * D_MODEL, :])
    ga_ref[0] = _sigmoid(gat).astype(bf16)

    u = _dot_nt(hn, wu_ref[...])
    zp = _dot_nt(hn, wb_ref[zp_rows, :])
    uext_ref[POOL_PAD + N_META:POOL_PAD + ext, :] = u

    def shifted_sum(ref, shift, cols):
        return ref[body, cols] + ref[POOL_PAD - shift:POOL_PAD - shift + ext, cols]

    g1, g2, g3 = (slice(POOL_GROUP * g, POOL_GROUP * (g + 1)) for g in range(3))
    sums = [shifted_sum(uext_ref, 1, g1)]
    lvl_a_ref[body, :] = shifted_sum(uext_ref, 1, slice(POOL_GROUP, POOL_WIDTH))
    sums.append(shifted_sum(lvl_a_ref, 2, g1))
    lvl_b_ref[body, :] = shifted_sum(lvl_a_ref, 2, slice(POOL_GROUP, 3 * POOL_GROUP))
    sums.append(shifted_sum(lvl_b_ref, 4, g1))
    lvl_a_ref[body, g1] = shifted_sum(lvl_b_ref, 4, g2)
    sums.append(shifted_sum(lvl_a_ref, 8, g1))
    uext_ref[POOL_PAD:POOL_PAD + N_META, :] = uext_ref[POOL_PAD + TM:POOL_PAD + ext, :]

    pooled = jnp.concatenate([sums[g][N_META:] * (1.0 / w) for g, w in enumerate(POOL_WINDOWS)], axis=1) - u
    y_pool = (pooled * ps_ref[...] * (zp * _sigmoid(zp))).astype(bf16)
    gp = _dot_nt(hn, wg_ref[0:D_MODEL, :])
    mp_ref[0] = (_sigmoid(gp) * _dot(y_pool, wup_ref[...])).astype(bf16)

    za =_dot_nt(hn, wb_ref[za_rows, :])
    sz_ref[0] = (za * _sigmoid(za)).astype(bf16)

    def store(h, ka):
        kaug_ref[0, h] = ka

    _augmented_keys(_dot_nt(hn, wb_ref[k_rows, :]), parts, store)


def _attn_kernel(r_ref, qT_ref, qnext_ref, kaug_ref, vT_ref, kmeta_ref, vmeta_ref, sz_ref, o_ref,
                 qaug_ref, qaug_next_ref, s_ref, p_ref, sm_ref, pm_ref, m_ref, mblk_ref, mmeta_ref, alpha_ref, acc_ref,
                 *, n_blocks):
    b = pl.program_id(0)
    t = pl.program_id(1)
    first = QB * t
    heads = range(N_HEADS)

    part_row = lax.broadcasted_iota(jnp.int32, (LANES - HEAD_DIM, BQ), 0)
    ones_rows = ((part_row % N_HEADS == 0) & (part_row < N_SPLIT * N_HEADS)).astype(bf16)
    for h in heads:
        for q_ref, aug_ref in ((qT_ref, qaug_ref), (qnext_ref, qaug_next_ref)):
            aug_ref[h, 0:HEAD_DIM, :] = q_ref[0, HEAD_DIM * h:HEAD_DIM * (h + 1), :]
            aug_ref[h, HEAD_DIM:LANES, :] = ones_rows

    def r_at(j, h):
        return r_ref[(b * n_blocks + j) * N_HEADS + h]

    def col_max(s):
        return jnp.max(s, axis=0, keepdims=True)

    key_index = functools.partial(lax.broadcasted_iota, jnp.int32, dimension=0)
    query_index = functools.partial(lax.broadcasted_iota, jnp.int32, dimension=1)

    def seen_by(diag):
        return slice(BLK * diag, BQ) if diag else slice(None)

    def score_item(j, slot, h, diag=None, meta=False, next_tile=False):
        q_ref = qaug_next_ref if next_tile else qaug_ref
        cols = seen_by(diag)
        s = _dot(kaug_ref[0, h, pl.ds(pl.multiple_of(j * BLK, BLK), BLK), :], q_ref[h, :, cols])
        if diag is not None:
            s = jnp.where(key_index(s.shape) <= query_index(s.shape), s, MASKED)
        s_ref[slot, h, :, cols] = s
        unseen = [jnp.full((1, BQ - s.shape[1]), MASKED, f32)] if s.shape[1] < BQ else []
        mblk_ref[slot, h] = jnp.concatenate(unseen + [col_max(s)], axis=1)
        if meta:
            sm = _dot(kmeta_ref[h], qaug_ref[h])
            sm_ref[h] = sm
            mmeta_ref[h] = col_max(sm)

    def softmax_item(j, slot, h, meta=False, diag=None):
        cols = seen_by(diag)
        assert not (meta and diag)
        ref = r_at(first, h)
        off = ref - r_at(j, h)
        m_old = m_ref[h]
        m_blk = mblk_ref[slot, h] + off
        if meta:
            m_blk = jnp.maximum(m_blk, mmeta_ref[h] + ref)
        m_new = jnp.maximum(m_old, m_blk)
        alpha_ref[h] = jnp.exp2(m_old - m_new)
        m_seen = m_new
        if diag:
            m_seen = jnp.maximum(m_ref[h, :, cols], mblk_ref[slot, h, :, cols] + off)
        m_ref[h] = m_new
        p_ref[h, :, cols] = jnp.exp2(s_ref[slot, h, :, cols] - (m_seen - off)).astype(bf16)
        if meta:
            pm_ref[h] = jnp.exp2(sm_ref[h] - (m_new - ref)).astype(bf16)

    def value_item(j, slot, h, meta=False, diag=None):
        cols = seen_by(diag)
        pv = _dot(vT_ref[0, j, h], p_ref[h, :, cols])
        if meta:
            pv = pv + _dot(vmeta_ref[h], pm_ref[h])
        acc_ref[h, :, cols] = acc_ref[h, :, cols] * alpha_ref[h, :, cols] + pv

    def run(score_blocks, finish_blocks):
        score_items = [(blk + (h,), kw) for blk, kw in score_blocks for h in heads]
        finish_items = [(blk + (h,), kw) for blk, kw in finish_blocks for h in heads]
        for i in range(max(len(score_items), len(finish_items) + SKEW + 1)):
            if i < len(score_items):
                args, kw = score_items[i]
                score_item(*args, **kw)
            if 0 <= i - SKEW - 1 < len(finish_items):
                args, kw = finish_items[i - SKEW - 1]
                value_item(*args, **kw)
            if 0 <= i - SKEW < len(finish_items):
                args, kw = finish_items[i - SKEW]
                softmax_item(*args, **kw)

    m_ref[...] = jnp.full(m_ref.shape, MASKED, f32)
    acc_ref[...] = jnp.zeros(acc_ref.shape, f32)

    @pl.when(first == 0)
    def _():
        run([((first, 0), dict(diag=0))], [])

    @pl.when(first > 0)
    def _():
        def body(k, carry):
            run([((2 * k + 1, 1), {}), ((2 * k + 2, 0), {})],
                [((2 * k, 0), {}), ((2 * k + 1, 1), {})])
            return carry

        lax.fori_loop(0, t - 1, body, 0)
        run([((first - 1, 1), {}), ((first, 0), dict(diag=0))],
            [((first - 2, 0), {}), ((first - 1, 1), {})])

    run([((first + 1, 1), dict(diag=1, meta=True)), ((0, 0), dict(next_tile=True))],
        [((first, 0), dict(meta=True)), ((first + 1, 1), dict(diag=1))])

    for pair in range(N_HEADS // 2):
        halves = []
        for h in (2 * pair, 2 * pair + 1):
            a = acc_ref[h]
            halves.append(a[0:HEAD_DIM] * (1.0 / a[HEAD_DIM:HEAD_DIM + 1]))
        cols = slice(LANES * pair, LANES * (pair + 1))
        o_ref[0, :, cols] = (jnp.concatenate(halves, axis=0).T * sz_ref[0, :, cols].astype(f32)).astype(bf16)


def _out_kernel(x_ref, y_ref, ga_ref, mp_ref, wua32_ref, wout32_ref, g_ref, out_ref, wua_ref, wout_ref):
    @pl.when((pl.program_id(0) == 0) & (pl.program_id(1) == 0))
    def _():
        wua_ref[...] = wua32_ref[...].astype(bf16)
        wout_ref[...] = wout32_ref[...].astype(bf16)

    merged = mp_ref[0].astype(f32) + ga_ref[0].astype(f32) * _dot(y_ref[0], wua_ref[...])
    h_out = x_ref[0] + _dot(merged.astype(bf16), wout_ref[...])
    out_ref[0] = _rmsnorm(h_out, g_ref[...])


def _const(shape):
    return pl.BlockSpec(shape, lambda *_: (0,) * len(shape), pipeline_mode=pl.Buffered(1))


def kernel(x, meta_tokens, norm_g, w_in, b_forget, pool_w, pool_scale, w_up_pool, w_up_attn, w_out, final_norm_g):
    batch, seq, _ = x.shape
    n_tiles = seq // TM
    n_blocks = seq // BLK

    w_t = jnp.transpose(w_in[0])
    b_f = b_forget.reshape(N_HEADS)
    g_in = norm_g[0].reshape(1, D_MODEL)
    g_out = final_norm_g.reshape(1, D_MODEL)
    pw = pool_w[0]
    ps = pool_scale[0].reshape(1, POOL_WIDTH)
    w_upp, w_upa, w_o = w_up_pool[0], w_up_attn[0], w_out[0]

    u_meta, k_meta, v_meta, r0 = pl.pallas_call(
        _meta_kernel,
        out_shape=(jax.ShapeDtypeStruct((N_META, POOL_WIDTH), f32),
                   jax.ShapeDtypeStruct((N_HEADS, N_META, LANES), bf16),
                   jax.ShapeDtypeStruct((N_HEADS, V_ROWS, N_META), bf16),
                   jax.ShapeDtypeStruct((8, LANES), f32)),
        grid=(1,),
        in_specs=[_const((N_META, D_MODEL)), _const((1, D_MODEL))]
        + [pl.BlockSpec((rows, D_MODEL), functools.partial(lambda r, i: (r, 0), col // rows),
                        pipeline_mode=pl.Buffered(1))
           for col, rows in ((COL_U, IN_GROUP), (COL_K, IN_GROUP), (COL_V, IN_GROUP), (COL_F, N_HEADS))]
        + [pl.BlockSpec(memory_space=pltpu.SMEM), _const((len(POOL_WINDOWS), POOL_GROUP, POOL_GROUP))],
        out_specs=[_const((N_META, POOL_WIDTH)), _const((N_HEADS, N_META, LANES)),
                   _const((N_HEADS, V_ROWS, N_META)), _const((8, LANES))],
        compiler_params=pltpu.CompilerParams(vmem_limit_bytes=VMEM_LIMIT),
        name="meta_proj",
    )(meta_tokens, g_in, w_t, w_t, w_t, w_t, b_f, pw)

    mp, ga, sz, qT, vT, kaug, r_tiles = pl.pallas_call(
        _proj_kernel,
        grid=(batch, n_tiles),
        in_specs=[
            pl.BlockSpec((1, TM, D_MODEL), lambda b, t: (b, t, 0)),
            _const((1, D_MODEL)),
            _const((N_IN, D_MODEL)),
            pl.BlockSpec(memory_space=pltpu.SMEM),
            _const((len(POOL_WINDOWS), POOL_GROUP, POOL_GROUP)),
            _const((1, POOL_WIDTH)),
            _const((POOL_WIDTH, D_MODEL)),
            _const((N_META, POOL_WIDTH)),
            _const((8, LANES)),
        ],
        out_specs=[
            pl.BlockSpec((1, TM, D_MODEL), lambda b, t: (b, t, 0)),
            pl.BlockSpec((1, TM, D_MODEL), lambda b, t: (b, t, 0)),
            pl.BlockSpec((1, TM, ATTN_WIDTH), lambda b, t: (b, t, 0)),
            pl.BlockSpec((1, ATTN_WIDTH, TM), lambda b, t: (b, 0, t)),
            pl.BlockSpec((1, NB, N_HEADS, V_ROWS, BLK), lambda b, t: (b, t, 0, 0, 0)),
            pl.BlockSpec((1, N_HEADS, TM, LANES), lambda b, t: (b, 0, t, 0)),
            pl.BlockSpec((1, 1, 8, LANES), lambda b, t: (b, t, 0, 0)),
        ],
        out_shape=(
            jax.ShapeDtypeStruct((batch, seq, D_MODEL), bf16),
            jax.ShapeDtypeStruct((batch, seq, D_MODEL), bf16),
            jax.ShapeDtypeStruct((batch, seq, ATTN_WIDTH), bf16),
            jax.ShapeDtypeStruct((batch, ATTN_WIDTH, seq), bf16),
            jax.ShapeDtypeStruct((batch, n_blocks, N_HEADS, V_ROWS, BLK), bf16),
            jax.ShapeDtypeStruct((batch, N_HEADS, seq, LANES), bf16),
            jax.ShapeDtypeStruct((batch, n_tiles, 8, LANES), f32),
        ),
        scratch_shapes=[pltpu.VMEM((POOL_PAD + N_META + TM, POOL_WIDTH), f32),
                        pltpu.VMEM((POOL_PAD + N_META + TM, 3 * POOL_GROUP), f32),
                        pltpu.VMEM((POOL_PAD + N_META + TM, 2 * POOL_GROUP), f32),
                        pltpu.VMEM((8, LANES), f32),
                        pltpu.VMEM((5 * IN_GROUP + F_ROWS, D_MODEL), bf16),
                        pltpu.VMEM((2 * D_MODEL, D_MODEL), bf16),
                        pltpu.VMEM((POOL_WIDTH, D_MODEL), bf16),
                        pltpu.VMEM((POOL_WIDTH, D_MODEL), bf16)],
        compiler_params=pltpu.CompilerParams(dimension_semantics=("arbitrary", "arbitrary"),
                                             vmem_limit_bytes=VMEM_LIMIT),
        name="in_proj",
    )(x, g_in, w_t, b_f, pw, ps, w_upp, u_meta, r0)

    r_blocks = r_tiles[:, :, :NB, :N_HEADS].reshape(batch * n_blocks * N_HEADS)

    y_attn = pl.pallas_call(
        functools.partial(_attn_kernel, n_blocks=n_blocks),
        grid=(batch, seq // BQ),
        in_specs=[
            pl.BlockSpec(memory_space=pltpu.SMEM),
            pl.BlockSpec((1, ATTN_WIDTH, BQ), lambda b, t: (b, 0, t)),
            pl.BlockSpec((1, ATTN_WIDTH, BQ), lambda b, t: (b, 0, jnp.minimum(t + 1, seq // BQ - 1))),
            pl.BlockSpec((1, N_HEADS, seq, LANES), lambda b, t: (b, 0, 0, 0)),
            pl.BlockSpec((1, n_blocks, N_HEADS, V_ROWS, BLK), lambda b, t: (b, 0, 0, 0, 0)),
            _const((N_HEADS, N_META, LANES)),
            _const((N_HEADS, V_ROWS, N_META)),
            pl.BlockSpec((1, BQ, ATTN_WIDTH), lambda b, t: (b, t, 0)),
        ],
        out_specs=pl.BlockSpec((1, BQ, ATTN_WIDTH), lambda b, t: (b, t, 0)),
        out_shape=jax.ShapeDtypeStruct((batch, seq, ATTN_WIDTH), bf16),
        scratch_shapes=[pltpu.VMEM((N_HEADS, LANES, BQ), bf16),
                        pltpu.VMEM((N_HEADS, LANES, BQ), bf16),
                        pltpu.VMEM((2, N_HEADS, BLK, BQ), f32),
                        pltpu.VMEM((N_HEADS, BLK, BQ), bf16),
                        pltpu.VMEM((N_HEADS, N_META, BQ), f32),
                        pltpu.VMEM((N_HEADS, N_META, BQ), bf16),
                        pltpu.VMEM((N_HEADS, 1, BQ), f32),
                        pltpu.VMEM((2, N_HEADS, 1, BQ), f32),
                        pltpu.VMEM((N_HEADS, 1, BQ), f32),
                        pltpu.VMEM((N_HEADS, 1, BQ), f32),
                        pltpu.VMEM((N_HEADS, V_ROWS, BQ), f32)],
        compiler_params=pltpu.CompilerParams(dimension_semantics=("arbitrary", "arbitrary"),
                                             vmem_limit_bytes=VMEM_LIMIT),
        name="attention",
    )(r_blocks, qT, qT, kaug, vT, k_meta, v_meta, sz)

    return pl.pallas_call(
        _out_kernel,
        grid=(batch, seq // TM_OUT),
        in_specs=[
            pl.BlockSpec((1, TM_OUT, D_MODEL), lambda b, t: (b, t, 0)),
            pl.BlockSpec((1, TM_OUT, ATTN_WIDTH), lambda b, t: (b, t, 0)),
            pl.BlockSpec((1, TM_OUT, D_MODEL), lambda b, t: (b, t, 0)),
            pl.BlockSpec((1, TM_OUT, D_MODEL), lambda b, t: (b, t, 0)),
            _const((ATTN_WIDTH, D_MODEL)),
            _const((D_MODEL, D_MODEL)),
            _const((1, D_MODEL)),
        ],
        out_specs=pl.BlockSpec((1, TM_OUT, D_MODEL), lambda b, t: (b, t, 0)),
        out_shape=jax.ShapeDtypeStruct((batch, seq, D_MODEL), f32),
        scratch_shapes=[pltpu.VMEM((ATTN_WIDTH, D_MODEL), bf16),
                        pltpu.VMEM((D_MODEL, D_MODEL), bf16)],
        compiler_params=pltpu.CompilerParams(dimension_semantics=("arbitrary", "arbitrary"),
                                             vmem_limit_bytes=VMEM_LIMIT),
        name="out_proj",
    )(x, y_attn, ga, mp, w_upa, w_o, g_out)
```

```python
import functools

import jax
import jax.numpy as jnp
from jax import lax
from jax.experimental import pallas as pl
from jax.experimental.pallas import tpu as pltpu

D_MODEL = 1024
N_META = 16
POOL_WIDTH = 512
POOL_WINDOWS = (2, 4, 8, 16)
POOL_GROUP = POOL_WIDTH // len(POOL_WINDOWS)
N_HEADS = 8
HEAD_DIM = 64
ATTN_WIDTH = N_HEADS * HEAD_DIM
RMS_EPS = 1e-6

LANES = 128
BLK = 256
BQ = 512
QB = BQ // BLK
assert QB == 2
TM = 512
NB = TM // BLK
TM_OUT = 1024
POOL_PAD = 8
assert POOL_WINDOWS == (2, 4, 8, 16) and POOL_PAD >= POOL_WINDOWS[-1] // 2
V_ROWS = HEAD_DIM + 16
N_SPLIT = 3
MASKED = -1e30
SKEW = 2
LOG2E = 1.4426950408889634
VMEM_LIMIT = 56 * 1024 * 1024
assert POOL_WIDTH == ATTN_WIDTH
IN_GROUP = POOL_WIDTH
COL_U, COL_ZP, COL_Q, COL_K, COL_V, COL_ZA, COL_F = (IN_GROUP * i for i in range(7))
COL_G = COL_F + N_HEADS
F_ROWS = 16
N_IN = COL_G + 2 * D_MODEL

f32 = jnp.float32
bf16 = jnp.bfloat16


def _rmsnorm(x, g):
    return x * lax.rsqrt(jnp.mean(x * x, axis=-1, keepdims=True) + RMS_EPS) * g


def _sigmoid(x):
    return 0.5 * jnp.tanh(0.5 * x) + 0.5


def _log_sigmoid(x):
    return jnp.minimum(x, 0.0) - jnp.log1p(jnp.exp(-jnp.abs(x)))


def _dot(a, b):
    return jnp.dot(a, b, preferred_element_type=f32)


def _dot_nt(a, b):
    return lax.dot_general(a, b, (((1,), (1,)), ((), ())), preferred_element_type=f32)


def _lane_iota(shape):
    return lax.broadcasted_iota(jnp.int32, shape, 1)


def _decay_parts(logf3, rows):
    n = logf3.shape[0] // rows
    r = lax.broadcasted_iota(jnp.int32, (rows, rows), 0)
    c = lax.broadcasted_iota(jnp.int32, (rows, rows), 1)
    tri = (c <= r).astype(f32)
    side_by_side = jnp.concatenate([logf3[rows * i:rows * (i + 1)] for i in range(n)], axis=1)
    sums = jnp.dot(tri, side_by_side, precision=lax.Precision.HIGHEST, preferred_element_type=f32)
    beta = jnp.concatenate([sums[:, LANES * i:LANES * (i + 1)] for i in range(n)], axis=0)
    return beta, _split_decay(beta)


def _decay_parts_from_rows(logf_t, rows):
    n = logf_t.shape[1] // rows
    k = lax.broadcasted_iota(jnp.int32, (rows, rows), 0)
    j = lax.broadcasted_iota(jnp.int32, (rows, rows), 1)
    stacked = jnp.concatenate([logf_t[:, rows * i:rows * (i + 1)] for i in range(n)], axis=0)
    sums_t = jnp.dot(stacked, (k <= j).astype(f32), precision=lax.Precision.HIGHEST, preferred_element_type=f32)
    unused = jnp.zeros((LANES - N_SPLIT * N_HEADS, rows), f32)
    beta = jnp.concatenate(
        [jnp.concatenate([sums_t[N_HEADS * i:N_HEADS * (i + 1)]] * N_SPLIT + [unused], axis=0).T for i in range(n)],
        axis=0)
    return beta, _split_decay(beta)


def _split_decay(beta):
    nb = beta * (-LOG2E)
    hi = nb.astype(bf16).astype(f32)
    mid = (nb - hi).astype(bf16).astype(f32)
    lo = (nb - hi - mid).astype(bf16).astype(f32)
    lane = _lane_iota(nb.shape)
    return jnp.where(lane < N_HEADS, hi, jnp.where(lane < 2 * N_HEADS, mid, lo))


def _augmented_keys(kproj, parts, store):
    lane = _lane_iota((kproj.shape[0], LANES))
    for h in range(N_HEADS):
        slab = kproj[:, LANES * (h // 2):LANES * (h // 2 + 1)]
        if h % 2:
            slab = pltpu.roll(slab, HEAD_DIM, axis=1)
        store(h, jnp.where(lane < HEAD_DIM, slab, pltpu.roll(parts, HEAD_DIM - h, axis=1)).astype(bf16))


def _pool_mapped(wu_t, pw_ref):
    return jnp.concatenate([_dot(pw_ref[g].T.astype(bf16), wu_t[POOL_GROUP * g:POOL_GROUP * (g + 1)])
                            for g in range(len(POOL_WINDOWS))], axis=0).astype(bf16)


def _forget_rows(wf_t):
    unused = jnp.zeros((LANES - N_SPLIT * N_HEADS, wf_t.shape[1]), f32)
    return jnp.concatenate([wf_t] * N_SPLIT + [unused], axis=0).astype(bf16)


def _forget_bias(b_ref, shape, axis):
    index = lax.broadcasted_iota(jnp.int32, shape, axis)
    bias = jnp.zeros(shape, f32)
    for h in range(N_HEADS):
        bias = jnp.where((index % N_HEADS == h) & (index < N_SPLIT * N_HEADS), b_ref[h], bias)
    return bias


def _meta_kernel(meta_ref, g_ref, wu_ref, wk_ref, wv_ref, wf_ref, bf_ref, pw_ref,
                 umeta_ref, kmeta_ref, vmeta_ref, r0_ref):
    hn = _rmsnorm(meta_ref[...], g_ref[...]).astype(bf16)
    umeta_ref[...] = _dot_nt(hn, _pool_mapped(wu_ref[...].astype(bf16), pw_ref))
    vt = _dot_nt(wv_ref[...].astype(bf16), hn)
    for h in range(N_HEADS):
        vmeta_ref[h, 0:HEAD_DIM, :] = vt[HEAD_DIM * h:HEAD_DIM * (h + 1), :].astype(bf16)
        vmeta_ref[h, HEAD_DIM:V_ROWS, :] = jnp.ones((V_ROWS - HEAD_DIM, N_META), bf16)
    logf3 = _log_sigmoid(_dot_nt(hn, _forget_rows(wf_ref[...])) + _forget_bias(bf_ref, (1, LANES), 1))
    beta, parts = _decay_parts(logf3, N_META)

    def store(h, ka):
        kmeta_ref[h] = ka

    _augmented_keys(_dot_nt(hn, wk_ref[...].astype(bf16)), parts, store)
    r0_ref[...] = jnp.broadcast_to(beta[N_META - 1:N_META, :], r0_ref.shape)


def _proj_kernel(x_ref, g_ref, wt_ref, bf_ref, pw_ref, ps_ref, wup32_ref,
                 umeta_ref, r0_ref,
                 mp_ref, ga_ref, sz_ref, qT_ref, vT_ref, kaug_ref, r_ref,
                 uext_ref, lvl_a_ref, lvl_b_ref, rcarry_ref, wb_ref, wg_ref, wu_ref, wup_ref):
    t = pl.program_id(1)
    zp_rows, k_rows, za_rows, q_rows, v_rows = (slice(IN_GROUP * i, IN_GROUP * (i + 1)) for i in range(5))
    f_rows = slice(v_rows.stop, v_rows.stop + F_ROWS)

    @pl.when((pl.program_id(0) == 0) & (t == 0))
    def _():
        for rows, col in ((zp_rows, COL_ZP), (k_rows, COL_K), (za_rows, COL_ZA), (q_rows, COL_Q), (v_rows, COL_V)):
            wb_ref[rows, :] = wt_ref[col:col + IN_GROUP, :].astype(bf16)
        wb_ref[f_rows, :] = wt_ref[COL_F:COL_F + F_ROWS, :].astype(bf16)
        wg_ref[...] = wt_ref[COL_G:N_IN, :].astype(bf16)
        wu_ref[...] = _pool_mapped(wt_ref[COL_U:COL_ZP, :].astype(bf16), pw_ref)
        wup_ref[...] = wup32_ref[...].astype(bf16)
    ext = N_META + TM
    body = slice(POOL_PAD, POOL_PAD + ext)

    @pl.when(t == 0)
    def _():
        uext_ref[0:POOL_PAD, :] = jnp.zeros((POOL_PAD, POOL_WIDTH), f32)
        lvl_a_ref[0:POOL_PAD, :] = jnp.zeros((POOL_PAD, lvl_a_ref.shape[1]), f32)
        lvl_b_ref[0:POOL_PAD, :] = jnp.zeros((POOL_PAD, lvl_b_ref.shape[1]), f32)
        uext_ref[POOL_PAD:POOL_PAD + N_META, :] = umeta_ref[...]
        rcarry_ref[...] = r0_ref[...]

    hn = _rmsnorm(x_ref[0], g_ref[...]).astype(bf16)

    qv = _dot_nt(wb_ref[q_rows.start:, :], hn)
    qT_ref[0] = (qv[0:ATTN_WIDTH] * (HEAD_DIM ** -0.5 * LOG2E)).astype(bf16)
    for c in range(NB):
        for h in range(N_HEADS):
            rows = slice(ATTN_WIDTH + HEAD_DIM * h, ATTN_WIDTH + HEAD_DIM * (h + 1))
            vT_ref[0, c, h, 0:HEAD_DIM, :] = qv[rows, BLK * c:BLK * (c + 1)].astype(bf16)
            vT_ref[0, c, h, HEAD_DIM:V_ROWS, :] = jnp.ones((V_ROWS - HEAD_DIM, BLK), bf16)

    gat =_dot_nt(hn, wg_ref[D_MODEL:2 * D_MODEL, :])
    ga_ref[0] = _sigmoid(gat).astype(bf16)

    u = _dot_nt(hn, wu_ref[...])
    zp = _dot_nt(hn, wb_ref[zp_rows, :])
    uext_ref[POOL_PAD + N_META:POOL_PAD + ext, :] = u

    def shifted_sum(ref, shift, cols):
        return ref[body, cols] + ref[POOL_PAD - shift:POOL_PAD - shift + ext, cols]

    g1, g2, g3 = (slice(POOL_GROUP * g, POOL_GROUP * (g + 1)) for g in range(3))
    sums = [shifted_sum(uext_ref, 1, g1)]
    lvl_a_ref[body, :] = shifted_sum(uext_ref, 1, slice(POOL_GROUP, POOL_WIDTH))
    sums.append(shifted_sum(lvl_a_ref, 2, g1))
    lvl_b_ref[body, :] = shifted_sum(lvl_a_ref, 2, slice(POOL_GROUP, 3 * POOL_GROUP))
    sums.append(shifted_sum(lvl_b_ref, 4, g1))
    lvl_a_ref[body, g1] = shifted_sum(lvl_b_ref, 4, g2)
    sums.append(shifted_sum(lvl_a_ref, 8, g1))
    uext_ref[POOL_PAD:POOL_PAD + N_META, :] = uext_ref[POOL_PAD + TM:POOL_PAD + ext, :]

    pooled = jnp.concatenate([sums[g][N_META:] * (1.0 / w) for g, w in enumerate(POOL_WINDOWS)], axis=1) - u
    y_pool = (pooled * ps_ref[...] * (zp * _sigmoid(zp))).astype(bf16)
    gp = _dot_nt(hn, wg_ref[0:D_MODEL, :])
    mp_ref[0] = (_sigmoid(gp) * _dot(y_pool, wup_ref[...])).astype(bf16)

    logf_t = _log_sigmoid(qv[2 * IN_GROUP:2 * IN_GROUP + N_HEADS] + _forget_bias(bf_ref, (N_HEADS, TM), 0))
    r_ref[...] = jnp.zeros(r_ref.shape, f32)
    beta, parts = _decay_parts_from_rows(logf_t, BLK)
    for c in range(NB):
        r_ref[0, 0, c:c + 1, :] = rcarry_ref[0:1, :] * LOG2E
        rcarry_ref[...] = rcarry_ref[...] + beta[BLK * (c + 1) - 1:BLK * (c + 1), :]

    def store(h, ka):
        kaug_ref[0, h] = ka

    _augmented_keys(_dot_nt(hn, wb_ref[k_rows, :]), parts, store)

    za = _dot_nt(hn, wb_ref[za_rows, :])
    sz_ref[0] = (za * _sigmoid(za)).astype(bf16)


def _attn_kernel(r_ref, qT_ref, qnext_ref, kaug_ref, vT_ref, kmeta_ref, vmeta_ref, sz_ref, o_ref,
                 qaug_ref, qaug_next_ref, s_ref, p_ref, sm_ref, pm_ref, m_ref, mblk_ref, mmeta_ref, alpha_ref, acc_ref,
                 *, n_blocks):
    b = pl.program_id(0)
    t = pl.program_id(1)
    first = QB * t
    heads = range(N_HEADS)

    part_row = lax.broadcasted_iota(jnp.int32, (LANES - HEAD_DIM, BQ), 0)
    ones_rows = ((part_row % N_HEADS == 0) & (part_row < N_SPLIT * N_HEADS)).astype(bf16)
    for h in heads:
        for q_ref, aug_ref in ((qT_ref, qaug_ref), (qnext_ref, qaug_next_ref)):
            aug_ref[h, 0:HEAD_DIM, :] = q_ref[0, HEAD_DIM * h:HEAD_DIM * (h + 1), :]
            aug_ref[h, HEAD_DIM:LANES, :] = ones_rows

    def r_at(j, h):
        return r_ref[(b * n_blocks + j) * N_HEADS + h]

    def col_max(s):
        return jnp.max(s, axis=0, keepdims=True)

    key_index = functools.partial(lax.broadcasted_iota, jnp.int32, dimension=0)
    query_index = functools.partial(lax.broadcasted_iota, jnp.int32, dimension=1)

    def seen_by(diag):
        return slice(BLK * diag, BQ) if diag else slice(None)

    def score_item(j, slot, h, diag=None, meta=False, next_tile=False):
        q_ref = qaug_next_ref if next_tile else qaug_ref
        cols = seen_by(diag)
        s = _dot(kaug_ref[0, h, pl.ds(pl.multiple_of(j * BLK, BLK), BLK), :], q_ref[h, :, cols])
        if diag is not None:
            s = jnp.where(key_index(s.shape) <= query_index(s.shape), s, MASKED)
        s_ref[slot, h, :, cols] = s
        unseen = [jnp.full((1, BQ - s.shape[1]), MASKED, f32)] if s.shape[1] < BQ else []
        mblk_ref[slot, h] = jnp.concatenate(unseen + [col_max(s)], axis=1)
        if meta:
            sm = _dot(kmeta_ref[h], qaug_ref[h])
            sm_ref[h] = sm
            mmeta_ref[h] = col_max(sm)

    def softmax_item(j, slot, h, meta=False, diag=None):
        cols = seen_by(diag)
        assert not (meta and diag)
        ref = r_at(first, h)
        off = ref - r_at(j, h)
        m_old = m_ref[h]
        m_blk = mblk_ref[slot, h] + off
        if meta:
            m_blk = jnp.maximum(m_blk, mmeta_ref[h] + ref)
        m_new = jnp.maximum(m_old, m_blk)
        alpha_ref[h] = jnp.exp2(m_old - m_new)
        m_seen = m_new
        if diag:
            m_seen = jnp.maximum(m_ref[h, :, cols], mblk_ref[slot, h, :, cols] + off)
        m_ref[h] = m_new
        p_ref[h, :, cols] = jnp.exp2(s_ref[slot, h, :, cols] - (m_seen - off)).astype(bf16)
        if meta:
            pm_ref[h] = jnp.exp2(sm_ref[h] - (m_new - ref)).astype(bf16)

    def value_item(j, slot, h, meta=False, diag=None):
        cols = seen_by(diag)
        pv = _dot(vT_ref[0, j, h], p_ref[h, :, cols])
        if meta:
            pv = pv + _dot(vmeta_ref[h], pm_ref[h])
        acc_ref[h, :, cols] = acc_ref[h, :, cols] * alpha_ref[h, :, cols] + pv

    def run(score_blocks, finish_blocks):
        score_items = [(blk + (h,), kw) for blk, kw in score_blocks for h in heads]
        finish_items = [(blk + (h,), kw) for blk, kw in finish_blocks for h in heads]
        for i in range(max(len(score_items), len(finish_items) + SKEW + 1)):
            if i < len(score_items):
                args, kw = score_items[i]
                score_item(*args, **kw)
            if 0 <= i - SKEW - 1 < len(finish_items):
                args, kw = finish_items[i - SKEW - 1]
                value_item(*args, **kw)
            if 0 <= i - SKEW < len(finish_items):
                args, kw = finish_items[i - SKEW]
                softmax_item(*args, **kw)

    m_ref[...] = jnp.full(m_ref.shape, MASKED, f32)
    acc_ref[...] = jnp.zeros(acc_ref.shape, f32)

    @pl.when(first == 0)
    def _():
        run([((first, 0), dict(diag=0))], [])

    @pl.when(first > 0)
    def _():
        def body(k, carry):
            run([((2 * k + 1, 1), {}), ((2 * k + 2, 0), {})],
                [((2 * k, 0), {}), ((2 * k + 1, 1), {})])
            return carry

        lax.fori_loop(0, t - 1, body, 0)
        run([((first - 1, 1), {}), ((first, 0), dict(diag=0))],
            [((first - 2, 0), {}), ((first - 1, 1), {})])

    run([((first + 1, 1), dict(diag=1, meta=True)), ((0, 0), dict(next_tile=True))],
        [((first, 0), dict(meta=True)), ((first + 1, 1), dict(diag=1))])

    for pair in range(N_HEADS // 2):
        halves = []
        for h in (2 * pair, 2 * pair + 1):
            a = acc_ref[h]
            halves.append(a[0:HEAD_DIM] * (1.0 / a[HEAD_DIM:HEAD_DIM + 1]))
        cols = slice(LANES * pair, LANES * (pair + 1))
        o_ref[0, :, cols] = (jnp.concatenate(halves, axis=0).T * sz_ref[0, :, cols].astype(f32)).astype(bf16)


def _out_kernel(x_ref, y_ref, ga_ref, mp_ref, wua32_ref, wout32_ref, g_ref, out_ref, wua_ref, wout_ref):
    @pl.when((pl.program_id(0) == 0) & (pl.program_id(1) == 0))
    def _():
        wua_ref[...] = wua32_ref[...].astype(bf16)
        wout_ref[...] = wout32_ref[...].astype(bf16)

    merged = mp_ref[0].astype(f32) + ga_ref[0].astype(f32) * _dot(y_ref[0], wua_ref[...])
    h_out = x_ref[0] + _dot(merged.astype(bf16), wout_ref[...])
    out_ref[0] = _rmsnorm(h_out, g_ref[...])


def _const(shape):
    return pl.BlockSpec(shape, lambda *_: (0,) * len(shape), pipeline_mode=pl.Buffered(1))


def kernel(x, meta_tokens, norm_g, w_in, b_forget, pool_w, pool_scale, w_up_pool, w_up_attn, w_out, final_norm_g):
    batch, seq, _ = x.shape
    n_tiles = seq // TM
    n_blocks = seq // BLK

    w_t = jnp.transpose(w_in[0])
    b_f = b_forget.reshape(N_HEADS)
    g_in = norm_g[0].reshape(1, D_MODEL)
    g_out = final_norm_g.reshape(1, D_MODEL)
    pw = pool_w[0]
    ps = pool_scale[0].reshape(1, POOL_WIDTH)
    w_upp, w_upa, w_o = w_up_pool[0], w_up_attn[0], w_out[0]

    u_meta, k_meta, v_meta, r0 = pl.pallas_call(
        _meta_kernel,
        out_shape=(jax.ShapeDtypeStruct((N_META, POOL_WIDTH), f32),
                   jax.ShapeDtypeStruct((N_HEADS, N_META, LANES), bf16),
                   jax.ShapeDtypeStruct((N_HEADS, V_ROWS, N_META), bf16),
                   jax.ShapeDtypeStruct((8, LANES), f32)),
        grid=(1,),
        in_specs=[_const((N_META, D_MODEL)), _const((1, D_MODEL))]
        + [pl.BlockSpec((rows, D_MODEL), functools.partial(lambda r, i: (r, 0), col // rows),
                        pipeline_mode=pl.Buffered(1))
           for col, rows in ((COL_U, IN_GROUP), (COL_K, IN_GROUP), (COL_V, IN_GROUP), (COL_F, N_HEADS))]
        + [pl.BlockSpec(memory_space=pltpu.SMEM), _const((len(POOL_WINDOWS), POOL_GROUP, POOL_GROUP))],
        out_specs=[_const((N_META, POOL_WIDTH)), _const((N_HEADS, N_META, LANES)),
                   _const((N_HEADS, V_ROWS, N_META)), _const((8, LANES))],
        compiler_params=pltpu.CompilerParams(vmem_limit_bytes=VMEM_LIMIT),
        name="meta_proj",
    )(meta_tokens, g_in, w_t, w_t, w_t, w_t, b_f, pw)

    mp, ga, sz, qT, vT, kaug, r_tiles = pl.pallas_call(
        _proj_kernel,
        grid=(batch, n_tiles),
        in_specs=[
            pl.BlockSpec((1, TM, D_MODEL), lambda b, t: (b, t, 0)),
            _const((1, D_MODEL)),
            _const((N_IN, D_MODEL)),
            pl.BlockSpec(memory_space=pltpu.SMEM),
            _const((len(POOL_WINDOWS), POOL_GROUP, POOL_GROUP)),
            _const((1, POOL_WIDTH)),
            _const((POOL_WIDTH, D_MODEL)),
            _const((N_META, POOL_WIDTH)),
            _const((8, LANES)),
        ],
        out_specs=[
            pl.BlockSpec((1, TM, D_MODEL), lambda b, t: (b, t, 0)),
            pl.BlockSpec((1, TM, D_MODEL), lambda b, t: (b, t, 0)),
            pl.BlockSpec((1, TM, ATTN_WIDTH), lambda b, t: (b, t, 0)),
            pl.BlockSpec((1, ATTN_WIDTH, TM), lambda b, t: (b, 0, t)),
            pl.BlockSpec((1, NB, N_HEADS, V_ROWS, BLK), lambda b, t: (b, t, 0, 0, 0)),
            pl.BlockSpec((1, N_HEADS, TM, LANES), lambda b, t: (b, 0, t, 0)),
            pl.BlockSpec((1, 1, 8, LANES), lambda b, t: (b, t, 0, 0)),
        ],
        out_shape=(
            jax.ShapeDtypeStruct((batch, seq, D_MODEL), bf16),
            jax.ShapeDtypeStruct((batch, seq, D_MODEL), bf16),
            jax.ShapeDtypeStruct((batch, seq, ATTN_WIDTH), bf16),
            jax.ShapeDtypeStruct((batch, ATTN_WIDTH, seq), bf16),
            jax.ShapeDtypeStruct((batch, n_blocks, N_HEADS, V_ROWS, BLK), bf16),
            jax.ShapeDtypeStruct((batch, N_HEADS, seq, LANES), bf16),
            jax.ShapeDtypeStruct((batch, n_tiles, 8, LANES), f32),
        ),
        scratch_shapes=[pltpu.VMEM((POOL_PAD + N_META + TM, POOL_WIDTH), f32),
                        pltpu.VMEM((POOL_PAD + N_META + TM, 3 * POOL_GROUP), f32),
                        pltpu.VMEM((POOL_PAD + N_META + TM, 2 * POOL_GROUP), f32),
                        pltpu.VMEM((8, LANES), f32),
                        pltpu.VMEM((5 * IN_GROUP + F_ROWS, D_MODEL), bf16),
                        pltpu.VMEM((2 * D_MODEL, D_MODEL), bf16),
                        pltpu.VMEM((POOL_WIDTH, D_MODEL), bf16),
                        pltpu.VMEM((POOL_WIDTH, D_MODEL), bf16)],
        compiler_params=pltpu.CompilerParams(dimension_semantics=("arbitrary", "arbitrary"),
                                             vmem_limit_bytes=VMEM_LIMIT),
        name="in_proj",
    )(x, g_in, w_t, b_f, pw, ps, w_upp, u_meta, r0)

    r_blocks = r_tiles[:, :, :NB, :N_HEADS].reshape(batch * n_blocks * N_HEADS)

    y_attn = pl.pallas_call(
        functools.partial(_attn_kernel, n_blocks=n_blocks),
        grid=(batch, seq // BQ),
        in_specs=[
            pl.BlockSpec(memory_space=pltpu.SMEM),
            pl.BlockSpec((1, ATTN_WIDTH, BQ), lambda b, t: (b, 0, t)),
            pl.BlockSpec((1, ATTN_WIDTH, BQ), lambda b, t: (b, 0, jnp.minimum(t + 1, seq // BQ - 1))),
            pl.BlockSpec((1, N_HEADS, seq, LANES), lambda b, t: (b, 0, 0, 0)),
            pl.BlockSpec((1, n_blocks, N_HEADS, V_ROWS, BLK), lambda b, t: (b, 0, 0, 0, 0)),
            _const((N_HEADS, N_META, LANES)),
            _const((N_HEADS, V_ROWS, N_META)),
            pl.BlockSpec((1, BQ, ATTN_WIDTH), lambda b, t: (b, t, 0)),
        ],
        out_specs=pl.BlockSpec((1, BQ, ATTN_WIDTH), lambda b, t: (b, t, 0)),
        out_shape=jax.ShapeDtypeStruct((batch, seq, ATTN_WIDTH), bf16),
        scratch_shapes=[pltpu.VMEM((N_HEADS, LANES, BQ), bf16),
                        pltpu.VMEM((N_HEADS, LANES, BQ), bf16),
                        pltpu.VMEM((2, N_HEADS, BLK, BQ), f32),
                        pltpu.VMEM((N_HEADS, BLK, BQ), bf16),
                        pltpu.VMEM((N_HEADS, N_META, BQ), f32),
                        pltpu.VMEM((N_HEADS, N_META, BQ), bf16),
                        pltpu.VMEM((N_HEADS, 1, BQ), f32),
                        pltpu.VMEM((2, N_HEADS, 1, BQ), f32),
                        pltpu.VMEM((N_HEADS, 1, BQ), f32),
                        pltpu.VMEM((N_HEADS, 1, BQ), f32),
                        pltpu.VMEM((N_HEADS, V_ROWS, BQ), f32)],
        compiler_params=pltpu.CompilerParams(dimension_semantics=("arbitrary", "arbitrary"),
                                             vmem_limit_bytes=VMEM_LIMIT),
        name="attention",
    )(r_blocks, qT, qT, kaug, vT, k_meta, v_meta, sz)

    return pl.pallas_call(
        _out_kernel,
        grid=(batch, seq // TM_OUT),
        in_specs=[
            pl.BlockSpec((1, TM_OUT, D_MODEL), lambda b, t: (b, t, 0)),
            pl.BlockSpec((1, TM_OUT, ATTN_WIDTH), lambda b, t: (b, t, 0)),
            pl.BlockSpec((1, TM_OUT, D_MODEL), lambda b, t: (b, t, 0)),
            pl.BlockSpec((1, TM_OUT, D_MODEL), lambda b, t: (b, t, 0)),
            _const((ATTN_WIDTH, D_MODEL)),
            _const((D_MODEL, D_MODEL)),
            _const((1, D_MODEL)),
        ],
        out_specs=pl.BlockSpec((1, TM_OUT, D_MODEL), lambda b, t: (b, t, 0)),
        out_shape=jax.ShapeDtypeStruct((batch, seq, D_MODEL), f32),
        scratch_shapes=[pltpu.VMEM((ATTN_WIDTH, D_MODEL), bf16),
                        pltpu.VMEM((D_MODEL, D_MODEL), bf16)],
        compiler_params=pltpu.CompilerParams(dimension_semantics=("arbitrary", "arbitrary"),
                                             vmem_limit_bytes=VMEM_LIMIT),
        name="out_proj",
    )(x, y_attn, ga, mp, w_upa, w_o, g_out)
```

```python
import functools

import jax
import jax.numpy as jnp
from jax import lax
from jax.experimental import pallas as pl
from jax.experimental.pallas import tpu as pltpu

D_MODEL = 1024
N_META = 16
POOL_WIDTH = 512
POOL_WINDOWS = (2, 4, 8, 16)
POOL_GROUP = POOL_WIDTH // len(POOL_WINDOWS)
N_HEADS = 8
HEAD_DIM = 64
ATTN_WIDTH = N_HEADS * HEAD_DIM
RMS_EPS = 1e-6

LANES = 128
BLK = 256
BQ = 512
QB = BQ // BLK
assert QB == 2
TM = 512
NB = TM // BLK
TM_OUT = 1024
OUT_CHUNK = 256
POOL_PAD = 8
assert POOL_WINDOWS == (2, 4, 8, 16) and POOL_PAD >= POOL_WINDOWS[-1] // 2
V_ROWS = HEAD_DIM + 16
N_SPLIT = 3
MASKED = -1e30
SKEW = 2
LOG2E = 1.4426950408889634
VMEM_LIMIT = 56 * 1024 * 1024
assert POOL_WIDTH == ATTN_WIDTH
IN_GROUP = POOL_WIDTH
COL_U, COL_ZP, COL_Q, COL_K, COL_V, COL_ZA, COL_F = (IN_GROUP * i for i in range(7))
COL_G = COL_F + N_HEADS
F_ROWS = 16
N_IN = COL_G + 2 * D_MODEL

f32 = jnp.float32
bf16 = jnp.bfloat16


def _rmsnorm(x, g):
    return x * lax.rsqrt(jnp.mean(x * x, axis=-1, keepdims=True) + RMS_EPS) * g


def _sigmoid(x):
    return 0.5 * jnp.tanh(0.5 * x) + 0.5


def _log_sigmoid(x):
    return jnp.minimum(x, 0.0) - jnp.log1p(jnp.exp(-jnp.abs(x)))


def _dot(a, b):
    return jnp.dot(a, b, preferred_element_type=f32)


def _dot_nt(a, b):
    return lax.dot_general(a, b, (((1,), (1,)), ((), ())), preferred_element_type=f32)


def _lane_iota(shape):
    return lax.broadcasted_iota(jnp.int32, shape, 1)


def _decay_parts(logf3, rows):
    n = logf3.shape[0] // rows
    r = lax.broadcasted_iota(jnp.int32, (rows, rows), 0)
    c = lax.broadcasted_iota(jnp.int32, (rows, rows), 1)
    tri = (c <= r).astype(f32)
    side_by_side = jnp.concatenate([logf3[rows * i:rows * (i + 1)] for i in range(n)], axis=1)
    sums = jnp.dot(tri, side_by_side, precision=lax.Precision.HIGHEST, preferred_element_type=f32)
    beta = jnp.concatenate([sums[:, LANES * i:LANES * (i + 1)] for i in range(n)], axis=0)
    return beta, _split_decay(beta)


def _decay_parts_from_rows(logf_t, rows):
    n = logf_t.shape[1] // rows
    k = lax.broadcasted_iota(jnp.int32, (rows, rows), 0)
    j = lax.broadcasted_iota(jnp.int32, (rows, rows), 1)
    stacked = jnp.concatenate([logf_t[:, rows * i:rows * (i + 1)] for i in range(n)], axis=0)
    sums_t = jnp.dot(stacked, (k <= j).astype(f32), precision=lax.Precision.HIGHEST, preferred_element_type=f32)
    unused = jnp.zeros((LANES - N_SPLIT * N_HEADS, rows), f32)
    beta = jnp.concatenate(
        [jnp.concatenate([sums_t[N_HEADS * i:N_HEADS * (i + 1)]] * N_SPLIT + [unused], axis=0).T for i in range(n)],
        axis=0)
    return beta, _split_decay(beta)


def _split_decay(beta):
    nb = beta * (-LOG2E)
    hi = nb.astype(bf16).astype(f32)
    mid = (nb - hi).astype(bf16).astype(f32)
    lo = (nb - hi - mid).astype(bf16).astype(f32)
    lane = _lane_iota(nb.shape)
    return jnp.where(lane < N_HEADS, hi, jnp.where(lane < 2 * N_HEADS, mid, lo))


def _augmented_keys(kproj, parts, store):
    lane = _lane_iota((kproj.shape[0], LANES))
    for h in range(N_HEADS):
        slab = kproj[:, LANES * (h // 2):LANES * (h // 2 + 1)]
        if h % 2:
            slab = pltpu.roll(slab, HEAD_DIM, axis=1)
        store(h, jnp.where(lane < HEAD_DIM, slab, pltpu.roll(parts, HEAD_DIM - h, axis=1)).astype(bf16))


def _pool_mapped(wu_t, pw_ref):
    return jnp.concatenate([_dot(pw_ref[g].T.astype(bf16), wu_t[POOL_GROUP * g:POOL_GROUP * (g + 1)])
                            for g in range(len(POOL_WINDOWS))], axis=0).astype(bf16)


def _forget_rows(wf_t):
    unused = jnp.zeros((LANES - N_SPLIT * N_HEADS, wf_t.shape[1]), f32)
    return jnp.concatenate([wf_t] * N_SPLIT + [unused], axis=0).astype(bf16)


def _forget_bias(b_ref, shape, axis):
    index = lax.broadcasted_iota(jnp.int32, shape, axis)
    bias = jnp.zeros(shape, f32)
    for h in range(N_HEADS):
        bias = jnp.where((index % N_HEADS == h) & (index < N_SPLIT * N_HEADS), b_ref[h], bias)
    return bias


def _meta_kernel(meta_ref, g_ref, wu_ref, wk_ref, wv_ref, wf_ref, bf_ref, pw_ref,
                 umeta_ref, kmeta_ref, vmeta_ref, r0_ref):
    hn = _rmsnorm(meta_ref[...], g_ref[...]).astype(bf16)
    umeta_ref[...] = _dot_nt(hn, _pool_mapped(wu_ref[...].astype(bf16), pw_ref))
    vt = _dot_nt(wv_ref[...].astype(bf16), hn)
    for h in range(N_HEADS):
        vmeta_ref[h, 0:HEAD_DIM, :] = vt[HEAD_DIM * h:HEAD_DIM * (h + 1), :].astype(bf16)
        vmeta_ref[h, HEAD_DIM:V_ROWS, :] = jnp.ones((V_ROWS - HEAD_DIM, N_META), bf16)
    logf3 = _log_sigmoid(_dot_nt(hn, _forget_rows(wf_ref[...])) + _forget_bias(bf_ref, (1, LANES), 1))
    beta, parts = _decay_parts(logf3, N_META)

    def store(h, ka):
        kmeta_ref[h] = ka

    _augmented_keys(_dot_nt(hn, wk_ref[...].astype(bf16)), parts, store)
    r0_ref[...] = jnp.broadcast_to(beta[N_META - 1:N_META, :], r0_ref.shape)


def _proj_kernel(x_ref, g_ref, wt_ref, bf_ref, pw_ref, ps_ref, wup32_ref,
                 umeta_ref, r0_ref,
                 mp_ref, ga_ref, sz_ref, qT_ref, vT_ref, kaug_ref, r_ref,
                 uext_ref, lvl_a_ref, lvl_b_ref, rcarry_ref, wb_ref, wg_ref, wu_ref, wup_ref):
    t = pl.program_id(1)
    zp_rows, k_rows, za_rows, q_rows, v_rows = (slice(IN_GROUP * i, IN_GROUP * (i + 1)) for i in range(5))
    f_rows = slice(v_rows.stop, v_rows.stop + F_ROWS)

    @pl.when((pl.program_id(0) == 0) & (t == 0))
    def _():
        for rows, col in ((zp_rows, COL_ZP), (k_rows, COL_K), (za_rows, COL_ZA), (q_rows, COL_Q), (v_rows, COL_V)):
            wb_ref[rows, :] = wt_ref[col:col + IN_GROUP, :].astype(bf16)
        wb_ref[f_rows, :] = wt_ref[COL_F:COL_F + F_ROWS, :].astype(bf16)
        wg_ref[...] = wt_ref[COL_G:N_IN, :].astype(bf16)
        wu_ref[...] = _pool_mapped(wt_ref[COL_U:COL_ZP, :].astype(bf16), pw_ref)
        wup_ref[...] = wup32_ref[...].astype(bf16)
    ext = N_META + TM
    body = slice(POOL_PAD, POOL_PAD + ext)

    @pl.when(t == 0)
    def _():
        uext_ref[0:POOL_PAD, :] = jnp.zeros((POOL_PAD, POOL_WIDTH), f32)
        lvl_a_ref[0:POOL_PAD, :] = jnp.zeros((POOL_PAD, lvl_a_ref.shape[1]), f32)
        lvl_b_ref[0:POOL_PAD, :] = jnp.zeros((POOL_PAD, lvl_b_ref.shape[1]), f32)
        uext_ref[POOL_PAD:POOL_PAD + N_META, :] = umeta_ref[...]
        rcarry_ref[...] = r0_ref[...]

    hn = _rmsnorm(x_ref[0], g_ref[...]).astype(bf16)

    qv = _dot_nt(wb_ref[q_rows.start:, :], hn)
    qT_ref[0] = (qv[0:ATTN_WIDTH] * (HEAD_DIM ** -0.5 * LOG2E)).astype(bf16)
    for c in range(NB):
        for h in range(N_HEADS):
            rows = slice(ATTN_WIDTH + HEAD_DIM * h, ATTN_WIDTH + HEAD_DIM * (h + 1))
            vT_ref[0, c, h, 0:HEAD_DIM, :] = qv[rows, BLK * c:BLK * (c + 1)].astype(bf16)
            vT_ref[0, c, h, HEAD_DIM:V_ROWS, :] = jnp.ones((V_ROWS - HEAD_DIM, BLK), bf16)

    gat =_dot_nt(hn, wg_ref[D_MODEL:2 * D_MODEL, :])
    ga_ref[0] = _sigmoid(gat).astype(bf16)

    u = _dot_nt(hn, wu_ref[...])
    zp = _dot_nt(hn, wb_ref[zp_rows, :])
    uext_ref[POOL_PAD + N_META:POOL_PAD + ext, :] = u

    def shifted_sum(ref, shift, cols):
        return ref[body, cols] + ref[POOL_PAD - shift:POOL_PAD - shift + ext, cols]

    g1, g2, g3 = (slice(POOL_GROUP * g, POOL_GROUP * (g + 1)) for g in range(3))
    sums = [shifted_sum(uext_ref, 1, g1)]
    lvl_a_ref[body, :] = shifted_sum(uext_ref, 1, slice(POOL_GROUP, POOL_WIDTH))
    sums.append(shifted_sum(lvl_a_ref, 2, g1))
    lvl_b_ref[body, :] = shifted_sum(lvl_a_ref, 2, slice(POOL_GROUP, 3 * POOL_GROUP))
    sums.append(shifted_sum(lvl_b_ref, 4, g1))
    lvl_a_ref[body, g1] = shifted_sum(lvl_b_ref, 4, g2)
    sums.append(shifted_sum(lvl_a_ref, 8, g1))
    uext_ref[POOL_PAD:POOL_PAD + N_META, :] = uext_ref[POOL_PAD + TM:POOL_PAD + ext, :]

    pooled = jnp.concatenate([sums[g][N_META:] * (1.0 / w) for g, w in enumerate(POOL_WINDOWS)], axis=1) - u
    y_pool = (pooled * ps_ref[...] * (zp * _sigmoid(zp))).astype(bf16)
    gp = _dot_nt(hn, wg_ref[0:D_MODEL, :])
    mp_ref[0] = (_sigmoid(gp) * _dot(y_pool, wup_ref[...])).astype(bf16)

    logf_t = _log_sigmoid(qv[2 * IN_GROUP:2 * IN_GROUP + N_HEADS] + _forget_bias(bf_ref, (N_HEADS, TM), 0))
    r_ref[...] = jnp.zeros(r_ref.shape, f32)
    beta, parts = _decay_parts_from_rows(logf_t, BLK)
    for c in range(NB):
        r_ref[0, 0, c:c + 1, :] = rcarry_ref[0:1, :] * LOG2E
        rcarry_ref[...] = rcarry_ref[...] + beta[BLK * (c + 1) - 1:BLK * (c + 1), :]

    def store(h, ka):
        kaug_ref[0, h] = ka

    _augmented_keys(_dot_nt(hn, wb_ref[k_rows, :]), parts, store)

    za = _dot_nt(hn, wb_ref[za_rows, :])
    sz_ref[0] = (za * _sigmoid(za)).astype(bf16)


def _attn_kernel(r_ref, qT_ref, qnext_ref, kaug_ref, vT_ref, kmeta_ref, vmeta_ref, sz_ref, o_ref,
                 qaug_ref, qaug_next_ref, s_ref, p_ref, sm_ref, pm_ref, m_ref, mblk_ref, mmeta_ref, alpha_ref, acc_ref,
                 *, n_blocks):
    b = pl.program_id(0)
    t = pl.program_id(1)
    first = QB * t
    heads = range(N_HEADS)

    part_row = lax.broadcasted_iota(jnp.int32, (LANES - HEAD_DIM, BQ), 0)
    ones_rows = ((part_row % N_HEADS == 0) & (part_row < N_SPLIT * N_HEADS)).astype(bf16)
    for h in heads:
        for q_ref, aug_ref in ((qT_ref, qaug_ref), (qnext_ref, qaug_next_ref)):
            aug_ref[h, 0:HEAD_DIM, :] = q_ref[0, HEAD_DIM * h:HEAD_DIM * (h + 1), :]
            aug_ref[h, HEAD_DIM:LANES, :] = ones_rows

    def r_at(j, h):
        return r_ref[(b * n_blocks + j) * N_HEADS + h]

    def col_max(s):
        return jnp.max(s, axis=0, keepdims=True)

    key_index = functools.partial(lax.broadcasted_iota, jnp.int32, dimension=0)
    query_index = functools.partial(lax.broadcasted_iota, jnp.int32, dimension=1)

    def seen_by(diag):
        return slice(BLK * diag, BQ) if diag else slice(None)

    def score_item(j, slot, h, diag=None, meta=False, next_tile=False):
        q_ref = qaug_next_ref if next_tile else qaug_ref
        cols = seen_by(diag)
        s = _dot(kaug_ref[0, h, pl.ds(pl.multiple_of(j * BLK, BLK), BLK), :], q_ref[h, :, cols])
        if diag is not None:
            s = jnp.where(key_index(s.shape) <= query_index(s.shape), s, MASKED)
        s_ref[slot, h, :, cols] = s
        unseen = [jnp.full((1, BQ - s.shape[1]), MASKED, f32)] if s.shape[1] < BQ else []
        mblk_ref[slot, h] = jnp.concatenate(unseen + [col_max(s)], axis=1)
        if meta:
            sm = _dot(kmeta_ref[h], qaug_ref[h])
            sm_ref[h] = sm
            mmeta_ref[h] = col_max(sm)

    def softmax_item(j, slot, h, meta=False, diag=None):
        cols = seen_by(diag)
        assert not (meta and diag)
        ref = r_at(first, h)
        off = ref - r_at(j, h)
        m_old = m_ref[h]
        m_blk = mblk_ref[slot, h] + off
        if meta:
            m_blk = jnp.maximum(m_blk, mmeta_ref[h] + ref)
        m_new = jnp.maximum(m_old, m_blk)
        alpha_ref[h] = jnp.exp2(m_old - m_new)
        m_seen = m_new
        if diag:
            m_seen = jnp.maximum(m_ref[h, :, cols], mblk_ref[slot, h, :, cols] + off)
        m_ref[h] = m_new
        p_ref[h, :, cols] = jnp.exp2(s_ref[slot, h, :, cols] - (m_seen - off)).astype(bf16)
        if meta:
            pm_ref[h] = jnp.exp2(sm_ref[h] - (m_new - ref)).astype(bf16)

    def value_item(j, slot, h, meta=False, diag=None):
        cols = seen_by(diag)
        pv = _dot(vT_ref[0, j, h], p_ref[h, :, cols])
        if meta:
            pv = pv + _dot(vmeta_ref[h], pm_ref[h])
        acc_ref[h, :, cols] = acc_ref[h, :, cols] * alpha_ref[h, :, cols] + pv

    def run(score_blocks, finish_blocks):
        score_items = [(blk + (h,), kw) for blk, kw in score_blocks for h in heads]
        finish_items = [(blk + (h,), kw) for blk, kw in finish_blocks for h in heads]
        for i in range(max(len(score_items), len(finish_items) + SKEW + 1)):
            if i < len(score_items):
                args, kw = score_items[i]
                score_item(*args, **kw)
            if 0 <= i - SKEW - 1 < len(finish_items):
                args, kw = finish_items[i - SKEW - 1]
                value_item(*args, **kw)
            if 0 <= i - SKEW < len(finish_items):
                args, kw = finish_items[i - SKEW]
                softmax_item(*args, **kw)

    m_ref[...] = jnp.full(m_ref.shape, MASKED, f32)
    acc_ref[...] = jnp.zeros(acc_ref.shape, f32)

    @pl.when(first == 0)
    def _():
        run([((first, 0), dict(diag=0))], [])

    @pl.when(first > 0)
    def _():
        def body(k, carry):
            run([((2 * k + 1, 1), {}), ((2 * k + 2, 0), {})],
                [((2 * k, 0), {}), ((2 * k + 1, 1), {})])
            return carry

        lax.fori_loop(0, t - 1, body, 0)
        run([((first - 1, 1), {}), ((first, 0), dict(diag=0))],
            [((first - 2, 0), {}), ((first - 1, 1), {})])

    run([((first + 1, 1), dict(diag=1, meta=True)), ((0, 0), dict(next_tile=True))],
        [((first, 0), dict(meta=True)), ((first + 1, 1), dict(diag=1))])

    for pair in range(N_HEADS // 2):
        halves = []
        for h in (2 * pair, 2 * pair + 1):
            a = acc_ref[h]
            halves.append(a[0:HEAD_DIM] * (1.0 / a[HEAD_DIM:HEAD_DIM + 1]))
        cols = slice(LANES * pair, LANES * (pair + 1))
        o_ref[0, :, cols] = (jnp.concatenate(halves, axis=0).T * sz_ref[0, :, cols].astype(f32)).astype(bf16)


def _out_kernel(x_ref, y_ref, ga_ref, mp_ref, wua32_ref, wout32_ref, g_ref, out_ref, wua_ref, wout_ref):
    @pl.when((pl.program_id(0) == 0) & (pl.program_id(1) == 0))
    def _():
        wua_ref[...] = wua32_ref[...].astype(bf16)
        wout_ref[...] = wout32_ref[...].astype(bf16)

    def merged(rows):
        up = _dot(y_ref[0, rows, :], wua_ref[...])
        return (mp_ref[0, rows, :].astype(f32) + ga_ref[0, rows, :].astype(f32) * up).astype(bf16)

    chunks = [slice(OUT_CHUNK * i, OUT_CHUNK * (i + 1)) for i in range(TM_OUT // OUT_CHUNK)]
    pending = merged(chunks[0])
    for i, rows in enumerate(chunks):
        ahead = merged(chunks[i + 1]) if i + 1 < len(chunks) else None
        h_out = x_ref[0, rows, :] + _dot(pending, wout_ref[...])
        out_ref[0, rows, :] = _rmsnorm(h_out, g_ref[...])
        pending = ahead


def _const(shape):
    return pl.BlockSpec(shape, lambda *_: (0,) * len(shape), pipeline_mode=pl.Buffered(1))


def kernel(x, meta_tokens, norm_g, w_in, b_forget, pool_w, pool_scale, w_up_pool, w_up_attn, w_out, final_norm_g):
    batch, seq, _ = x.shape
    n_tiles = seq // TM
    n_blocks = seq // BLK

    w_t = jnp.transpose(w_in[0])
    b_f = b_forget.reshape(N_HEADS)
    g_in = norm_g[0].reshape(1, D_MODEL)
    g_out = final_norm_g.reshape(1, D_MODEL)
    pw = pool_w[0]
    ps = pool_scale[0].reshape(1, POOL_WIDTH)
    w_upp, w_upa, w_o = w_up_pool[0], w_up_attn[0], w_out[0]

    u_meta, k_meta, v_meta, r0 = pl.pallas_call(
        _meta_kernel,
        out_shape=(jax.ShapeDtypeStruct((N_META, POOL_WIDTH), f32),
                   jax.ShapeDtypeStruct((N_HEADS, N_META, LANES), bf16),
                   jax.ShapeDtypeStruct((N_HEADS, V_ROWS, N_META), bf16),
                   jax.ShapeDtypeStruct((8, LANES), f32)),
        grid=(1,),
        in_specs=[_const((N_META, D_MODEL)), _const((1, D_MODEL))]
        + [pl.BlockSpec((rows, D_MODEL), functools.partial(lambda r, i: (r, 0), col // rows),
                        pipeline_mode=pl.Buffered(1))
           for col, rows in ((COL_U, IN_GROUP), (COL_K, IN_GROUP), (COL_V, IN_GROUP), (COL_F, N_HEADS))]
        + [pl.BlockSpec(memory_space=pltpu.SMEM), _const((len(POOL_WINDOWS), POOL_GROUP, POOL_GROUP))],
        out_specs=[_const((N_META, POOL_WIDTH)), _const((N_HEADS, N_META, LANES)),
                   _const((N_HEADS, V_ROWS, N_META)), _const((8, LANES))],
        compiler_params=pltpu.CompilerParams(vmem_limit_bytes=VMEM_LIMIT),
        name="meta_proj",
    )(meta_tokens, g_in, w_t, w_t, w_t, w_t, b_f, pw)

    mp, ga, sz, qT, vT, kaug, r_tiles = pl.pallas_call(
        _proj_kernel,
        grid=(batch, n_tiles),
        in_specs=[
            pl.BlockSpec((1, TM, D_MODEL), lambda b, t: (b, t, 0)),
            _const((1, D_MODEL)),
            _const((N_IN, D_MODEL)),
            pl.BlockSpec(memory_space=pltpu.SMEM),
            _const((len(POOL_WINDOWS), POOL_GROUP, POOL_GROUP)),
            _const((1, POOL_WIDTH)),
            _const((POOL_WIDTH, D_MODEL)),
            _const((N_META, POOL_WIDTH)),
            _const((8, LANES)),
        ],
        out_specs=[
            pl.BlockSpec((1, TM, D_MODEL), lambda b, t: (b, t, 0)),
            pl.BlockSpec((1, TM, D_MODEL), lambda b, t: (b, t, 0)),
            pl.BlockSpec((1, TM, ATTN_WIDTH), lambda b, t: (b, t, 0)),
            pl.BlockSpec((1, ATTN_WIDTH, TM), lambda b, t: (b, 0, t)),
            pl.BlockSpec((1, NB, N_HEADS, V_ROWS, BLK), lambda b, t: (b, t, 0, 0, 0)),
            pl.BlockSpec((1, N_HEADS, TM, LANES), lambda b, t: (b, 0, t, 0)),
            pl.BlockSpec((1, 1, 8, LANES), lambda b, t: (b, t, 0, 0)),
        ],
        out_shape=(
            jax.ShapeDtypeStruct((batch, seq, D_MODEL), bf16),
            jax.ShapeDtypeStruct((batch, seq, D_MODEL), bf16),
            jax.ShapeDtypeStruct((batch, seq, ATTN_WIDTH), bf16),
            jax.ShapeDtypeStruct((batch, ATTN_WIDTH, seq), bf16),
            jax.ShapeDtypeStruct((batch, n_blocks, N_HEADS, V_ROWS, BLK), bf16),
            jax.ShapeDtypeStruct((batch, N_HEADS, seq, LANES), bf16),
            jax.ShapeDtypeStruct((batch, n_tiles, 8, LANES), f32),
        ),
        scratch_shapes=[pltpu.VMEM((POOL_PAD + N_META + TM, POOL_WIDTH), f32),
                        pltpu.VMEM((POOL_PAD + N_META + TM, 3 * POOL_GROUP), f32),
                        pltpu.VMEM((POOL_PAD + N_META + TM, 2 * POOL_GROUP), f32),
                        pltpu.VMEM((8, LANES), f32),
                        pltpu.VMEM((5 * IN_GROUP + F_ROWS, D_MODEL), bf16),
                        pltpu.VMEM((2 * D_MODEL, D_MODEL), bf16),
                        pltpu.VMEM((POOL_WIDTH, D_MODEL), bf16),
                        pltpu.VMEM((POOL_WIDTH, D_MODEL), bf16)],
        compiler_params=pltpu.CompilerParams(dimension_semantics=("arbitrary", "arbitrary"),
                                             vmem_limit_bytes=VMEM_LIMIT),
        name="in_proj",
    )(x, g_in, w_t, b_f, pw, ps, w_upp, u_meta, r0)

    r_blocks = r_tiles[:, :, :NB, :N_HEADS].reshape(batch * n_blocks * N_HEADS)

    y_attn = pl.pallas_call(
        functools.partial(_attn_kernel, n_blocks=n_blocks),
        grid=(batch, seq // BQ),
        in_specs=[
            pl.BlockSpec(memory_space=pltpu.SMEM),
            pl.BlockSpec((1, ATTN_WIDTH, BQ), lambda b, t: (b, 0, t)),
            pl.BlockSpec((1, ATTN_WIDTH, BQ), lambda b, t: (b, 0, jnp.minimum(t + 1, seq // BQ - 1))),
            pl.BlockSpec((1, N_HEADS, seq, LANES), lambda b, t: (b, 0, 0, 0)),
            pl.BlockSpec((1, n_blocks, N_HEADS, V_ROWS, BLK), lambda b, t: (b, 0, 0, 0, 0)),
            _const((N_HEADS, N_META, LANES)),
            _const((N_HEADS, V_ROWS, N_META)),
            pl.BlockSpec((1, BQ, ATTN_WIDTH), lambda b, t: (b, t, 0)),
        ],
        out_specs=pl.BlockSpec((1, BQ, ATTN_WIDTH), lambda b, t: (b, t, 0)),
        out_shape=jax.ShapeDtypeStruct((batch, seq, ATTN_WIDTH), bf16),
        scratch_shapes=[pltpu.VMEM((N_HEADS, LANES, BQ), bf16),
                        pltpu.VMEM((N_HEADS, LANES, BQ), bf16),
                        pltpu.VMEM((2, N_HEADS, BLK, BQ), f32),
                        pltpu.VMEM((N_HEADS, BLK, BQ), bf16),
                        pltpu.VMEM((N_HEADS, N_META, BQ), f32),
                        pltpu.VMEM((N_HEADS, N_META, BQ), bf16),
                        pltpu.VMEM((N_HEADS, 1, BQ), f32),
                        pltpu.VMEM((2, N_HEADS, 1, BQ), f32),
                        pltpu.VMEM((N_HEADS, 1, BQ), f32),
                        pltpu.VMEM((N_HEADS, 1, BQ), f32),
                        pltpu.VMEM((N_HEADS, V_ROWS, BQ), f32)],
        compiler_params=pltpu.CompilerParams(dimension_semantics=("arbitrary", "arbitrary"),
                                             vmem_limit_bytes=VMEM_LIMIT),
        name="attention",
    )(r_blocks, qT, qT, kaug, vT, k_meta, v_meta, sz)

    return pl.pallas_call(
        _out_kernel,
        grid=(batch, seq // TM_OUT),
        in_specs=[
            pl.BlockSpec((1, TM_OUT, D_MODEL), lambda b, t: (b, t, 0)),
            pl.BlockSpec((1, TM_OUT, ATTN_WIDTH), lambda b, t: (b, t, 0)),
            pl.BlockSpec((1, TM_OUT, D_MODEL), lambda b, t: (b, t, 0)),
            pl.BlockSpec((1, TM_OUT, D_MODEL), lambda b, t: (b, t, 0)),
            _const((ATTN_WIDTH, D_MODEL)),
            _const((D_MODEL, D_MODEL)),
            _const((1, D_MODEL)),
        ],
        out_specs=pl.BlockSpec((1, TM_OUT, D_MODEL), lambda b, t: (b, t, 0)),
        out_shape=jax.ShapeDtypeStruct((batch, seq, D_MODEL), f32),
        scratch_shapes=[pltpu.VMEM((ATTN_WIDTH, D_MODEL), bf16),
                        pltpu.VMEM((D_MODEL, D_MODEL), bf16)],
        compiler_params=pltpu.CompilerParams(dimension_semantics=("arbitrary", "arbitrary"),
                                             vmem_limit_bytes=VMEM_LIMIT),
        name="out_proj",
    )(x, y_attn, ga, mp, w_upa, w_o, g_out)
```

```python
import functools

import jax
import jax.numpy as jnp
from jax import lax
from jax.experimental import pallas as pl
from jax.experimental.pallas import tpu as pltpu

D_MODEL = 1024
N_META = 16
POOL_WIDTH = 512
POOL_WINDOWS = (2, 4, 8, 16)
POOL_GROUP = POOL_WIDTH // len(POOL_WINDOWS)
N_HEADS = 8
HEAD_DIM = 64
ATTN_WIDTH = N_HEADS * HEAD_DIM
RMS_EPS = 1e-6

LANES = 128
BLK = 256
BQ = 512
QB = BQ // BLK
assert QB == 2
TM = 512
NB = TM // BLK
TM_OUT = 1024
OUT_CHUNK = 256
POOL_PAD = 8
assert POOL_WINDOWS == (2, 4, 8, 16) and POOL_PAD >= POOL_WINDOWS[-1] // 2
V_ROWS = HEAD_DIM + 16
N_SPLIT = 3
MASKED = -1e30
SKEW = 2
LOG2E = 1.4426950408889634
VMEM_LIMIT = 56 * 1024 * 1024
assert POOL_WIDTH == ATTN_WIDTH
IN_GROUP = POOL_WIDTH
COL_U, COL_ZP, COL_Q, COL_K, COL_V, COL_ZA, COL_F = (IN_GROUP * i for i in range(7))
COL_G = COL_F + N_HEADS
F_ROWS = 16
N_IN = COL_G + 2 * D_MODEL

f32 = jnp.float32
bf16 = jnp.bfloat16


def _rmsnorm(x, g):
    return x * lax.rsqrt(jnp.mean(x * x, axis=-1, keepdims=True) + RMS_EPS) * g


def _sigmoid(x):
    return 0.5 * jnp.tanh(0.5 * x) + 0.5


def _log_sigmoid(x):
    return jnp.minimum(x, 0.0) - jnp.log1p(jnp.exp(-jnp.abs(x)))


def _dot(a, b):
    return jnp.dot(a, b, preferred_element_type=f32)


def _dot_nt(a, b):
    return lax.dot_general(a, b, (((1,), (1,)), ((), ())), preferred_element_type=f32)


def _lane_iota(shape):
    return lax.broadcasted_iota(jnp.int32, shape, 1)


def _decay_parts(logf3, rows):
    n = logf3.shape[0] // rows
    r = lax.broadcasted_iota(jnp.int32, (rows, rows), 0)
    c = lax.broadcasted_iota(jnp.int32, (rows, rows), 1)
    tri = (c <= r).astype(f32)
    side_by_side = jnp.concatenate([logf3[rows * i:rows * (i + 1)] for i in range(n)], axis=1)
    sums = jnp.dot(tri, side_by_side, precision=lax.Precision.HIGHEST, preferred_element_type=f32)
    beta = jnp.concatenate([sums[:, LANES * i:LANES * (i + 1)] for i in range(n)], axis=0)
    return beta, _split_decay(beta)


def _decay_parts_from_rows(logf_t, rows):
    n = logf_t.shape[1] // rows
    k = lax.broadcasted_iota(jnp.int32, (rows, rows), 0)
    j = lax.broadcasted_iota(jnp.int32, (rows, rows), 1)
    stacked = jnp.concatenate([logf_t[:, rows * i:rows * (i + 1)] for i in range(n)], axis=0)
    sums_t = jnp.dot(stacked, (k <= j).astype(f32), precision=lax.Precision.HIGHEST, preferred_element_type=f32)
    unused = jnp.zeros((LANES - N_SPLIT * N_HEADS, rows), f32)
    beta = jnp.concatenate(
        [jnp.concatenate([sums_t[N_HEADS * i:N_HEADS * (i + 1)]] * N_SPLIT + [unused], axis=0).T for i in range(n)],
        axis=0)
    return beta, _split_decay(beta)


def _split_decay(beta):
    nb = beta * (-LOG2E)
    hi = nb.astype(bf16).astype(f32)
    mid = (nb - hi).astype(bf16).astype(f32)
    lo = (nb - hi - mid).astype(bf16).astype(f32)
    lane = _lane_iota(nb.shape)
    return jnp.where(lane < N_HEADS, hi, jnp.where(lane < 2 * N_HEADS, mid, lo))


def _augmented_keys(kproj, parts, store):
    lane = _lane_iota((kproj.shape[0], LANES))
    for h in range(N_HEADS):
        slab = kproj[:, LANES * (h // 2):LANES * (h // 2 + 1)]
        if h % 2:
            slab = pltpu.roll(slab, HEAD_DIM, axis=1)
        store(h, jnp.where(lane < HEAD_DIM, slab, pltpu.roll(parts, HEAD_DIM - h, axis=1)).astype(bf16))


def _pool_mapped(wu_t, pw_ref):
    return jnp.concatenate([_dot(pw_ref[g].T.astype(bf16), wu_t[POOL_GROUP * g:POOL_GROUP * (g + 1)])
                            for g in range(len(POOL_WINDOWS))], axis=0).astype(bf16)


def _forget_rows(wf_t):
    unused = jnp.zeros((LANES - N_SPLIT * N_HEADS, wf_t.shape[1]), f32)
    return jnp.concatenate([wf_t] * N_SPLIT + [unused], axis=0).astype(bf16)


def _forget_bias(b_ref, shape, axis):
    index = lax.broadcasted_iota(jnp.int32, shape, axis)
    bias = jnp.zeros(shape, f32)
    for h in range(N_HEADS):
        bias = jnp.where((index % N_HEADS == h) & (index < N_SPLIT * N_HEADS), b_ref[h], bias)
    return bias


def _meta_kernel(meta_ref, g_ref, wu_ref, wk_ref, wv_ref, wf_ref, bf_ref, pw_ref,
                 umeta_ref, kmeta_ref, vmeta_ref, r0_ref):
    hn = _rmsnorm(meta_ref[...], g_ref[...]).astype(bf16)
    umeta_ref[...] = _dot_nt(hn, _pool_mapped(wu_ref[...].astype(bf16), pw_ref))
    vt = _dot_nt(wv_ref[...].astype(bf16), hn)
    for h in range(N_HEADS):
        vmeta_ref[h, 0:HEAD_DIM, :] = vt[HEAD_DIM * h:HEAD_DIM * (h + 1), :].astype(bf16)
        vmeta_ref[h, HEAD_DIM:V_ROWS, :] = jnp.ones((V_ROWS - HEAD_DIM, N_META), bf16)
    logf3 = _log_sigmoid(_dot_nt(hn, _forget_rows(wf_ref[...])) + _forget_bias(bf_ref, (1, LANES), 1))
    beta, parts = _decay_parts(logf3, N_META)

    def store(h, ka):
        kmeta_ref[h] = ka

    _augmented_keys(_dot_nt(hn, wk_ref[...].astype(bf16)), parts, store)
    r0_ref[...] = jnp.broadcast_to(beta[N_META - 1:N_META, :], r0_ref.shape)


def _proj_kernel(x_ref, g_ref, wt_ref, bf_ref, pw_ref, ps_ref, wup32_ref,
                 umeta_ref, r0_ref,
                 mp_ref, ga_ref, sz_ref, qT_ref, vT_ref, kaug_ref, r_ref,
                 uext_ref, lvl_a_ref, lvl_b_ref, rcarry_ref, wb_ref, wg_ref, wu_ref, wup_ref):
    t = pl.program_id(1)
    zp_rows, k_rows, za_rows, q_rows, v_rows = (slice(IN_GROUP * i, IN_GROUP * (i + 1)) for i in range(5))
    f_rows = slice(v_rows.stop, v_rows.stop + F_ROWS)

    @pl.when((pl.program_id(0) == 0) & (t == 0))
    def _():
        for rows, col in ((zp_rows, COL_ZP), (k_rows, COL_K), (za_rows, COL_ZA), (q_rows, COL_Q), (v_rows, COL_V)):
            wb_ref[rows, :] = wt_ref[col:col + IN_GROUP, :].astype(bf16)
        wb_ref[f_rows, :] = wt_ref[COL_F:COL_F + F_ROWS, :].astype(bf16)
        wg_ref[...] = wt_ref[COL_G:N_IN, :].astype(bf16)
        wu_ref[...] = _pool_mapped(wt_ref[COL_U:COL_ZP, :].astype(bf16), pw_ref)
        wup_ref[...] = wup32_ref[...].astype(bf16)
    ext = N_META + TM
    body = slice(POOL_PAD, POOL_PAD + ext)

    @pl.when(t == 0)
    def _():
        uext_ref[0:POOL_PAD, :] = jnp.zeros((POOL_PAD, POOL_WIDTH), f32)
        lvl_a_ref[0:POOL_PAD, :] = jnp.zeros((POOL_PAD, lvl_a_ref.shape[1]), f32)
        lvl_b_ref[0:POOL_PAD, :] = jnp.zeros((POOL_PAD, lvl_b_ref.shape[1]), f32)
        uext_ref[POOL_PAD:POOL_PAD + N_META, :] = umeta_ref[...]
        rcarry_ref[...] = r0_ref[...]

    hn = _rmsnorm(x_ref[0], g_ref[...]).astype(bf16)

    qv = _dot_nt(wb_ref[q_rows.start:, :], hn)
    qT_ref[0] = (qv[0:ATTN_WIDTH] * (HEAD_DIM ** -0.5 * LOG2E)).astype(bf16)
    for c in range(NB):
        for h in range(N_HEADS):
            rows = slice(ATTN_WIDTH + HEAD_DIM * h, ATTN_WIDTH + HEAD_DIM * (h + 1))
            vT_ref[0, c, h, 0:HEAD_DIM, :] = qv[rows, BLK * c:BLK * (c + 1)].astype(bf16)
            vT_ref[0, c, h, HEAD_DIM:V_ROWS, :] = jnp.ones((V_ROWS - HEAD_DIM, BLK), bf16)

    gat = _dot_nt(hn, wg_ref[D_MODEL:2 * D_MODEL, :])
    ga_ref[0] = _sigmoid(gat).astype(bf16)

    u = _dot_nt(hn, wu_ref[...])
    zp = _dot_nt(hn, wb_ref[zp_rows, :])
    uext_ref[POOL_PAD + N_META:POOL_PAD + ext, :] = u

    def shifted_sum(ref, shift, cols):
        return ref[body, cols] + ref[POOL_PAD - shift:POOL_PAD - shift + ext, cols]

    g1, g2, g3 = (slice(POOL_GROUP * g, POOL_GROUP * (g + 1)) for g in range(3))
    sums = [shifted_sum(uext_ref, 1, g1)]
    lvl_a_ref[body, :] = shifted_sum(uext_ref, 1, slice(POOL_GROUP, POOL_WIDTH))
    sums.append(shifted_sum(lvl_a_ref, 2, g1))
    lvl_b_ref[body, :] = shifted_sum(lvl_a_ref, 2, slice(POOL_GROUP, 3 * POOL_GROUP))
    sums.append(shifted_sum(lvl_b_ref, 4, g1))
    lvl_a_ref[body, g1] = shifted_sum(lvl_b_ref, 4, g2)
    sums.append(shifted_sum(lvl_a_ref, 8, g1))
    uext_ref[POOL_PAD:POOL_PAD + N_META, :] = uext_ref[POOL_PAD + TM:POOL_PAD + ext, :]

    pooled = jnp.concatenate([sums[g][N_META:] * (1.0 / w) for g, w in enumerate(POOL_WINDOWS)], axis=1) - u
    y_pool = (pooled * ps_ref[...] * (zp * _sigmoid(zp))).astype(bf16)
    gp = _dot_nt(hn, wg_ref[0:D_MODEL, :])
    mp_ref[0] = (_sigmoid(gp) * _dot(y_pool, wup_ref[...])).astype(bf16)

    logf_t = _log_sigmoid(qv[2 * IN_GROUP:2 * IN_GROUP + N_HEADS] + _forget_bias(bf_ref, (N_HEADS, TM), 0))
    r_ref[...] = jnp.zeros(r_ref.shape, f32)
    beta, parts = _decay_parts_from_rows(logf_t, BLK)
    for c in range(NB):
        r_ref[0, 0, c:c + 1, :] = rcarry_ref[0:1, :] * LOG2E
        rcarry_ref[...] = rcarry_ref[...] + beta[BLK * (c + 1) - 1:BLK * (c + 1), :]

    def store(h, ka):
        kaug_ref[0, h] = ka

    _augmented_keys(_dot_nt(hn, wb_ref[k_rows, :]), parts, store)

    za = _dot_nt(hn, wb_ref[za_rows, :])
    sz_ref[0] = (za * _sigmoid(za)).astype(bf16)


def _attn_kernel(r_ref, qT_ref, qnext_ref, kaug_ref, vT_ref, kmeta_ref, vmeta_ref, sz_ref, o_ref,
                 qaug_ref, s_ref, p_ref, sm_ref, pm_ref, m_ref, mblk_ref, mmeta_ref, alpha_ref, acc_ref,
                 *, n_blocks):
    b = pl.program_id(0)
    t = pl.program_id(1)
    first = QB * t
    heads = range(N_HEADS)

    this_slot, next_slot = t % 2, (t + 1) % 2

    @pl.when((b == 0) & (t == 0))
    def _():
        part_row = lax.broadcasted_iota(jnp.int32, (LANES - HEAD_DIM, BQ), 0)
        ones_rows = ((part_row % N_HEADS == 0) & (part_row < N_SPLIT * N_HEADS)).astype(bf16)
        for slot in range(2):
            for h in heads:
                qaug_ref[slot, h, HEAD_DIM:LANES, :] = ones_rows

    @pl.when(t == 0)
    def _():
        for h in heads:
            qaug_ref[0, h, 0:HEAD_DIM, :] = qT_ref[0, HEAD_DIM * h:HEAD_DIM * (h + 1), :]

    for h in heads:
        qaug_ref[next_slot, h, 0:HEAD_DIM, :] = qnext_ref[0, HEAD_DIM * h:HEAD_DIM * (h + 1), :]

    def r_at(j, h):
        return r_ref[(b * n_blocks + j) * N_HEADS + h]

    def col_max(s):
        return jnp.max(s, axis=0, keepdims=True)

    key_index = functools.partial(lax.broadcasted_iota, jnp.int32, dimension=0)
    query_index = functools.partial(lax.broadcasted_iota, jnp.int32, dimension=1)

    def seen_by(diag):
        return slice(BLK * diag, BQ) if diag else slice(None)

    def score_item(j, slot, h, diag=None, meta=False, next_tile=False):
        q_slot = next_slot if next_tile else this_slot
        cols = seen_by(diag)
        s = _dot(kaug_ref[0, h, pl.ds(pl.multiple_of(j * BLK, BLK), BLK), :], qaug_ref[q_slot, h, :, cols])
        if diag is not None:
            s = jnp.where(key_index(s.shape) <= query_index(s.shape), s, MASKED)
        s_ref[slot, h, :, cols] = s
        unseen = [jnp.full((1, BQ - s.shape[1]), MASKED, f32)] if s.shape[1] < BQ else []
        mblk_ref[slot, h] = jnp.concatenate(unseen + [col_max(s)], axis=1)
        if meta:
            sm = _dot(kmeta_ref[h], qaug_ref[this_slot, h])
            sm_ref[h] = sm
            mmeta_ref[h] = col_max(sm)

    def softmax_item(j, slot, h, meta=False, diag=None):
        cols = seen_by(diag)
        assert not (meta and diag)
        ref = r_at(first, h)
        off = ref - r_at(j, h)
        m_old = m_ref[h]
        m_blk = mblk_ref[slot, h] + off
        if meta:
            m_blk = jnp.maximum(m_blk, mmeta_ref[h] + ref)
        m_new = jnp.maximum(m_old, m_blk)
        alpha_ref[h] = jnp.exp2(m_old - m_new)
        m_seen = m_new
        if diag:
            m_seen = jnp.maximum(m_ref[h, :, cols], mblk_ref[slot, h, :, cols] + off)
        m_ref[h] = m_new
        p_ref[h, :, cols] = jnp.exp2(s_ref[slot, h, :, cols] - (m_seen - off)).astype(bf16)
        if meta:
            pm_ref[h] = jnp.exp2(sm_ref[h] - (m_new - ref)).astype(bf16)

    def value_item(j, slot, h, meta=False, diag=None):
        cols = seen_by(diag)
        pv = _dot(vT_ref[0, j, h], p_ref[h, :, cols])
        if meta:
            pv = pv + _dot(vmeta_ref[h], pm_ref[h])
        acc_ref[h, :, cols] = acc_ref[h, :, cols] * alpha_ref[h, :, cols] + pv

    def run(score_blocks, finish_blocks):
        score_items = [(blk + (h,), kw) for blk, kw in score_blocks for h in heads]
        finish_items = [(blk + (h,), kw) for blk, kw in finish_blocks for h in heads]
        for i in range(max(len(score_items), len(finish_items) + SKEW + 1)):
            if i < len(score_items):
                args, kw = score_items[i]
                score_item(*args, **kw)
            if 0 <= i - SKEW - 1 < len(finish_items):
                args, kw = finish_items[i - SKEW - 1]
                value_item(*args, **kw)
            if 0 <= i - SKEW < len(finish_items):
                args, kw = finish_items[i - SKEW]
                softmax_item(*args, **kw)

    m_ref[...] = jnp.full(m_ref.shape, MASKED, f32)
    acc_ref[...] = jnp.zeros(acc_ref.shape, f32)

    @pl.when(first == 0)
    def _():
        run([((first, 0), dict(diag=0))], [])

    @pl.when(first > 0)
    def _():
        def body(k, carry):
            run([((2 * k + 1, 1), {}), ((2 * k + 2, 0), {})],
                [((2 * k, 0), {}), ((2 * k + 1, 1), {})])
            return carry

        lax.fori_loop(0, t - 1, body, 0)
        run([((first - 1, 1), {}), ((first, 0), dict(diag=0))],
            [((first - 2, 0), {}), ((first - 1, 1), {})])

    run([((first + 1, 1), dict(diag=1, meta=True)), ((0, 0), dict(next_tile=True))],
        [((first, 0), dict(meta=True)), ((first + 1, 1), dict(diag=1))])

    for pair in range(N_HEADS // 2):
        halves = []
        for h in (2 * pair, 2 * pair + 1):
            a = acc_ref[h]
            halves.append(a[0:HEAD_DIM] * (1.0 / a[HEAD_DIM:HEAD_DIM + 1]))
        cols = slice(LANES * pair, LANES * (pair + 1))
        o_ref[0, :, cols] = (jnp.concatenate(halves, axis=0).T * sz_ref[0, :, cols].astype(f32)).astype(bf16)


def _out_kernel(x_ref, y_ref, ga_ref, mp_ref, wua32_ref, wout32_ref, g_ref, out_ref, wua_ref, wout_ref):
    @pl.when((pl.program_id(0) == 0) & (pl.program_id(1) == 0))
    def _():
        wua_ref[...] = wua32_ref[...].astype(bf16)
        wout_ref[...] = wout32_ref[...].astype(bf16)

    def merged(rows):
        up = _dot(y_ref[0, rows, :], wua_ref[...])
        return (mp_ref[0, rows, :].astype(f32) + ga_ref[0, rows, :].astype(f32) * up).astype(bf16)

    chunks = [slice(OUT_CHUNK * i, OUT_CHUNK * (i + 1)) for i in range(TM_OUT // OUT_CHUNK)]
    pending = merged(chunks[0])
    for i, rows in enumerate(chunks):
        ahead = merged(chunks[i + 1]) if i + 1 < len(chunks) else None
        h_out = x_ref[0, rows, :] + _dot(pending, wout_ref[...])
        out_ref[0, rows, :] = _rmsnorm(h_out, g_ref[...])
        pending = ahead


def _const(shape):
    return pl.BlockSpec(shape, lambda *_: (0,) * len(shape), pipeline_mode=pl.Buffered(1))


def kernel(x, meta_tokens, norm_g, w_in, b_forget, pool_w, pool_scale, w_up_pool, w_up_attn, w_out, final_norm_g):
    batch, seq, _ = x.shape
    n_tiles = seq // TM
    n_blocks = seq // BLK

    w_t = jnp.transpose(w_in[0])
    b_f = b_forget.reshape(N_HEADS)
    g_in = norm_g[0].reshape(1, D_MODEL)
    g_out = final_norm_g.reshape(1, D_MODEL)
    pw = pool_w[0]
    ps = pool_scale[0].reshape(1, POOL_WIDTH)
    w_upp, w_upa, w_o = w_up_pool[0], w_up_attn[0], w_out[0]

    u_meta, k_meta, v_meta, r0 = pl.pallas_call(
        _meta_kernel,
        out_shape=(jax.ShapeDtypeStruct((N_META, POOL_WIDTH), f32),
                   jax.ShapeDtypeStruct((N_HEADS, N_META, LANES), bf16),
                   jax.ShapeDtypeStruct((N_HEADS, V_ROWS, N_META), bf16),
                   jax.ShapeDtypeStruct((8, LANES), f32)),
        grid=(1,),
        in_specs=[_const((N_META, D_MODEL)), _const((1, D_MODEL))]
        + [pl.BlockSpec((rows, D_MODEL), functools.partial(lambda r, i: (r, 0), col // rows),
                        pipeline_mode=pl.Buffered(1))
           for col, rows in ((COL_U, IN_GROUP), (COL_K, IN_GROUP), (COL_V, IN_GROUP), (COL_F, N_HEADS))]
        + [pl.BlockSpec(memory_space=pltpu.SMEM), _const((len(POOL_WINDOWS), POOL_GROUP, POOL_GROUP))],
        out_specs=[_const((N_META, POOL_WIDTH)), _const((N_HEADS, N_META, LANES)),
                   _const((N_HEADS, V_ROWS, N_META)), _const((8, LANES))],
        compiler_params=pltpu.CompilerParams(vmem_limit_bytes=VMEM_LIMIT),
        name="meta_proj",
    )(meta_tokens, g_in, w_t, w_t, w_t, w_t, b_f, pw)

    mp, ga, sz, qT, vT, kaug, r_tiles = pl.pallas_call(
        _proj_kernel,
        grid=(batch, n_tiles),
        in_specs=[
            pl.BlockSpec((1, TM, D_MODEL), lambda b, t: (b, t, 0)),
            _const((1, D_MODEL)),
            _const((N_IN, D_MODEL)),
            pl.BlockSpec(memory_space=pltpu.SMEM),
            _const((len(POOL_WINDOWS), POOL_GROUP, POOL_GROUP)),
            _const((1, POOL_WIDTH)),
            _const((POOL_WIDTH, D_MODEL)),
            _const((N_META, POOL_WIDTH)),
            _const((8, LANES)),
        ],
        out_specs=[
            pl.BlockSpec((1, TM, D_MODEL), lambda b, t: (b, t, 0)),
            pl.BlockSpec((1, TM, D_MODEL), lambda b, t: (b, t, 0)),
            pl.BlockSpec((1, TM, ATTN_WIDTH), lambda b, t: (b, t, 0)),
            pl.BlockSpec((1, ATTN_WIDTH, TM), lambda b, t: (b, 0, t)),
            pl.BlockSpec((1, NB, N_HEADS, V_ROWS, BLK), lambda b, t: (b, t, 0, 0, 0)),
            pl.BlockSpec((1, N_HEADS, TM, LANES), lambda b, t: (b, 0, t, 0)),
            pl.BlockSpec((1, 1, 8, LANES), lambda b, t: (b, t, 0, 0)),
        ],
        out_shape=(
            jax.ShapeDtypeStruct((batch, seq, D_MODEL), bf16),
            jax.ShapeDtypeStruct((batch, seq, D_MODEL), bf16),
            jax.ShapeDtypeStruct((batch, seq, ATTN_WIDTH), bf16),
            jax.ShapeDtypeStruct((batch, ATTN_WIDTH, seq), bf16),
            jax.ShapeDtypeStruct((batch, n_blocks, N_HEADS, V_ROWS, BLK), bf16),
            jax.ShapeDtypeStruct((batch, N_HEADS, seq, LANES), bf16),
            jax.ShapeDtypeStruct((batch, n_tiles, 8, LANES), f32),
        ),
        scratch_shapes=[pltpu.VMEM((POOL_PAD + N_META + TM, POOL_WIDTH), f32),
                        pltpu.VMEM((POOL_PAD + N_META + TM, 3 * POOL_GROUP), f32),
                        pltpu.VMEM((POOL_PAD + N_META + TM, 2 * POOL_GROUP), f32),
                        pltpu.VMEM((8, LANES), f32),
                        pltpu.VMEM((5 * IN_GROUP + F_ROWS, D_MODEL), bf16),
                        pltpu.VMEM((2 * D_MODEL, D_MODEL), bf16),
                        pltpu.VMEM((POOL_WIDTH, D_MODEL), bf16),
                        pltpu.VMEM((POOL_WIDTH, D_MODEL), bf16)],
        compiler_params=pltpu.CompilerParams(dimension_semantics=("arbitrary", "arbitrary"),
                                             vmem_limit_bytes=VMEM_LIMIT),
        name="in_proj",
    )(x, g_in, w_t, b_f, pw, ps, w_upp, u_meta, r0)

    r_blocks = r_tiles[:, :, :NB, :N_HEADS].reshape(batch * n_blocks * N_HEADS)

    y_attn = pl.pallas_call(
        functools.partial(_attn_kernel, n_blocks=n_blocks),
        grid=(batch, seq // BQ),
        in_specs=[
            pl.BlockSpec(memory_space=pltpu.SMEM),
            pl.BlockSpec((1, ATTN_WIDTH, BQ), lambda b, t: (b, 0, t)),
            pl.BlockSpec((1, ATTN_WIDTH, BQ), lambda b, t: (b, 0, jnp.minimum(t + 1, seq // BQ - 1))),
            pl.BlockSpec((1, N_HEADS, seq, LANES), lambda b, t: (b, 0, 0, 0)),
            pl.BlockSpec((1, n_blocks, N_HEADS, V_ROWS, BLK), lambda b, t: (b, 0, 0, 0, 0)),
            _const((N_HEADS, N_META, LANES)),
            _const((N_HEADS, V_ROWS, N_META)),
            pl.BlockSpec((1, BQ, ATTN_WIDTH), lambda b, t: (b, t, 0)),
        ],
        out_specs=pl.BlockSpec((1, BQ, ATTN_WIDTH), lambda b, t: (b, t, 0)),
        out_shape=jax.ShapeDtypeStruct((batch, seq, ATTN_WIDTH), bf16),
        scratch_shapes=[pltpu.VMEM((2, N_HEADS, LANES, BQ), bf16),
                        pltpu.VMEM((2, N_HEADS, BLK, BQ), f32),
                        pltpu.VMEM((N_HEADS, BLK, BQ), bf16),
                        pltpu.VMEM((N_HEADS, N_META, BQ), f32),
                        pltpu.VMEM((N_HEADS, N_META, BQ), bf16),
                        pltpu.VMEM((N_HEADS, 1, BQ), f32),
                        pltpu.VMEM((2, N_HEADS, 1, BQ), f32),
                        pltpu.VMEM((N_HEADS, 1, BQ), f32),
                        pltpu.VMEM((N_HEADS, 1, BQ), f32),
                        pltpu.VMEM((N_HEADS, V_ROWS, BQ), f32)],
        compiler_params=pltpu.CompilerParams(dimension_semantics=("arbitrary", "arbitrary"),
                                             vmem_limit_bytes=VMEM_LIMIT),
        name="attention",
    )(r_blocks, qT, qT, kaug, vT, k_meta, v_meta, sz)

    return pl.pallas_call(
        _out_kernel,
        grid=(batch, seq // TM_OUT),
        in_specs=[
            pl.BlockSpec((1, TM_OUT, D_MODEL), lambda b, t: (b, t, 0)),
            pl.BlockSpec((1, TM_OUT, ATTN_WIDTH), lambda b, t: (b, t, 0)),
            pl.BlockSpec((1, TM_OUT, D_MODEL), lambda b, t: (b, t, 0)),
            pl.BlockSpec((1, TM_OUT, D_MODEL), lambda b, t: (b, t, 0)),
            _const((ATTN_WIDTH, D_MODEL)),
            _const((D_MODEL, D_MODEL)),
            _const((1, D_MODEL)),
        ],
        out_specs=pl.BlockSpec((1, TM_OUT, D_MODEL), lambda b, t: (b, t, 0)),
        out_shape=jax.ShapeDtypeStruct((batch, seq, D_MODEL), f32),
        scratch_shapes=[pltpu.VMEM((ATTN_WIDTH, D_MODEL), bf16),
                        pltpu.VMEM((D_MODEL, D_MODEL), bf16)],
        compiler_params=pltpu.CompilerParams(dimension_semantics=("arbitrary", "arbitrary"),
                                             vmem_limit_bytes=VMEM_LIMIT),
        name="out_proj",
    )(x, y_attn, ga, mp, w_upa, w_o, g_out)
```

```python
import functools

import jax
import jax.numpy as jnp
from jax import lax
from jax.experimental import pallas as pl
from jax.experimental.pallas import tpu as pltpu

D_MODEL = 1024
N_META = 16
POOL_WIDTH = 512
POOL_WINDOWS = (2, 4, 8, 16)
POOL_GROUP = POOL_WIDTH // len(POOL_WINDOWS)
N_HEADS = 8
HEAD_DIM = 64
ATTN_WIDTH = N_HEADS * HEAD_DIM
RMS_EPS = 1e-6

LANES = 128
BLK = 256
BQ = 512
QB = BQ // BLK
assert QB == 2
TM = 512
NB = TM // BLK
TM_OUT = 1024
OUT_CHUNK = 256
POOL_PAD = 8
assert POOL_WINDOWS == (2, 4, 8, 16) and POOL_PAD >= POOL_WINDOWS[-1] // 2
V_ROWS = HEAD_DIM + 16
N_SPLIT = 3
MASKED = -1e30
SKEW = 2
LOG2E = 1.4426950408889634
VMEM_LIMIT = 56 * 1024 * 1024
assert POOL_WIDTH == ATTN_WIDTH
IN_GROUP = POOL_WIDTH
COL_U, COL_ZP, COL_Q, COL_K, COL_V, COL_ZA, COL_F = (IN_GROUP * i for i in range(7))
COL_G = COL_F + N_HEADS
F_ROWS = 16
N_IN = COL_G + 2 * D_MODEL

f32 = jnp.float32
bf16 = jnp.bfloat16


def _rmsnorm(x, g):
    return x * lax.rsqrt(jnp.mean(x * x, axis=-1, keepdims=True) + RMS_EPS) * g


def _sigmoid(x):
    return 0.5 * jnp.tanh(0.5 * x) + 0.5


def _log_sigmoid(x):
    return jnp.minimum(x, 0.0) - jnp.log1p(jnp.exp(-jnp.abs(x)))


def _dot(a, b):
    return jnp.dot(a, b, preferred_element_type=f32)


def _dot_nt(a, b):
    return lax.dot_general(a, b, (((1,), (1,)), ((), ())), preferred_element_type=f32)


def _lane_iota(shape):
    return lax.broadcasted_iota(jnp.int32, shape, 1)


def _decay_parts(logf3, rows):
    n = logf3.shape[0] // rows
    r = lax.broadcasted_iota(jnp.int32, (rows, rows), 0)
    c = lax.broadcasted_iota(jnp.int32, (rows, rows), 1)
    tri = (c <= r).astype(f32)
    side_by_side = jnp.concatenate([logf3[rows * i:rows * (i + 1)] for i in range(n)], axis=1)
    sums = jnp.dot(tri, side_by_side, precision=lax.Precision.HIGHEST, preferred_element_type=f32)
    beta = jnp.concatenate([sums[:, LANES * i:LANES * (i + 1)] for i in range(n)], axis=0)
    return beta, _split_decay(beta)


def _decay_parts_from_rows(logf_t, rows):
    n = logf_t.shape[1] // rows
    k = lax.broadcasted_iota(jnp.int32, (rows, rows), 0)
    j = lax.broadcasted_iota(jnp.int32, (rows, rows), 1)
    stacked = jnp.concatenate([logf_t[:, rows * i:rows * (i + 1)] for i in range(n)], axis=0)
    sums_t = jnp.dot(stacked, (k <= j).astype(f32), precision=lax.Precision.HIGHEST, preferred_element_type=f32)
    unused = jnp.zeros((LANES - N_SPLIT * N_HEADS, rows), f32)
    beta = jnp.concatenate(
        [jnp.concatenate([sums_t[N_HEADS * i:N_HEADS * (i + 1)]] * N_SPLIT + [unused], axis=0).T for i in range(n)],
        axis=0)
    return beta, _split_decay(beta)


def _split_decay(beta):
    nb = beta * (-LOG2E)
    hi = nb.astype(bf16).astype(f32)
    mid = (nb - hi).astype(bf16).astype(f32)
    lo = (nb - hi - mid).astype(bf16).astype(f32)
    lane = _lane_iota(nb.shape)
    return jnp.where(lane < N_HEADS, hi, jnp.where(lane < 2 * N_HEADS, mid, lo))


def _augmented_keys(kproj, parts, store):
    lane = _lane_iota((kproj.shape[0], LANES))
    for h in range(N_HEADS):
        slab = kproj[:, LANES * (h // 2):LANES * (h // 2 + 1)]
        if h % 2:
            slab = pltpu.roll(slab, HEAD_DIM, axis=1)
        store(h, jnp.where(lane < HEAD_DIM, slab, pltpu.roll(parts, HEAD_DIM - h, axis=1)).astype(bf16))


def _pool_mapped(wu_t, pw_ref):
    return jnp.concatenate([_dot(pw_ref[g].T.astype(bf16), wu_t[POOL_GROUP * g:POOL_GROUP * (g + 1)])
                            for g in range(len(POOL_WINDOWS))], axis=0).astype(bf16)


def _forget_rows(wf_t):
    unused = jnp.zeros((LANES - N_SPLIT * N_HEADS, wf_t.shape[1]), f32)
    return jnp.concatenate([wf_t] * N_SPLIT + [unused], axis=0).astype(bf16)


def _forget_bias(b_ref, shape, axis):
    index = lax.broadcasted_iota(jnp.int32, shape, axis)
    bias = jnp.zeros(shape, f32)
    for h in range(N_HEADS):
        bias = jnp.where((index % N_HEADS == h) & (index < N_SPLIT * N_HEADS), b_ref[h], bias)
    return bias


def _meta_kernel(meta_ref, g_ref, wu_ref, wk_ref, wv_ref, wf_ref, bf_ref, pw_ref,
                 umeta_ref, kmeta_ref, vmeta_ref, r0_ref):
    hn = _rmsnorm(meta_ref[...], g_ref[...]).astype(bf16)
    umeta_ref[...] = _dot_nt(hn, _pool_mapped(wu_ref[...].astype(bf16), pw_ref))
    vt = _dot_nt(wv_ref[...].astype(bf16), hn)
    for h in range(N_HEADS):
        vmeta_ref[h, 0:HEAD_DIM, :] = vt[HEAD_DIM * h:HEAD_DIM * (h + 1), :].astype(bf16)
        vmeta_ref[h, HEAD_DIM:V_ROWS, :] = jnp.ones((V_ROWS - HEAD_DIM, N_META), bf16)
    logf3 = _log_sigmoid(_dot_nt(hn, _forget_rows(wf_ref[...])) + _forget_bias(bf_ref, (1, LANES), 1))
    beta, parts = _decay_parts(logf3, N_META)

    def store(h, ka):
        kmeta_ref[h] = ka

    _augmented_keys(_dot_nt(hn, wk_ref[...].astype(bf16)), parts, store)
    r0_ref[...] = jnp.broadcast_to(beta[N_META - 1:N_META, :], r0_ref.shape)


def _proj_kernel(x_ref, g_ref, wt_ref, bf_ref, pw_ref, ps_ref, wup32_ref,
                 umeta_ref, r0_ref,
                 mp_ref, ga_ref, sz_ref, qT_ref, vT_ref, kaug_ref, r_ref,
                 uext_ref, lvl_a_ref, lvl_b_ref, rcarry_ref, wb_ref, wg_ref, wu_ref, wup_ref):
    t = pl.program_id(1)
    zp_rows, k_rows, za_rows, q_rows, v_rows = (slice(IN_GROUP * i, IN_GROUP * (i + 1)) for i in range(5))
    f_rows = slice(v_rows.stop, v_rows.stop + F_ROWS)

    @pl.when((pl.program_id(0) == 0) & (t == 0))
    def _():
        for rows, col in ((zp_rows, COL_ZP), (k_rows, COL_K), (za_rows, COL_ZA), (q_rows, COL_Q), (v_rows, COL_V)):
            wb_ref[rows, :] = wt_ref[col:col + IN_GROUP, :].astype(bf16)
        wb_ref[f_rows, :] = wt_ref[COL_F:COL_F + F_ROWS, :].astype(bf16)
        wg_ref[...] = wt_ref[COL_G:N_IN, :].astype(bf16)
        wu_ref[...] = _pool_mapped(wt_ref[COL_U:COL_ZP, :].astype(bf16), pw_ref)
        wup_ref[...] = wup32_ref[...].astype(bf16)
    ext = N_META + TM
    body = slice(POOL_PAD, POOL_PAD + ext)

    @pl.when(t == 0)
    def _():
        uext_ref[0:POOL_PAD, :] = jnp.zeros((POOL_PAD, POOL_WIDTH), f32)
        lvl_a_ref[0:POOL_PAD, :] = jnp.zeros((POOL_PAD, lvl_a_ref.shape[1]), f32)
        lvl_b_ref[0:POOL_PAD, :] = jnp.zeros((POOL_PAD, lvl_b_ref.shape[1]), f32)
        uext_ref[POOL_PAD:POOL_PAD + N_META, :] = umeta_ref[...]
        rcarry_ref[...] = r0_ref[...]

    hn = _rmsnorm(x_ref[0], g_ref[...]).astype(bf16)

    gat = _dot_nt(hn, wg_ref[D_MODEL:2 * D_MODEL, :])
    ga_ref[0] = _sigmoid(gat).astype(bf16)

    qv = _dot_nt(wb_ref[q_rows.start:, :], hn)
    qT_ref[0] = (qv[0:ATTN_WIDTH] * (HEAD_DIM ** -0.5 * LOG2E)).astype(bf16)
    for c in range(NB):
        for h in range(N_HEADS):
            rows = slice(ATTN_WIDTH + HEAD_DIM * h, ATTN_WIDTH + HEAD_DIM * (h + 1))
            vT_ref[0, c, h, 0:HEAD_DIM, :] = qv[rows, BLK * c:BLK * (c + 1)].astype(bf16)
            vT_ref[0, c, h, HEAD_DIM:V_ROWS, :] = jnp.ones((V_ROWS - HEAD_DIM, BLK), bf16)

    u = _dot_nt(hn, wu_ref[...])
    zp = _dot_nt(hn, wb_ref[zp_rows, :])
    uext_ref[POOL_PAD + N_META:POOL_PAD + ext, :] = u

    def shifted_sum(ref, shift, cols):
        return ref[body, cols] + ref[POOL_PAD - shift:POOL_PAD - shift + ext, cols]

    g1, g2, g3 = (slice(POOL_GROUP * g, POOL_GROUP * (g + 1)) for g in range(3))
    sums = [shifted_sum(uext_ref, 1, g1)]
    lvl_a_ref[body, :] = shifted_sum(uext_ref, 1, slice(POOL_GROUP, POOL_WIDTH))
    sums.append(shifted_sum(lvl_a_ref, 2, g1))
    lvl_b_ref[body, :] = shifted_sum(lvl_a_ref, 2, slice(POOL_GROUP, 3 * POOL_GROUP))
    sums.append(shifted_sum(lvl_b_ref, 4, g1))
    lvl_a_ref[body, g1] = shifted_sum(lvl_b_ref, 4, g2)
    sums.append(shifted_sum(lvl_a_ref, 8, g1))
    uext_ref[POOL_PAD:POOL_PAD + N_META, :] = uext_ref[POOL_PAD + TM:POOL_PAD + ext, :]

    pooled = jnp.concatenate([sums[g][N_META:] * (1.0 / w) for g, w in enumerate(POOL_WINDOWS)], axis=1) - u
    y_pool = (pooled * ps_ref[...] * (zp * _sigmoid(zp))).astype(bf16)
    gp = _dot_nt(hn, wg_ref[0:D_MODEL, :])
    mp_ref[0] = (_sigmoid(gp) * _dot(y_pool, wup_ref[...])).astype(bf16)

    logf_t = _log_sigmoid(qv[2 * IN_GROUP:2 * IN_GROUP + N_HEADS] + _forget_bias(bf_ref, (N_HEADS, TM), 0))
    r_ref[...] = jnp.zeros(r_ref.shape, f32)
    beta, parts = _decay_parts_from_rows(logf_t, BLK)
    for c in range(NB):
        r_ref[0, 0, c:c + 1, :] = rcarry_ref[0:1, :] * LOG2E
        rcarry_ref[...] = rcarry_ref[...] + beta[BLK * (c + 1) - 1:BLK * (c + 1), :]

    def store(h, ka):
        kaug_ref[0, h] = ka

    _augmented_keys(_dot_nt(hn, wb_ref[k_rows, :]), parts, store)

    for c in range(NB):
        tokens = slice(BLK * c, BLK * (c + 1))
        za = _dot_nt(hn[tokens], wb_ref[za_rows, :])
        sz_ref[0, tokens, :] = (za * _sigmoid(za)).astype(bf16)


def _attn_kernel(r_ref, qT_ref, qnext_ref, kaug_ref, vT_ref, kmeta_ref, vmeta_ref, sz_ref, o_ref,
                 qaug_ref, qaug_next_ref, s_ref, p_ref, sm_ref, pm_ref, m_ref, mblk_ref, mmeta_ref, alpha_ref, acc_ref,
                 *, n_blocks):
    b = pl.program_id(0)
    t = pl.program_id(1)
    first = QB * t
    heads = range(N_HEADS)

    part_row = lax.broadcasted_iota(jnp.int32, (LANES - HEAD_DIM, BQ), 0)
    ones_rows = ((part_row % N_HEADS == 0) & (part_row < N_SPLIT * N_HEADS)).astype(bf16)
    for h in heads:
        for q_ref, aug_ref in ((qT_ref, qaug_ref), (qnext_ref, qaug_next_ref)):
            aug_ref[h, 0:HEAD_DIM, :] = q_ref[0, HEAD_DIM * h:HEAD_DIM * (h + 1), :]
            aug_ref[h, HEAD_DIM:LANES, :] = ones_rows

    def r_at(j, h):
        return r_ref[(b * n_blocks + j) * N_HEADS + h]

    def col_max(s):
        return jnp.max(s, axis=0, keepdims=True)

    key_index = functools.partial(lax.broadcasted_iota, jnp.int32, dimension=0)
    query_index = functools.partial(lax.broadcasted_iota, jnp.int32, dimension=1)

    def seen_by(diag):
        return slice(BLK * diag, BQ) if diag else slice(None)

    def score_item(j, slot, h, diag=None, meta=False, next_tile=False):
        q_ref = qaug_next_ref if next_tile else qaug_ref
        cols = seen_by(diag)
        s = _dot(kaug_ref[0, h, pl.ds(pl.multiple_of(j * BLK, BLK), BLK), :], q_ref[h, :, cols])
        if diag is not None:
            s = jnp.where(key_index(s.shape) <= query_index(s.shape), s, MASKED)
        s_ref[slot, h, :, cols] = s
        unseen = [jnp.full((1, BQ - s.shape[1]), MASKED, f32)] if s.shape[1] < BQ else []
        mblk_ref[slot, h] = jnp.concatenate(unseen + [col_max(s)], axis=1)
        if meta:
            sm = _dot(kmeta_ref[h], qaug_ref[h])
            sm_ref[h] = sm
            mmeta_ref[h] = col_max(sm)

    def softmax_item(j, slot, h, meta=False, diag=None):
        cols = seen_by(diag)
        assert not (meta and diag)
        ref = r_at(first, h)
        off = ref - r_at(j, h)
        m_old = m_ref[h]
        m_blk = mblk_ref[slot, h] + off
        if meta:
            m_blk = jnp.maximum(m_blk, mmeta_ref[h] + ref)
        m_new = jnp.maximum(m_old, m_blk)
        alpha_ref[h] = jnp.exp2(m_old - m_new)
        m_seen = m_new
        if diag:
            m_seen = jnp.maximum(m_ref[h, :, cols], mblk_ref[slot, h, :, cols] + off)
        m_ref[h] = m_new
        p_ref[h, :, cols] = jnp.exp2(s_ref[slot, h, :, cols] - (m_seen - off)).astype(bf16)
        if meta:
            pm_ref[h] = jnp.exp2(sm_ref[h] - (m_new - ref)).astype(bf16)

    def value_item(j, slot, h, meta=False, diag=None):
        cols = seen_by(diag)
        pv = _dot(vT_ref[0, j, h], p_ref[h, :, cols])
        if meta:
            pv = pv + _dot(vmeta_ref[h], pm_ref[h])
        acc_ref[h, :, cols] = acc_ref[h, :, cols] * alpha_ref[h, :, cols] + pv

    def run(score_blocks, finish_blocks):
        score_items = [(blk + (h,), kw) for blk, kw in score_blocks for h in heads]
        finish_items = [(blk + (h,), kw) for blk, kw in finish_blocks for h in heads]
        for i in range(max(len(score_items), len(finish_items) + SKEW + 1)):
            if i < len(score_items):
                args, kw = score_items[i]
                score_item(*args, **kw)
            if 0 <= i - SKEW - 1 < len(finish_items):
                args, kw = finish_items[i - SKEW - 1]
                value_item(*args, **kw)
            if 0 <= i - SKEW < len(finish_items):
                args, kw = finish_items[i - SKEW]
                softmax_item(*args, **kw)

    m_ref[...] = jnp.full(m_ref.shape, MASKED, f32)
    acc_ref[...] = jnp.zeros(acc_ref.shape, f32)

    @pl.when(first == 0)
    def _():
        run([((first, 0), dict(diag=0))], [])

    @pl.when(first > 0)
    def _():
        def body(k, carry):
            run([((2 * k + 1, 1), {}), ((2 * k + 2, 0), {})],
                [((2 * k, 0), {}), ((2 * k + 1, 1), {})])
            return carry

        lax.fori_loop(0, t - 1, body, 0)
        run([((first - 1, 1), {}), ((first, 0), dict(diag=0))],
            [((first - 2, 0), {}), ((first - 1, 1), {})])

    run([((first + 1, 1), dict(diag=1, meta=True)), ((0, 0), dict(next_tile=True))],
        [((first, 0), dict(meta=True)), ((first + 1, 1), dict(diag=1))])

    for pair in range(N_HEADS // 2):
        halves = []
        for h in (2 * pair, 2 * pair + 1):
            a = acc_ref[h]
            halves.append(a[0:HEAD_DIM] * (1.0 / a[HEAD_DIM:HEAD_DIM + 1]))
        cols = slice(LANES * pair, LANES * (pair + 1))
        o_ref[0, :, cols] = (jnp.concatenate(halves, axis=0).T * sz_ref[0, :, cols].astype(f32)).astype(bf16)


def _out_kernel(x_ref, y_ref, ga_ref, mp_ref, wua32_ref, wout32_ref, g_ref, out_ref, wua_ref, wout_ref):
    @pl.when((pl.program_id(0) == 0) & (pl.program_id(1) == 0))
    def _():
        wua_ref[...] = wua32_ref[...].astype(bf16)
        wout_ref[...] = wout32_ref[...].astype(bf16)

    def merged(rows):
        up = _dot(y_ref[0, rows, :], wua_ref[...])
        return (mp_ref[0, rows, :].astype(f32) + ga_ref[0, rows, :].astype(f32) * up).astype(bf16)

    chunks = [slice(OUT_CHUNK * i, OUT_CHUNK * (i + 1)) for i in range(TM_OUT // OUT_CHUNK)]
    pending = merged(chunks[0])
    for i, rows in enumerate(chunks):
        ahead = merged(chunks[i + 1]) if i + 1 < len(chunks) else None
        h_out = x_ref[0, rows, :] + _dot(pending, wout_ref[...])
        out_ref[0, rows, :] = _rmsnorm(h_out, g_ref[...])
        pending = ahead


def _const(shape):
    return pl.BlockSpec(shape, lambda *_: (0,) * len(shape), pipeline_mode=pl.Buffered(1))


def kernel(x, meta_tokens, norm_g, w_in, b_forget, pool_w, pool_scale, w_up_pool, w_up_attn, w_out, final_norm_g):
    batch, seq, _ = x.shape
    n_tiles = seq // TM
    n_blocks = seq // BLK

    w_t = jnp.transpose(w_in[0])
    b_f = b_forget.reshape(N_HEADS)
    g_in = norm_g[0].reshape(1, D_MODEL)
    g_out = final_norm_g.reshape(1, D_MODEL)
    pw = pool_w[0]
    ps = pool_scale[0].reshape(1, POOL_WIDTH)
    w_upp, w_upa, w_o = w_up_pool[0], w_up_attn[0], w_out[0]

    u_meta, k_meta, v_meta, r0 = pl.pallas_call(
        _meta_kernel,
        out_shape=(jax.ShapeDtypeStruct((N_META, POOL_WIDTH), f32),
                   jax.ShapeDtypeStruct((N_HEADS, N_META, LANES), bf16),
                   jax.ShapeDtypeStruct((N_HEADS, V_ROWS, N_META), bf16),
                   jax.ShapeDtypeStruct((8, LANES), f32)),
        grid=(1,),
        in_specs=[_const((N_META, D_MODEL)), _const((1, D_MODEL))]
        + [pl.BlockSpec((rows, D_MODEL), functools.partial(lambda r, i: (r, 0), col // rows),
                        pipeline_mode=pl.Buffered(1))
           for col, rows in ((COL_U, IN_GROUP), (COL_K, IN_GROUP), (COL_V, IN_GROUP), (COL_F, N_HEADS))]
        + [pl.BlockSpec(memory_space=pltpu.SMEM), _const((len(POOL_WINDOWS), POOL_GROUP, POOL_GROUP))],
        out_specs=[_const((N_META, POOL_WIDTH)), _const((N_HEADS, N_META, LANES)),
                   _const((N_HEADS, V_ROWS, N_META)), _const((8, LANES))],
        compiler_params=pltpu.CompilerParams(vmem_limit_bytes=VMEM_LIMIT),
        name="meta_proj",
    )(meta_tokens, g_in, w_t, w_t, w_t, w_t, b_f, pw)

    mp, ga, sz, qT, vT, kaug, r_tiles = pl.pallas_call(
        _proj_kernel,
        grid=(batch, n_tiles),
        in_specs=[
            pl.BlockSpec((1, TM, D_MODEL), lambda b, t: (b, t, 0)),
            _const((1, D_MODEL)),
            _const((N_IN, D_MODEL)),
            pl.BlockSpec(memory_space=pltpu.SMEM),
            _const((len(POOL_WINDOWS), POOL_GROUP, POOL_GROUP)),
            _const((1, POOL_WIDTH)),
            _const((POOL_WIDTH, D_MODEL)),
            _const((N_META, POOL_WIDTH)),
            _const((8, LANES)),
        ],
        out_specs=[
            pl.BlockSpec((1, TM, D_MODEL), lambda b, t: (b, t, 0)),
            pl.BlockSpec((1, TM, D_MODEL), lambda b, t: (b, t, 0)),
            pl.BlockSpec((1, TM, ATTN_WIDTH), lambda b, t: (b, t, 0)),
            pl.BlockSpec((1, ATTN_WIDTH, TM), lambda b, t: (b, 0, t)),
            pl.BlockSpec((1, NB, N_HEADS, V_ROWS, BLK), lambda b, t: (b, t, 0, 0, 0)),
            pl.BlockSpec((1, N_HEADS, TM, LANES), lambda b, t: (b, 0, t, 0)),
            pl.BlockSpec((1, 1, 8, LANES), lambda b, t: (b, t, 0, 0)),
        ],
        out_shape=(
            jax.ShapeDtypeStruct((batch, seq, D_MODEL), bf16),
            jax.ShapeDtypeStruct((batch, seq, D_MODEL), bf16),
            jax.ShapeDtypeStruct((batch, seq, ATTN_WIDTH), bf16),
            jax.ShapeDtypeStruct((batch, ATTN_WIDTH, seq), bf16),
            jax.ShapeDtypeStruct((batch, n_blocks, N_HEADS, V_ROWS, BLK), bf16),
            jax.ShapeDtypeStruct((batch, N_HEADS, seq, LANES), bf16),
            jax.ShapeDtypeStruct((batch, n_tiles, 8, LANES), f32),
        ),
        scratch_shapes=[pltpu.VMEM((POOL_PAD + N_META + TM, POOL_WIDTH), f32),
                        pltpu.VMEM((POOL_PAD + N_META + TM, 3 * POOL_GROUP), f32),
                        pltpu.VMEM((POOL_PAD + N_META + TM, 2 * POOL_GROUP), f32),
                        pltpu.VMEM((8, LANES), f32),
                        pltpu.VMEM((5 * IN_GROUP + F_ROWS, D_MODEL), bf16),
                        pltpu.VMEM((2 * D_MODEL, D_MODEL), bf16),
                        pltpu.VMEM((POOL_WIDTH, D_MODEL), bf16),
                        pltpu.VMEM((POOL_WIDTH, D_MODEL), bf16)],
        compiler_params=pltpu.CompilerParams(dimension_semantics=("arbitrary", "arbitrary"),
                                             vmem_limit_bytes=VMEM_LIMIT),
        name="in_proj",
    )(x, g_in, w_t, b_f, pw, ps, w_upp, u_meta, r0)

    r_blocks = r_tiles[:, :, :NB, :N_HEADS].reshape(batch * n_blocks * N_HEADS)

    y_attn = pl.pallas_call(
        functools.partial(_attn_kernel, n_blocks=n_blocks),
        grid=(batch, seq // BQ),
        in_specs=[
            pl.BlockSpec(memory_space=pltpu.SMEM),
            pl.BlockSpec((1, ATTN_WIDTH, BQ), lambda b, t: (b, 0, t)),
            pl.BlockSpec((1, ATTN_WIDTH, BQ), lambda b, t: (b, 0, jnp.minimum(t + 1, seq // BQ - 1))),
            pl.BlockSpec((1, N_HEADS, seq, LANES), lambda b, t: (b, 0, 0, 0)),
            pl.BlockSpec((1, n_blocks, N_HEADS, V_ROWS, BLK), lambda b, t: (b, 0, 0, 0, 0)),
            _const((N_HEADS, N_META, LANES)),
            _const((N_HEADS, V_ROWS, N_META)),
            pl.BlockSpec((1, BQ, ATTN_WIDTH), lambda b, t: (b, t, 0)),
        ],
        out_specs=pl.BlockSpec((1, BQ, ATTN_WIDTH), lambda b, t: (b, t, 0)),
        out_shape=jax.ShapeDtypeStruct((batch, seq, ATTN_WIDTH), bf16),
        scratch_shapes=[pltpu.VMEM((N_HEADS, LANES, BQ), bf16),
                        pltpu.VMEM((N_HEADS, LANES, BQ), bf16),
                        pltpu.VMEM((2, N_HEADS, BLK, BQ), f32),
                        pltpu.VMEM((N_HEADS, BLK, BQ), bf16),
                        pltpu.VMEM((N_HEADS, N_META, BQ), f32),
                        pltpu.VMEM((N_HEADS, N_META, BQ), bf16),
                        pltpu.VMEM((N_HEADS, 1, BQ), f32),
                        pltpu.VMEM((2, N_HEADS, 1, BQ), f32),
                        pltpu.VMEM((N_HEADS, 1, BQ), f32),
                        pltpu.VMEM((N_HEADS, 1, BQ), f32),
                        pltpu.VMEM((N_HEADS, V_ROWS, BQ), f32)],
        compiler_params=pltpu.CompilerParams(dimension_semantics=("arbitrary", "arbitrary"),
                                             vmem_limit_bytes=VMEM_LIMIT),
        name="attention",
    )(r_blocks, qT, qT, kaug, vT, k_meta, v_meta, sz)

    return pl.pallas_call(
        _out_kernel,
        grid=(batch, seq // TM_OUT),
        in_specs=[
            pl.BlockSpec((1, TM_OUT, D_MODEL), lambda b, t: (b, t, 0)),
            pl.BlockSpec((1, TM_OUT, ATTN_WIDTH), lambda b, t: (b, t, 0)),
            pl.BlockSpec((1, TM_OUT, D_MODEL), lambda b, t: (b, t, 0)),
            pl.BlockSpec((1, TM_OUT, D_MODEL), lambda b, t: (b, t, 0)),
            _const((ATTN_WIDTH, D_MODEL)),
            _const((D_MODEL, D_MODEL)),
            _const((1, D_MODEL)),
        ],
        out_specs=pl.BlockSpec((1, TM_OUT, D_MODEL), lambda b, t: (b, t, 0)),
        out_shape=jax.ShapeDtypeStruct((batch, seq, D_MODEL), f32),
        scratch_shapes=[pltpu.VMEM((ATTN_WIDTH, D_MODEL), bf16),
                        pltpu.VMEM((D_MODEL, D_MODEL), bf16)],
        compiler_params=pltpu.CompilerParams(dimension_semantics=("arbitrary", "arbitrary"),
                                             vmem_limit_bytes=VMEM_LIMIT),
        name="out_proj",
    )(x, y_attn, ga, mp, w_upa, w_o, g_out)
```

```python
import functools

import jax
import jax.numpy as jnp
from jax import lax
from jax.experimental import pallas as pl
from jax.experimental.pallas import tpu as pltpu

D_MODEL = 1024
N_META = 16
POOL_WIDTH = 512
POOL_WINDOWS = (2, 4, 8, 16)
POOL_GROUP = POOL_WIDTH // len(POOL_WINDOWS)
N_HEADS = 8
HEAD_DIM = 64
ATTN_WIDTH = N_HEADS * HEAD_DIM
RMS_EPS = 1e-6

LANES = 128
BLK = 256
BQ = 512
QB = BQ // BLK
assert QB == 2
TM = 512
NB = TM // BLK
TM_OUT = 1024
OUT_CHUNK = 256
OUT_BUFFERS = 3
POOL_PAD = 8
assert POOL_WINDOWS == (2, 4, 8, 16) and POOL_PAD >= POOL_WINDOWS[-1] // 2
V_ROWS = HEAD_DIM + 16
N_SPLIT = 3
MASKED = -1e30
SKEW = 2
LOG2E = 1.4426950408889634
VMEM_LIMIT = 56 * 1024 * 1024
assert POOL_WIDTH == ATTN_WIDTH
IN_GROUP = POOL_WIDTH
COL_U, COL_ZP, COL_Q, COL_K, COL_V, COL_ZA, COL_F = (IN_GROUP * i for i in range(7))
COL_G = COL_F + N_HEADS
F_ROWS = 16
N_IN = COL_G + 2 * D_MODEL

f32 = jnp.float32
bf16 = jnp.bfloat16


def _rmsnorm(x, g):
    return x * lax.rsqrt(jnp.mean(x * x, axis=-1, keepdims=True) + RMS_EPS) * g


def _sigmoid(x):
    return 0.5 * jnp.tanh(0.5 * x) + 0.5


def _log_sigmoid(x):
    return jnp.minimum(x, 0.0) - jnp.log1p(jnp.exp(-jnp.abs(x)))


def _dot(a, b):
    return jnp.dot(a, b, preferred_element_type=f32)


def _dot_nt(a, b):
    return lax.dot_general(a, b, (((1,), (1,)), ((), ())), preferred_element_type=f32)


def _lane_iota(shape):
    return lax.broadcasted_iota(jnp.int32, shape, 1)


def _decay_parts(logf3, rows):
    n = logf3.shape[0] // rows
    r = lax.broadcasted_iota(jnp.int32, (rows, rows), 0)
    c = lax.broadcasted_iota(jnp.int32, (rows, rows), 1)
    tri = (c <= r).astype(f32)
    side_by_side = jnp.concatenate([logf3[rows * i:rows * (i + 1)] for i in range(n)], axis=1)
    sums = jnp.dot(tri, side_by_side, precision=lax.Precision.HIGHEST, preferred_element_type=f32)
    beta = jnp.concatenate([sums[:, LANES * i:LANES * (i + 1)] for i in range(n)], axis=0)
    return beta, _split_decay(beta)


def _decay_parts_from_rows(logf_t, rows):
    n = logf_t.shape[1] // rows
    k = lax.broadcasted_iota(jnp.int32, (rows, rows), 0)
    j = lax.broadcasted_iota(jnp.int32, (rows, rows), 1)
    stacked = jnp.concatenate([logf_t[:, rows * i:rows * (i + 1)] for i in range(n)], axis=0)
    sums_t = jnp.dot(stacked, (k <= j).astype(f32), precision=lax.Precision.HIGHEST, preferred_element_type=f32)
    unused = jnp.zeros((LANES - N_SPLIT * N_HEADS, rows), f32)
    beta = jnp.concatenate(
        [jnp.concatenate([sums_t[N_HEADS * i:N_HEADS * (i + 1)]] * N_SPLIT + [unused], axis=0).T for i in range(n)],
        axis=0)
    return beta, _split_decay(beta)


def _split_decay(beta):
    nb = beta * (-LOG2E)
    hi = nb.astype(bf16).astype(f32)
    mid = (nb - hi).astype(bf16).astype(f32)
    lo = (nb - hi - mid).astype(bf16).astype(f32)
    lane = _lane_iota(nb.shape)
    return jnp.where(lane < N_HEADS, hi, jnp.where(lane < 2 * N_HEADS, mid, lo))


def _augmented_keys(kproj, parts, store):
    lane = _lane_iota((kproj.shape[0], LANES))
    for h in range(N_HEADS):
        slab = kproj[:, LANES * (h // 2):LANES * (h // 2 + 1)]
        if h % 2:
            slab = pltpu.roll(slab, HEAD_DIM, axis=1)
        store(h, jnp.where(lane < HEAD_DIM, slab, pltpu.roll(parts, HEAD_DIM - h, axis=1)).astype(bf16))


def _pool_mapped(wu_t, pw_ref):
    return jnp.concatenate([_dot(pw_ref[g].T.astype(bf16), wu_t[POOL_GROUP * g:POOL_GROUP * (g + 1)])
                            for g in range(len(POOL_WINDOWS))], axis=0).astype(bf16)


def _forget_rows(wf_t):
    unused = jnp.zeros((LANES - N_SPLIT * N_HEADS, wf_t.shape[1]), f32)
    return jnp.concatenate([wf_t] * N_SPLIT + [unused], axis=0).astype(bf16)


def _forget_bias(b_ref, shape, axis):
    index = lax.broadcasted_iota(jnp.int32, shape, axis)
    bias = jnp.zeros(shape, f32)
    for h in range(N_HEADS):
        bias = jnp.where((index % N_HEADS == h) & (index < N_SPLIT * N_HEADS), b_ref[h], bias)
    return bias


def _meta_kernel(meta_ref, g_ref, wu_ref, wk_ref, wv_ref, wf_ref, bf_ref, pw_ref,
                 umeta_ref, kmeta_ref, vmeta_ref, r0_ref):
    hn = _rmsnorm(meta_ref[...], g_ref[...]).astype(bf16)
    umeta_ref[...] = _dot_nt(hn, _pool_mapped(wu_ref[...].astype(bf16), pw_ref))
    vt = _dot_nt(wv_ref[...].astype(bf16), hn)
    for h in range(N_HEADS):
        vmeta_ref[h, 0:HEAD_DIM, :] = vt[HEAD_DIM * h:HEAD_DIM * (h + 1), :].astype(bf16)
        vmeta_ref[h, HEAD_DIM:V_ROWS, :] = jnp.ones((V_ROWS - HEAD_DIM, N_META), bf16)
    logf3 = _log_sigmoid(_dot_nt(hn, _forget_rows(wf_ref[...])) + _forget_bias(bf_ref, (1, LANES), 1))
    beta, parts = _decay_parts(logf3, N_META)

    def store(h, ka):
        kmeta_ref[h] = ka

    _augmented_keys(_dot_nt(hn, wk_ref[...].astype(bf16)), parts, store)
    r0_ref[...] = jnp.broadcast_to(beta[N_META - 1:N_META, :], r0_ref.shape)


def _proj_kernel(x_ref, g_ref, wt_ref, bf_ref, pw_ref, ps_ref, wup32_ref,
                 umeta_ref, r0_ref,
                 mp_ref, ga_ref, sz_ref, qT_ref, vT_ref, kaug_ref, r_ref,
                 uext_ref, lvl_a_ref, lvl_b_ref, rcarry_ref, wb_ref, wg_ref, wu_ref, wup_ref):
    t = pl.program_id(1)
    zp_rows, k_rows, za_rows, q_rows, v_rows = (slice(IN_GROUP * i, IN_GROUP * (i + 1)) for i in range(5))
    f_rows = slice(v_rows.stop, v_rows.stop + F_ROWS)

    @pl.when((pl.program_id(0) == 0) & (t == 0))
    def _():
        for rows, col in ((zp_rows, COL_ZP), (k_rows, COL_K), (za_rows, COL_ZA), (q_rows, COL_Q), (v_rows, COL_V)):
            wb_ref[rows, :] = wt_ref[col:col + IN_GROUP, :].astype(bf16)
        wb_ref[f_rows, :] = wt_ref[COL_F:COL_F + F_ROWS, :].astype(bf16)
        wg_ref[...] = wt_ref[COL_G:N_IN, :].astype(bf16)
        wu_ref[...] = _pool_mapped(wt_ref[COL_U:COL_ZP, :].astype(bf16), pw_ref)
        wup_ref[...] = wup32_ref[...].astype(bf16)
    ext = N_META + TM
    body = slice(POOL_PAD, POOL_PAD + ext)

    @pl.when(t == 0)
    def _():
        uext_ref[0:POOL_PAD, :] = jnp.zeros((POOL_PAD, POOL_WIDTH), f32)
        lvl_a_ref[0:POOL_PAD, :] = jnp.zeros((POOL_PAD, lvl_a_ref.shape[1]), f32)
        lvl_b_ref[0:POOL_PAD, :] = jnp.zeros((POOL_PAD, lvl_b_ref.shape[1]), f32)
        uext_ref[POOL_PAD:POOL_PAD + N_META, :] = umeta_ref[...]
        rcarry_ref[...] = r0_ref[...]

    hn = _rmsnorm(x_ref[0], g_ref[...]).astype(bf16)

    qv = _dot_nt(wb_ref[q_rows.start:, :], hn)
    qT_ref[0] = (qv[0:ATTN_WIDTH] * (HEAD_DIM ** -0.5 * LOG2E)).astype(bf16)
    for c in range(NB):
        for h in range(N_HEADS):
            rows = slice(ATTN_WIDTH + HEAD_DIM * h, ATTN_WIDTH + HEAD_DIM * (h + 1))
            vT_ref[0, c, h, 0:HEAD_DIM, :] = qv[rows, BLK * c:BLK * (c + 1)].astype(bf16)
            vT_ref[0, c, h, HEAD_DIM:V_ROWS, :] = jnp.ones((V_ROWS - HEAD_DIM, BLK), bf16)

    gat =_dot_nt(hn, wg_ref[D_MODEL:2 * D_MODEL, :])
    ga_ref[0] = _sigmoid(gat).astype(bf16)

    u = _dot_nt(hn, wu_ref[...])
    zp = _dot_nt(hn, wb_ref[zp_rows, :])
    uext_ref[POOL_PAD + N_META:POOL_PAD + ext, :] = u

    def shifted_sum(ref, shift, cols):
        return ref[body, cols] + ref[POOL_PAD - shift:POOL_PAD - shift + ext, cols]

    g1, g2, g3 = (slice(POOL_GROUP * g, POOL_GROUP * (g + 1)) for g in range(3))
    sums = [shifted_sum(uext_ref, 1, g1)]
    lvl_a_ref[body, :] = shifted_sum(uext_ref, 1, slice(POOL_GROUP, POOL_WIDTH))
    sums.append(shifted_sum(lvl_a_ref, 2, g1))
    lvl_b_ref[body, :] = shifted_sum(lvl_a_ref, 2, slice(POOL_GROUP, 3 * POOL_GROUP))
    sums.append(shifted_sum(lvl_b_ref, 4, g1))
    lvl_a_ref[body, g1] = shifted_sum(lvl_b_ref, 4, g2)
    sums.append(shifted_sum(lvl_a_ref, 8, g1))
    uext_ref[POOL_PAD:POOL_PAD + N_META, :] = uext_ref[POOL_PAD + TM:POOL_PAD + ext, :]

    pooled = jnp.concatenate([sums[g][N_META:] * (1.0 / w) for g, w in enumerate(POOL_WINDOWS)], axis=1) - u
    y_pool = (pooled * ps_ref[...] * (zp * _sigmoid(zp))).astype(bf16)
    gp = _dot_nt(hn, wg_ref[0:D_MODEL, :])
    mp_ref[0] = (_sigmoid(gp) * _dot(y_pool, wup_ref[...])).astype(bf16)

    logf_t = _log_sigmoid(qv[2 * IN_GROUP:2 * IN_GROUP + N_HEADS] + _forget_bias(bf_ref, (N_HEADS, TM), 0))
    r_ref[...] = jnp.zeros(r_ref.shape, f32)
    beta, parts = _decay_parts_from_rows(logf_t, BLK)
    for c in range(NB):
        r_ref[0, 0, c:c + 1, :] = rcarry_ref[0:1, :] * LOG2E
        rcarry_ref[...] = rcarry_ref[...] + beta[BLK * (c + 1) - 1:BLK * (c + 1), :]

    def store(h, ka):
        kaug_ref[0, h] = ka

    _augmented_keys(_dot_nt(hn, wb_ref[k_rows, :]), parts, store)

    za = _dot_nt(hn, wb_ref[za_rows, :])
    sz_ref[0] = (za * _sigmoid(za)).astype(bf16)


def _attn_kernel(r_ref, qT_ref, qnext_ref, kaug_ref, vT_ref, kmeta_ref, vmeta_ref, sz_ref, o_ref,
                 qaug_ref, qaug_next_ref, s_ref, p_ref, sm_ref, pm_ref, m_ref, mblk_ref, mmeta_ref, alpha_ref, acc_ref,
                 *, n_blocks):
    b = pl.program_id(0)
    t = pl.program_id(1)
    first = QB * t
    heads = range(N_HEADS)

    part_row = lax.broadcasted_iota(jnp.int32, (LANES - HEAD_DIM, BQ), 0)
    ones_rows = ((part_row % N_HEADS == 0) & (part_row < N_SPLIT * N_HEADS)).astype(bf16)
    for h in heads:
        for q_ref, aug_ref in ((qT_ref, qaug_ref), (qnext_ref, qaug_next_ref)):
            aug_ref[h, 0:HEAD_DIM, :] = q_ref[0, HEAD_DIM * h:HEAD_DIM * (h + 1), :]
            aug_ref[h, HEAD_DIM:LANES, :] = ones_rows

    def r_at(j, h):
        return r_ref[(b * n_blocks + j) * N_HEADS + h]

    def col_max(s):
        return jnp.max(s, axis=0, keepdims=True)

    key_index = functools.partial(lax.broadcasted_iota, jnp.int32, dimension=0)
    query_index = functools.partial(lax.broadcasted_iota, jnp.int32, dimension=1)

    def seen_by(diag):
        return slice(BLK * diag, BQ) if diag else slice(None)

    def score_item(j, slot, h, diag=None, meta=False, next_tile=False):
        q_ref = qaug_next_ref if next_tile else qaug_ref
        cols = seen_by(diag)
        s = _dot(kaug_ref[0, h, pl.ds(pl.multiple_of(j * BLK, BLK), BLK), :], q_ref[h, :, cols])
        if diag is not None:
            s = jnp.where(key_index(s.shape) <= query_index(s.shape), s, MASKED)
        s_ref[slot, h, :, cols] = s
        unseen = [jnp.full((1, BQ - s.shape[1]), MASKED, f32)] if s.shape[1] < BQ else []
        mblk_ref[slot, h] = jnp.concatenate(unseen + [col_max(s)], axis=1)
        if meta:
            sm = _dot(kmeta_ref[h], qaug_ref[h])
            sm_ref[h] = sm
            mmeta_ref[h] = col_max(sm)

    def softmax_item(j, slot, h, meta=False, diag=None):
        cols = seen_by(diag)
        assert not (meta and diag)
        ref = r_at(first, h)
        off = ref - r_at(j, h)
        m_old = m_ref[h]
        m_blk = mblk_ref[slot, h] + off
        if meta:
            m_blk = jnp.maximum(m_blk, mmeta_ref[h] + ref)
        m_new = jnp.maximum(m_old, m_blk)
        alpha_ref[h] = jnp.exp2(m_old - m_new)
        m_seen = m_new
        if diag:
            m_seen = jnp.maximum(m_ref[h, :, cols], mblk_ref[slot, h, :, cols] + off)
        m_ref[h] = m_new
        p_ref[h, :, cols] = jnp.exp2(s_ref[slot, h, :, cols] - (m_seen - off)).astype(bf16)
        if meta:
            pm_ref[h] = jnp.exp2(sm_ref[h] - (m_new - ref)).astype(bf16)

    def value_item(j, slot, h, meta=False, diag=None):
        cols = seen_by(diag)
        pv = _dot(vT_ref[0, j, h], p_ref[h, :, cols])
        if meta:
            pv = pv + _dot(vmeta_ref[h], pm_ref[h])
        acc_ref[h, :, cols] = acc_ref[h, :, cols] * alpha_ref[h, :, cols] + pv

    def run(score_blocks, finish_blocks):
        score_items = [(blk + (h,), kw) for blk, kw in score_blocks for h in heads]
        finish_items = [(blk + (h,), kw) for blk, kw in finish_blocks for h in heads]
        for i in range(max(len(score_items), len(finish_items) + SKEW + 1)):
            if i < len(score_items):
                args, kw = score_items[i]
                score_item(*args, **kw)
            if 0 <= i - SKEW - 1 < len(finish_items):
                args, kw = finish_items[i - SKEW - 1]
                value_item(*args, **kw)
            if 0 <= i - SKEW < len(finish_items):
                args, kw = finish_items[i - SKEW]
                softmax_item(*args, **kw)

    m_ref[...] = jnp.full(m_ref.shape, MASKED, f32)
    acc_ref[...] = jnp.zeros(acc_ref.shape, f32)

    @pl.when(first == 0)
    def _():
        run([((first, 0), dict(diag=0))], [])

    @pl.when(first > 0)
    def _():
        def body(k, carry):
            run([((2 * k + 1, 1), {}), ((2 * k + 2, 0), {})],
                [((2 * k, 0), {}), ((2 * k + 1, 1), {})])
            return carry

        lax.fori_loop(0, t - 1, body, 0)
        run([((first - 1, 1), {}), ((first, 0), dict(diag=0))],
            [((first - 2, 0), {}), ((first - 1, 1), {})])

    run([((first + 1, 1), dict(diag=1, meta=True)), ((0, 0), dict(next_tile=True))],
        [((first, 0), dict(meta=True)), ((first + 1, 1), dict(diag=1))])

    for pair in range(N_HEADS // 2):
        halves = []
        for h in (2 * pair, 2 * pair + 1):
            a = acc_ref[h]
            halves.append(a[0:HEAD_DIM] * (1.0 / a[HEAD_DIM:HEAD_DIM + 1]))
        cols = slice(LANES * pair, LANES * (pair + 1))
        o_ref[0, :, cols] = (jnp.concatenate(halves, axis=0).T * sz_ref[0, :, cols].astype(f32)).astype(bf16)


def _out_kernel(x_hbm, y_hbm, ga_hbm, mp_hbm, wua32_ref, wout32_ref, g_ref, out_hbm, wua_ref, wout_ref,
                *, batch, seq):
    wua_ref[...] = wua32_ref[...].astype(bf16)
    wout_ref[...] = wout32_ref[...].astype(bf16)

    def tile(x_ref, y_ref, ga_ref, mp_ref, out_ref):
        def merged(rows):
            up = _dot(y_ref[0, rows, :], wua_ref[...])
            return (mp_ref[0, rows, :].astype(f32) + ga_ref[0, rows, :].astype(f32) * up).astype(bf16)

        chunks = [slice(OUT_CHUNK * i, OUT_CHUNK * (i + 1)) for i in range(TM_OUT // OUT_CHUNK)]
        pending = merged(chunks[0])
        for i, rows in enumerate(chunks):
            ahead = merged(chunks[i + 1]) if i + 1 < len(chunks) else None
            h_out = x_ref[0, rows, :] + _dot(pending, wout_ref[...])
            out_ref[0, rows, :] = _rmsnorm(h_out, g_ref[...])
            pending = ahead

    def rows_of(width, **kw):
        return pl.BlockSpec((1, TM_OUT, width), lambda b, t: (b, t, 0), **kw)

    streamed = dict(pipeline_mode=pl.Buffered(OUT_BUFFERS))
    pltpu.emit_pipeline(
        tile,
        grid=(batch, seq // TM_OUT),
        in_specs=[rows_of(D_MODEL, **streamed), rows_of(ATTN_WIDTH, **streamed),
                  rows_of(D_MODEL, **streamed), rows_of(D_MODEL, **streamed)],
        out_specs=[rows_of(D_MODEL)],
    )(x_hbm, y_hbm, ga_hbm, mp_hbm, out_hbm)


def _const(shape):
    return pl.BlockSpec(shape, lambda *_: (0,) * len(shape), pipeline_mode=pl.Buffered(1))


def kernel(x, meta_tokens, norm_g, w_in, b_forget, pool_w, pool_scale, w_up_pool, w_up_attn, w_out, final_norm_g):
    batch, seq, _ = x.shape
    n_tiles = seq // TM
    n_blocks = seq // BLK

    w_t = jnp.transpose(w_in[0])
    b_f = b_forget.reshape(N_HEADS)
    g_in = norm_g[0].reshape(1, D_MODEL)
    g_out = final_norm_g.reshape(1, D_MODEL)
    pw = pool_w[0]
    ps = pool_scale[0].reshape(1, POOL_WIDTH)
    w_upp, w_upa, w_o = w_up_pool[0], w_up_attn[0], w_out[0]

    u_meta, k_meta, v_meta, r0 = pl.pallas_call(
        _meta_kernel,
        out_shape=(jax.ShapeDtypeStruct((N_META, POOL_WIDTH), f32),
                   jax.ShapeDtypeStruct((N_HEADS, N_META, LANES), bf16),
                   jax.ShapeDtypeStruct((N_HEADS, V_ROWS, N_META), bf16),
                   jax.ShapeDtypeStruct((8, LANES), f32)),
        grid=(1,),
        in_specs=[_const((N_META, D_MODEL)), _const((1, D_MODEL))]
        + [pl.BlockSpec((rows, D_MODEL), functools.partial(lambda r, i: (r, 0), col // rows),
                        pipeline_mode=pl.Buffered(1))
           for col, rows in ((COL_U, IN_GROUP), (COL_K, IN_GROUP), (COL_V, IN_GROUP), (COL_F, N_HEADS))]
        + [pl.BlockSpec(memory_space=pltpu.SMEM), _const((len(POOL_WINDOWS), POOL_GROUP, POOL_GROUP))],
        out_specs=[_const((N_META, POOL_WIDTH)), _const((N_HEADS, N_META, LANES)),
                   _const((N_HEADS, V_ROWS, N_META)), _const((8, LANES))],
        compiler_params=pltpu.CompilerParams(vmem_limit_bytes=VMEM_LIMIT),
        name="meta_proj",
    )(meta_tokens, g_in, w_t, w_t, w_t, w_t, b_f, pw)

    mp, ga, sz, qT, vT, kaug, r_tiles = pl.pallas_call(
        _proj_kernel,
        grid=(batch, n_tiles),
        in_specs=[
            pl.BlockSpec((1, TM, D_MODEL), lambda b, t: (b, t, 0)),
            _const((1, D_MODEL)),
            _const((N_IN, D_MODEL)),
            pl.BlockSpec(memory_space=pltpu.SMEM),
            _const((len(POOL_WINDOWS), POOL_GROUP, POOL_GROUP)),
            _const((1, POOL_WIDTH)),
            _const((POOL_WIDTH, D_MODEL)),
            _const((N_META, POOL_WIDTH)),
            _const((8, LANES)),
        ],
        out_specs=[
            pl.BlockSpec((1, TM, D_MODEL), lambda b, t: (b, t, 0)),
            pl.BlockSpec((1, TM, D_MODEL), lambda b, t: (b, t, 0)),
            pl.BlockSpec((1, TM, ATTN_WIDTH), lambda b, t: (b, t, 0)),
            pl.BlockSpec((1, ATTN_WIDTH, TM), lambda b, t: (b, 0, t)),
            pl.BlockSpec((1, NB, N_HEADS, V_ROWS, BLK), lambda b, t: (b, t, 0, 0, 0)),
            pl.BlockSpec((1, N_HEADS, TM, LANES), lambda b, t: (b, 0, t, 0)),
            pl.BlockSpec((1, 1, 8, LANES), lambda b, t: (b, t, 0, 0)),
        ],
        out_shape=(
            jax.ShapeDtypeStruct((batch, seq, D_MODEL), bf16),
            jax.ShapeDtypeStruct((batch, seq, D_MODEL), bf16),
            jax.ShapeDtypeStruct((batch, seq, ATTN_WIDTH), bf16),
            jax.ShapeDtypeStruct((batch, ATTN_WIDTH, seq), bf16),
            jax.ShapeDtypeStruct((batch, n_blocks, N_HEADS, V_ROWS, BLK), bf16),
            jax.ShapeDtypeStruct((batch, N_HEADS, seq, LANES), bf16),
            jax.ShapeDtypeStruct((batch, n_tiles, 8, LANES), f32),
        ),
        scratch_shapes=[pltpu.VMEM((POOL_PAD + N_META + TM, POOL_WIDTH), f32),
                        pltpu.VMEM((POOL_PAD + N_META + TM, 3 * POOL_GROUP), f32),
                        pltpu.VMEM((POOL_PAD + N_META + TM, 2 * POOL_GROUP), f32),
                        pltpu.VMEM((8, LANES), f32),
                        pltpu.VMEM((5 * IN_GROUP + F_ROWS, D_MODEL), bf16),
                        pltpu.VMEM((2 * D_MODEL, D_MODEL), bf16),
                        pltpu.VMEM((POOL_WIDTH, D_MODEL), bf16),
                        pltpu.VMEM((POOL_WIDTH, D_MODEL), bf16)],
        compiler_params=pltpu.CompilerParams(dimension_semantics=("arbitrary", "arbitrary"),
                                             vmem_limit_bytes=VMEM_LIMIT),
        name="in_proj",
    )(x, g_in, w_t, b_f, pw, ps, w_upp, u_meta, r0)

    r_blocks = r_tiles[:, :, :NB, :N_HEADS].reshape(batch * n_blocks * N_HEADS)

    y_attn = pl.pallas_call(
        functools.partial(_attn_kernel, n_blocks=n_blocks),
        grid=(batch, seq // BQ),
        in_specs=[
            pl.BlockSpec(memory_space=pltpu.SMEM),
            pl.BlockSpec((1, ATTN_WIDTH, BQ), lambda b, t: (b, 0, t)),
            pl.BlockSpec((1, ATTN_WIDTH, BQ), lambda b, t: (b, 0, jnp.minimum(t + 1, seq // BQ - 1))),
            pl.BlockSpec((1, N_HEADS, seq, LANES), lambda b, t: (b, 0, 0, 0)),
            pl.BlockSpec((1, n_blocks, N_HEADS, V_ROWS, BLK), lambda b, t: (b, 0, 0, 0, 0)),
            _const((N_HEADS, N_META, LANES)),
            _const((N_HEADS, V_ROWS, N_META)),
            pl.BlockSpec((1, BQ, ATTN_WIDTH), lambda b, t: (b, t, 0)),
        ],
        out_specs=pl.BlockSpec((1, BQ, ATTN_WIDTH), lambda b, t: (b, t, 0)),
        out_shape=jax.ShapeDtypeStruct((batch, seq, ATTN_WIDTH), bf16),
        scratch_shapes=[pltpu.VMEM((N_HEADS, LANES, BQ), bf16),
                        pltpu.VMEM((N_HEADS, LANES, BQ), bf16),
                        pltpu.VMEM((2, N_HEADS, BLK, BQ), f32),
                        pltpu.VMEM((N_HEADS, BLK, BQ), bf16),
                        pltpu.VMEM((N_HEADS, N_META, BQ), f32),
                        pltpu.VMEM((N_HEADS, N_META, BQ), bf16),
                        pltpu.VMEM((N_HEADS, 1, BQ), f32),
                        pltpu.VMEM((2, N_HEADS, 1, BQ), f32),
                        pltpu.VMEM((N_HEADS, 1, BQ), f32),
                        pltpu.VMEM((N_HEADS, 1, BQ), f32),
                        pltpu.VMEM((N_HEADS, V_ROWS, BQ), f32)],
        compiler_params=pltpu.CompilerParams(dimension_semantics=("arbitrary", "arbitrary"),
                                             vmem_limit_bytes=VMEM_LIMIT),
        name="attention",
    )(r_blocks, qT, qT, kaug, vT, k_meta, v_meta, sz)

    in_hbm = pl.BlockSpec(memory_space=pl.ANY)
    in_vmem = pl.BlockSpec(memory_space=pltpu.VMEM)
    return pl.pallas_call(
        functools.partial(_out_kernel, batch=batch, seq=seq),
        in_specs=[in_hbm, in_hbm, in_hbm, in_hbm, in_vmem, in_vmem, in_vmem],
        out_specs=pl.BlockSpec(memory_space=pl.ANY),
        out_shape=jax.ShapeDtypeStruct((batch, seq, D_MODEL), f32),
        scratch_shapes=[pltpu.VMEM((ATTN_WIDTH, D_MODEL), bf16),
                        pltpu.VMEM((D_MODEL, D_MODEL), bf16)],
        compiler_params=pltpu.CompilerParams(vmem_limit_bytes=VMEM_LIMIT),
        name="out_proj",
    )(x, y_attn, ga, mp, w_upa, w_o, g_out)
```

```python
import functools

import jax
import jax.numpy as jnp
from jax import lax
from jax.experimental import pallas as pl
from jax.experimental.pallas import tpu as pltpu

D_MODEL = 1024
N_META = 16
POOL_WIDTH = 512
POOL_WINDOWS = (2, 4, 8, 16)
POOL_GROUP = POOL_WIDTH // len(POOL_WINDOWS)
N_HEADS = 8
HEAD_DIM = 64
ATTN_WIDTH = N_HEADS * HEAD_DIM
RMS_EPS = 1e-6

LANES = 128
BLK = 256
BQ = 512
QB = BQ // BLK
assert QB == 2
TM = 512
NB = TM // BLK
TM_OUT = 512
OUT_CHUNK = 256
OUT_BUFFERS = 4
POOL_PAD = 8
assert POOL_WINDOWS == (2, 4, 8, 16) and POOL_PAD >= POOL_WINDOWS[-1] // 2
V_ROWS = HEAD_DIM + 16
N_SPLIT = 3
MASKED = -1e30
SKEW = 2
LOG2E = 1.4426950408889634
VMEM_LIMIT = 56 * 1024 * 1024
assert POOL_WIDTH == ATTN_WIDTH
IN_GROUP = POOL_WIDTH
COL_U, COL_ZP, COL_Q, COL_K, COL_V, COL_ZA, COL_F = (IN_GROUP * i for i in range(7))
COL_G = COL_F + N_HEADS
F_ROWS = 16
N_IN = COL_G + 2 * D_MODEL

f32 = jnp.float32
bf16 = jnp.bfloat16


def _rmsnorm(x, g):
    return x * lax.rsqrt(jnp.mean(x * x, axis=-1, keepdims=True) + RMS_EPS) * g


def _sigmoid(x):
    return 0.5 * jnp.tanh(0.5 * x) + 0.5


def _log_sigmoid(x):
    return jnp.minimum(x, 0.0) - jnp.log1p(jnp.exp(-jnp.abs(x)))


def _dot(a, b):
    return jnp.dot(a, b, preferred_element_type=f32)


def _dot_nt(a, b):
    return lax.dot_general(a, b, (((1,), (1,)), ((), ())), preferred_element_type=f32)


def _lane_iota(shape):
    return lax.broadcasted_iota(jnp.int32, shape, 1)


def _decay_parts(logf3, rows):
    n = logf3.shape[0] // rows
    r = lax.broadcasted_iota(jnp.int32, (rows, rows), 0)
    c = lax.broadcasted_iota(jnp.int32, (rows, rows), 1)
    tri = (c <= r).astype(f32)
    side_by_side = jnp.concatenate([logf3[rows * i:rows * (i + 1)] for i in range(n)], axis=1)
    sums = jnp.dot(tri, side_by_side, precision=lax.Precision.HIGHEST, preferred_element_type=f32)
    beta = jnp.concatenate([sums[:, LANES * i:LANES * (i + 1)] for i in range(n)], axis=0)
    return beta, _split_decay(beta)


def _decay_parts_from_rows(logf_t, rows):
    n = logf_t.shape[1] // rows
    k = lax.broadcasted_iota(jnp.int32, (rows, rows), 0)
    j = lax.broadcasted_iota(jnp.int32, (rows, rows), 1)
    stacked = jnp.concatenate([logf_t[:, rows * i:rows * (i + 1)] for i in range(n)], axis=0)
    sums_t = jnp.dot(stacked, (k <= j).astype(f32), precision=lax.Precision.HIGHEST, preferred_element_type=f32)
    unused = jnp.zeros((LANES - N_SPLIT * N_HEADS, rows), f32)
    beta = jnp.concatenate(
        [jnp.concatenate([sums_t[N_HEADS * i:N_HEADS * (i + 1)]] * N_SPLIT + [unused], axis=0).T for i in range(n)],
        axis=0)
    return beta, _split_decay(beta)


def _split_decay(beta):
    nb = beta * (-LOG2E)
    hi = nb.astype(bf16).astype(f32)
    mid = (nb - hi).astype(bf16).astype(f32)
    lo = (nb - hi - mid).astype(bf16).astype(f32)
    lane = _lane_iota(nb.shape)
    return jnp.where(lane < N_HEADS, hi, jnp.where(lane < 2 * N_HEADS, mid, lo))


def _augmented_keys(kproj, parts, store):
    lane = _lane_iota((kproj.shape[0], LANES))
    for h in range(N_HEADS):
        slab = kproj[:, LANES * (h // 2):LANES * (h // 2 + 1)]
        if h % 2:
            slab = pltpu.roll(slab, HEAD_DIM, axis=1)
        store(h, jnp.where(lane < HEAD_DIM, slab, pltpu.roll(parts, HEAD_DIM - h, axis=1)).astype(bf16))


def _pool_mapped(wu_t, pw_ref):
    return jnp.concatenate([_dot(pw_ref[g].T.astype(bf16), wu_t[POOL_GROUP * g:POOL_GROUP * (g + 1)])
                            for g in range(len(POOL_WINDOWS))], axis=0).astype(bf16)


def _forget_rows(wf_t):
    unused = jnp.zeros((LANES - N_SPLIT * N_HEADS, wf_t.shape[1]), f32)
    return jnp.concatenate([wf_t] * N_SPLIT + [unused], axis=0).astype(bf16)


def _forget_bias(b_ref, shape, axis):
    index = lax.broadcasted_iota(jnp.int32, shape, axis)
    bias = jnp.zeros(shape, f32)
    for h in range(N_HEADS):
        bias = jnp.where((index % N_HEADS == h) & (index < N_SPLIT * N_HEADS), b_ref[h], bias)
    return bias


def _meta_kernel(meta_ref, g_ref, wu_ref, wk_ref, wv_ref, wf_ref, bf_ref, pw_ref,
                 umeta_ref, kmeta_ref, vmeta_ref, r0_ref):
    hn = _rmsnorm(meta_ref[...], g_ref[...]).astype(bf16)
    umeta_ref[...] = _dot_nt(hn, _pool_mapped(wu_ref[...].astype(bf16), pw_ref))
    vt = _dot_nt(wv_ref[...].astype(bf16), hn)
    for h in range(N_HEADS):
        vmeta_ref[h, 0:HEAD_DIM, :] = vt[HEAD_DIM * h:HEAD_DIM * (h + 1), :].astype(bf16)
        vmeta_ref[h, HEAD_DIM:V_ROWS, :] = jnp.ones((V_ROWS - HEAD_DIM, N_META), bf16)
    logf3 = _log_sigmoid(_dot_nt(hn, _forget_rows(wf_ref[...])) + _forget_bias(bf_ref, (1, LANES), 1))
    beta, parts = _decay_parts(logf3, N_META)

    def store(h, ka):
        kmeta_ref[h] = ka

    _augmented_keys(_dot_nt(hn, wk_ref[...].astype(bf16)), parts, store)
    r0_ref[...] = jnp.broadcast_to(beta[N_META - 1:N_META, :], r0_ref.shape)


def _proj_kernel(x_ref, g_ref, wt_ref, bf_ref, pw_ref, ps_ref, wup32_ref,
                 umeta_ref, r0_ref,
                 mp_ref, ga_ref, sz_ref, qT_ref, vT_ref, kaug_ref, r_ref,
                 uext_ref, lvl_a_ref, lvl_b_ref, rcarry_ref, wb_ref, wg_ref, wu_ref, wup_ref):
    t = pl.program_id(1)
    zp_rows, k_rows, za_rows, q_rows, v_rows = (slice(IN_GROUP * i, IN_GROUP * (i + 1)) for i in range(5))
    f_rows = slice(v_rows.stop, v_rows.stop + F_ROWS)

    @pl.when((pl.program_id(0) == 0) & (t == 0))
    def _():
        for rows, col in ((zp_rows, COL_ZP), (k_rows, COL_K), (za_rows, COL_ZA), (q_rows, COL_Q), (v_rows, COL_V)):
            wb_ref[rows, :] = wt_ref[col:col + IN_GROUP, :].astype(bf16)
        wb_ref[f_rows, :] = wt_ref[COL_F:COL_F + F_ROWS, :].astype(bf16)
        wg_ref[...] = wt_ref[COL_G:N_IN, :].astype(bf16)
        wu_ref[...] = _pool_mapped(wt_ref[COL_U:COL_ZP, :].astype(bf16), pw_ref)
        wup_ref[...] = wup32_ref[...].astype(bf16)
    ext = N_META + TM
    body = slice(POOL_PAD, POOL_PAD + ext)

    @pl.when(t == 0)
    def _():
        uext_ref[0:POOL_PAD, :] = jnp.zeros((POOL_PAD, POOL_WIDTH), f32)
        lvl_a_ref[0:POOL_PAD, :] = jnp.zeros((POOL_PAD, lvl_a_ref.shape[1]), f32)
        lvl_b_ref[0:POOL_PAD, :] = jnp.zeros((POOL_PAD, lvl_b_ref.shape[1]), f32)
        uext_ref[POOL_PAD:POOL_PAD + N_META, :] = umeta_ref[...]
        rcarry_ref[...] = r0_ref[...]

    hn = _rmsnorm(x_ref[0], g_ref[...]).astype(bf16)

    qv = _dot_nt(wb_ref[q_rows.start:, :], hn)
    qT_ref[0] = (qv[0:ATTN_WIDTH] * (HEAD_DIM ** -0.5 * LOG2E)).astype(bf16)
    for c in range(NB):
        for h in range(N_HEADS):
            rows = slice(ATTN_WIDTH + HEAD_DIM * h, ATTN_WIDTH + HEAD_DIM * (h + 1))
            vT_ref[0, c, h, 0:HEAD_DIM, :] = qv[rows, BLK * c:BLK * (c + 1)].astype(bf16)
            vT_ref[0, c, h, HEAD_DIM:V_ROWS, :] = jnp.ones((V_ROWS - HEAD_DIM, BLK), bf16)

    gat =_dot_nt(hn, wg_ref[D_MODEL:2 * D_MODEL, :])
    ga_ref[0] = _sigmoid(gat).astype(bf16)

    u = _dot_nt(hn, wu_ref[...])
    zp = _dot_nt(hn, wb_ref[zp_rows, :])
    uext_ref[POOL_PAD + N_META:POOL_PAD + ext, :] = u

    def shifted_sum(ref, shift, cols):
        return ref[body, cols] + ref[POOL_PAD - shift:POOL_PAD - shift + ext, cols]

    g1, g2, g3 = (slice(POOL_GROUP * g, POOL_GROUP * (g + 1)) for g in range(3))
    sums = [shifted_sum(uext_ref, 1, g1)]
    lvl_a_ref[body, :] = shifted_sum(uext_ref, 1, slice(POOL_GROUP, POOL_WIDTH))
    sums.append(shifted_sum(lvl_a_ref, 2, g1))
    lvl_b_ref[body, :] = shifted_sum(lvl_a_ref, 2, slice(POOL_GROUP, 3 * POOL_GROUP))
    sums.append(shifted_sum(lvl_b_ref, 4, g1))
    lvl_a_ref[body, g1] = shifted_sum(lvl_b_ref, 4, g2)
    sums.append(shifted_sum(lvl_a_ref, 8, g1))
    uext_ref[POOL_PAD:POOL_PAD + N_META, :] = uext_ref[POOL_PAD + TM:POOL_PAD + ext, :]

    pooled = jnp.concatenate([sums[g][N_META:] * (1.0 / w) for g, w in enumerate(POOL_WINDOWS)], axis=1) - u
    y_pool = (pooled * ps_ref[...] * (zp * _sigmoid(zp))).astype(bf16)
    gp = _dot_nt(hn, wg_ref[0:D_MODEL, :])
    mp_ref[0] = (_sigmoid(gp) * _dot(y_pool, wup_ref[...])).astype(bf16)

    logf_t = _log_sigmoid(qv[2 * IN_GROUP:2 * IN_GROUP + N_HEADS] + _forget_bias(bf_ref, (N_HEADS, TM), 0))
    r_ref[...] = jnp.zeros(r_ref.shape, f32)
    beta, parts = _decay_parts_from_rows(logf_t, BLK)
    for c in range(NB):
        r_ref[0, 0, c:c + 1, :] = rcarry_ref[0:1, :] * LOG2E
        rcarry_ref[...] = rcarry_ref[...] + beta[BLK * (c + 1) - 1:BLK * (c + 1), :]

    def store(h, ka):
        kaug_ref[0, h] = ka

    _augmented_keys(_dot_nt(hn, wb_ref[k_rows, :]), parts, store)

    za = _dot_nt(hn, wb_ref[za_rows, :])
    sz_ref[0] = (za * _sigmoid(za)).astype(bf16)


def _attn_kernel(r_ref, qT_ref, qnext_ref, kaug_ref, vT_ref, kmeta_ref, vmeta_ref, sz_ref, o_ref,
                 qaug_ref, qaug_next_ref, s_ref, p_ref, sm_ref, pm_ref, m_ref, mblk_ref, mmeta_ref, alpha_ref, acc_ref,
                 *, n_blocks):
    b = pl.program_id(0)
    t = pl.program_id(1)
    first = QB * t
    heads = range(N_HEADS)

    part_row = lax.broadcasted_iota(jnp.int32, (LANES - HEAD_DIM, BQ), 0)
    ones_rows = ((part_row % N_HEADS == 0) & (part_row < N_SPLIT * N_HEADS)).astype(bf16)
    for h in heads:
        for q_ref, aug_ref in ((qT_ref, qaug_ref), (qnext_ref, qaug_next_ref)):
            aug_ref[h, 0:HEAD_DIM, :] = q_ref[0, HEAD_DIM * h:HEAD_DIM * (h + 1), :]
            aug_ref[h, HEAD_DIM:LANES, :] = ones_rows

    def r_at(j, h):
        return r_ref[(b * n_blocks + j) * N_HEADS + h]

    def col_max(s):
        return jnp.max(s, axis=0, keepdims=True)

    key_index = functools.partial(lax.broadcasted_iota, jnp.int32, dimension=0)
    query_index = functools.partial(lax.broadcasted_iota, jnp.int32, dimension=1)

    def seen_by(diag):
        return slice(BLK * diag, BQ) if diag else slice(None)

    def score_item(j, slot, h, diag=None, meta=False, next_tile=False):
        q_ref = qaug_next_ref if next_tile else qaug_ref
        cols = seen_by(diag)
        s = _dot(kaug_ref[0, h, pl.ds(pl.multiple_of(j * BLK, BLK), BLK), :], q_ref[h, :, cols])
        if diag is not None:
            s = jnp.where(key_index(s.shape) <= query_index(s.shape), s, MASKED)
        s_ref[slot, h, :, cols] = s
        unseen = [jnp.full((1, BQ - s.shape[1]), MASKED, f32)] if s.shape[1] < BQ else []
        mblk_ref[slot, h] = jnp.concatenate(unseen + [col_max(s)], axis=1)
        if meta:
            sm = _dot(kmeta_ref[h], qaug_ref[h])
            sm_ref[h] = sm
            mmeta_ref[h] = col_max(sm)

    def softmax_item(j, slot, h, meta=False, diag=None):
        cols = seen_by(diag)
        assert not (meta and diag)
        ref = r_at(first, h)
        off = ref - r_at(j, h)
        m_old = m_ref[h]
        m_blk = mblk_ref[slot, h] + off
        if meta:
            m_blk = jnp.maximum(m_blk, mmeta_ref[h] + ref)
        m_new = jnp.maximum(m_old, m_blk)
        alpha_ref[h] = jnp.exp2(m_old - m_new)
        m_seen = m_new
        if diag:
            m_seen = jnp.maximum(m_ref[h, :, cols], mblk_ref[slot, h, :, cols] + off)
        m_ref[h] = m_new
        p_ref[h, :, cols] = jnp.exp2(s_ref[slot, h, :, cols] - (m_seen - off)).astype(bf16)
        if meta:
            pm_ref[h] = jnp.exp2(sm_ref[h] - (m_new - ref)).astype(bf16)

    def value_item(j, slot, h, meta=False, diag=None):
        cols = seen_by(diag)
        pv = _dot(vT_ref[0, j, h], p_ref[h, :, cols])
        if meta:
            pv = pv + _dot(vmeta_ref[h], pm_ref[h])
        acc_ref[h, :, cols] = acc_ref[h, :, cols] * alpha_ref[h, :, cols] + pv

    def run(score_blocks, finish_blocks):
        score_items = [(blk + (h,), kw) for blk, kw in score_blocks for h in heads]
        finish_items = [(blk + (h,), kw) for blk, kw in finish_blocks for h in heads]
        for i in range(max(len(score_items), len(finish_items) + SKEW + 1)):
            if i < len(score_items):
                args, kw = score_items[i]
                score_item(*args, **kw)
            if 0 <= i - SKEW - 1 < len(finish_items):
                args, kw = finish_items[i - SKEW - 1]
                value_item(*args, **kw)
            if 0 <= i - SKEW < len(finish_items):
                args, kw = finish_items[i - SKEW]
                softmax_item(*args, **kw)

    m_ref[...] = jnp.full(m_ref.shape, MASKED, f32)
    acc_ref[...] = jnp.zeros(acc_ref.shape, f32)

    @pl.when(first == 0)
    def _():
        run([((first, 0), dict(diag=0))], [])

    @pl.when(first > 0)
    def _():
        def body(k, carry):
            run([((2 * k + 1, 1), {}), ((2 * k + 2, 0), {})],
                [((2 * k, 0), {}), ((2 * k + 1, 1), {})])
            return carry

        lax.fori_loop(0, t - 1, body, 0)
        run([((first - 1, 1), {}), ((first, 0), dict(diag=0))],
            [((first - 2, 0), {}), ((first - 1, 1), {})])

    run([((first + 1, 1), dict(diag=1, meta=True)), ((0, 0), dict(next_tile=True))],
        [((first, 0), dict(meta=True)), ((first + 1, 1), dict(diag=1))])

    for pair in range(N_HEADS // 2):
        halves = []
        for h in (2 * pair, 2 * pair + 1):
            a = acc_ref[h]
            halves.append(a[0:HEAD_DIM] * (1.0 / a[HEAD_DIM:HEAD_DIM + 1]))
        cols = slice(LANES * pair, LANES * (pair + 1))
        o_ref[0, :, cols] = (jnp.concatenate(halves, axis=0).T * sz_ref[0, :, cols].astype(f32)).astype(bf16)


def _out_kernel(x_hbm, y_hbm, ga_hbm, mp_hbm, wua32_ref, wout32_ref, g_ref, out_hbm, wua_ref, wout_ref,
                *, batch, seq):
    wua_ref[...] = wua32_ref[...].astype(bf16)
    wout_ref[...] = wout32_ref[...].astype(bf16)

    def tile(x_ref, y_ref, ga_ref, mp_ref, out_ref):
        def merged(rows):
            up = _dot(y_ref[0, rows, :], wua_ref[...])
            return (mp_ref[0, rows, :].astype(f32) + ga_ref[0, rows, :].astype(f32) * up).astype(bf16)

        chunks = [slice(OUT_CHUNK * i, OUT_CHUNK * (i + 1)) for i in range(TM_OUT // OUT_CHUNK)]
        pending = merged(chunks[0])
        for i, rows in enumerate(chunks):
            ahead = merged(chunks[i + 1]) if i + 1 < len(chunks) else None
            h_out = x_ref[0, rows, :] + _dot(pending, wout_ref[...])
            out_ref[0, rows, :] = _rmsnorm(h_out, g_ref[...])
            pending = ahead

    def rows_of(width, **kw):
        return pl.BlockSpec((1, TM_OUT, width), lambda b, t: (b, t, 0), **kw)

    streamed = dict(pipeline_mode=pl.Buffered(OUT_BUFFERS))
    pltpu.emit_pipeline(
        tile,
        grid=(batch, seq // TM_OUT),
        in_specs=[rows_of(D_MODEL, **streamed), rows_of(ATTN_WIDTH, **streamed),
                  rows_of(D_MODEL, **streamed), rows_of(D_MODEL, **streamed)],
        out_specs=[rows_of(D_MODEL)],
    )(x_hbm, y_hbm, ga_hbm, mp_hbm, out_hbm)


def _const(shape):
    return pl.BlockSpec(shape, lambda *_: (0,) * len(shape), pipeline_mode=pl.Buffered(1))


def kernel(x, meta_tokens, norm_g, w_in, b_forget, pool_w, pool_scale, w_up_pool, w_up_attn, w_out, final_norm_g):
    batch, seq, _ = x.shape
    n_tiles = seq // TM
    n_blocks = seq // BLK

    w_t = jnp.transpose(w_in[0])
    b_f = b_forget.reshape(N_HEADS)
    g_in = norm_g[0].reshape(1, D_MODEL)
    g_out = final_norm_g.reshape(1, D_MODEL)
    pw = pool_w[0]
    ps = pool_scale[0].reshape(1, POOL_WIDTH)
    w_upp, w_upa, w_o = w_up_pool[0], w_up_attn[0], w_out[0]

    u_meta, k_meta, v_meta, r0 = pl.pallas_call(
        _meta_kernel,
        out_shape=(jax.ShapeDtypeStruct((N_META, POOL_WIDTH), f32),
                   jax.ShapeDtypeStruct((N_HEADS, N_META, LANES), bf16),
                   jax.ShapeDtypeStruct((N_HEADS, V_ROWS, N_META), bf16),
                   jax.ShapeDtypeStruct((8, LANES), f32)),
        grid=(1,),
        in_specs=[_const((N_META, D_MODEL)), _const((1, D_MODEL))]
        + [pl.BlockSpec((rows, D_MODEL), functools.partial(lambda r, i: (r, 0), col // rows),
                        pipeline_mode=pl.Buffered(1))
           for col, rows in ((COL_U, IN_GROUP), (COL_K, IN_GROUP), (COL_V, IN_GROUP), (COL_F, N_HEADS))]
        + [pl.BlockSpec(memory_space=pltpu.SMEM), _const((len(POOL_WINDOWS), POOL_GROUP, POOL_GROUP))],
        out_specs=[_const((N_META, POOL_WIDTH)), _const((N_HEADS, N_META, LANES)),
                   _const((N_HEADS, V_ROWS, N_META)), _const((8, LANES))],
        compiler_params=pltpu.CompilerParams(vmem_limit_bytes=VMEM_LIMIT),
        name="meta_proj",
    )(meta_tokens, g_in, w_t, w_t, w_t, w_t, b_f, pw)

    mp, ga, sz, qT, vT, kaug, r_tiles = pl.pallas_call(
        _proj_kernel,
        grid=(batch, n_tiles),
        in_specs=[
            pl.BlockSpec((1, TM, D_MODEL), lambda b, t: (b, t, 0)),
            _const((1, D_MODEL)),
            _const((N_IN, D_MODEL)),
            pl.BlockSpec(memory_space=pltpu.SMEM),
            _const((len(POOL_WINDOWS), POOL_GROUP, POOL_GROUP)),
            _const((1, POOL_WIDTH)),
            _const((POOL_WIDTH, D_MODEL)),
            _const((N_META, POOL_WIDTH)),
            _const((8, LANES)),
        ],
        out_specs=[
            pl.BlockSpec((1, TM, D_MODEL), lambda b, t: (b, t, 0)),
            pl.BlockSpec((1, TM, D_MODEL), lambda b, t: (b, t, 0)),
            pl.BlockSpec((1, TM, ATTN_WIDTH), lambda b, t: (b, t, 0)),
            pl.BlockSpec((1, ATTN_WIDTH, TM), lambda b, t: (b, 0, t)),
            pl.BlockSpec((1, NB, N_HEADS, V_ROWS, BLK), lambda b, t: (b, t, 0, 0, 0)),
            pl.BlockSpec((1, N_HEADS, TM, LANES), lambda b, t: (b, 0, t, 0)),
            pl.BlockSpec((1, 1, 8, LANES), lambda b, t: (b, t, 0, 0)),
        ],
        out_shape=(
            jax.ShapeDtypeStruct((batch, seq, D_MODEL), bf16),
            jax.ShapeDtypeStruct((batch, seq, D_MODEL), bf16),
            jax.ShapeDtypeStruct((batch, seq, ATTN_WIDTH), bf16),
            jax.ShapeDtypeStruct((batch, ATTN_WIDTH, seq), bf16),
            jax.ShapeDtypeStruct((batch, n_blocks, N_HEADS, V_ROWS, BLK), bf16),
            jax.ShapeDtypeStruct((batch, N_HEADS, seq, LANES), bf16),
            jax.ShapeDtypeStruct((batch, n_tiles, 8, LANES), f32),
        ),
        scratch_shapes=[pltpu.VMEM((POOL_PAD + N_META + TM, POOL_WIDTH), f32),
                        pltpu.VMEM((POOL_PAD + N_META + TM, 3 * POOL_GROUP), f32),
                        pltpu.VMEM((POOL_PAD + N_META + TM, 2 * POOL_GROUP), f32),
                        pltpu.VMEM((8, LANES), f32),
                        pltpu.VMEM((5 * IN_GROUP + F_ROWS, D_MODEL), bf16),
                        pltpu.VMEM((2 * D_MODEL, D_MODEL), bf16),
                        pltpu.VMEM((POOL_WIDTH, D_MODEL), bf16),
                        pltpu.VMEM((POOL_WIDTH, D_MODEL), bf16)],
        compiler_params=pltpu.CompilerParams(dimension_semantics=("arbitrary", "arbitrary"),
                                             vmem_limit_bytes=VMEM_LIMIT),
        name="in_proj",
    )(x, g_in, w_t, b_f, pw, ps, w_upp, u_meta, r0)

    r_blocks = r_tiles[:, :, :NB, :N_HEADS].reshape(batch * n_blocks * N_HEADS)

    y_attn = pl.pallas_call(
        functools.partial(_attn_kernel, n_blocks=n_blocks),
        grid=(batch, seq // BQ),
        in_specs=[
            pl.BlockSpec(memory_space=pltpu.SMEM),
            pl.BlockSpec((1, ATTN_WIDTH, BQ), lambda b, t: (b, 0, t)),
            pl.BlockSpec((1, ATTN_WIDTH, BQ), lambda b, t: (b, 0, jnp.minimum(t + 1, seq // BQ - 1))),
            pl.BlockSpec((1, N_HEADS, seq, LANES), lambda b, t: (b, 0, 0, 0)),
            pl.BlockSpec((1, n_blocks, N_HEADS, V_ROWS, BLK), lambda b, t: (b, 0, 0, 0, 0)),
            _const((N_HEADS, N_META, LANES)),
            _const((N_HEADS, V_ROWS, N_META)),
            pl.BlockSpec((1, BQ, ATTN_WIDTH), lambda b, t: (b, t, 0)),
        ],
        out_specs=pl.BlockSpec((1, BQ, ATTN_WIDTH), lambda b, t: (b, t, 0)),
        out_shape=jax.ShapeDtypeStruct((batch, seq, ATTN_WIDTH), bf16),
        scratch_shapes=[pltpu.VMEM((N_HEADS, LANES, BQ), bf16),
                        pltpu.VMEM((N_HEADS, LANES, BQ), bf16),
                        pltpu.VMEM((2, N_HEADS, BLK, BQ), f32),
                        pltpu.VMEM((N_HEADS, BLK, BQ), bf16),
                        pltpu.VMEM((N_HEADS, N_META, BQ), f32),
                        pltpu.VMEM((N_HEADS, N_META, BQ), bf16),
                        pltpu.VMEM((N_HEADS, 1, BQ), f32),
                        pltpu.VMEM((2, N_HEADS, 1, BQ), f32),
                        pltpu.VMEM((N_HEADS, 1, BQ), f32),
                        pltpu.VMEM((N_HEADS, 1, BQ), f32),
                        pltpu.VMEM((N_HEADS, V_ROWS, BQ), f32)],
        compiler_params=pltpu.CompilerParams(dimension_semantics=("arbitrary", "arbitrary"),
                                             vmem_limit_bytes=VMEM_LIMIT),
        name="attention",
    )(r_blocks, qT, qT, kaug, vT, k_meta, v_meta, sz)

    in_hbm = pl.BlockSpec(memory_space=pl.ANY)
    in_vmem = pl.BlockSpec(memory_space=pltpu.VMEM)
    return pl.pallas_call(
        functools.partial(_out_kernel, batch=batch, seq=seq),
        in_specs=[in_hbm, in_hbm, in_hbm, in_hbm, in_vmem, in_vmem, in_vmem],
        out_specs=pl.BlockSpec(memory_space=pl.ANY),
        out_shape=jax.ShapeDtypeStruct((batch, seq, D_MODEL), f32),
        scratch_shapes=[pltpu.VMEM((ATTN_WIDTH, D_MODEL), bf16),
                        pltpu.VMEM((D_MODEL, D_MODEL), bf16)],
        compiler_params=pltpu.CompilerParams(vmem_limit_bytes=VMEM_LIMIT),
        name="out_proj",
    )(x, y_attn, ga, mp, w_upa, w_o, g_out)
```

```python
import functools

import jax
import jax.numpy as jnp
from jax import lax
from jax.experimental import pallas as pl
from jax.experimental.pallas import tpu as pltpu

D_MODEL = 1024
N_META = 16
POOL_WIDTH = 512
POOL_WINDOWS = (2, 4, 8, 16)
POOL_GROUP = POOL_WIDTH // len(POOL_WINDOWS)
N_HEADS = 8
HEAD_DIM = 64
ATTN_WIDTH = N_HEADS * HEAD_DIM
RMS_EPS = 1e-6

LANES = 128
BLK = 256
BQ = 512
QB = BQ // BLK
assert QB == 2
TM = 512
NB = TM // BLK
TM_OUT = 1024
OUT_CHUNK = 256
OUT_BUFFERS = 3
POOL_PAD = 8
assert POOL_WINDOWS == (2, 4, 8, 16) and POOL_PAD >= POOL_WINDOWS[-1] // 2
V_ROWS = HEAD_DIM + 16
N_SPLIT = 3
MASKED = -1e30
SKEW = 2
LOG2E = 1.4426950408889634
VMEM_LIMIT = 56 * 1024 * 1024
assert POOL_WIDTH == ATTN_WIDTH
IN_GROUP = POOL_WIDTH
COL_U, COL_ZP, COL_Q, COL_K, COL_V, COL_ZA, COL_F = (IN_GROUP * i for i in range(7))
COL_G = COL_F + N_HEADS
F_ROWS = 16
N_IN = COL_G + 2 * D_MODEL

f32 = jnp.float32
bf16 = jnp.bfloat16


def _rmsnorm(x, g):
    return x * lax.rsqrt(jnp.mean(x * x, axis=-1, keepdims=True) + RMS_EPS) * g


def _sigmoid(x):
    return 0.5 * jnp.tanh(0.5 * x) + 0.5


def _log_sigmoid(x):
    return jnp.minimum(x, 0.0) - jnp.log1p(jnp.exp(-jnp.abs(x)))


def _dot(a, b):
    return jnp.dot(a, b, preferred_element_type=f32)


def _dot_nt(a, b):
    return lax.dot_general(a, b, (((1,), (1,)), ((), ())), preferred_element_type=f32)


def _lane_iota(shape):
    return lax.broadcasted_iota(jnp.int32, shape, 1)


def _decay_parts(logf3, rows):
    n = logf3.shape[0] // rows
    r = lax.broadcasted_iota(jnp.int32, (rows, rows), 0)
    c = lax.broadcasted_iota(jnp.int32, (rows, rows), 1)
    tri = (c <= r).astype(f32)
    side_by_side = jnp.concatenate([logf3[rows * i:rows * (i + 1)] for i in range(n)], axis=1)
    sums = jnp.dot(tri, side_by_side, precision=lax.Precision.HIGHEST, preferred_element_type=f32)
    beta = jnp.concatenate([sums[:, LANES * i:LANES * (i + 1)] for i in range(n)], axis=0)
    return beta, _split_decay(beta)


def _decay_parts_from_rows(logf_t, rows):
    n = logf_t.shape[1] // rows
    k = lax.broadcasted_iota(jnp.int32, (rows, rows), 0)
    j = lax.broadcasted_iota(jnp.int32, (rows, rows), 1)
    stacked = jnp.concatenate([logf_t[:, rows * i:rows * (i + 1)] for i in range(n)], axis=0)
    sums_t = jnp.dot(stacked, (k <= j).astype(f32), precision=lax.Precision.HIGHEST, preferred_element_type=f32)
    unused = jnp.zeros((LANES - N_SPLIT * N_HEADS, rows), f32)
    beta = jnp.concatenate(
        [jnp.concatenate([sums_t[N_HEADS * i:N_HEADS * (i + 1)]] * N_SPLIT + [unused], axis=0).T for i in range(n)],
        axis=0)
    return beta, _split_decay(beta)


def _split_decay(beta):
    nb = beta * (-LOG2E)
    hi = nb.astype(bf16).astype(f32)
    mid = (nb - hi).astype(bf16).astype(f32)
    lo = (nb - hi - mid).astype(bf16).astype(f32)
    lane = _lane_iota(nb.shape)
    return jnp.where(lane < N_HEADS, hi, jnp.where(lane < 2 * N_HEADS, mid, lo))


def _augmented_keys(kproj, parts, store):
    lane = _lane_iota((kproj.shape[0], LANES))
    for h in range(N_HEADS):
        slab = kproj[:, LANES * (h // 2):LANES * (h // 2 + 1)]
        if h % 2:
            slab = pltpu.roll(slab, HEAD_DIM, axis=1)
        store(h, jnp.where(lane < HEAD_DIM, slab, pltpu.roll(parts, HEAD_DIM - h, axis=1)).astype(bf16))


def _pool_mapped(wu_t, pw_ref):
    return jnp.concatenate([_dot(pw_ref[g].T.astype(bf16), wu_t[POOL_GROUP * g:POOL_GROUP * (g + 1)])
                            for g in range(len(POOL_WINDOWS))], axis=0).astype(bf16)


def _forget_rows(wf_t):
    unused = jnp.zeros((LANES - N_SPLIT * N_HEADS, wf_t.shape[1]), f32)
    return jnp.concatenate([wf_t] * N_SPLIT + [unused], axis=0).astype(bf16)


def _forget_bias(b_ref, shape, axis):
    index = lax.broadcasted_iota(jnp.int32, shape, axis)
    bias = jnp.zeros(shape, f32)
    for h in range(N_HEADS):
        bias = jnp.where((index % N_HEADS == h) & (index < N_SPLIT * N_HEADS), b_ref[h], bias)
    return bias


def _meta_kernel(meta_ref, g_ref, wu_ref, wk_ref, wv_ref, wf_ref, bf_ref, pw_ref,
                 umeta_ref, kmeta_ref, vmeta_ref, r0_ref):
    hn = _rmsnorm(meta_ref[...], g_ref[...]).astype(bf16)
    umeta_ref[...] = _dot_nt(hn, _pool_mapped(wu_ref[...].astype(bf16), pw_ref))
    vt = _dot_nt(wv_ref[...].astype(bf16), hn)
    for h in range(N_HEADS):
        vmeta_ref[h, 0:HEAD_DIM, :] = vt[HEAD_DIM * h:HEAD_DIM * (h + 1), :].astype(bf16)
        vmeta_ref[h, HEAD_DIM:V_ROWS, :] = jnp.ones((V_ROWS - HEAD_DIM, N_META), bf16)
    logf3 = _log_sigmoid(_dot_nt(hn, _forget_rows(wf_ref[...])) + _forget_bias(bf_ref, (1, LANES), 1))
    beta, parts = _decay_parts(logf3, N_META)

    def store(h, ka):
        kmeta_ref[h] = ka

    _augmented_keys(_dot_nt(hn, wk_ref[...].astype(bf16)), parts, store)
    r0_ref[...] = jnp.broadcast_to(beta[N_META - 1:N_META, :], r0_ref.shape)


def _proj_kernel(x_ref, g_ref, wt_ref, bf_ref, pw_ref, ps_ref, wup32_ref,
                 umeta_ref, r0_ref,
                 mp_ref, ga_ref, sz_ref, qT_ref, vT_ref, kaug_ref, r_ref,
                 uext_ref, lvl_a_ref, lvl_b_ref, rcarry_ref, wb_ref, wg_ref, wu_ref, wup_ref):
    t = pl.program_id(1)
    zp_rows, k_rows, za_rows, q_rows, v_rows = (slice(IN_GROUP * i, IN_GROUP * (i + 1)) for i in range(5))
    f_rows = slice(v_rows.stop, v_rows.stop + F_ROWS)

    @pl.when((pl.program_id(0) == 0) & (t == 0))
    def _():
        for rows, col in ((zp_rows, COL_ZP), (k_rows, COL_K), (za_rows, COL_ZA), (q_rows, COL_Q), (v_rows, COL_V)):
            wb_ref[rows, :] = wt_ref[col:col + IN_GROUP, :].astype(bf16)
        wb_ref[f_rows, :] = wt_ref[COL_F:COL_F + F_ROWS, :].astype(bf16)
        wg_ref[...] = wt_ref[COL_G:N_IN, :].astype(bf16)
        wu_ref[...] = _pool_mapped(wt_ref[COL_U:COL_ZP, :].astype(bf16), pw_ref)
        wup_ref[...] = wup32_ref[...].astype(bf16)
    ext = N_META + TM
    body = slice(POOL_PAD, POOL_PAD + ext)

    @pl.when(t == 0)
    def _():
        uext_ref[0:POOL_PAD, :] = jnp.zeros((POOL_PAD, POOL_WIDTH), f32)
        lvl_a_ref[0:POOL_PAD, :] = jnp.zeros((POOL_PAD, lvl_a_ref.shape[1]), f32)
        lvl_b_ref[0:POOL_PAD, :] = jnp.zeros((POOL_PAD, lvl_b_ref.shape[1]), f32)
        uext_ref[POOL_PAD:POOL_PAD + N_META, :] = umeta_ref[...]
        rcarry_ref[...] = r0_ref[...]

    hn = _rmsnorm(x_ref[0], g_ref[...]).astype(bf16)

    qv = _dot_nt(wb_ref[q_rows.start:, :], hn)
    qT_ref[0] = (qv[0:ATTN_WIDTH] * (HEAD_DIM ** -0.5 * LOG2E)).astype(bf16)
    for c in range(NB):
        for h in range(N_HEADS):
            rows = slice(ATTN_WIDTH + HEAD_DIM * h, ATTN_WIDTH + HEAD_DIM * (h + 1))
            vT_ref[0, c, h, 0:HEAD_DIM, :] = qv[rows, BLK * c:BLK * (c + 1)].astype(bf16)
            vT_ref[0, c, h, HEAD_DIM:V_ROWS, :] = jnp.ones((V_ROWS - HEAD_DIM, BLK), bf16)

    gat = _dot_nt(hn, wg_ref[D_MODEL:2 * D_MODEL, :])
    ga_ref[0] = _sigmoid(gat).astype(bf16)

    u = _dot_nt(hn, wu_ref[...])
    zp = _dot_nt(hn, wb_ref[zp_rows, :])
    uext_ref[POOL_PAD + N_META:POOL_PAD + ext, :] = u

    def shifted_sum(ref, shift, cols):
        return ref[body, cols] + ref[POOL_PAD - shift:POOL_PAD - shift + ext, cols]

    g1, g2, g3 = (slice(POOL_GROUP * g, POOL_GROUP * (g + 1)) for g in range(3))
    sums = [shifted_sum(uext_ref, 1, g1)]
    lvl_a_ref[body, :] = shifted_sum(uext_ref, 1, slice(POOL_GROUP, POOL_WIDTH))
    sums.append(shifted_sum(lvl_a_ref, 2, g1))
    lvl_b_ref[body, :] = shifted_sum(lvl_a_ref, 2, slice(POOL_GROUP, 3 * POOL_GROUP))
    sums.append(shifted_sum(lvl_b_ref, 4, g1))
    lvl_a_ref[body, g1] = shifted_sum(lvl_b_ref, 4, g2)
    sums.append(shifted_sum(lvl_a_ref, 8, g1))
    uext_ref[POOL_PAD:POOL_PAD + N_META, :] = uext_ref[POOL_PAD + TM:POOL_PAD + ext, :]

    pooled = jnp.concatenate([sums[g][N_META:] * (1.0 / w) for g, w in enumerate(POOL_WINDOWS)], axis=1) - u
    y_pool = (pooled * ps_ref[...] * (zp * _sigmoid(zp))).astype(bf16)
    gp = _dot_nt(hn, wg_ref[0:D_MODEL, :])
    mp_ref[0] = (_sigmoid(gp) * _dot(y_pool, wup_ref[...])).astype(bf16)

    logf_t = _log_sigmoid(qv[2 * IN_GROUP:2 * IN_GROUP + N_HEADS] + _forget_bias(bf_ref, (N_HEADS, TM), 0))
    r_ref[...] = jnp.zeros(r_ref.shape, f32)
    beta, parts = _decay_parts_from_rows(logf_t, BLK)
    for c in range(NB):
        r_ref[0, 0, c:c + 1, :] = rcarry_ref[0:1, :] * LOG2E
        rcarry_ref[...] = rcarry_ref[...] + beta[BLK * (c + 1) - 1:BLK * (c + 1), :]

    def store(h, ka):
        kaug_ref[0, h] = ka

    _augmented_keys(_dot_nt(hn, wb_ref[k_rows, :]), parts, store)

    za = _dot_nt(hn, wb_ref[za_rows, :])
    sz_ref[0] = (za * _sigmoid(za)).astype(bf16)


def _attn_kernel(r_ref, qT_ref, qnext_ref, kaug_ref, vT_ref, kmeta_ref, vmeta_ref, sz_ref, o_ref,
                 qaug_ref, s_ref, p_ref, sm_ref, pm_ref, m_ref, mblk_ref, mmeta_ref, alpha_ref, acc_ref,
                 *, n_blocks):
    b = pl.program_id(0)
    t = pl.program_id(1)
    first = QB * t
    heads = range(N_HEADS)

    this_slot, next_slot = t % 2, (t + 1) % 2

    @pl.when((b == 0) & (t == 0))
    def _():
        part_row = lax.broadcasted_iota(jnp.int32, (LANES - HEAD_DIM, BQ), 0)
        ones_rows = ((part_row % N_HEADS == 0) & (part_row < N_SPLIT * N_HEADS)).astype(bf16)
        for slot in range(2):
            for h in heads:
                qaug_ref[slot, h, HEAD_DIM:LANES, :] = ones_rows

    @pl.when(t == 0)
    def _():
        for h in heads:
            qaug_ref[0, h, 0:HEAD_DIM, :] = qT_ref[0, HEAD_DIM * h:HEAD_DIM * (h + 1), :]

    for h in heads:
        qaug_ref[next_slot, h, 0:HEAD_DIM, :] = qnext_ref[0, HEAD_DIM * h:HEAD_DIM * (h + 1), :]

    def r_at(j, h):
        return r_ref[(b * n_blocks + j) * N_HEADS + h]

    def col_max(s):
        return jnp.max(s, axis=0, keepdims=True)

    key_index = functools.partial(lax.broadcasted_iota, jnp.int32, dimension=0)
    query_index = functools.partial(lax.broadcasted_iota, jnp.int32, dimension=1)

    def seen_by(diag):
        return slice(BLK * diag, BQ) if diag else slice(None)

    def score_item(j, slot, h, diag=None, meta=False, next_tile=False):
        q_slot = next_slot if next_tile else this_slot
        cols = seen_by(diag)
        s = _dot(kaug_ref[0, h, pl.ds(pl.multiple_of(j * BLK, BLK), BLK), :], qaug_ref[q_slot, h, :, cols])
        if diag is not None:
            s = jnp.where(key_index(s.shape) <= query_index(s.shape), s, MASKED)
        s_ref[slot, h, :, cols] = s
        unseen = [jnp.full((1, BQ - s.shape[1]), MASKED, f32)] if s.shape[1] < BQ else []
        mblk_ref[slot, h] = jnp.concatenate(unseen + [col_max(s)], axis=1)
        if meta:
            sm = _dot(kmeta_ref[h], qaug_ref[this_slot, h])
            sm_ref[h] = sm
            mmeta_ref[h] = col_max(sm)

    def softmax_item(j, slot, h, meta=False, diag=None):
        cols = seen_by(diag)
        assert not (meta and diag)
        ref = r_at(first, h)
        off = ref - r_at(j, h)
        m_old = m_ref[h]
        m_blk = mblk_ref[slot, h] + off
        if meta:
            m_blk = jnp.maximum(m_blk, mmeta_ref[h] + ref)
        m_new = jnp.maximum(m_old, m_blk)
        alpha_ref[h] = jnp.exp2(m_old - m_new)
        m_seen = m_new
        if diag:
            m_seen = jnp.maximum(m_ref[h, :, cols], mblk_ref[slot, h, :, cols] + off)
        m_ref[h] = m_new
        p_ref[h, :, cols] = jnp.exp2(s_ref[slot, h, :, cols] - (m_seen - off)).astype(bf16)
        if meta:
            pm_ref[h] = jnp.exp2(sm_ref[h] - (m_new - ref)).astype(bf16)

    def value_item(j, slot, h, meta=False, diag=None):
        cols = seen_by(diag)
        pv = _dot(vT_ref[0, j, h], p_ref[h, :, cols])
        if meta:
            pv = pv + _dot(vmeta_ref[h], pm_ref[h])
        acc_ref[h, :, cols] = acc_ref[h, :, cols] * alpha_ref[h, :, cols] + pv

    def run(score_blocks, finish_blocks):
        score_items = [(blk + (h,), kw) for blk, kw in score_blocks for h in heads]
        finish_items = [(blk + (h,), kw) for blk, kw in finish_blocks for h in heads]
        for i in range(max(len(score_items), len(finish_items) + SKEW + 1)):
            if i < len(score_items):
                args, kw = score_items[i]
                score_item(*args, **kw)
            if 0 <= i - SKEW - 1 < len(finish_items):
                args, kw = finish_items[i - SKEW - 1]
                value_item(*args, **kw)
            if 0 <= i - SKEW < len(finish_items):
                args, kw = finish_items[i - SKEW]
                softmax_item(*args, **kw)

    m_ref[...] = jnp.full(m_ref.shape, MASKED, f32)
    acc_ref[...] = jnp.zeros(acc_ref.shape, f32)

    @pl.when(first == 0)
    def _():
        run([((first, 0), dict(diag=0))], [])

    @pl.when(first > 0)
    def _():
        def body(k, carry):
            run([((2 * k + 1, 1), {}), ((2 * k + 2, 0), {})],
                [((2 * k, 0), {}), ((2 * k + 1, 1), {})])
            return carry

        lax.fori_loop(0, t - 1, body, 0)
        run([((first - 1, 1), {}), ((first, 0), dict(diag=0))],
            [((first - 2, 0), {}), ((first - 1, 1), {})])

    run([((first + 1, 1), dict(diag=1, meta=True)), ((0, 0), dict(next_tile=True))],
        [((first, 0), dict(meta=True)), ((first + 1, 1), dict(diag=1))])

    for pair in range(N_HEADS // 2):
        halves = []
        for h in (2 * pair, 2 * pair + 1):
            a = acc_ref[h]
            halves.append(a[0:HEAD_DIM] * (1.0 / a[HEAD_DIM:HEAD_DIM + 1]))
        cols = slice(LANES * pair, LANES * (pair + 1))
        o_ref[0, :, cols] = (jnp.concatenate(halves, axis=0).T * sz_ref[0, :, cols].astype(f32)).astype(bf16)


def _out_kernel(x_hbm, y_hbm, ga_hbm, mp_hbm, wua32_ref, wout32_ref, g_ref, out_hbm, wua_ref, wout_ref,
                *, batch, seq):
    wua_ref[...] = wua32_ref[...].astype(bf16)
    wout_ref[...] = wout32_ref[...].astype(bf16)

    def tile(x_ref, y_ref, ga_ref, mp_ref, out_ref):
        def merged(rows):
            up = _dot(y_ref[0, rows, :], wua_ref[...])
            return (mp_ref[0, rows, :].astype(f32) + ga_ref[0, rows, :].astype(f32) * up).astype(bf16)

        chunks = [slice(OUT_CHUNK * i, OUT_CHUNK * (i + 1)) for i in range(TM_OUT // OUT_CHUNK)]
        pending = merged(chunks[0])
        for i, rows in enumerate(chunks):
            ahead = merged(chunks[i + 1]) if i + 1 < len(chunks) else None
            h_out = x_ref[0, rows, :] + _dot(pending, wout_ref[...])
            out_ref[0, rows, :] = _rmsnorm(h_out, g_ref[...])
            pending = ahead

    def rows_of(width, **kw):
        return pl.BlockSpec((1, TM_OUT, width), lambda b, t: (b, t, 0), **kw)

    streamed = dict(pipeline_mode=pl.Buffered(OUT_BUFFERS))
    pltpu.emit_pipeline(
        tile,
        grid=(batch, seq // TM_OUT),
        in_specs=[rows_of(D_MODEL, **streamed), rows_of(ATTN_WIDTH, **streamed),
                  rows_of(D_MODEL, **streamed), rows_of(D_MODEL, **streamed)],
        out_specs=[rows_of(D_MODEL)],
    )(x_hbm, y_hbm, ga_hbm, mp_hbm, out_hbm)


def _const(shape):
    return pl.BlockSpec(shape, lambda *_: (0,) * len(shape), pipeline_mode=pl.Buffered(1))


def kernel(x, meta_tokens, norm_g, w_in, b_forget, pool_w, pool_scale, w_up_pool, w_up_attn, w_out, final_norm_g):
    batch, seq, _ = x.shape
    n_tiles = seq // TM
    n_blocks = seq // BLK

    w_t = jnp.transpose(w_in[0])
    b_f = b_forget.reshape(N_HEADS)
    g_in = norm_g[0].reshape(1, D_MODEL)
    g_out = final_norm_g.reshape(1, D_MODEL)
    pw = pool_w[0]
    ps = pool_scale[0].reshape(1, POOL_WIDTH)
    w_upp, w_upa, w_o = w_up_pool[0], w_up_attn[0], w_out[0]

    u_meta, k_meta, v_meta, r0 = pl.pallas_call(
        _meta_kernel,
        out_shape=(jax.ShapeDtypeStruct((N_META, POOL_WIDTH), f32),
                   jax.ShapeDtypeStruct((N_HEADS, N_META, LANES), bf16),
                   jax.ShapeDtypeStruct((N_HEADS, V_ROWS, N_META), bf16),
                   jax.ShapeDtypeStruct((8, LANES), f32)),
        grid=(1,),
        in_specs=[_const((N_META, D_MODEL)), _const((1, D_MODEL))]
        + [pl.BlockSpec((rows, D_MODEL), functools.partial(lambda r, i: (r, 0), col // rows),
                        pipeline_mode=pl.Buffered(1))
           for col, rows in ((COL_U, IN_GROUP), (COL_K, IN_GROUP), (COL_V, IN_GROUP), (COL_F, N_HEADS))]
        + [pl.BlockSpec(memory_space=pltpu.SMEM), _const((len(POOL_WINDOWS), POOL_GROUP, POOL_GROUP))],
        out_specs=[_const((N_META, POOL_WIDTH)), _const((N_HEADS, N_META, LANES)),
                   _const((N_HEADS, V_ROWS, N_META)), _const((8, LANES))],
        compiler_params=pltpu.CompilerParams(vmem_limit_bytes=VMEM_LIMIT),
        name="meta_proj",
    )(meta_tokens, g_in, w_t, w_t, w_t, w_t, b_f, pw)

    mp, ga, sz, qT, vT, kaug, r_tiles = pl.pallas_call(
        _proj_kernel,
        grid=(batch, n_tiles),
        in_specs=[
            pl.BlockSpec((1, TM, D_MODEL), lambda b, t: (b, t, 0)),
            _const((1, D_MODEL)),
            _const((N_IN, D_MODEL)),
            pl.BlockSpec(memory_space=pltpu.SMEM),
            _const((len(POOL_WINDOWS), POOL_GROUP, POOL_GROUP)),
            _const((1, POOL_WIDTH)),
            _const((POOL_WIDTH, D_MODEL)),
            _const((N_META, POOL_WIDTH)),
            _const((8, LANES)),
        ],
        out_specs=[
            pl.BlockSpec((1, TM, D_MODEL), lambda b, t: (b, t, 0)),
            pl.BlockSpec((1, TM, D_MODEL), lambda b, t: (b, t, 0)),
            pl.BlockSpec((1, TM, ATTN_WIDTH), lambda b, t: (b, t, 0)),
            pl.BlockSpec((1, ATTN_WIDTH, TM), lambda b, t: (b, 0, t)),
            pl.BlockSpec((1, NB, N_HEADS, V_ROWS, BLK), lambda b, t: (b, t, 0, 0, 0)),
            pl.BlockSpec((1, N_HEADS, TM, LANES), lambda b, t: (b, 0, t, 0)),
            pl.BlockSpec((1, 1, 8, LANES), lambda b, t: (b, t, 0, 0)),
        ],
        out_shape=(
            jax.ShapeDtypeStruct((batch, seq, D_MODEL), bf16),
            jax.ShapeDtypeStruct((batch, seq, D_MODEL), bf16),
            jax.ShapeDtypeStruct((batch, seq, ATTN_WIDTH), bf16),
            jax.ShapeDtypeStruct((batch, ATTN_WIDTH, seq), bf16),
            jax.ShapeDtypeStruct((batch, n_blocks, N_HEADS, V_ROWS, BLK), bf16),
            jax.ShapeDtypeStruct((batch, N_HEADS, seq, LANES), bf16),
            jax.ShapeDtypeStruct((batch, n_tiles, 8, LANES), f32),
        ),
        scratch_shapes=[pltpu.VMEM((POOL_PAD + N_META + TM, POOL_WIDTH), f32),
                        pltpu.VMEM((POOL_PAD + N_META + TM, 3 * POOL_GROUP), f32),
                        pltpu.VMEM((POOL_PAD + N_META + TM, 2 * POOL_GROUP), f32),
                        pltpu.VMEM((8, LANES), f32),
                        pltpu.VMEM((5 * IN_GROUP + F_ROWS, D_MODEL), bf16),
                        pltpu.VMEM((2 * D_MODEL, D_MODEL), bf16),
                        pltpu.VMEM((POOL_WIDTH, D_MODEL), bf16),
                        pltpu.VMEM((POOL_WIDTH, D_MODEL), bf16)],
        compiler_params=pltpu.CompilerParams(dimension_semantics=("arbitrary", "arbitrary"),
                                             vmem_limit_bytes=VMEM_LIMIT),
        name="in_proj",
    )(x, g_in, w_t, b_f, pw, ps, w_upp, u_meta, r0)

    r_blocks = r_tiles[:, :, :NB, :N_HEADS].reshape(batch * n_blocks * N_HEADS)

    y_attn = pl.pallas_call(
        functools.partial(_attn_kernel, n_blocks=n_blocks),
        grid=(batch, seq // BQ),
        in_specs=[
            pl.BlockSpec(memory_space=pltpu.SMEM),
            pl.BlockSpec((1, ATTN_WIDTH, BQ), lambda b, t: (b, 0, t)),
            pl.BlockSpec((1, ATTN_WIDTH, BQ), lambda b, t: (b, 0, jnp.minimum(t + 1, seq // BQ - 1))),
            pl.BlockSpec((1, N_HEADS, seq, LANES), lambda b, t: (b, 0, 0, 0)),
            pl.BlockSpec((1, n_blocks, N_HEADS, V_ROWS, BLK), lambda b, t: (b, 0, 0, 0, 0)),
            _const((N_HEADS, N_META, LANES)),
            _const((N_HEADS, V_ROWS, N_META)),
            pl.BlockSpec((1, BQ, ATTN_WIDTH), lambda b, t: (b, t, 0)),
        ],
        out_specs=pl.BlockSpec((1, BQ, ATTN_WIDTH), lambda b, t: (b, t, 0)),
        out_shape=jax.ShapeDtypeStruct((batch, seq, ATTN_WIDTH), bf16),
        scratch_shapes=[pltpu.VMEM((2, N_HEADS, LANES, BQ), bf16),
                        pltpu.VMEM((2, N_HEADS, BLK, BQ), f32),
                        pltpu.VMEM((N_HEADS, BLK, BQ), bf16),
                        pltpu.VMEM((N_HEADS, N_META, BQ), f32),
                        pltpu.VMEM((N_HEADS, N_META, BQ), bf16),
                        pltpu.VMEM((N_HEADS, 1, BQ), f32),
                        pltpu.VMEM((2, N_HEADS, 1, BQ), f32),
                        pltpu.VMEM((N_HEADS, 1, BQ), f32),
                        pltpu.VMEM((N_HEADS, 1, BQ), f32),
                        pltpu.VMEM((N_HEADS, V_ROWS, BQ), f32)],
        compiler_params=pltpu.CompilerParams(dimension_semantics=("arbitrary", "arbitrary"),
                                             vmem_limit_bytes=VMEM_LIMIT),
        name="attention",
    )(r_blocks, qT, qT, kaug, vT, k_meta, v_meta, sz)

    in_hbm = pl.BlockSpec(memory_space=pl.ANY)
    in_vmem = pl.BlockSpec(memory_space=pltpu.VMEM)
    return pl.pallas_call(
        functools.partial(_out_kernel, batch=batch, seq=seq),
        in_specs=[in_hbm, in_hbm, in_hbm, in_hbm, in_vmem, in_vmem, in_vmem],
        out_specs=pl.BlockSpec(memory_space=pl.ANY),
        out_shape=jax.ShapeDtypeStruct((batch, seq, D_MODEL), f32),
        scratch_shapes=[pltpu.VMEM((ATTN_WIDTH, D_MODEL), bf16),
                        pltpu.VMEM((D_MODEL, D_MODEL), bf16)],
        compiler_params=pltpu.CompilerParams(vmem_limit_bytes=VMEM_LIMIT),
        name="out_proj",
    )(x, y_attn, ga, mp, w_upa, w_o, g_out)
```

```python
import functools

import jax
import jax.numpy as jnp
from jax import lax
from jax.experimental import pallas as pl
from jax.experimental.pallas import tpu as pltpu

D_MODEL = 1024
N_META = 16
POOL_WIDTH = 512
POOL_WINDOWS = (2, 4, 8, 16)
POOL_GROUP = POOL_WIDTH // len(POOL_WINDOWS)
N_HEADS = 8
HEAD_DIM = 64
ATTN_WIDTH = N_HEADS * HEAD_DIM
RMS_EPS = 1e-6

LANES = 128
BLK = 256
BQ = 512
QB = BQ // BLK
assert QB == 2
TM = 512
NB = TM // BLK
TM_OUT = 1024
OUT_CHUNK = 256
OUT_BUFFERS = 3
POOL_PAD = 8
assert POOL_WINDOWS == (2, 4, 8, 16) and POOL_PAD >= POOL_WINDOWS[-1] // 2
V_ROWS = HEAD_DIM + 16
N_SPLIT = 3
MASKED = -1e30
SKEW = 2
LOG2E = 1.4426950408889634
VMEM_LIMIT = 56 * 1024 * 1024
assert POOL_WIDTH == ATTN_WIDTH
IN_GROUP = POOL_WIDTH
COL_U, COL_ZP, COL_Q, COL_K, COL_V, COL_ZA, COL_F = (IN_GROUP * i for i in range(7))
COL_G = COL_F + N_HEADS
F_ROWS = 16
N_IN = COL_G + 2 * D_MODEL

f32 = jnp.float32
bf16 = jnp.bfloat16


def _rmsnorm(x, g):
    return x * lax.rsqrt(jnp.mean(x * x, axis=-1, keepdims=True) + RMS_EPS) * g


def _sigmoid(x):
    return 0.5 * jnp.tanh(0.5 * x) + 0.5


def _log_sigmoid(x):
    return jnp.minimum(x, 0.0) - jnp.log1p(jnp.exp(-jnp.abs(x)))


def _dot(a, b):
    return jnp.dot(a, b, preferred_element_type=f32)


def _dot_nt(a, b):
    return lax.dot_general(a, b, (((1,), (1,)), ((), ())), preferred_element_type=f32)


def _lane_iota(shape):
    return lax.broadcasted_iota(jnp.int32, shape, 1)


def _decay_parts(logf3, rows):
    n = logf3.shape[0] // rows
    r = lax.broadcasted_iota(jnp.int32, (rows, rows), 0)
    c = lax.broadcasted_iota(jnp.int32, (rows, rows), 1)
    tri = (c <= r).astype(f32)
    side_by_side = jnp.concatenate([logf3[rows * i:rows * (i + 1)] for i in range(n)], axis=1)
    sums = jnp.dot(tri, side_by_side, precision=lax.Precision.HIGHEST, preferred_element_type=f32)
    beta = jnp.concatenate([sums[:, LANES * i:LANES * (i + 1)] for i in range(n)], axis=0)
    return beta, _split_decay(beta)


def _decay_parts_from_rows(logf_t, rows):
    n = logf_t.shape[1] // rows
    k = lax.broadcasted_iota(jnp.int32, (rows, rows), 0)
    j = lax.broadcasted_iota(jnp.int32, (rows, rows), 1)
    stacked = jnp.concatenate([logf_t[:, rows * i:rows * (i + 1)] for i in range(n)], axis=0)
    sums_t = jnp.dot(stacked, (k <= j).astype(f32), precision=lax.Precision.HIGHEST, preferred_element_type=f32)
    unused = jnp.zeros((LANES - N_SPLIT * N_HEADS, rows), f32)
    beta = jnp.concatenate(
        [jnp.concatenate([sums_t[N_HEADS * i:N_HEADS * (i + 1)]] * N_SPLIT + [unused], axis=0).T for i in range(n)],
        axis=0)
    return beta, _split_decay(beta)


def _split_decay(beta):
    nb = beta * (-LOG2E)
    hi = nb.astype(bf16).astype(f32)
    mid = (nb - hi).astype(bf16).astype(f32)
    lo = (nb - hi - mid).astype(bf16).astype(f32)
    lane = _lane_iota(nb.shape)
    return jnp.where(lane < N_HEADS, hi, jnp.where(lane < 2 * N_HEADS, mid, lo))


def _augmented_keys(kproj, parts, store):
    lane = _lane_iota((kproj.shape[0], LANES))
    for h in range(N_HEADS):
        slab = kproj[:, LANES * (h // 2):LANES * (h // 2 + 1)]
        if h % 2:
            slab = pltpu.roll(slab, HEAD_DIM, axis=1)
        store(h, jnp.where(lane < HEAD_DIM, slab, pltpu.roll(parts, HEAD_DIM - h, axis=1)).astype(bf16))


def _pool_mapped(wu_t, pw_ref):
    return jnp.concatenate([_dot(pw_ref[g].T.astype(bf16), wu_t[POOL_GROUP * g:POOL_GROUP * (g + 1)])
                            for g in range(len(POOL_WINDOWS))], axis=0).astype(bf16)


def _forget_rows(wf_t):
    unused = jnp.zeros((LANES - N_SPLIT * N_HEADS, wf_t.shape[1]), f32)
    return jnp.concatenate([wf_t] * N_SPLIT + [unused], axis=0).astype(bf16)


def _forget_bias(b_ref, shape, axis):
    index = lax.broadcasted_iota(jnp.int32, shape, axis)
    bias = jnp.zeros(shape, f32)
    for h in range(N_HEADS):
        bias = jnp.where((index % N_HEADS == h) & (index < N_SPLIT * N_HEADS), b_ref[h], bias)
    return bias


def _meta_kernel(meta_ref, g_ref, wu_ref, wk_ref, wv_ref, wf_ref, bf_ref, pw_ref,
                 umeta_ref, kmeta_ref, vmeta_ref, r0_ref):
    hn = _rmsnorm(meta_ref[...], g_ref[...]).astype(bf16)
    umeta_ref[...] = _dot_nt(hn, _pool_mapped(wu_ref[...].astype(bf16), pw_ref))
    vt = _dot_nt(wv_ref[...].astype(bf16), hn)
    for h in range(N_HEADS):
        vmeta_ref[h, 0:HEAD_DIM, :] = vt[HEAD_DIM * h:HEAD_DIM * (h + 1), :].astype(bf16)
        vmeta_ref[h, HEAD_DIM:V_ROWS, :] = jnp.ones((V_ROWS - HEAD_DIM, N_META), bf16)
    logf3 = _log_sigmoid(_dot_nt(hn, _forget_rows(wf_ref[...])) + _forget_bias(bf_ref, (1, LANES), 1))
    beta, parts = _decay_parts(logf3, N_META)

    def store(h, ka):
        kmeta_ref[h] = ka

    _augmented_keys(_dot_nt(hn, wk_ref[...].astype(bf16)), parts, store)
    r0_ref[...] = jnp.broadcast_to(beta[N_META - 1:N_META, :], r0_ref.shape)


def _proj_kernel(x_ref, g_ref, wt_ref, bf_ref, pw_ref, ps_ref, wup32_ref,
                 umeta_ref, r0_ref,
                 mp_ref, ga_ref, sz_ref, qT_ref, vT_ref, kaug_ref, r_ref,
                 uext_ref, lvl_a_ref, lvl_b_ref, rcarry_ref, wb_ref, wg_ref, wu_ref, wup_ref):
    t = pl.program_id(1)
    zp_rows, k_rows, za_rows, q_rows, v_rows = (slice(IN_GROUP * i, IN_GROUP * (i + 1)) for i in range(5))
    f_rows = slice(v_rows.stop, v_rows.stop + F_ROWS)

    @pl.when((pl.program_id(0) == 0) & (t == 0))
    def _():
        for rows, col in ((zp_rows, COL_ZP), (k_rows, COL_K), (za_rows, COL_ZA), (q_rows, COL_Q), (v_rows, COL_V)):
            wb_ref[rows, :] = wt_ref[col:col + IN_GROUP, :].astype(bf16)
        wb_ref[f_rows, :] = wt_ref[COL_F:COL_F + F_ROWS, :].astype(bf16)
        wg_ref[...] = wt_ref[COL_G:N_IN, :].astype(bf16)
        wu_ref[...] = _pool_mapped(wt_ref[COL_U:COL_ZP, :].astype(bf16), pw_ref)
        wup_ref[...] = wup32_ref[...].astype(bf16)
    ext = N_META + TM
    body = slice(POOL_PAD, POOL_PAD + ext)

    @pl.when(t == 0)
    def _():
        uext_ref[0:POOL_PAD, :] = jnp.zeros((POOL_PAD, POOL_WIDTH), f32)
        lvl_a_ref[0:POOL_PAD, :] = jnp.zeros((POOL_PAD, lvl_a_ref.shape[1]), f32)
        lvl_b_ref[0:POOL_PAD, :] = jnp.zeros((POOL_PAD, lvl_b_ref.shape[1]), f32)
        uext_ref[POOL_PAD:POOL_PAD + N_META, :] = umeta_ref[...]
        rcarry_ref[...] = r0_ref[...]

    hn = _rmsnorm(x_ref[0], g_ref[...]).astype(bf16)

    qv = _dot_nt(wb_ref[q_rows.start:, :], hn)
    qT_ref[0] = (qv[0:ATTN_WIDTH] * (HEAD_DIM ** -0.5 * LOG2E)).astype(bf16)
    for c in range(NB):
        for h in range(N_HEADS):
            rows = slice(ATTN_WIDTH + HEAD_DIM * h, ATTN_WIDTH + HEAD_DIM * (h + 1))
            vT_ref[0, c, h, 0:HEAD_DIM, :] = qv[rows, BLK * c:BLK * (c + 1)].astype(bf16)
            vT_ref[0, c, h, HEAD_DIM:V_ROWS, :] = jnp.ones((V_ROWS - HEAD_DIM, BLK), bf16)

    gat = _dot_nt(hn, wg_ref[D_MODEL:2 * D_MODEL, :])
    ga_ref[0] = _sigmoid(gat).astype(bf16)

    u = _dot_nt(hn, wu_ref[...])
    zp = _dot_nt(hn, wb_ref[zp_rows, :])
    uext_ref[POOL_PAD + N_META:POOL_PAD + ext, :] = u

    def shifted_sum(ref, shift, cols):
        return ref[body, cols] + ref[POOL_PAD - shift:POOL_PAD - shift + ext, cols]

    g1, g2, g3 = (slice(POOL_GROUP * g, POOL_GROUP * (g + 1)) for g in range(3))
    sums = [shifted_sum(uext_ref, 1, g1)]
    lvl_a_ref[body, :] = shifted_sum(uext_ref, 1, slice(POOL_GROUP, POOL_WIDTH))
    sums.append(shifted_sum(lvl_a_ref, 2, g1))
    lvl_b_ref[body, :] = shifted_sum(lvl_a_ref, 2, slice(POOL_GROUP, 3 * POOL_GROUP))
    sums.append(shifted_sum(lvl_b_ref, 4, g1))
    lvl_a_ref[body, g1] = shifted_sum(lvl_b_ref, 4, g2)
    sums.append(shifted_sum(lvl_a_ref, 8, g1))
    uext_ref[POOL_PAD:POOL_PAD + N_META, :] = uext_ref[POOL_PAD + TM:POOL_PAD + ext, :]

    pooled = jnp.concatenate([sums[g][N_META:] * (1.0 / w) for g, w in enumerate(POOL_WINDOWS)], axis=1) - u
    y_pool = (pooled * ps_ref[...] * (zp * _sigmoid(zp))).astype(bf16)
    gp = _dot_nt(hn, wg_ref[0:D_MODEL, :])
    mp_ref[0] = (_sigmoid(gp) * _dot(y_pool, wup_ref[...])).astype(bf16)

    logf_t = _log_sigmoid(qv[2 * IN_GROUP:2 * IN_GROUP + N_HEADS] + _forget_bias(bf_ref, (N_HEADS, TM), 0))
    r_ref[...] = jnp.zeros(r_ref.shape, f32)
    beta, parts = _decay_parts_from_rows(logf_t, BLK)
    for c in range(NB):
        r_ref[0, 0, c:c + 1, :] = rcarry_ref[0:1, :] * LOG2E
        rcarry_ref[...] = rcarry_ref[...] + beta[BLK * (c + 1) - 1:BLK * (c + 1), :]

    def store(h, ka):
        kaug_ref[0, h] = ka

    _augmented_keys(_dot_nt(hn, wb_ref[k_rows, :]), parts, store)

    za = _dot_nt(hn, wb_ref[za_rows, :])
    sz_ref[0] = (za * _sigmoid(za)).astype(bf16)


def _attn_kernel(r_ref, qT_ref, qnext_ref, kaug_ref, vT_ref, kmeta_ref, vmeta_ref, sz_ref, o_ref,
                 qaug_ref, s_ref, p_ref, sm_ref, pm_ref, m_ref, mblk_ref, mmeta_ref, alpha_ref, acc_ref,
                 *, n_blocks):
    b = pl.program_id(0)
    t = pl.program_id(1)
    first = QB * t
    heads = range(N_HEADS)

    this_slot, next_slot = t % 2, (t + 1) % 2

    @pl.when((b == 0) & (t == 0))
    def _():
        part_row = lax.broadcasted_iota(jnp.int32, (LANES - HEAD_DIM, BQ), 0)
        ones_rows = ((part_row % N_HEADS == 0) & (part_row < N_SPLIT * N_HEADS)).astype(bf16)
        for slot in range(2):
            for h in heads:
                qaug_ref[slot, h, HEAD_DIM:LANES, :] = ones_rows

    @pl.when(t == 0)
    def _():
        for h in heads:
            qaug_ref[0, h, 0:HEAD_DIM, :] = qT_ref[0, HEAD_DIM * h:HEAD_DIM * (h + 1), :]

    for h in heads:
        qaug_ref[next_slot, h, 0:HEAD_DIM, :] = qnext_ref[0, HEAD_DIM * h:HEAD_DIM * (h + 1), :]

    def r_at(j, h):
        return r_ref[(b * n_blocks + j) * N_HEADS + h]

    def col_max(s):
        return jnp.max(s, axis=0, keepdims=True)

    key_index = functools.partial(lax.broadcasted_iota, jnp.int32, dimension=0)
    query_index = functools.partial(lax.broadcasted_iota, jnp.int32, dimension=1)

    def seen_by(diag):
        return slice(BLK * diag, BQ) if diag else slice(None)

    def score_item(j, slot, h, diag=None, meta=False, next_tile=False):
        q_slot = next_slot if next_tile else this_slot
        cols = seen_by(diag)
        s = _dot(kaug_ref[0, h, pl.ds(pl.multiple_of(j * BLK, BLK), BLK), :], qaug_ref[q_slot, h, :, cols])
        if diag is not None:
            s = jnp.where(key_index(s.shape) <= query_index(s.shape), s, MASKED)
        s_ref[slot, h, :, cols] = s
        unseen = [jnp.full((1, BQ - s.shape[1]), MASKED, f32)] if s.shape[1] < BQ else []
        mblk_ref[slot, h] = jnp.concatenate(unseen + [col_max(s)], axis=1)
        if meta:
            sm = _dot(kmeta_ref[h], qaug_ref[this_slot, h])
            sm_ref[h] = sm
            mmeta_ref[h] = col_max(sm)

    def softmax_item(j, slot, h, meta=False, diag=None):
        cols = seen_by(diag)
        assert not (meta and diag)
        ref = r_at(first, h)
        off = ref - r_at(j, h)
        m_old = m_ref[h]
        m_blk = mblk_ref[slot, h] + off
        if meta:
            m_blk = jnp.maximum(m_blk, mmeta_ref[h] + ref)
        m_new = jnp.maximum(m_old, m_blk)
        alpha_ref[h] = jnp.exp2(m_old - m_new)
        m_seen = m_new
        if diag:
            m_seen = jnp.maximum(m_ref[h, :, cols], mblk_ref[slot, h, :, cols] + off)
        m_ref[h] = m_new
        p_ref[h, :, cols] = jnp.exp2(s_ref[slot, h, :, cols] - (m_seen - off)).astype(bf16)
        if meta:
            pm_ref[h] = jnp.exp2(sm_ref[h] - (m_new - ref)).astype(bf16)

    def value_item(j, slot, h, meta=False, diag=None):
        cols = seen_by(diag)
        pv = _dot(vT_ref[0, j, h], p_ref[h, :, cols])
        if meta:
            pv = pv + _dot(vmeta_ref[h], pm_ref[h])
        acc_ref[h, :, cols] = acc_ref[h, :, cols] * alpha_ref[h, :, cols] + pv

    def run(score_blocks, finish_blocks):
        score_items = [(blk + (h,), kw) for blk, kw in score_blocks for h in heads]
        finish_items = [(blk + (h,), kw) for blk, kw in finish_blocks for h in heads]
        for i in range(max(len(score_items), len(finish_items) + SKEW + 1)):
            if i < len(score_items):
                args, kw = score_items[i]
                score_item(*args, **kw)
            if 0 <= i - SKEW - 1 < len(finish_items):
                args, kw = finish_items[i - SKEW - 1]
                value_item(*args, **kw)
            if 0 <= i - SKEW < len(finish_items):
                args, kw = finish_items[i - SKEW]
                softmax_item(*args, **kw)

    def reset_state(h):
        m_ref[h] = jnp.full(m_ref.shape[1:], MASKED, f32)
        acc_ref[h] = jnp.zeros(acc_ref.shape[1:], f32)

    @pl.when((b == 0) & (t == 0))
    def _():
        for h in heads:
            reset_state(h)

    @pl.when(first == 0)
    def _():
        run([((first, 0), dict(diag=0))], [])

    @pl.when(first > 0)
    def _():
        def body(k, carry):
            run([((2 * k + 1, 1), {}), ((2 * k + 2, 0), {})],
                [((2 * k, 0), {}), ((2 * k + 1, 1), {})])
            return carry

        lax.fori_loop(0, t - 1, body, 0)
        run([((first - 1, 1), {}), ((first, 0), dict(diag=0))],
            [((first - 2, 0), {}), ((first - 1, 1), {})])

    run([((first + 1, 1), dict(diag=1, meta=True)), ((0, 0), dict(next_tile=True))],
        [((first, 0), dict(meta=True)), ((first + 1, 1), dict(diag=1))])

    for pair in range(N_HEADS // 2):
        halves = []
        for h in (2 * pair, 2 * pair + 1):
            a = acc_ref[h]
            halves.append(a[0:HEAD_DIM] * (1.0 / a[HEAD_DIM:HEAD_DIM + 1]))
        cols = slice(LANES * pair, LANES * (pair + 1))
        o_ref[0, :, cols] = (jnp.concatenate(halves, axis=0).T * sz_ref[0, :, cols].astype(f32)).astype(bf16)
        for h in (2 * pair, 2 * pair + 1):
            reset_state(h)


def _out_kernel(x_hbm, y_hbm, ga_hbm, mp_hbm, wua32_ref, wout32_ref, g_ref, out_hbm, wua_ref, wout_ref,
                *, batch, seq):
    wua_ref[...] = wua32_ref[...].astype(bf16)
    wout_ref[...] = wout32_ref[...].astype(bf16)

    def tile(x_ref, y_ref, ga_ref, mp_ref, out_ref):
        def merged(rows):
            up = _dot(y_ref[0, rows, :], wua_ref[...])
            return (mp_ref[0, rows, :].astype(f32) + ga_ref[0, rows, :].astype(f32) * up).astype(bf16)

        chunks = [slice(OUT_CHUNK * i, OUT_CHUNK * (i + 1)) for i in range(TM_OUT // OUT_CHUNK)]
        pending = merged(chunks[0])
        for i, rows in enumerate(chunks):
            ahead = merged(chunks[i + 1]) if i + 1 < len(chunks) else None
            h_out = x_ref[0, rows, :] + _dot(pending, wout_ref[...])
            out_ref[0, rows, :] = _rmsnorm(h_out, g_ref[...])
            pending = ahead

    def rows_of(width, **kw):
        return pl.BlockSpec((1, TM_OUT, width), lambda b, t: (b, t, 0), **kw)

    streamed = dict(pipeline_mode=pl.Buffered(OUT_BUFFERS))
    pltpu.emit_pipeline(
        tile,
        grid=(batch, seq // TM_OUT),
        in_specs=[rows_of(D_MODEL, **streamed), rows_of(ATTN_WIDTH, **streamed),
                  rows_of(D_MODEL, **streamed), rows_of(D_MODEL, **streamed)],
        out_specs=[rows_of(D_MODEL)],
    )(x_hbm, y_hbm, ga_hbm, mp_hbm, out_hbm)


def _const(shape):
    return pl.BlockSpec(shape, lambda *_: (0,) * len(shape), pipeline_mode=pl.Buffered(1))


def kernel(x, meta_tokens, norm_g, w_in, b_forget, pool_w, pool_scale, w_up_pool, w_up_attn, w_out, final_norm_g):
    batch, seq, _ = x.shape
    n_tiles = seq // TM
    n_blocks = seq // BLK

    w_t = jnp.transpose(w_in[0])
    b_f = b_forget.reshape(N_HEADS)
    g_in = norm_g[0].reshape(1, D_MODEL)
    g_out = final_norm_g.reshape(1, D_MODEL)
    pw = pool_w[0]
    ps = pool_scale[0].reshape(1, POOL_WIDTH)
    w_upp, w_upa, w_o = w_up_pool[0], w_up_attn[0], w_out[0]

    u_meta, k_meta, v_meta, r0 = pl.pallas_call(
        _meta_kernel,
        out_shape=(jax.ShapeDtypeStruct((N_META, POOL_WIDTH), f32),
                   jax.ShapeDtypeStruct((N_HEADS, N_META, LANES), bf16),
                   jax.ShapeDtypeStruct((N_HEADS, V_ROWS, N_META), bf16),
                   jax.ShapeDtypeStruct((8, LANES), f32)),
        grid=(1,),
        in_specs=[_const((N_META, D_MODEL)), _const((1, D_MODEL))]
        + [pl.BlockSpec((rows, D_MODEL), functools.partial(lambda r, i: (r, 0), col // rows),
                        pipeline_mode=pl.Buffered(1))
           for col, rows in ((COL_U, IN_GROUP), (COL_K, IN_GROUP), (COL_V, IN_GROUP), (COL_F, N_HEADS))]
        + [pl.BlockSpec(memory_space=pltpu.SMEM), _const((len(POOL_WINDOWS), POOL_GROUP, POOL_GROUP))],
        out_specs=[_const((N_META, POOL_WIDTH)), _const((N_HEADS, N_META, LANES)),
                   _const((N_HEADS, V_ROWS, N_META)), _const((8, LANES))],
        compiler_params=pltpu.CompilerParams(vmem_limit_bytes=VMEM_LIMIT),
        name="meta_proj",
    )(meta_tokens, g_in, w_t, w_t, w_t, w_t, b_f, pw)

    mp, ga, sz, qT, vT, kaug, r_tiles = pl.pallas_call(
        _proj_kernel,
        grid=(batch, n_tiles),
        in_specs=[
            pl.BlockSpec((1, TM, D_MODEL), lambda b, t: (b, t, 0)),
            _const((1, D_MODEL)),
            _const((N_IN, D_MODEL)),
            pl.BlockSpec(memory_space=pltpu.SMEM),
            _const((len(POOL_WINDOWS), POOL_GROUP, POOL_GROUP)),
            _const((1, POOL_WIDTH)),
            _const((POOL_WIDTH, D_MODEL)),
            _const((N_META, POOL_WIDTH)),
            _const((8, LANES)),
        ],
        out_specs=[
            pl.BlockSpec((1, TM, D_MODEL), lambda b, t: (b, t, 0)),
            pl.BlockSpec((1, TM, D_MODEL), lambda b, t: (b, t, 0)),
            pl.BlockSpec((1, TM, ATTN_WIDTH), lambda b, t: (b, t, 0)),
            pl.BlockSpec((1, ATTN_WIDTH, TM), lambda b, t: (b, 0, t)),
            pl.BlockSpec((1, NB, N_HEADS, V_ROWS, BLK), lambda b, t: (b, t, 0, 0, 0)),
            pl.BlockSpec((1, N_HEADS, TM, LANES), lambda b, t: (b, 0, t, 0)),
            pl.BlockSpec((1, 1, 8, LANES), lambda b, t: (b, t, 0, 0)),
        ],
        out_shape=(
            jax.ShapeDtypeStruct((batch, seq, D_MODEL), bf16),
            jax.ShapeDtypeStruct((batch, seq, D_MODEL), bf16),
            jax.ShapeDtypeStruct((batch, seq, ATTN_WIDTH), bf16),
            jax.ShapeDtypeStruct((batch, ATTN_WIDTH, seq), bf16),
            jax.ShapeDtypeStruct((batch, n_blocks, N_HEADS, V_ROWS, BLK), bf16),
            jax.ShapeDtypeStruct((batch, N_HEADS, seq, LANES), bf16),
            jax.ShapeDtypeStruct((batch, n_tiles, 8, LANES), f32),
        ),
        scratch_shapes=[pltpu.VMEM((POOL_PAD + N_META + TM, POOL_WIDTH), f32),
                        pltpu.VMEM((POOL_PAD + N_META + TM, 3 * POOL_GROUP), f32),
                        pltpu.VMEM((POOL_PAD + N_META + TM, 2 * POOL_GROUP), f32),
                        pltpu.VMEM((8, LANES), f32),
                        pltpu.VMEM((5 * IN_GROUP + F_ROWS, D_MODEL), bf16),
                        pltpu.VMEM((2 * D_MODEL, D_MODEL), bf16),
                        pltpu.VMEM((POOL_WIDTH, D_MODEL), bf16),
                        pltpu.VMEM((POOL_WIDTH, D_MODEL), bf16)],
        compiler_params=pltpu.CompilerParams(dimension_semantics=("arbitrary", "arbitrary"),
                                             vmem_limit_bytes=VMEM_LIMIT),
        name="in_proj",
    )(x, g_in, w_t, b_f, pw, ps, w_upp, u_meta, r0)

    r_blocks = r_tiles[:, :, :NB, :N_HEADS].reshape(batch * n_blocks * N_HEADS)

    y_attn = pl.pallas_call(
        functools.partial(_attn_kernel, n_blocks=n_blocks),
        grid=(batch, seq // BQ),
        in_specs=[
            pl.BlockSpec(memory_space=pltpu.SMEM),
            pl.BlockSpec((1, ATTN_WIDTH, BQ), lambda b, t: (b, 0, t)),
            pl.BlockSpec((1, ATTN_WIDTH, BQ), lambda b, t: (b, 0, jnp.minimum(t + 1, seq // BQ - 1))),
            pl.BlockSpec((1, N_HEADS, seq, LANES), lambda b, t: (b, 0, 0, 0)),
            pl.BlockSpec((1, n_blocks, N_HEADS, V_ROWS, BLK), lambda b, t: (b, 0, 0, 0, 0)),
            _const((N_HEADS, N_META, LANES)),
            _const((N_HEADS, V_ROWS, N_META)),
            pl.BlockSpec((1, BQ, ATTN_WIDTH), lambda b, t: (b, t, 0)),
        ],
        out_specs=pl.BlockSpec((1, BQ, ATTN_WIDTH), lambda b, t: (b, t, 0)),
        out_shape=jax.ShapeDtypeStruct((batch, seq, ATTN_WIDTH), bf16),
        scratch_shapes=[pltpu.VMEM((2, N_HEADS, LANES, BQ), bf16),
                        pltpu.VMEM((2, N_HEADS, BLK, BQ), f32),
                        pltpu.VMEM((N_HEADS, BLK, BQ), bf16),
                        pltpu.VMEM((N_HEADS, N_META, BQ), f32),
                        pltpu.VMEM((N_HEADS, N_META, BQ), bf16),
                        pltpu.VMEM((N_HEADS, 1, BQ), f32),
                        pltpu.VMEM((2, N_HEADS, 1, BQ), f32),
                        pltpu.VMEM((N_HEADS, 1, BQ), f32),
                        pltpu.VMEM((N_HEADS, 1, BQ), f32),
                        pltpu.VMEM((N_HEADS, V_ROWS, BQ), f32)],
        compiler_params=pltpu.CompilerParams(dimension_semantics=("arbitrary", "arbitrary"),
                                             vmem_limit_bytes=VMEM_LIMIT),
        name="attention",
    )(r_blocks, qT, qT, kaug, vT, k_meta, v_meta, sz)

    in_hbm = pl.BlockSpec(memory_space=pl.ANY)
    in_vmem = pl.BlockSpec(memory_space=pltpu.VMEM)
    return pl.pallas_call(
        functools.partial(_out_kernel, batch=batch, seq=seq),
        in_specs=[in_hbm, in_hbm, in_hbm, in_hbm, in_vmem, in_vmem, in_vmem],
        out_specs=pl.BlockSpec(memory_space=pl.ANY),
        out_shape=jax.ShapeDtypeStruct((batch, seq, D_MODEL), f32),
        scratch_shapes=[pltpu.VMEM((ATTN_WIDTH, D_MODEL), bf16),
                        pltpu.VMEM((D_MODEL, D_MODEL), bf16)],
        compiler_params=pltpu.CompilerParams(vmem_limit_bytes=VMEM_LIMIT),
        name="out_proj",
    )(x, y_attn, ga, mp, w_upa, w_o, g_out)
```
